```python
import math
import jax, jax.numpy as jnp
from jax import lax
import numpy as np

D_MODEL = 4096
BATCH = 8
SEQ = 4096
DEPTH = 1

PLE_DIM = 256
ATTN_HEAD_DIM = 128
ATTN_HEADS_PER_GROUP = 4
DILATED_GROUPS = ((128, 1), (512, 4), (2048, 16))
N_ATTN_GROUPS = len(DILATED_GROUPS)
N_ATTN_HEADS = N_ATTN_GROUPS * ATTN_HEADS_PER_GROUP
ATTN_WIDTH = N_ATTN_HEADS * ATTN_HEAD_DIM
ATTN_OUT_WIDTH = ATTN_HEADS_PER_GROUP * ATTN_HEAD_DIM
ATTN_BLOCK = 128
N_BUCKETS = 32
MAX_DISTANCE = 2048
RWKV_HEAD_DIM = 64
RWKV_WIDTH = D_MODEL // 2
RWKV_HEADS = RWKV_WIDTH // RWKV_HEAD_DIM
DECAY_LORA = 96
AAA_LORA = 96
GATE_LORA = 256
RWKV_IN_WIDTH = 3 * RWKV_WIDTH + DECAY_LORA + AAA_LORA + GATE_LORA
D_FF = 4 * D_MODEL
N_BRANCHES = 2
IN_WIDTH = 3 * ATTN_WIDTH + RWKV_IN_WIDTH + N_BRANCHES * D_MODEL
RMS_EPS = 1e-6
GN_EPS = 64e-5

kernel_name = "hybrid_dilated_attn_rwkv7_gated_block"


def rms_norm(x, gain):
    x32 = x.astype(jnp.float32)
    y = x32 * lax.rsqrt(jnp.mean(jnp.square(x32), axis=-1, keepdims=True) + RMS_EPS)
    return (y * gain.astype(jnp.float32)).astype(x.dtype)


def t5_bucket(dist):
    max_exact = N_BUCKETS // 2
    d_f = jnp.maximum(dist, 1).astype(jnp.float32)
    large = max_exact + (jnp.log(d_f / max_exact) / math.log(MAX_DISTANCE / max_exact)
                         * (N_BUCKETS - max_exact)).astype(jnp.int32)
    large = jnp.minimum(large, N_BUCKETS - 1)
    return jnp.where(dist < max_exact, dist, large)


def dilated_window_attention(q, k, v, bias_table, window, dilation):
    batch, seq, heads, hd = q.shape
    n_dist = window // dilation
    blk = ATTN_BLOCK
    span = dilation * blk
    s_pad = -(-seq // span) * span
    length = s_pad // dilation
    nb = length // blk

    def to_blocks(t):
        t = jnp.pad(t, ((0, 0), (0, s_pad - seq), (0, 0), (0, 0)))
        t = t.reshape(batch, length, dilation, heads, hd).transpose(0, 2, 3, 1, 4)
        return t.reshape(batch, dilation, heads, nb, blk, hd)

    def with_prev(t):
        prev = jnp.pad(t[:, :, :, :-1], ((0, 0), (0, 0), (0, 0), (1, 0), (0, 0), (0, 0)))
        return jnp.concatenate([prev, t], axis=4)

    qb = to_blocks(q).astype(jnp.float32)
    kb = with_prev(to_blocks(k)).astype(jnp.float32)
    vb = with_prev(to_blocks(v)).astype(jnp.float32)
    s = jnp.einsum('bdhnqe,bdhnke->bdhnqk', qb, kb) * (hd ** -0.5)

    q_idx = blk + jnp.arange(blk)
    k_idx = jnp.arange(2 * blk)
    rel = q_idx[:, None] - k_idx[None, :]
    band = (rel >= 0) & (rel <= n_dist)
    first = (jnp.arange(nb)[:, None, None] == 0) & (k_idx[None, None, :] < blk)
    valid = band[None] & ~first
    bucket = t5_bucket(jnp.maximum(rel, 0) * dilation)
    bias = jnp.transpose(bias_table[bucket].astype(jnp.float32), (2, 0, 1))
    s = jnp.where(valid, s + bias[:, None], -jnp.inf)

    m = jnp.max(s, axis=-1, keepdims=True)
    e = jnp.exp(s - m)
    l = jnp.sum(e, axis=-1, keepdims=True)
    o = jnp.einsum('bdhnqk,bdhnke->bdhnqe', e, vb) / l
    lse = (m + jnp.log(l))[..., 0]

    o = o.reshape(batch, dilation, heads, length, hd).transpose(0, 3, 1, 2, 4)
    o = o.reshape(batch, s_pad, heads, hd)[:, :seq]
    lse = lse.reshape(batch, dilation, heads, length).transpose(0, 3, 1, 2)
    lse = lse.reshape(batch, s_pad, heads)[:, :seq]
    return o, lse


def token_shift(z, mix):
    prev = jnp.pad(z, ((0, 0), (1, 0), (0, 0)))[:, :-1]
    return z + mix * (prev - z)


def rwkv7_time_mix(z, w0, w_decay_up, a0, w_aaa_up, w_gate_up, k_k, k_a, r_k, gn_w, gn_b):
    batch, seq, _ = z.shape
    f32 = jnp.float32
    z = z.astype(f32)
    c0 = RWKV_WIDTH
    r = z[..., :c0]
    k = z[..., c0:2 * c0]
    v = z[..., 2 * c0:3 * c0]
    xw = z[..., 3 * c0:3 * c0 + DECAY_LORA]
    xa = z[..., 3 * c0 + DECAY_LORA:3 * c0 + DECAY_LORA + AAA_LORA]
    xg = z[..., 3 * c0 + DECAY_LORA + AAA_LORA:]

    w = -jax.nn.softplus(-(w0.astype(f32) + jnp.tanh(xw) @ w_decay_up.astype(f32))) - 0.5
    a = jax.nn.sigmoid(a0.astype(f32) + xa @ w_aaa_up.astype(f32))
    g = jax.nn.sigmoid(xg) @ w_gate_up.astype(f32)
    decay = jnp.exp(-jnp.exp(w))

    hs = (batch, seq, RWKV_HEADS, RWKV_HEAD_DIM)
    kk = (k * k_k.astype(f32)).reshape(hs)
    kk = kk / jnp.maximum(jnp.linalg.norm(kk, axis=-1, keepdims=True), 1e-12)
    k = k * (1.0 + (a - 1.0) * k_a.astype(f32))
    r_h, k_h, v_h, a_h, d_h = (t.reshape(hs) for t in (r, k, v, a, decay))

    def step(state, inp):
        r_t, w_t, k_t, v_t, aa_t, bb_t = inp
        sa = jnp.einsum('bhij,bhj->bhi', state, aa_t)
        state = (state * w_t[:, :, None, :] + sa[..., None] * bb_t[:, :, None, :]
                 + v_t[..., None] * k_t[:, :, None, :])
        return state, jnp.einsum('bhij,bhj->bhi', state, r_t)

    xs = tuple(jnp.moveaxis(t, 1, 0) for t in (r_h, d_h, k_h, v_h, -kk, kk * a_h))
    state0 = jnp.zeros((batch, RWKV_HEADS, RWKV_HEAD_DIM, RWKV_HEAD_DIM), f32)
    _, y = lax.scan(step, state0, xs)
    y = jnp.moveaxis(y, 0, 1)

    mu = jnp.mean(y, axis=-1, keepdims=True)
    var = jnp.mean(jnp.square(y - mu), axis=-1, keepdims=True)
    y = ((y - mu) * lax.rsqrt(var + GN_EPS)).reshape(batch, seq, RWKV_WIDTH)
    y = (y * gn_w.astype(f32) + gn_b.astype(f32)).reshape(hs)
    y = y + jnp.sum(r_h * k_h * r_k.astype(f32), axis=-1, keepdims=True) * v_h
    return y.reshape(batch, seq, RWKV_WIDTH) * g


def _fwd_setup_inputs(seed: int = 0) -> dict:
    key = jax.random.key(seed)
    ks = jax.random.split(key, 32)
    f32 = jnp.float32

    def nrm(k, shape, scale):
        return jax.random.normal(k, shape, f32) * scale

    def gain(k, shape):
        return 1.0 + 0.05 * jax.random.normal(k, shape, f32)

    L = DEPTH
    return {
        "x": nrm(ks[0], (BATCH, SEQ, D_MODEL), 1.0),
        "p": nrm(ks[1], (DEPTH, BATCH, SEQ, PLE_DIM), 1.0),
        "norm_mix": gain(ks[2], (L, D_MODEL)),
        "w_in": nrm(ks[3], (L, D_MODEL, IN_WIDTH), D_MODEL ** -0.5),
        "q_gain": gain(ks[4], (L, ATTN_HEAD_DIM)),
        "k_gain": gain(ks[5], (L, ATTN_HEAD_DIM)),
        "rel_bias": nrm(ks[6], (N_BUCKETS, N_ATTN_HEADS), 0.5),
        "w_attn_up": nrm(ks[7], (L, ATTN_OUT_WIDTH, D_MODEL), ATTN_OUT_WIDTH ** -0.5),
        "shift_mix": jax.random.uniform(ks[8], (L, RWKV_IN_WIDTH), f32),
        "w0": jax.random.uniform(ks[9], (L, RWKV_WIDTH), f32, minval=-5.0, maxval=1.0),
        "w_decay_up": nrm(ks[10], (L, DECAY_LORA, RWKV_WIDTH), DECAY_LORA ** -0.5),
        "a0": nrm(ks[11], (L, RWKV_WIDTH), 0.1),
        "w_aaa_up": nrm(ks[12], (L, AAA_LORA, RWKV_WIDTH), AAA_LORA ** -0.5),
        "w_gate_up": nrm(ks[13], (L, GATE_LORA, RWKV_WIDTH), GATE_LORA ** -0.5),
        "k_k": 0.85 + nrm(ks[14], (L, RWKV_WIDTH), 0.05),
        "k_a": gain(ks[15], (L, RWKV_WIDTH)),
        "r_k": nrm(ks[16], (L, RWKV_HEADS, RWKV_HEAD_DIM), 0.1),
        "gn_w": gain(ks[17], (L, RWKV_WIDTH)),
        "gn_b": nrm(ks[18], (L, RWKV_WIDTH), 0.02),
        "w_rwkv_up": nrm(ks[19], (L, RWKV_WIDTH, D_MODEL), RWKV_WIDTH ** -0.5),
        "w_out": nrm(ks[20], (L, D_MODEL, D_MODEL), D_MODEL ** -0.5),
        "norm_mlp": gain(ks[21], (L, D_MODEL)),
        "w_mlp_in": nrm(ks[22], (L, D_MODEL, D_FF), D_MODEL ** -0.5),
        "w_mlp_out": nrm(ks[23], (L, D_FF, D_MODEL), D_FF ** -0.5),
        "norm_ple": gain(ks[24], (L, D_MODEL)),
        "w_ple_gate": nrm(ks[25], (L, D_MODEL, D_MODEL), D_MODEL ** -0.5),
        "w_ple_proj": nrm(ks[26], (L, PLE_DIM, D_MODEL), PLE_DIM ** -0.5),
    }


def _fwd_reference(x, p, norm_mix, w_in, q_gain, k_gain, rel_bias, w_attn_up, shift_mix, w0, w_decay_up,
              a0, w_aaa_up, w_gate_up, k_k, k_a, r_k, gn_w, gn_b, w_rwkv_up, w_out, norm_mlp,
              w_mlp_in, w_mlp_out, norm_ple, w_ple_gate, w_ple_proj):
    batch, seq, _ = x.shape
    a_end = 3 * ATTN_WIDTH
    r_end = a_end + RWKV_IN_WIDTH
    for i in range(DEPTH):
        h = rms_norm(x, norm_mix[i])
        proj = h @ w_in[i]
        qkv = proj[..., :a_end].reshape(batch, seq, 3, N_ATTN_HEADS, ATTN_HEAD_DIM)
        z = proj[..., a_end:r_end]
        gates = jax.nn.sigmoid(proj[..., r_end:]).reshape(batch, seq, N_BRANCHES, D_MODEL)

        q = rms_norm(qkv[:, :, 0], q_gain[i])
        k = rms_norm(qkv[:, :, 1], k_gain[i])
        v = qkv[:, :, 2]
        outs, lses = [], []
        for gi, (window, dilation) in enumerate(DILATED_GROUPS):
            sl = slice(gi * ATTN_HEADS_PER_GROUP, (gi + 1) * ATTN_HEADS_PER_GROUP)
            o, lse = dilated_window_attention(q[:, :, sl], k[:, :, sl], v[:, :, sl],
                                              rel_bias[:, sl], window, dilation)
            outs.append(o)
            lses.append(lse)
        mix_w = jax.nn.softmax(jnp.stack(lses, axis=0), axis=0)
        attn = jnp.sum(mix_w[..., None] * jnp.stack(outs, axis=0), axis=0)
        attn = attn.reshape(batch, seq, ATTN_OUT_WIDTH).astype(x.dtype)
        attn_d = attn @ w_attn_up[i]

        zs = token_shift(z, shift_mix[i])
        rw = rwkv7_time_mix(zs, w0[i], w_decay_up[i], a0[i], w_aaa_up[i], w_gate_up[i],
                            k_k[i], k_a[i], r_k[i], gn_w[i], gn_b[i]).astype(x.dtype)
        rwkv_d = rw @ w_rwkv_up[i]

        merged = gates[:, :, 0] * attn_d + gates[:, :, 1] * rwkv_d
        x = x + merged @ w_out[i]

        h = rms_norm(x, norm_mlp[i])
        x = x + jnp.square(jax.nn.relu(h @ w_mlp_in[i])) @ w_mlp_out[i]

        ple_gate = jax.nn.sigmoid(rms_norm(x, norm_ple[i]) @ w_ple_gate[i])
        x = x + ple_gate * (p[i] @ w_ple_proj[i])
    return x


import jax as _jax
import jax.numpy as _jnp

TWIN_FORMAT = 'train_step'
FWD_PARAMS = ['x', 'p', 'norm_mix', 'w_in', 'q_gain', 'k_gain', 'rel_bias', 'w_attn_up', 'shift_mix', 'w0', 'w_decay_up', 'a0', 'w_aaa_up', 'w_gate_up', 'k_k', 'k_a', 'r_k', 'gn_w', 'gn_b', 'w_rwkv_up', 'w_out', 'norm_mlp', 'w_mlp_in', 'w_mlp_out', 'norm_ple', 'w_ple_gate', 'w_ple_proj']
TWIN_WEIGHTS = ['norm_mix', 'w_in', 'q_gain', 'k_gain', 'rel_bias', 'w_attn_up', 'shift_mix', 'w0', 'w_decay_up', 'a0', 'w_aaa_up', 'w_gate_up', 'k_k', 'k_a', 'r_k', 'gn_w', 'gn_b', 'w_rwkv_up', 'w_out', 'norm_mlp', 'w_mlp_in', 'w_mlp_out', 'norm_ple', 'w_ple_gate', 'w_ple_proj']
TWIN_DIFF_INPUT = 'x'
TWIN_INPUTS = ['x', 'p', 'norm_mix', 'w_in', 'q_gain', 'k_gain', 'rel_bias', 'w_attn_up', 'shift_mix', 'w0', 'w_decay_up', 'a0', 'w_aaa_up', 'w_gate_up', 'k_k', 'k_a', 'r_k', 'gn_w', 'gn_b', 'w_rwkv_up', 'w_out', 'norm_mlp', 'w_mlp_in', 'w_mlp_out', 'norm_ple', 'w_ple_gate', 'w_ple_proj', 'loss_target', 'm_norm_mix', 'm_w_in', 'm_q_gain', 'm_k_gain', 'm_rel_bias', 'm_w_attn_up', 'm_shift_mix', 'm_w0', 'm_w_decay_up', 'm_a0', 'm_w_aaa_up', 'm_w_gate_up', 'm_k_k', 'm_k_a', 'm_r_k', 'm_gn_w', 'm_gn_b', 'm_w_rwkv_up', 'm_w_out', 'm_norm_mlp', 'm_w_mlp_in', 'm_w_mlp_out', 'm_norm_ple', 'm_w_ple_gate', 'm_w_ple_proj', 'v_norm_mix', 'v_w_in', 'v_q_gain', 'v_k_gain', 'v_rel_bias', 'v_w_attn_up', 'v_shift_mix', 'v_w0', 'v_w_decay_up', 'v_a0', 'v_w_aaa_up', 'v_w_gate_up', 'v_k_k', 'v_k_a', 'v_r_k', 'v_gn_w', 'v_gn_b', 'v_w_rwkv_up', 'v_w_out', 'v_norm_mlp', 'v_w_mlp_in', 'v_w_mlp_out', 'v_norm_ple', 'v_w_ple_gate', 'v_w_ple_proj']
TWIN_OUTPUTS = ['loss', 'grad_x', 'grad_norm_mix', 'grad_w_in', 'grad_q_gain', 'grad_k_gain', 'grad_rel_bias', 'grad_w_attn_up', 'grad_shift_mix', 'grad_w0', 'grad_w_decay_up', 'grad_a0', 'grad_w_aaa_up', 'grad_w_gate_up', 'grad_k_k', 'grad_k_a', 'grad_r_k', 'grad_gn_w', 'grad_gn_b', 'grad_w_rwkv_up', 'grad_w_out', 'grad_norm_mlp', 'grad_w_mlp_in', 'grad_w_mlp_out', 'grad_norm_ple', 'grad_w_ple_gate', 'grad_w_ple_proj', 'delta_norm_mix', 'delta_w_in', 'delta_q_gain', 'delta_k_gain', 'delta_rel_bias', 'delta_w_attn_up', 'delta_shift_mix', 'delta_w0', 'delta_w_decay_up', 'delta_a0', 'delta_w_aaa_up', 'delta_w_gate_up', 'delta_k_k', 'delta_k_a', 'delta_r_k', 'delta_gn_w', 'delta_gn_b', 'delta_w_rwkv_up', 'delta_w_out', 'delta_norm_mlp', 'delta_w_mlp_in', 'delta_w_mlp_out', 'delta_norm_ple', 'delta_w_ple_gate', 'delta_w_ple_proj', 'new_m_norm_mix', 'new_m_w_in', 'new_m_q_gain', 'new_m_k_gain', 'new_m_rel_bias', 'new_m_w_attn_up', 'new_m_shift_mix', 'new_m_w0', 'new_m_w_decay_up', 'new_m_a0', 'new_m_w_aaa_up', 'new_m_w_gate_up', 'new_m_k_k', 'new_m_k_a', 'new_m_r_k', 'new_m_gn_w', 'new_m_gn_b', 'new_m_w_rwkv_up', 'new_m_w_out', 'new_m_norm_mlp', 'new_m_w_mlp_in', 'new_m_w_mlp_out', 'new_m_norm_ple', 'new_m_w_ple_gate', 'new_m_w_ple_proj', 'new_v_norm_mix', 'new_v_w_in', 'new_v_q_gain', 'new_v_k_gain', 'new_v_rel_bias', 'new_v_w_attn_up', 'new_v_shift_mix', 'new_v_w0', 'new_v_w_decay_up', 'new_v_a0', 'new_v_w_aaa_up', 'new_v_w_gate_up', 'new_v_k_k', 'new_v_k_a', 'new_v_r_k', 'new_v_gn_w', 'new_v_gn_b', 'new_v_w_rwkv_up', 'new_v_w_out', 'new_v_norm_mlp', 'new_v_w_mlp_in', 'new_v_w_mlp_out', 'new_v_norm_ple', 'new_v_w_ple_gate', 'new_v_w_ple_proj']
TWIN_LEAF_KINDS = {'loss': 'loss', 'grad_x': 'grad_x', 'grad_norm_mix': 'grad_w', 'grad_w_in': 'grad_w', 'grad_q_gain': 'grad_w', 'grad_k_gain': 'grad_w', 'grad_rel_bias': 'grad_w', 'grad_w_attn_up': 'grad_w', 'grad_shift_mix': 'grad_w', 'grad_w0': 'grad_w', 'grad_w_decay_up': 'grad_w', 'grad_a0': 'grad_w', 'grad_w_aaa_up': 'grad_w', 'grad_w_gate_up': 'grad_w', 'grad_k_k': 'grad_w', 'grad_k_a': 'grad_w', 'grad_r_k': 'grad_w', 'grad_gn_w': 'grad_w', 'grad_gn_b': 'grad_w', 'grad_w_rwkv_up': 'grad_w', 'grad_w_out': 'grad_w', 'grad_norm_mlp': 'grad_w', 'grad_w_mlp_in': 'grad_w', 'grad_w_mlp_out': 'grad_w', 'grad_norm_ple': 'grad_w', 'grad_w_ple_gate': 'grad_w', 'grad_w_ple_proj': 'grad_w', 'delta_norm_mix': 'delta_w', 'delta_w_in': 'delta_w', 'delta_q_gain': 'delta_w', 'delta_k_gain': 'delta_w', 'delta_rel_bias': 'delta_w', 'delta_w_attn_up': 'delta_w', 'delta_shift_mix': 'delta_w', 'delta_w0': 'delta_w', 'delta_w_decay_up': 'delta_w', 'delta_a0': 'delta_w', 'delta_w_aaa_up': 'delta_w', 'delta_w_gate_up': 'delta_w', 'delta_k_k': 'delta_w', 'delta_k_a': 'delta_w', 'delta_r_k': 'delta_w', 'delta_gn_w': 'delta_w', 'delta_gn_b': 'delta_w', 'delta_w_rwkv_up': 'delta_w', 'delta_w_out': 'delta_w', 'delta_norm_mlp': 'delta_w', 'delta_w_mlp_in': 'delta_w', 'delta_w_mlp_out': 'delta_w', 'delta_norm_ple': 'delta_w', 'delta_w_ple_gate': 'delta_w', 'delta_w_ple_proj': 'delta_w', 'new_m_norm_mix': 'new_m', 'new_m_w_in': 'new_m', 'new_m_q_gain': 'new_m', 'new_m_k_gain': 'new_m', 'new_m_rel_bias': 'new_m', 'new_m_w_attn_up': 'new_m', 'new_m_shift_mix': 'new_m', 'new_m_w0': 'new_m', 'new_m_w_decay_up': 'new_m', 'new_m_a0': 'new_m', 'new_m_w_aaa_up': 'new_m', 'new_m_w_gate_up': 'new_m', 'new_m_k_k': 'new_m', 'new_m_k_a': 'new_m', 'new_m_r_k': 'new_m', 'new_m_gn_w': 'new_m', 'new_m_gn_b': 'new_m', 'new_m_w_rwkv_up': 'new_m', 'new_m_w_out': 'new_m', 'new_m_norm_mlp': 'new_m', 'new_m_w_mlp_in': 'new_m', 'new_m_w_mlp_out': 'new_m', 'new_m_norm_ple': 'new_m', 'new_m_w_ple_gate': 'new_m', 'new_m_w_ple_proj': 'new_m', 'new_v_norm_mix': 'new_v', 'new_v_w_in': 'new_v', 'new_v_q_gain': 'new_v', 'new_v_k_gain': 'new_v', 'new_v_rel_bias': 'new_v', 'new_v_w_attn_up': 'new_v', 'new_v_shift_mix': 'new_v', 'new_v_w0': 'new_v', 'new_v_w_decay_up': 'new_v', 'new_v_a0': 'new_v', 'new_v_w_aaa_up': 'new_v', 'new_v_w_gate_up': 'new_v', 'new_v_k_k': 'new_v', 'new_v_k_a': 'new_v', 'new_v_r_k': 'new_v', 'new_v_gn_w': 'new_v', 'new_v_gn_b': 'new_v', 'new_v_w_rwkv_up': 'new_v', 'new_v_w_out': 'new_v', 'new_v_norm_mlp': 'new_v', 'new_v_w_mlp_in': 'new_v', 'new_v_w_mlp_out': 'new_v', 'new_v_norm_ple': 'new_v', 'new_v_w_ple_gate': 'new_v', 'new_v_w_ple_proj': 'new_v'}


def _forward(args):
    return _fwd_reference(*[args[k] for k in FWD_PARAMS])


def _output_shape():
    out = _jax.eval_shape(lambda: _forward(_fwd_setup_inputs(0)))
    return out.shape, out.dtype

N_MICROBATCH = 1
ADAM_LR = 0.001
ADAM_B1 = 0.9
ADAM_B2 = 0.999
ADAM_EPS = 1e-08
ADAM_WD = 0.01
ADAM_STEP = 10
PER_EXAMPLE_BATCH_AXIS = {'x': 0, 'p': 1, 'loss_target': 0}
SHARED_INPUTS = []
_WEIGHT_DTYPES = {'norm_mix': _jnp.float32, 'w_in': _jnp.float32, 'q_gain': _jnp.float32, 'k_gain': _jnp.float32, 'rel_bias': _jnp.float32, 'w_attn_up': _jnp.float32, 'shift_mix': _jnp.float32, 'w0': _jnp.float32, 'w_decay_up': _jnp.float32, 'a0': _jnp.float32, 'w_aaa_up': _jnp.float32, 'w_gate_up': _jnp.float32, 'k_k': _jnp.float32, 'k_a': _jnp.float32, 'r_k': _jnp.float32, 'gn_w': _jnp.float32, 'gn_b': _jnp.float32, 'w_rwkv_up': _jnp.float32, 'w_out': _jnp.float32, 'norm_mlp': _jnp.float32, 'w_mlp_in': _jnp.float32, 'w_mlp_out': _jnp.float32, 'norm_ple': _jnp.float32, 'w_ple_gate': _jnp.float32, 'w_ple_proj': _jnp.float32}
MOMENT_SCALE = {'norm_mix': 1.949971e-01, 'w_in': 5.441426e-02, 'q_gain': 4.294222e-01, 'k_gain': 4.258463e-01, 'rel_bias': 1.200235e-01, 'w_attn_up': 2.849561e-02, 'shift_mix': 5.048243e-01, 'w0': 3.887514e-02, 'w_decay_up': 5.488588e-03, 'a0': 5.992899e-02, 'w_aaa_up': 2.895352e-02, 'w_gate_up': 1.183862e+00, 'k_k': 7.089285e-02, 'k_a': 1.375847e-01, 'r_k': 9.410018e-01, 'gn_w': 2.057641e+00, 'gn_b': 2.725229e+00, 'w_rwkv_up': 6.651354e-02, 'w_out': 6.460476e-02, 'norm_mlp': 2.385019e+01, 'w_mlp_in': 1.092155e-01, 'w_mlp_out': 1.904678e+00, 'norm_ple': 3.433751e-01, 'w_ple_gate': 2.358914e-01, 'w_ple_proj': 1.060242e-01}


def _to_microbatches(a, axis):
    t = _jnp.moveaxis(a, axis, 0)
    t = t.reshape((N_MICROBATCH, t.shape[0] // N_MICROBATCH) + t.shape[1:])
    return _jnp.moveaxis(t, 1, axis + 1)


def setup_inputs(seed: int = 0) -> dict:
    inp = _fwd_setup_inputs(seed)
    key = _jax.random.fold_in(_jax.random.key(seed), 7919)
    shape, _ = _output_shape()
    out = dict(inp)
    out["loss_target"] = _jax.random.normal(_jax.random.fold_in(key, 0), shape, _jnp.float32)
    for i, name in enumerate(TWIN_WEIGHTS):
        w = inp[name].astype(_jnp.float32)
        if MOMENT_SCALE is None:
            s = _jnp.sqrt(_jnp.mean(_jnp.square(w)) + 1e-30)
        else:
            s = MOMENT_SCALE[name]
        km, kv = _jax.random.split(_jax.random.fold_in(key, i + 1))
        out[name] = w
        out["m_" + name] = s * _jax.random.normal(km, w.shape, _jnp.float32)
        out["v_" + name] = (s * s) * _jax.random.uniform(kv, w.shape, _jnp.float32, 0.5, 1.5)
    if N_MICROBATCH > 1:
        for name, axis in PER_EXAMPLE_BATCH_AXIS.items():
            out[name] = _to_microbatches(out[name], axis)
    return {'x': out['x'], 'p': out['p'], 'norm_mix': out['norm_mix'], 'w_in': out['w_in'], 'q_gain': out['q_gain'], 'k_gain': out['k_gain'], 'rel_bias': out['rel_bias'], 'w_attn_up': out['w_attn_up'], 'shift_mix': out['shift_mix'], 'w0': out['w0'], 'w_decay_up': out['w_decay_up'], 'a0': out['a0'], 'w_aaa_up': out['w_aaa_up'], 'w_gate_up': out['w_gate_up'], 'k_k': out['k_k'], 'k_a': out['k_a'], 'r_k': out['r_k'], 'gn_w': out['gn_w'], 'gn_b': out['gn_b'], 'w_rwkv_up': out['w_rwkv_up'], 'w_out': out['w_out'], 'norm_mlp': out['norm_mlp'], 'w_mlp_in': out['w_mlp_in'], 'w_mlp_out': out['w_mlp_out'], 'norm_ple': out['norm_ple'], 'w_ple_gate': out['w_ple_gate'], 'w_ple_proj': out['w_ple_proj'], 'loss_target': out['loss_target'], 'm_norm_mix': out['m_norm_mix'], 'm_w_in': out['m_w_in'], 'm_q_gain': out['m_q_gain'], 'm_k_gain': out['m_k_gain'], 'm_rel_bias': out['m_rel_bias'], 'm_w_attn_up': out['m_w_attn_up'], 'm_shift_mix': out['m_shift_mix'], 'm_w0': out['m_w0'], 'm_w_decay_up': out['m_w_decay_up'], 'm_a0': out['m_a0'], 'm_w_aaa_up': out['m_w_aaa_up'], 'm_w_gate_up': out['m_w_gate_up'], 'm_k_k': out['m_k_k'], 'm_k_a': out['m_k_a'], 'm_r_k': out['m_r_k'], 'm_gn_w': out['m_gn_w'], 'm_gn_b': out['m_gn_b'], 'm_w_rwkv_up': out['m_w_rwkv_up'], 'm_w_out': out['m_w_out'], 'm_norm_mlp': out['m_norm_mlp'], 'm_w_mlp_in': out['m_w_mlp_in'], 'm_w_mlp_out': out['m_w_mlp_out'], 'm_norm_ple': out['m_norm_ple'], 'm_w_ple_gate': out['m_w_ple_gate'], 'm_w_ple_proj': out['m_w_ple_proj'], 'v_norm_mix': out['v_norm_mix'], 'v_w_in': out['v_w_in'], 'v_q_gain': out['v_q_gain'], 'v_k_gain': out['v_k_gain'], 'v_rel_bias': out['v_rel_bias'], 'v_w_attn_up': out['v_w_attn_up'], 'v_shift_mix': out['v_shift_mix'], 'v_w0': out['v_w0'], 'v_w_decay_up': out['v_w_decay_up'], 'v_a0': out['v_a0'], 'v_w_aaa_up': out['v_w_aaa_up'], 'v_w_gate_up': out['v_w_gate_up'], 'v_k_k': out['v_k_k'], 'v_k_a': out['v_k_a'], 'v_r_k': out['v_r_k'], 'v_gn_w': out['v_gn_w'], 'v_gn_b': out['v_gn_b'], 'v_w_rwkv_up': out['v_w_rwkv_up'], 'v_w_out': out['v_w_out'], 'v_norm_mlp': out['v_norm_mlp'], 'v_w_mlp_in': out['v_w_mlp_in'], 'v_w_mlp_out': out['v_w_mlp_out'], 'v_norm_ple': out['v_norm_ple'], 'v_w_ple_gate': out['v_w_ple_gate'], 'v_w_ple_proj': out['v_w_ple_proj']}


def _loss(weights, diff, rest, loss_target):
    with _jax.named_scope("forward"):
        args = {**rest, TWIN_DIFF_INPUT: diff, **{k: w.astype(_WEIGHT_DTYPES[k]) for k, w in weights.items()}}
        y = _forward(args)
    with _jax.named_scope("loss_head"):
        err = _jnp.square(y.astype(_jnp.float32) - loss_target)
        return 0.5 * _jnp.sum(_jnp.mean(err, axis=-1)) if err.ndim else 0.5 * err


def _adamw(w, g, m, v):
    m = ADAM_B1 * m + (1.0 - ADAM_B1) * g
    v = ADAM_B2 * v + (1.0 - ADAM_B2) * _jnp.square(g)
    m_hat = m / (1.0 - ADAM_B1 ** ADAM_STEP)
    v_hat = v / (1.0 - ADAM_B2 ** ADAM_STEP)
    delta = -ADAM_LR * (m_hat / (_jnp.sqrt(v_hat) + ADAM_EPS) + ADAM_WD * w)
    return delta, m, v


def reference(x, p, norm_mix, w_in, q_gain, k_gain, rel_bias, w_attn_up, shift_mix, w0, w_decay_up, a0, w_aaa_up, w_gate_up, k_k, k_a, r_k, gn_w, gn_b, w_rwkv_up, w_out, norm_mlp, w_mlp_in, w_mlp_out, norm_ple, w_ple_gate, w_ple_proj, loss_target, m_norm_mix, m_w_in, m_q_gain, m_k_gain, m_rel_bias, m_w_attn_up, m_shift_mix, m_w0, m_w_decay_up, m_a0, m_w_aaa_up, m_w_gate_up, m_k_k, m_k_a, m_r_k, m_gn_w, m_gn_b, m_w_rwkv_up, m_w_out, m_norm_mlp, m_w_mlp_in, m_w_mlp_out, m_norm_ple, m_w_ple_gate, m_w_ple_proj, v_norm_mix, v_w_in, v_q_gain, v_k_gain, v_rel_bias, v_w_attn_up, v_shift_mix, v_w0, v_w_decay_up, v_a0, v_w_aaa_up, v_w_gate_up, v_k_k, v_k_a, v_r_k, v_gn_w, v_gn_b, v_w_rwkv_up, v_w_out, v_norm_mlp, v_w_mlp_in, v_w_mlp_out, v_norm_ple, v_w_ple_gate, v_w_ple_proj):
    given = dict(x=x, p=p, norm_mix=norm_mix, w_in=w_in, q_gain=q_gain, k_gain=k_gain, rel_bias=rel_bias, w_attn_up=w_attn_up, shift_mix=shift_mix, w0=w0, w_decay_up=w_decay_up, a0=a0, w_aaa_up=w_aaa_up, w_gate_up=w_gate_up, k_k=k_k, k_a=k_a, r_k=r_k, gn_w=gn_w, gn_b=gn_b, w_rwkv_up=w_rwkv_up, w_out=w_out, norm_mlp=norm_mlp, w_mlp_in=w_mlp_in, w_mlp_out=w_mlp_out, norm_ple=norm_ple, w_ple_gate=w_ple_gate, w_ple_proj=w_ple_proj, loss_target=loss_target, m_norm_mix=m_norm_mix, m_w_in=m_w_in, m_q_gain=m_q_gain, m_k_gain=m_k_gain, m_rel_bias=m_rel_bias, m_w_attn_up=m_w_attn_up, m_shift_mix=m_shift_mix, m_w0=m_w0, m_w_decay_up=m_w_decay_up, m_a0=m_a0, m_w_aaa_up=m_w_aaa_up, m_w_gate_up=m_w_gate_up, m_k_k=m_k_k, m_k_a=m_k_a, m_r_k=m_r_k, m_gn_w=m_gn_w, m_gn_b=m_gn_b, m_w_rwkv_up=m_w_rwkv_up, m_w_out=m_w_out, m_norm_mlp=m_norm_mlp, m_w_mlp_in=m_w_mlp_in, m_w_mlp_out=m_w_mlp_out, m_norm_ple=m_norm_ple, m_w_ple_gate=m_w_ple_gate, m_w_ple_proj=m_w_ple_proj, v_norm_mix=v_norm_mix, v_w_in=v_w_in, v_q_gain=v_q_gain, v_k_gain=v_k_gain, v_rel_bias=v_rel_bias, v_w_attn_up=v_w_attn_up, v_shift_mix=v_shift_mix, v_w0=v_w0, v_w_decay_up=v_w_decay_up, v_a0=v_a0, v_w_aaa_up=v_w_aaa_up, v_w_gate_up=v_w_gate_up, v_k_k=v_k_k, v_k_a=v_k_a, v_r_k=v_r_k, v_gn_w=v_gn_w, v_gn_b=v_gn_b, v_w_rwkv_up=v_w_rwkv_up, v_w_out=v_w_out, v_norm_mlp=v_norm_mlp, v_w_mlp_in=v_w_mlp_in, v_w_mlp_out=v_w_mlp_out, v_norm_ple=v_norm_ple, v_w_ple_gate=v_w_ple_gate, v_w_ple_proj=v_w_ple_proj)
    weights = {n: given[n] for n in TWIN_WEIGHTS}
    shared = {n: given[n] for n in SHARED_INPUTS}
    per_example = {n: given[n] for n in ['x', 'p']}
    grad_fn = _jax.value_and_grad(_loss, argnums=(0, 1))

    def one_microbatch(ex, loss_target):
        ex = dict(ex)
        diff = ex.pop(TWIN_DIFF_INPUT)
        return grad_fn(weights, diff, {**shared, **ex}, loss_target)

    if N_MICROBATCH == 1:
        loss, (grad_w, grad_x) = one_microbatch(per_example, given["loss_target"])
    else:
        def body(carry, xs):
            loss_sum, grad_sum = carry
            l_k, (gw_k, gx_k) = one_microbatch(xs[0], xs[1])
            with _jax.named_scope("update"):
                return (loss_sum + l_k, _jax.tree.map(_jnp.add, grad_sum, gw_k)), gx_k

        init = (_jnp.zeros((), _jnp.float32), _jax.tree.map(_jnp.zeros_like, weights))
        (loss, grad_w), grad_x = _jax.lax.scan(body, init, (per_example, given["loss_target"]))
    with _jax.named_scope("update"):
        delta_w, new_m, new_v = {}, {}, {}
        for n in TWIN_WEIGHTS:
            delta_w[n], new_m[n], new_v[n] = _adamw(weights[n], grad_w[n], given["m_" + n], given["v_" + n])
    return (loss, grad_x, *[grad_w[n] for n in TWIN_WEIGHTS], *[delta_w[n] for n in TWIN_WEIGHTS],
            *[new_m[n] for n in TWIN_WEIGHTS], *[new_v[n] for n in TWIN_WEIGHTS])
```

```python
import functools
import math

import jax
import jax.numpy as jnp
from jax import lax
from jax.experimental import pallas as pl
from jax.experimental.pallas import tpu as pltpu

F32 = jnp.float32
BF16 = jnp.bfloat16
I32 = jnp.int32

N_DEV = 8
MESH_AXES = ("x", "y", "c")
LANES = 128
PACK_ROWS = 16
VMEM_LIMIT_BYTES = 48 * 2**20
ROW_BLOCK_BYTES = 3 * 2**20

HEAD_DIM = 128
ATTN_BLOCK = 128
HEADS_PER_GROUP = 4
DILATED_GROUPS = ((128, 1), (512, 4), (2048, 16))
N_HEADS = HEADS_PER_GROUP * len(DILATED_GROUPS)
ATTN_WIDTH = N_HEADS * HEAD_DIM
ATTN_OUT = HEADS_PER_GROUP * HEAD_DIM
N_BUCKETS = 32
MAX_DISTANCE = 2048
RWKV_HEAD = 64
RWKV_CHUNK = 64
RMS_EPS = 1e-6
GN_EPS = 64e-5
NEG_INF = -1e30

ADAM_LR = 0.001
ADAM_B1 = 0.9
ADAM_B2 = 0.999
ADAM_EPS = 1e-08
ADAM_WD = 0.01
ADAM_STEP = 10

WEIGHTS = ['norm_mix', 'w_in', 'q_gain', 'k_gain', 'rel_bias', 'w_attn_up', 'shift_mix', 'w0', 'w_decay_up', 'a0',
           'w_aaa_up', 'w_gate_up', 'k_k', 'k_a', 'r_k', 'gn_w', 'gn_b', 'w_rwkv_up', 'w_out', 'norm_mlp', 'w_mlp_in',
           'w_mlp_out', 'norm_ple', 'w_ple_gate', 'w_ple_proj']
SHARD_AXIS = {'w_in': 1, 'w_attn_up': 1, 'w_decay_up': 1, 'w_aaa_up': 1, 'w_gate_up': 1, 'w_rwkv_up': 1, 'w_out': 0,
              'w_mlp_in': 1, 'w_mlp_out': 0, 'w_ple_gate': 0, 'w_ple_proj': 1}
BIG = [n for n in WEIGHTS if n in SHARD_AXIS]
SMALL = [n for n in WEIGHTS if n not in SHARD_AXIS]


def _params(sem):
    return pltpu.CompilerParams(dimension_semantics=sem, vmem_limit_bytes=VMEM_LIMIT_BYTES)


_DN = {'nn': (((1,), (0,)), ((), ())), 'nt': (((1,), (1,)), ((), ())), 'tn': (((0,), (0,)), ((), ()))}


def _dot(a, b, mode, exact):
    if exact:
        return lax.dot_general(a, b, _DN[mode], precision=lax.Precision.HIGHEST, preferred_element_type=F32)
    return lax.dot_general(a.astype(BF16), b.astype(BF16), _DN[mode], preferred_element_type=F32)


@functools.partial(jax.custom_vjp, nondiff_argnums=(2, 3))
def _dot_ad(a, b, mode, exact):
    return _dot(a, b, mode, exact)


def _dot_ad_fwd(a, b, mode, exact):
    return _dot(a, b, mode, exact), (a, b)


def _dot_ad_bwd(mode, exact, res, g):
    a, b = res
    if mode == 'nn':
        return _dot(g, b, 'nt', exact), _dot(a, g, 'tn', exact)
    if mode == 'nt':
        return _dot(g, b, 'nn', exact), _dot(g, a, 'tn', exact)
    return _dot(b, g, 'nt', exact), _dot(a, g, 'nn', exact)


_dot_ad.defvjp(_dot_ad_fwd, _dot_ad_bwd)


def _pick(n, cands):
    for c in cands:
        if n % c == 0:
            return c
    return n


def matmul(a, b, mode, out_dtype, name):
    assert a.dtype == BF16 and b.dtype == BF16, (name, a.dtype, b.dtype)
    if mode == 'tn':
        k, m = a.shape
    else:
        m, k = a.shape
    n = b.shape[0] if mode == 'nt' else b.shape[1]
    assert (b.shape[1] if mode == 'nt' else b.shape[0]) == k, (name, a.shape, b.shape)
    tm = _pick(m, (1024, 512, 256, 128))
    tn = _pick(n, (1024, 512, 256, 128))
    tk = _pick(k, (512, 256, 128))
    nk = k // tk

    def body(a_ref, b_ref, o_ref, acc_ref):
        kk = pl.program_id(2)

        @pl.when(kk == 0)
        def _():
            acc_ref[...] = jnp.zeros_like(acc_ref)

        acc_ref[...] += lax.dot_general(a_ref[...], b_ref[...], _DN[mode], preferred_element_type=F32)

        @pl.when(kk == nk - 1)
        def _():
            o_ref[...] = acc_ref[...].astype(o_ref.dtype)

    if mode == 'tn':
        a_spec = pl.BlockSpec((tk, tm), lambda i, j, kk: (kk, i))
    else:
        a_spec = pl.BlockSpec((tm, tk), lambda i, j, kk: (i, kk))
    if mode == 'nt':
        b_spec = pl.BlockSpec((tn, tk), lambda i, j, kk: (j, kk))
    else:
        b_spec = pl.BlockSpec((tk, tn), lambda i, j, kk: (kk, j))
    return pl.pallas_call(
        body, grid=(m // tm, n // tn, nk), in_specs=[a_spec, b_spec],
        out_specs=pl.BlockSpec((tm, tn), lambda i, j, kk: (i, j)),
        out_shape=jax.ShapeDtypeStruct((m, n), out_dtype),
        scratch_shapes=[pltpu.VMEM((tm, tn), F32)],
        compiler_params=_params(("parallel", "parallel", "arbitrary")), name=name,
    )(a, b)


def _row_bytes(shape, dtype):
    dims = list(shape[1:])
    dims[-1] = -(-dims[-1] // LANES) * LANES
    return math.prod(dims) * jnp.dtype(dtype).itemsize


def _row_tile(n, row_bytes):
    t = 1024
    while t > 16 and (n % t or t * row_bytes > ROW_BLOCK_BYTES):
        t //= 2
    assert n % t == 0, (n, t)
    return t


def rowmap(fn, tiled, bcast, out_tiled, out_acc, name):
    n = tiled[0].shape[0]
    tile = _row_tile(n, sum(_row_bytes(t.shape, t.dtype) for t in list(tiled) + list(out_tiled)))
    n_in, n_out = len(tiled) + len(bcast), len(out_tiled)

    def body(*refs):
        outs, accs = fn(*[r[...] for r in refs[:n_in]])
        assert len(outs) == n_out and len(accs) == len(out_acc), name
        for r, v in zip(refs[n_in:n_in + n_out], outs):
            r[...] = v.astype(r.dtype)
        acc_refs = refs[n_in + n_out:]
        if acc_refs:
            @pl.when(pl.program_id(0) == 0)
            def _():
                for r, v in zip(acc_refs, accs):
                    r[...] = v.astype(r.dtype)

            @pl.when(pl.program_id(0) != 0)
            def _():
                for r, v in zip(acc_refs, accs):
                    r[...] += v.astype(r.dtype)

    def tspec(s):
        nd = len(s.shape)
        return pl.BlockSpec((tile,) + tuple(s.shape[1:]), lambda i, nd=nd: (i,) + (0,) * (nd - 1))

    def bspec(s):
        nd = len(s.shape)
        return pl.BlockSpec(tuple(s.shape), lambda i, nd=nd: (0,) * nd)

    res = pl.pallas_call(
        body, grid=(n // tile,),
        in_specs=[tspec(t) for t in tiled] + [bspec(t) for t in bcast],
        out_specs=[tspec(t) for t in out_tiled] + [bspec(t) for t in out_acc],
        out_shape=list(out_tiled) + list(out_acc),
        compiler_params=_params(("arbitrary",)), name=name,
    )(*tiled, *bcast)
    return list(res[:n_out]), list(res[n_out:])


def rowmap_fwd(fwd, tiled, bcast, out_dtypes, name):
    shapes = jax.eval_shape(fwd, *tiled, *bcast)
    out_tiled = [jax.ShapeDtypeStruct(s.shape, d) for s, d in zip(shapes, out_dtypes)]
    outs, _ = rowmap(lambda *blk: (fwd(*[b.astype(F32) for b in blk]), ()), tiled, bcast, out_tiled, [], name)
    return outs


def rowmap_bwd(fwd, tiled, bcast, cts, want, name):
    cts = [[] if c is None else (list(c) if isinstance(c, (list, tuple)) else [c]) for c in cts]
    flat_cts = [c for group in cts for c in group]
    nt_, nc_ = len(tiled), len(flat_cts)

    def fn(*blk):
        ins = [b.astype(F32) for b in blk[:nt_]] + [b.astype(F32) for b in blk[nt_ + nc_:]]
        ctb = list(blk[nt_:nt_ + nc_])
        outs, vjp = jax.vjp(fwd, *ins)
        full = []
        for o, group in zip(outs, cts):
            acc = jnp.zeros_like(o)
            for _ in group:
                acc = acc + ctb.pop(0).astype(F32)
            full.append(acc)
        g = vjp(tuple(full))
        return [g[i] for i in range(nt_) if want[i] is not None], list(g[nt_:])

    out_tiled = [jax.ShapeDtypeStruct(t.shape, w) for t, w in zip(tiled, want) if w is not None]
    out_acc = [jax.ShapeDtypeStruct(b.shape, F32) for b in bcast]
    return rowmap(fn, list(tiled) + flat_cts, bcast, out_tiled, out_acc, name)


def _rms(x, gain):
    return x * lax.rsqrt(jnp.mean(jnp.square(x), axis=-1, keepdims=True) + RMS_EPS) * gain


def _sigmoid(x):
    return 1.0 / (1.0 + jnp.exp(-x))


def _softplus(x):
    return jnp.maximum(x, 0.0) + jnp.log(1.0 + jnp.exp(-jnp.abs(x)))


def st_norm(x, gain):
    return x, _rms(x, gain)


def st_res_norm(x, delta, gain):
    y = x + delta
    return y, _rms(y, gain)


def st_qk_norm(q, k, q_gain, k_gain):
    return _rms(q, q_gain), _rms(k, k_gain)


def st_merge(o0, o1, o2, l0, l1, l2):
    m = jnp.maximum(jnp.maximum(l0, l1), l2)
    e0, e1, e2 = jnp.exp(l0 - m), jnp.exp(l1 - m), jnp.exp(l2 - m)
    return ((e0 * o0 + e1 * o1 + e2 * o2) / (e0 + e1 + e2),)


def _st_rwkv_pre(dot, zr, zk, zv, xw, xa, xg, pr, pk, pv, pw, pa, pg, mr, mk, mv, mw, ma, mg,
                 w0, w_decay, a0, w_aaa, w_gate, k_k, k_a):
    def shift(cur, prev, mix):
        return cur + mix * (prev - cur)

    r, k, v = shift(zr, pr, mr), shift(zk, pk, mk), shift(zv, pv, mv)
    xw, xa, xg = shift(xw, pw, mw), shift(xa, pa, ma), shift(xg, pg, mg)
    w = -_softplus(-(w0 + dot(jnp.tanh(xw), w_decay, 'nn', False))) - 0.5
    a = _sigmoid(a0 + dot(xa, w_aaa, 'nn', False))
    g = dot(_sigmoid(xg), w_gate, 'nn', False)
    log_decay = -jnp.exp(w)
    return r, log_decay, k * (1.0 + (a - 1.0) * k_a), v, k * k_k, a, g


def st_rwkv_kk(kk0, a):
    kk = kk0 / jnp.maximum(jnp.sqrt(jnp.sum(jnp.square(kk0), axis=-1, keepdims=True)), 1e-12)
    return -kk, kk * a


def st_rwkv_post(y, r, k, v, g, gn_w, gn_b, r_k):
    mu = jnp.mean(y, axis=-1, keepdims=True)
    var = jnp.mean(jnp.square(y - mu), axis=-1, keepdims=True)
    out = (y - mu) * lax.rsqrt(var + GN_EPS) * gn_w + gn_b
    out = out + jnp.sum(r * k * r_k, axis=-1, keepdims=True) * v
    return (out * g,)


def st_gate(g0, g1, attn_d, rwkv_d):
    return (_sigmoid(g0) * attn_d + _sigmoid(g1) * rwkv_d,)


def st_relu2(u):
    return (jnp.square(jnp.maximum(u, 0.0)),)


def st_add(a, b):
    return (a + b,)


def loss_head(x2, pg, pp, target, name):
    d_model = x2.shape[1]

    def fn(x2, pg, pp, tgt):
        s = _sigmoid(pg)
        err = x2 + s * pp - tgt
        dy = err * (1.0 / d_model)
        part = 0.5 * jnp.sum(jnp.square(err)) * (1.0 / d_model)
        return [dy, dy * pp * s * (1.0 - s), dy * s], [jnp.full((8, LANES), part, F32)]

    sds = jax.ShapeDtypeStruct
    outs, accs = rowmap(fn, [x2, pg, pp, target], [],
                        [sds(x2.shape, F32), sds(x2.shape, BF16), sds(x2.shape, BF16)], [sds((8, LANES), F32)], name)
    return outs[0], outs[1], outs[2], accs[0][0, 0]


def _attn_block(dot, q, kp, kc, vp, vc, bp, bc, prev_offset):
    blk = q.shape[0]
    qi = lax.broadcasted_iota(I32, (blk, blk), 0)
    ki = lax.broadcasted_iota(I32, (blk, blk), 1)
    mask_c = ki <= qi
    mask_p = ki >= qi + prev_offset
    scale = HEAD_DIM ** -0.5
    s_c = jnp.where(mask_c, dot(q, kc, 'nt', False) * scale + bc, NEG_INF)
    s_p = jnp.where(mask_p, dot(q, kp, 'nt', False) * scale + bp, NEG_INF)
    m = lax.stop_gradient(jnp.maximum(jnp.max(s_c, axis=1, keepdims=True), jnp.max(s_p, axis=1, keepdims=True)))
    e_c = jnp.where(mask_c, jnp.exp(s_c - m), 0.0)
    e_p = jnp.where(mask_p, jnp.exp(s_p - m), 0.0)
    l = jnp.sum(e_c, axis=1, keepdims=True) + jnp.sum(e_p, axis=1, keepdims=True)
    o = (dot(e_c, vc, 'nn', False) + dot(e_p, vp, 'nn', False)) / l
    return o, jnp.broadcast_to(m + jnp.log(l), o.shape)


def _attn_specs(blk, hd):
    cur = pl.BlockSpec((None, None, blk, hd), lambda h, r, n: (h, r, n, 0))
    prev = pl.BlockSpec((None, None, blk, hd), lambda h, r, n: (h, r, jnp.maximum(n - 1, 0), 0))
    bias = pl.BlockSpec((None, blk, blk), lambda h, r, n: (h, 0, 0))
    return cur, prev, bias


def attn_fwd(q, k, v, bp, bc, name):
    hg, d, length, hd = q.shape
    blk = ATTN_BLOCK
    cur, prev, bias = _attn_specs(blk, hd)

    def body(q_ref, kp_ref, kc_ref, vp_ref, vc_ref, bp_ref, bc_ref, o_ref, l_ref):
        off = jnp.where(pl.program_id(2) > 0, 0, blk)
        o, l = _attn_block(_dot, q_ref[...], kp_ref[...], kc_ref[...], vp_ref[...], vc_ref[...], bp_ref[...],
                           bc_ref[...], off)
        o_ref[...] = o
        l_ref[...] = l

    return pl.pallas_call(
        body, grid=(hg, d, length // blk), in_specs=[cur, prev, cur, prev, cur, bias, bias], out_specs=[cur, cur],
        out_shape=[jax.ShapeDtypeStruct(q.shape, F32)] * 2,
        compiler_params=_params(("parallel", "parallel", "parallel")), name=name,
    )(q, k, k, v, v, bp, bc)


def attn_bwd(q, k, v, bp, bc, do, dl, name):
    hg, d, length, hd = q.shape
    blk = ATTN_BLOCK
    nb = length // blk
    cur, prev, bias = _attn_specs(blk, hd)

    def body(q_ref, kp_ref, kc_ref, vp_ref, vc_ref, bp_ref, bc_ref, do_ref, dl_ref,
             dq_ref, dkp_ref, dkc_ref, dvp_ref, dvc_ref, dbp_ref, dbc_ref):
        off = jnp.where(pl.program_id(2) > 0, 0, blk)
        f = functools.partial(_attn_block, _dot_ad, prev_offset=off)
        _, vjp = jax.vjp(f, q_ref[...], kp_ref[...], kc_ref[...], vp_ref[...], vc_ref[...], bp_ref[...], bc_ref[...])
        dq, dkp, dkc, dvp, dvc, dbp, dbc = vjp((do_ref[...], dl_ref[...]))
        dq_ref[...] = dq
        dkp_ref[...] = dkp
        dkc_ref[...] = dkc
        dvp_ref[...] = dvp
        dvc_ref[...] = dvc
        first = jnp.logical_and(pl.program_id(1) == 0, pl.program_id(2) == 0)

        @pl.when(first)
        def _():
            dbp_ref[...] = dbp
            dbc_ref[...] = dbc

        @pl.when(jnp.logical_not(first))
        def _():
            dbp_ref[...] += dbp
            dbc_ref[...] += dbc

    blocks = jax.ShapeDtypeStruct(q.shape, F32)
    dq, dkp, dkc, dvp, dvc, dbp, dbc = pl.pallas_call(
        body, grid=(hg, d, nb), in_specs=[cur, prev, cur, prev, cur, bias, bias, cur, cur],
        out_specs=[cur] * 5 + [bias] * 2, out_shape=[blocks] * 5 + [jax.ShapeDtypeStruct(bp.shape, F32)] * 2,
        compiler_params=_params(("arbitrary", "arbitrary", "arbitrary")), name=name,
    )(q, k, k, v, v, bp, bc, do, dl)

    nxt = pl.BlockSpec((None, None, blk, hd), lambda h, r, n: (h, r, jnp.minimum(n + 1, nb - 1), 0))

    def add_body(kc_ref, kp_ref, vc_ref, vp_ref, dk_ref, dv_ref):
        has_next = (pl.program_id(2) + 1 < nb).astype(F32)
        dk_ref[...] = kc_ref[...] + kp_ref[...] * has_next
        dv_ref[...] = vc_ref[...] + vp_ref[...] * has_next

    dk, dv = pl.pallas_call(
        add_body, grid=(hg, d, nb), in_specs=[cur, nxt, cur, nxt], out_specs=[cur, cur], out_shape=[blocks] * 2,
        compiler_params=_params(("parallel", "parallel", "parallel")), name=name + "_kv",
    )(dkc, dkp, dvc, dvp)
    return dq, dk, dv, dbp, dbc


def _t5_bucket(dist):
    max_exact = N_BUCKETS // 2
    d_f = jnp.maximum(dist, 1).astype(F32)
    large = max_exact + (jnp.log(d_f / max_exact) / math.log(MAX_DISTANCE / max_exact)
                         * (N_BUCKETS - max_exact)).astype(I32)
    large = jnp.minimum(large, N_BUCKETS - 1)
    return jnp.where(dist < max_exact, dist, large)


def _bucket_tables():
    blk = ATTN_BLOCK
    qi = jnp.arange(blk)[:, None]
    ki = jnp.arange(blk)[None, :]
    out = []
    for _, dilation in DILATED_GROUPS:
        rel_p = jnp.maximum(blk + qi - ki, 0) * dilation
        rel_c = jnp.maximum(qi - ki, 0) * dilation
        out.append(jnp.stack([_t5_bucket(rel_p), _t5_bucket(rel_c)]))
    return jnp.stack(out).astype(I32)


def bias_fwd(table, buckets, name):
    blk = ATTN_BLOCK

    def body(tab_ref, bkt_ref, out_ref):
        for g in range(len(DILATED_GROUPS)):
            for half in range(2):
                bk = bkt_ref[g, half]
                for hh in range(HEADS_PER_GROUP):
                    h = g * HEADS_PER_GROUP + hh
                    acc = jnp.zeros((blk, blk), F32)
                    for b in range(N_BUCKETS):
                        acc = jnp.where(bk == b, tab_ref[b, h], acc)
                    out_ref[h, half] = acc

    return pl.pallas_call(
        body, in_specs=[pl.BlockSpec(memory_space=pltpu.SMEM), pl.BlockSpec(memory_space=pltpu.VMEM)],
        out_specs=pl.BlockSpec(memory_space=pltpu.VMEM),
        out_shape=jax.ShapeDtypeStruct((N_HEADS, 2, blk, blk), F32), name=name,
    )(table, buckets)


def bias_bwd(dbias, buckets, name):
    def body(db_ref, bkt_ref, out_ref):
        rows = lax.broadcasted_iota(I32, (N_BUCKETS, LANES), 0)
        cols = lax.broadcasted_iota(I32, (N_BUCKETS, LANES), 1)
        acc = jnp.zeros((N_BUCKETS, LANES), F32)
        for g in range(len(DILATED_GROUPS)):
            bk_p, bk_c = bkt_ref[g, 0], bkt_ref[g, 1]
            for hh in range(HEADS_PER_GROUP):
                h = g * HEADS_PER_GROUP + hh
                d_p, d_c = db_ref[h, 0], db_ref[h, 1]
                for b in range(N_BUCKETS):
                    s = jnp.sum(jnp.where(bk_p == b, d_p, 0.0)) + jnp.sum(jnp.where(bk_c == b, d_c, 0.0))
                    acc = jnp.where(jnp.logical_and(rows == b, cols == h), s, acc)
        out_ref[...] = acc

    return pl.pallas_call(
        body, in_specs=[pl.BlockSpec(memory_space=pltpu.VMEM)] * 2, out_specs=pl.BlockSpec(memory_space=pltpu.VMEM),
        out_shape=jax.ShapeDtypeStruct((N_BUCKETS, LANES), F32), name=name,
    )(dbias, buckets)


def _rwkv_chunk(dot, s0, r, lw, k, v, a, b):
    c = r.shape[0]
    ti = lax.broadcasted_iota(I32, (c, c), 0)
    si = lax.broadcasted_iota(I32, (c, c), 1)
    incl = si <= ti
    strict = si < ti
    cum = dot(incl.astype(F32), lw, 'nn', True)
    w_incl = jnp.exp(cum)
    w_prev = jnp.exp(cum - lw)
    w_inv = jnp.exp(-cum)
    w_end = jnp.exp(jnp.sum(lw, axis=0, keepdims=True))
    a_t, r_t, b_t, k_t = a * w_prev, r * w_incl, b * w_inv, k * w_inv
    l_ab = jnp.where(strict, dot(a_t, b_t, 'nt', True), 0.0)
    l_ak = jnp.where(strict, dot(a_t, k_t, 'nt', True), 0.0)
    u = dot(a_t, s0, 'nt', True) + dot(l_ak, v, 'nn', True)
    u = u + dot(l_ab, u, 'nn', True)
    power = l_ab
    for _ in range(int(math.log2(c)) - 1):
        power = dot(power, power, 'nn', True)
        u = u + dot(power, u, 'nn', True)
    m_rb = jnp.where(incl, dot(r_t, b_t, 'nt', True), 0.0)
    m_rk = jnp.where(incl, dot(r_t, k_t, 'nt', True), 0.0)
    y = dot(r_t, s0, 'nt', True) + dot(m_rb, u, 'nn', True) + dot(m_rk, v, 'nn', True)
    s1 = (s0 + dot(u, b_t, 'tn', True) + dot(v, k_t, 'tn', True)) * w_end
    return y, s1


def rwkv_fwd(r, lw, k, v, a, b, name):
    h, t, n = r.shape
    c = RWKV_CHUNK
    nc = t // c
    row = pl.BlockSpec((None, c, n), lambda i, j: (i, j, 0))

    def body(r_ref, lw_ref, k_ref, v_ref, a_ref, b_ref, y_ref, s0_ref, state):
        @pl.when(pl.program_id(1) == 0)
        def _():
            state[...] = jnp.zeros_like(state)

        s0 = state[...]
        s0_ref[...] = s0
        y, s1 = _rwkv_chunk(_dot, s0, r_ref[...], lw_ref[...], k_ref[...], v_ref[...], a_ref[...], b_ref[...])
        y_ref[...] = y
        state[...] = s1

    return pl.pallas_call(
        body, grid=(h, nc), in_specs=[row] * 6,
        out_specs=[row, pl.BlockSpec((None, None, n, n), lambda i, j: (i, j, 0, 0))],
        out_shape=[jax.ShapeDtypeStruct((h, t, n), F32), jax.ShapeDtypeStruct((h, nc, n, n), F32)],
        scratch_shapes=[pltpu.VMEM((n, n), F32)],
        compiler_params=_params(("parallel", "arbitrary")), name=name,
    )(r, lw, k, v, a, b)


def rwkv_bwd(r, lw, k, v, a, b, s0, dy, name):
    h, t, n = r.shape
    c = RWKV_CHUNK
    nc = t // c
    row = pl.BlockSpec((None, c, n), lambda i, j: (i, nc - 1 - j, 0))
    st = pl.BlockSpec((None, None, n, n), lambda i, j: (i, nc - 1 - j, 0, 0))

    def body(r_ref, lw_ref, k_ref, v_ref, a_ref, b_ref, s0_ref, dy_ref, dr_ref, dlw_ref, dk_ref, dv_ref, da_ref,
             db_ref, dstate):
        @pl.when(pl.program_id(1) == 0)
        def _():
            dstate[...] = jnp.zeros_like(dstate)

        _, vjp = jax.vjp(functools.partial(_rwkv_chunk, _dot_ad), s0_ref[...], r_ref[...], lw_ref[...], k_ref[...],
                         v_ref[...], a_ref[...], b_ref[...])
        ds0, dr, dlw, dk, dv, da, db = vjp((dy_ref[...], dstate[...]))
        dr_ref[...] = dr
        dlw_ref[...] = dlw
        dk_ref[...] = dk
        dv_ref[...] = dv
        da_ref[...] = da
        db_ref[...] = db
        dstate[...] = ds0

    return pl.pallas_call(
        body, grid=(h, nc), in_specs=[row] * 6 + [st, row], out_specs=[row] * 6,
        out_shape=[jax.ShapeDtypeStruct((h, t, n), F32)] * 6, scratch_shapes=[pltpu.VMEM((n, n), F32)],
        compiler_params=_params(("parallel", "arbitrary")), name=name,
    )(r, lw, k, v, a, b, s0, dy)


_ANY = pl.BlockSpec(memory_space=pl.ANY)


def all_gather(shard, name):
    def body(x_ref, out_ref, send_sems, recv_sems, local_sem):
        x, y, c = lax.axis_index("x"), lax.axis_index("y"), lax.axis_index("c")
        me, sibling = (x, y, c), (x, y, 1 - c)
        chips = [(1 - x, y), (x, 1 - y), (1 - x, 1 - y)]

        def slot(px, py, pc):
            return out_ref.at[4 * px + 2 * py + pc]

        def copy(i, block, to, src=None):
            return pltpu.make_async_remote_copy(
                src_ref=slot(*block) if src is None else src, dst_ref=slot(*block), send_sem=send_sems.at[i],
                recv_sem=recv_sems.at[i], device_id=to, device_id_type=pl.DeviceIdType.MESH)

        mine = pltpu.make_async_copy(x_ref, slot(*me), local_sem)
        mine.start()
        first = [copy(0, me, sibling, src=x_ref)]
        first += [copy(1 + j, me, (*chip, c), src=x_ref) for j, chip in enumerate(chips)]
        for cp in first:
            cp.start()
        passed = [copy(4 + j, (*chip, c), sibling) for j, chip in enumerate(chips)]
        for j, chip in enumerate(chips):
            copy(1 + j, (*chip, c), me).wait_recv()
            passed[j].start()
        copy(0, sibling, me).wait_recv()
        for j, chip in enumerate(chips):
            copy(4 + j, (*chip, 1 - c), me).wait_recv()
        for cp in first + passed:
            cp.wait_send()
        mine.wait()

    return pl.pallas_call(
        body, in_specs=[_ANY], out_specs=_ANY, out_shape=jax.ShapeDtypeStruct((N_DEV,) + shard.shape, shard.dtype),
        scratch_shapes=[pltpu.SemaphoreType.DMA((7,)), pltpu.SemaphoreType.DMA((7,)), pltpu.SemaphoreType.DMA],
        name=name,
    )(shard)


def reduce_scatter_exchange(parts, name):
    def body(in_ref, out_ref, send_sems, recv_sems, local_sem):
        x, y, c = lax.axis_index("x"), lax.axis_index("y"), lax.axis_index("c")
        me = 4 * x + 2 * y + c
        mine = pltpu.make_async_copy(in_ref.at[me], out_ref.at[me], local_sem)
        mine.start()
        copies = []
        for i in range(1, N_DEV):
            px, py, pc = x ^ (i >> 2), y ^ ((i >> 1) & 1), c ^ (i & 1)
            copies.append(pltpu.make_async_remote_copy(
                src_ref=in_ref.at[4 * px + 2 * py + pc], dst_ref=out_ref.at[me], send_sem=send_sems.at[i - 1],
                recv_sem=recv_sems.at[i - 1], device_id=(px, py, pc), device_id_type=pl.DeviceIdType.MESH))
        for cp in copies:
            cp.start()
        for cp in copies:
            cp.wait_recv()
        for cp in copies:
            cp.wait_send()
        mine.wait()

    return pl.pallas_call(
        body, in_specs=[_ANY], out_specs=_ANY, out_shape=jax.ShapeDtypeStruct(parts.shape, parts.dtype),
        scratch_shapes=[pltpu.SemaphoreType.DMA((7,)), pltpu.SemaphoreType.DMA((7,)), pltpu.SemaphoreType.DMA],
        name=name,
    )(parts)


def adamw(parts, w, m, v, name):
    rows = w.shape[0]
    tile = _row_tile(rows, N_DEV * LANES * jnp.dtype(parts.dtype).itemsize + 7 * LANES * 4)

    def body(p_ref, w_ref, m_ref, v_ref, g_ref, d_ref, nm_ref, nv_ref):
        g = p_ref[0].astype(F32)
        for j in range(1, N_DEV):
            g = g + p_ref[j].astype(F32)
        new_m = ADAM_B1 * m_ref[...] + (1.0 - ADAM_B1) * g
        new_v = ADAM_B2 * v_ref[...] + (1.0 - ADAM_B2) * jnp.square(g)
        m_hat = new_m / (1.0 - ADAM_B1 ** ADAM_STEP)
        v_hat = new_v / (1.0 - ADAM_B2 ** ADAM_STEP)
        g_ref[...] = g
        d_ref[...] = -ADAM_LR * (m_hat / (jnp.sqrt(v_hat) + ADAM_EPS) + ADAM_WD * w_ref[...])
        nm_ref[...] = new_m
        nv_ref[...] = new_v

    flat = pl.BlockSpec((tile, LANES), lambda i: (i, 0))
    return pl.pallas_call(
        body, grid=(rows // tile,), in_specs=[pl.BlockSpec((N_DEV, tile, LANES), lambda i: (0, i, 0))] + [flat] * 3,
        out_specs=[flat] * 4, out_shape=[jax.ShapeDtypeStruct(w.shape, F32)] * 4,
        compiler_params=_params(("parallel",)), name=name,
    )(parts, w, m, v)


def _part_rows(n_elems):
    return -(-n_elems // (PACK_ROWS * LANES)) * PACK_ROWS


def _pack(arrays, dtype, lead=()):
    parts, layout, off = [], [], 0
    for arr in arrays:
        n = math.prod(arr.shape[len(lead):])
        rows = _part_rows(n)
        flat = arr.reshape(lead + (n,)).astype(dtype)
        flat = jnp.pad(flat, [(0, 0)] * len(lead) + [(0, rows * LANES - n)])
        parts.append(flat.reshape(lead + (rows, LANES)))
        layout.append((off, rows))
        off += rows
    total = -(-off // 1024) * 1024
    if total > off:
        parts.append(jnp.zeros(lead + (total - off, LANES), dtype))
    return jnp.concatenate(parts, axis=len(lead)), layout


def _unpack(buf, layout, shapes, lead=()):
    out = []
    for (off, rows), shape in zip(layout, shapes):
        n = math.prod(shape)
        piece = lax.slice_in_dim(buf, off, off + rows, axis=len(lead))
        out.append(piece.reshape(lead + (rows * LANES,))[..., :n].reshape(lead + tuple(shape)))
    return out


def _split_shards(full, axis):
    if axis == 0:
        return full.reshape((N_DEV, full.shape[0] // N_DEV, full.shape[1]))
    return full.reshape((full.shape[0], N_DEV, full.shape[1] // N_DEV)).transpose(1, 0, 2)


def _join_shards(shards, axis):
    if axis == 0:
        return shards.reshape((-1, shards.shape[2]))
    return shards.transpose(1, 0, 2).reshape((shards.shape[1], -1))


def _shift_down(t):
    return jnp.pad(t, ((1, 0), (0, 0)))[:-1]


def _shift_up(t):
    return jnp.pad(t, ((0, 1), (0, 0)))[1:]


def kernel(x, p, norm_mix, w_in, q_gain, k_gain, rel_bias, w_attn_up, shift_mix, w0, w_decay_up, a0, w_aaa_up, w_gate_up, k_k, k_a, r_k, gn_w, gn_b, w_rwkv_up, w_out, norm_mlp, w_mlp_in, w_mlp_out, norm_ple, w_ple_gate, w_ple_proj, loss_target, m_norm_mix, m_w_in, m_q_gain, m_k_gain, m_rel_bias, m_w_attn_up, m_shift_mix, m_w0, m_w_decay_up, m_a0, m_w_aaa_up, m_w_gate_up, m_k_k, m_k_a, m_r_k, m_gn_w, m_gn_b, m_w_rwkv_up, m_w_out, m_norm_mlp, m_w_mlp_in, m_w_mlp_out, m_norm_ple, m_w_ple_gate, m_w_ple_proj, v_norm_mix, v_w_in, v_q_gain, v_k_gain, v_rel_bias, v_w_attn_up, v_shift_mix, v_w0, v_w_decay_up, v_a0, v_w_aaa_up, v_w_gate_up, v_k_k, v_k_a, v_r_k, v_gn_w, v_gn_b, v_w_rwkv_up, v_w_out, v_norm_mlp, v_w_mlp_in, v_w_mlp_out, v_norm_ple, v_w_ple_gate, v_w_ple_proj):
    given = dict(locals())
    sds = jax.ShapeDtypeStruct
    xs = x[0]
    t_len, d_model = xs.shape
    target = loss_target[0]
    p_bf = p[0, 0].astype(BF16)
    rw_width = w0.shape[1]
    n_rheads = rw_width // RWKV_HEAD
    lora_d, lora_a, lora_g = w_decay_up.shape[1], w_aaa_up.shape[1], w_gate_up.shape[1]
    z_width = shift_mix.shape[1]
    z_pad = -(-z_width // LANES) * LANES
    qkv_width = 3 * ATTN_WIDTH
    assert z_width == 3 * rw_width + lora_d + lora_a + lora_g
    assert N_DEV * w_in.shape[2] == qkv_width + z_width + 2 * d_model
    for window, dilation in DILATED_GROUPS:
        assert window // dilation == ATTN_BLOCK and t_len % (dilation * ATTN_BLOCK) == 0

    shard_shapes = [given[n].shape[1:] for n in BIG]
    w_buf, big_layout = _pack([given[n][0] for n in BIG], BF16)
    gathered = _unpack(all_gather(w_buf, "gather_weights"), big_layout, shard_shapes, lead=(N_DEV,))
    full = {n: _join_shards(g, SHARD_AXIS[n]) for n, g in zip(BIG, gathered)}
    w_qkv = full['w_in'][:, :qkv_width]
    w_z = jnp.pad(full['w_in'][:, qkv_width:qkv_width + z_width], ((0, 0), (0, z_pad - z_width)))
    w_g = full['w_in'][:, qkv_width + z_width:]

    (h_in,) = rowmap_fwd(lambda a, g: st_norm(a, g)[1:], [xs], [norm_mix], [BF16], "norm_in")
    qkv = matmul(h_in, w_qkv, 'nn', F32, "proj_qkv")
    z = matmul(h_in, w_z, 'nn', F32, "proj_z")
    gates = matmul(h_in, w_g, 'nn', F32, "proj_gates")
    gate_a, gate_r = gates[:, :d_model], gates[:, d_model:]

    q_raw = qkv[:, :ATTN_WIDTH].reshape(t_len * N_HEADS, HEAD_DIM)
    k_raw = qkv[:, ATTN_WIDTH:2 * ATTN_WIDTH].reshape(t_len * N_HEADS, HEAD_DIM)
    v_att = qkv[:, 2 * ATTN_WIDTH:]
    q_n, k_n = rowmap_fwd(st_qk_norm, [q_raw, k_raw], [q_gain, k_gain], [F32, F32], "qk_norm")
    buckets = _bucket_tables()
    bias = bias_fwd(rel_bias, buckets, "attn_bias")

    def to_blocks(tok, g):
        dil = DILATED_GROUPS[g][1]
        tok = tok.reshape(t_len, N_HEADS, HEAD_DIM)[:, g * HEADS_PER_GROUP:(g + 1) * HEADS_PER_GROUP]
        return tok.reshape(t_len // dil, dil, HEADS_PER_GROUP, HEAD_DIM).transpose(2, 1, 0, 3)

    def from_blocks(blk):
        return blk.transpose(2, 1, 0, 3).reshape(t_len, ATTN_OUT)

    def group_tokens(tok, g):
        return tok.reshape(t_len, N_HEADS * HEAD_DIM)[:, g * ATTN_OUT:(g + 1) * ATTN_OUT]

    att_in, att_o, att_l = [], [], []
    for g in range(len(DILATED_GROUPS)):
        hs = slice(g * HEADS_PER_GROUP, (g + 1) * HEADS_PER_GROUP)
        ops = (to_blocks(q_n, g), to_blocks(k_n, g), to_blocks(v_att, g), bias[hs, 0], bias[hs, 1])
        o_g, l_g = attn_fwd(*ops, name=f"attn_fwd_{g}")
        att_in.append(ops)
        att_o.append(from_blocks(o_g))
        att_l.append(from_blocks(l_g))
    (attn,) = rowmap_fwd(st_merge, att_o + att_l, [], [BF16], "attn_merge")
    attn_d = matmul(attn, full['w_attn_up'], 'nn', F32, "attn_up")

    c0 = rw_width
    cuts = [0, c0, 2 * c0, 3 * c0, 3 * c0 + lora_d, 3 * c0 + lora_d + lora_a, z_width]
    z_parts = [z[:, lo:hi] for lo, hi in zip(cuts[:-1], cuts[1:])]
    z_prev = [_shift_down(t) for t in z_parts]
    mixes = [shift_mix[:, lo:hi] for lo, hi in zip(cuts[:-1], cuts[1:])]
    pre_params = mixes + [w0, full['w_decay_up'], a0, full['w_aaa_up'], full['w_gate_up'], k_k, k_a]
    pre_out = rowmap_fwd(functools.partial(_st_rwkv_pre, _dot), z_parts + z_prev, pre_params, [F32] * 7, "rwkv_pre")
    r_s, lw_s, k_s, v_s, kk0_s, a_s, g_s = pre_out

    def heads(tok):
        return tok.reshape(t_len, n_rheads, RWKV_HEAD)

    def head_major(tok):
        return heads(tok).transpose(1, 0, 2)

    aa_s, bb_s = rowmap_fwd(st_rwkv_kk, [heads(kk0_s), heads(a_s)], [], [F32, F32], "rwkv_kk")
    scan_in = [head_major(r_s), head_major(lw_s), head_major(k_s), head_major(v_s),
               aa_s.transpose(1, 0, 2), bb_s.transpose(1, 0, 2)]
    y_h, s0_h = rwkv_fwd(*scan_in, name="rwkv_scan")
    y_s = y_h.transpose(1, 0, 2)
    post_params = [gn_w.reshape(1, n_rheads, RWKV_HEAD), gn_b.reshape(1, n_rheads, RWKV_HEAD), r_k]
    post_in = [y_s, heads(r_s), heads(k_s), heads(v_s), heads(g_s)]
    (rw,) = rowmap_fwd(st_rwkv_post, post_in, post_params, [BF16], "rwkv_post")
    rw = rw.reshape(t_len, rw_width)
    rwkv_d = matmul(rw, full['w_rwkv_up'], 'nn', F32, "rwkv_up")

    (merged,) = rowmap_fwd(st_gate, [gate_a, gate_r, attn_d, rwkv_d], [], [BF16], "gate_merge")
    mix_out = matmul(merged, full['w_out'], 'nn', F32, "out_proj")
    x1, h_mlp = rowmap_fwd(st_res_norm, [xs, mix_out], [norm_mlp], [F32, BF16], "res_norm_mlp")
    u = matmul(h_mlp, full['w_mlp_in'], 'nn', F32, "mlp_in")
    (act,) = rowmap_fwd(st_relu2, [u], [], [BF16], "mlp_act")
    mlp_out = matmul(act, full['w_mlp_out'], 'nn', F32, "mlp_out")
    x2, h_ple = rowmap_fwd(st_res_norm, [x1, mlp_out], [norm_ple], [F32, BF16], "res_norm_ple")
    pg = matmul(h_ple, full['w_ple_gate'], 'nn', F32, "ple_gate")
    pp = matmul(p_bf, full['w_ple_proj'], 'nn', F32, "ple_proj")
    dy, d_pg, d_pp, loss_local = loss_head(x2, pg, pp, target, "loss_head")

    grads = {}
    grads['w_ple_gate'] = matmul(h_ple, d_pg, 'tn', BF16, "d_w_ple_gate")
    grads['w_ple_proj'] = matmul(p_bf, d_pp, 'tn', BF16, "d_w_ple_proj")
    d_h_ple = matmul(d_pg, full['w_ple_gate'], 'nt', F32, "d_h_ple")
    (d_x2, d_x2_bf), (grads['norm_ple'],) = rowmap_bwd(
        st_res_norm, [x1, mlp_out], [norm_ple], [dy, d_h_ple], [F32, BF16], "res_norm_ple_bwd")
    grads['w_mlp_out'] = matmul(act, d_x2_bf, 'tn', BF16, "d_w_mlp_out")
    d_act = matmul(d_x2_bf, full['w_mlp_out'], 'nt', F32, "d_act")
    (d_u,), _ = rowmap_bwd(st_relu2, [u], [], [d_act], [BF16], "mlp_act_bwd")
    grads['w_mlp_in'] = matmul(h_mlp, d_u, 'tn', BF16, "d_w_mlp_in")
    d_h_mlp = matmul(d_u, full['w_mlp_in'], 'nt', F32, "d_h_mlp")
    (d_x1, d_x1_bf), (grads['norm_mlp'],) = rowmap_bwd(
        st_res_norm, [xs, mix_out], [norm_mlp], [d_x2, d_h_mlp], [F32, BF16], "res_norm_mlp_bwd")

    grads['w_out'] = matmul(merged, d_x1_bf, 'tn', BF16, "d_w_out")
    d_merged = matmul(d_x1_bf, full['w_out'], 'nt', F32, "d_merged")
    (d_gate_a, d_gate_r, d_attn_d, d_rwkv_d), _ = rowmap_bwd(
        st_gate, [gate_a, gate_r, attn_d, rwkv_d], [], [d_merged], [BF16] * 4, "gate_merge_bwd")
    grads['w_attn_up'] = matmul(attn, d_attn_d, 'tn', BF16, "d_w_attn_up")
    grads['w_rwkv_up'] = matmul(rw, d_rwkv_d, 'tn', BF16, "d_w_rwkv_up")
    d_attn = matmul(d_attn_d, full['w_attn_up'], 'nt', F32, "d_attn")
    d_rw = matmul(d_rwkv_d, full['w_rwkv_up'], 'nt', F32, "d_rw")

    (d_y, d_r1, d_k1, d_v1, d_g), (d_gn_w, d_gn_b, grads['r_k']) = rowmap_bwd(
        st_rwkv_post, post_in, post_params, [heads(d_rw)], [F32] * 5, "rwkv_post_bwd")
    grads['gn_w'], grads['gn_b'] = d_gn_w.reshape(1, rw_width), d_gn_b.reshape(1, rw_width)
    scan_grads = rwkv_bwd(*scan_in, s0_h, d_y.transpose(1, 0, 2), name="rwkv_scan_bwd")
    d_r2, d_lw, d_k2, d_v2, d_aa, d_bb = [t.transpose(1, 0, 2) for t in scan_grads]
    (d_kk0, d_a), _ = rowmap_bwd(st_rwkv_kk, [heads(kk0_s), heads(a_s)], [], [d_aa, d_bb], [F32, F32], "rwkv_kk_bwd")

    def flat(tok):
        return tok.reshape(t_len, rw_width)

    pre_cts = [[flat(d_r1), flat(d_r2)], flat(d_lw), [flat(d_k1), flat(d_k2)], [flat(d_v1), flat(d_v2)],
               flat(d_kk0), flat(d_a), flat(d_g)]
    d_zp, d_pre = rowmap_bwd(functools.partial(_st_rwkv_pre, _dot_ad), z_parts + z_prev, pre_params, pre_cts,
                             [F32] * 12, "rwkv_pre_bwd")
    grads['shift_mix'] = jnp.concatenate(d_pre[:6], axis=1)
    (grads['w0'], grads['w_decay_up'], grads['a0'], grads['w_aaa_up'], grads['w_gate_up'], grads['k_k'],
     grads['k_a']) = d_pre[6:]
    z_fill = [jnp.zeros((t_len, z_pad - z_width), F32)] if z_pad > z_width else []
    d_z_cur = jnp.concatenate(d_zp[:6] + z_fill, axis=1)
    d_z_prev = _shift_up(jnp.concatenate(d_zp[6:] + z_fill, axis=1))
    (d_z,) = rowmap_fwd(st_add, [d_z_cur, d_z_prev], [], [BF16], "d_z_sum")

    d_merge, _ = rowmap_bwd(st_merge, att_o + att_l, [], [d_attn], [F32] * 6, "attn_merge_bwd")
    d_qn, d_kn, d_vs, d_bias = [], [], [], []
    for g in range(len(DILATED_GROUPS)):
        tok_g = functools.partial(to_blocks_grad, t_len=t_len, dil=DILATED_GROUPS[g][1])
        dq, dk, dv, dbp, dbc = attn_bwd(*att_in[g], tok_g(d_merge[g]), tok_g(d_merge[3 + g]), name=f"attn_bwd_{g}")
        d_qn.append(from_blocks(dq))
        d_kn.append(from_blocks(dk))
        d_vs.append(from_blocks(dv))
        d_bias.append(jnp.stack([dbp, dbc], axis=1))
    d_table = bias_bwd(jnp.concatenate(d_bias, axis=0), buckets, "attn_bias_bwd")
    grads['rel_bias'] = d_table[:, :N_HEADS]
    d_qn = jnp.concatenate(d_qn, axis=1).reshape(t_len * N_HEADS, HEAD_DIM)
    d_kn = jnp.concatenate(d_kn, axis=1).reshape(t_len * N_HEADS, HEAD_DIM)
    (d_q, d_k), (grads['q_gain'], grads['k_gain']) = rowmap_bwd(
        st_qk_norm, [q_raw, k_raw], [q_gain, k_gain], [d_qn, d_kn], [BF16, BF16], "qk_norm_bwd")
    d_qkv = jnp.concatenate([d_q.reshape(t_len, ATTN_WIDTH), d_k.reshape(t_len, ATTN_WIDTH)]
                            + [t.astype(BF16) for t in d_vs], axis=1)

    d_gates = jnp.concatenate([d_gate_a, d_gate_r], axis=1)
    grads['w_in'] = jnp.concatenate([
        matmul(h_in, d_qkv, 'tn', BF16, "d_w_qkv"),
        matmul(h_in, d_z, 'tn', BF16, "d_w_z")[:, :z_width],
        matmul(h_in, d_gates, 'tn', BF16, "d_w_gates")], axis=1)
    d_h_in = [matmul(d_qkv, w_qkv, 'nt', F32, "d_h_qkv"), matmul(d_z, w_z, 'nt', F32, "d_h_z"),
              matmul(d_gates, w_g, 'nt', F32, "d_h_gates")]
    (grad_x,), (grads['norm_mix'],) = rowmap_bwd(st_norm, [xs], [norm_mix], [d_x1, d_h_in], [F32], "norm_in_bwd")

    send, _ = _pack([_split_shards(grads[n], SHARD_AXIS[n]) for n in BIG], BF16, lead=(N_DEV,))
    received = reduce_scatter_exchange(send, "scatter_grads")
    pk = lambda prefix, names: _pack([given[prefix + n] for n in names], F32)[0]
    big_out = adamw(received, pk('', BIG), pk('m_', BIG), pk('v_', BIG), "adamw_sharded")
    small_buf, small_layout = _pack([grads[n].reshape(given[n].shape) for n in SMALL], F32)
    small_all = all_gather(small_buf, "gather_small_grads")
    small_out = adamw(small_all, pk('', SMALL), pk('m_', SMALL), pk('v_', SMALL), "adamw_replicated")

    results = []
    for big_buf, small_b in zip(big_out, small_out):
        by_name = dict(zip(BIG, _unpack(big_buf, big_layout, [given[n].shape for n in BIG])))
        by_name.update(zip(SMALL, _unpack(small_b, small_layout, [given[n].shape for n in SMALL])))
        results.append([by_name[n] for n in WEIGHTS])
    loss = lax.psum(loss_local, MESH_AXES)
    return (loss, grad_x[None], *results[0], *results[1], *results[2], *results[3])


def to_blocks_grad(tok, t_len, dil):
    return tok.reshape(t_len // dil, dil, HEADS_PER_GROUP, HEAD_DIM).transpose(2, 1, 0, 3)
```

```python
import functools
import math

import jax
import jax.numpy as jnp
from jax import lax
from jax.experimental import pallas as pl
from jax.experimental.pallas import tpu as pltpu

F32 = jnp.float32
BF16 = jnp.bfloat16
I32 = jnp.int32

N_DEV = 8
MESH_AXES = ("x", "y", "c")
LANES = 128
PACK_ROWS = 16
VMEM_LIMIT_BYTES = 48 * 2**20
ROW_BLOCK_BYTES = 3 * 2**20

HEAD_DIM = 128
ATTN_BLOCK = 128
HEADS_PER_GROUP = 4
DILATED_GROUPS = ((128, 1), (512, 4), (2048, 16))
N_HEADS = HEADS_PER_GROUP * len(DILATED_GROUPS)
ATTN_WIDTH = N_HEADS * HEAD_DIM
ATTN_OUT = HEADS_PER_GROUP * HEAD_DIM
N_BUCKETS = 32
MAX_DISTANCE = 2048
RWKV_HEAD = 64
RWKV_CHUNK = 64
RWKV_HEADS_PER_STEP = 4
RMS_EPS = 1e-6
GN_EPS = 64e-5
NEG_INF = -1e30

ADAM_LR = 0.001
ADAM_B1 = 0.9
ADAM_B2 = 0.999
ADAM_EPS = 1e-08
ADAM_WD = 0.01
ADAM_STEP = 10

WEIGHTS = ['norm_mix', 'w_in', 'q_gain', 'k_gain', 'rel_bias', 'w_attn_up', 'shift_mix', 'w0', 'w_decay_up', 'a0',
           'w_aaa_up', 'w_gate_up', 'k_k', 'k_a', 'r_k', 'gn_w', 'gn_b', 'w_rwkv_up', 'w_out', 'norm_mlp', 'w_mlp_in',
           'w_mlp_out', 'norm_ple', 'w_ple_gate', 'w_ple_proj']
SHARD_AXIS = {'w_in': 1, 'w_attn_up': 1, 'w_decay_up': 1, 'w_aaa_up': 1, 'w_gate_up': 1, 'w_rwkv_up': 1, 'w_out': 0,
              'w_mlp_in': 1, 'w_mlp_out': 0, 'w_ple_gate': 0, 'w_ple_proj': 1}
BIG = [n for n in WEIGHTS if n in SHARD_AXIS]
SMALL = [n for n in WEIGHTS if n not in SHARD_AXIS]


def _params(sem):
    return pltpu.CompilerParams(dimension_semantics=sem, vmem_limit_bytes=VMEM_LIMIT_BYTES)


_DN = {'nn': (((1,), (0,)), ((), ())), 'nt': (((1,), (1,)), ((), ())), 'tn': (((0,), (0,)), ((), ()))}


def _dot(a, b, mode, exact):
    if exact:
        return lax.dot_general(a, b, _DN[mode], precision=lax.Precision.HIGH, preferred_element_type=F32)
    return lax.dot_general(a.astype(BF16), b.astype(BF16), _DN[mode], preferred_element_type=F32)


@functools.partial(jax.custom_vjp, nondiff_argnums=(2, 3))
def _dot_ad(a, b, mode, exact):
    return _dot(a, b, mode, exact)


def _dot_ad_fwd(a, b, mode, exact):
    return _dot(a, b, mode, exact), (a, b)


def _dot_ad_bwd(mode, exact, res, g):
    a, b = res
    if mode == 'nn':
        return _dot(g, b, 'nt', exact), _dot(a, g, 'tn', exact)
    if mode == 'nt':
        return _dot(g, b, 'nn', exact), _dot(g, a, 'tn', exact)
    return _dot(b, g, 'nt', exact), _dot(a, g, 'nn', exact)


_dot_ad.defvjp(_dot_ad_fwd, _dot_ad_bwd)


def _pick(n, cands):
    for c in cands:
        if n % c == 0:
            return c
    return n


def matmul(a, b, mode, out_dtype, name, b_shards=False, out_shards=False):
    assert a.dtype == BF16 and b.dtype == BF16, (name, a.dtype, b.dtype)
    if mode == 'tn':
        k, m = a.shape
    else:
        m, k = a.shape
    b_rows, b_cols = (b.shape[1], N_DEV * b.shape[2]) if b_shards else b.shape
    n, kb = (b_rows, b_cols) if mode == 'nt' else (b_cols, b_rows)
    assert kb == k, (name, a.shape, b.shape)
    tm = _pick(m, (1024, 512, 256, 128))
    tn = _pick(n // N_DEV if (out_shards or (b_shards and mode != 'nt')) else n, (1024, 512, 256, 128))
    tk = _pick(k // N_DEV if (b_shards and mode == 'nt') else k, (512, 256, 128))
    nk = k // tk

    def body(a_ref, b_ref, o_ref, acc_ref):
        kk = pl.program_id(2)

        @pl.when(kk == 0)
        def _():
            acc_ref[...] = jnp.zeros_like(acc_ref)

        acc_ref[...] += lax.dot_general(a_ref[...], b_ref[...], _DN[mode], preferred_element_type=F32)

        @pl.when(kk == nk - 1)
        def _():
            o_ref[...] = acc_ref[...].astype(o_ref.dtype)

    if mode == 'tn':
        a_spec = pl.BlockSpec((tk, tm), lambda i, j, kk: (kk, i))
    else:
        a_spec = pl.BlockSpec((tm, tk), lambda i, j, kk: (i, kk))
    if mode == 'nt':
        if b_shards:
            per = b.shape[2] // tk
            b_spec = pl.BlockSpec((None, tn, tk), lambda i, j, kk: (kk // per, j, kk % per))
        else:
            b_spec = pl.BlockSpec((tn, tk), lambda i, j, kk: (j, kk))
    else:
        if b_shards:
            per = b.shape[2] // tn
            b_spec = pl.BlockSpec((None, tk, tn), lambda i, j, kk: (j // per, kk, j % per))
        else:
            b_spec = pl.BlockSpec((tk, tn), lambda i, j, kk: (kk, j))
    if out_shards:
        per_o = n // N_DEV // tn
        o_spec = pl.BlockSpec((None, tm, tn), lambda i, j, kk: (j // per_o, i, j % per_o))
        o_shape = jax.ShapeDtypeStruct((N_DEV, m, n // N_DEV), out_dtype)
    else:
        o_spec = pl.BlockSpec((tm, tn), lambda i, j, kk: (i, j))
        o_shape = jax.ShapeDtypeStruct((m, n), out_dtype)
    return pl.pallas_call(
        body, grid=(m // tm, n // tn, nk), in_specs=[a_spec, b_spec], out_specs=o_spec, out_shape=o_shape,
        scratch_shapes=[pltpu.VMEM((tm, tn), F32)],
        compiler_params=_params(("parallel", "parallel", "arbitrary")), name=name,
    )(a, b)


def _row_bytes(shape, dtype):
    dims = list(shape[1:])
    dims[-1] = -(-dims[-1] // LANES) * LANES
    return math.prod(dims) * jnp.dtype(dtype).itemsize


def _row_tile(n, row_bytes):
    t = 1024
    while t > 16 and (n % t or t * row_bytes > ROW_BLOCK_BYTES):
        t //= 2
    assert n % t == 0, (n, t)
    return t


def rowmap(fn, tiled, bcast, out_tiled, out_acc, name):
    n = tiled[0].shape[0]
    tile = _row_tile(n, sum(_row_bytes(t.shape, t.dtype) for t in list(tiled) + list(out_tiled)))
    n_in, n_out = len(tiled) + len(bcast), len(out_tiled)

    def body(*refs):
        outs, accs = fn(*[r[...] for r in refs[:n_in]])
        assert len(outs) == n_out and len(accs) == len(out_acc), name
        for r, v in zip(refs[n_in:n_in + n_out], outs):
            r[...] = v.astype(r.dtype)
        acc_refs = refs[n_in + n_out:]
        if acc_refs:
            @pl.when(pl.program_id(0) == 0)
            def _():
                for r, v in zip(acc_refs, accs):
                    r[...] = v.astype(r.dtype)

            @pl.when(pl.program_id(0) != 0)
            def _():
                for r, v in zip(acc_refs, accs):
                    r[...] += v.astype(r.dtype)

    def tspec(s):
        nd = len(s.shape)
        return pl.BlockSpec((tile,) + tuple(s.shape[1:]), lambda i, nd=nd: (i,) + (0,) * (nd - 1))

    def bspec(s):
        nd = len(s.shape)
        return pl.BlockSpec(tuple(s.shape), lambda i, nd=nd: (0,) * nd)

    res = pl.pallas_call(
        body, grid=(n // tile,),
        in_specs=[tspec(t) for t in tiled] + [bspec(t) for t in bcast],
        out_specs=[tspec(t) for t in out_tiled] + [bspec(t) for t in out_acc],
        out_shape=list(out_tiled) + list(out_acc),
        compiler_params=_params(("arbitrary",)), name=name,
    )(*tiled, *bcast)
    return list(res[:n_out]), list(res[n_out:])


def rowmap_fwd(fwd, tiled, bcast, out_dtypes, name):
    shapes = jax.eval_shape(fwd, *tiled, *bcast)
    out_tiled = [jax.ShapeDtypeStruct(s.shape, d) for s, d in zip(shapes, out_dtypes)]
    outs, _ = rowmap(lambda *blk: (fwd(*[b.astype(F32) for b in blk]), ()), tiled, bcast, out_tiled, [], name)
    return outs


def rowmap_bwd(fwd, tiled, bcast, cts, want, name):
    cts = [[] if c is None else (list(c) if isinstance(c, (list, tuple)) else [c]) for c in cts]
    flat_cts = [c for group in cts for c in group]
    nt_, nc_ = len(tiled), len(flat_cts)

    def fn(*blk):
        ins = [b.astype(F32) for b in blk[:nt_]] + [b.astype(F32) for b in blk[nt_ + nc_:]]
        ctb = list(blk[nt_:nt_ + nc_])
        outs, vjp = jax.vjp(fwd, *ins)
        full = []
        for o, group in zip(outs, cts):
            acc = jnp.zeros_like(o)
            for _ in group:
                acc = acc + ctb.pop(0).astype(F32)
            full.append(acc)
        g = vjp(tuple(full))
        return [g[i] for i in range(nt_) if want[i] is not None], list(g[nt_:])

    out_tiled = [jax.ShapeDtypeStruct(t.shape, w) for t, w in zip(tiled, want) if w is not None]
    out_acc = [jax.ShapeDtypeStruct(b.shape, F32) for b in bcast]
    return rowmap(fn, list(tiled) + flat_cts, bcast, out_tiled, out_acc, name)


def _rms(x, gain):
    return x * lax.rsqrt(jnp.mean(jnp.square(x), axis=-1, keepdims=True) + RMS_EPS) * gain


def _sigmoid(x):
    return 1.0 / (1.0 + jnp.exp(-x))


def _softplus(x):
    return jnp.maximum(x, 0.0) + jnp.log(1.0 + jnp.exp(-jnp.abs(x)))


def st_norm(x, gain):
    return x, _rms(x, gain)


def st_res_norm(x, delta, gain):
    y = x + delta
    return y, _rms(y, gain)


def st_qk_norm(q, k, q_gain, k_gain):
    return _rms(q, q_gain), _rms(k, k_gain)


def st_merge(o0, o1, o2, l0, l1, l2):
    m = jnp.maximum(jnp.maximum(l0, l1), l2)
    e0, e1, e2 = jnp.exp(l0 - m), jnp.exp(l1 - m), jnp.exp(l2 - m)
    return ((e0 * o0 + e1 * o1 + e2 * o2) / (e0 + e1 + e2),)


def _st_rwkv_pre(dot, zr, zk, zv, xw, xa, xg, pr, pk, pv, pw, pa, pg, mr, mk, mv, mw, ma, mg,
                 w0, w_decay, a0, w_aaa, w_gate, k_k, k_a):
    def shift(cur, prev, mix):
        return cur + mix * (prev - cur)

    r, k, v = shift(zr, pr, mr), shift(zk, pk, mk), shift(zv, pv, mv)
    xw, xa, xg = shift(xw, pw, mw), shift(xa, pa, ma), shift(xg, pg, mg)
    w = -_softplus(-(w0 + dot(jnp.tanh(xw), w_decay, 'nn', False))) - 0.5
    a = _sigmoid(a0 + dot(xa, w_aaa, 'nn', False))
    g = dot(_sigmoid(xg), w_gate, 'nn', False)
    log_decay = -jnp.exp(w)
    return r, log_decay, k * (1.0 + (a - 1.0) * k_a), v, k * k_k, a, g


def st_rwkv_kk(kk0, a):
    kk = kk0 / jnp.maximum(jnp.sqrt(jnp.sum(jnp.square(kk0), axis=-1, keepdims=True)), 1e-12)
    return -kk, kk * a


def st_rwkv_post(y, r, k, v, g, gn_w, gn_b, r_k):
    mu = jnp.mean(y, axis=-1, keepdims=True)
    var = jnp.mean(jnp.square(y - mu), axis=-1, keepdims=True)
    out = (y - mu) * lax.rsqrt(var + GN_EPS) * gn_w + gn_b
    out = out + jnp.sum(r * k * r_k, axis=-1, keepdims=True) * v
    return (out * g,)


def st_gate(g0, g1, attn_d, rwkv_d):
    return (_sigmoid(g0) * attn_d + _sigmoid(g1) * rwkv_d,)


def st_relu2(u):
    return (jnp.square(jnp.maximum(u, 0.0)),)


def st_add(a, b):
    return (a + b,)


def loss_head(x2, pg, pp, target, name):
    d_model = x2.shape[1]

    def fn(x2, pg, pp, tgt):
        s = _sigmoid(pg)
        err = x2 + s * pp - tgt
        dy = err * (1.0 / d_model)
        part = 0.5 * jnp.sum(jnp.square(err)) * (1.0 / d_model)
        return [dy, dy * pp * s * (1.0 - s), dy * s], [jnp.full((8, LANES), part, F32)]

    sds = jax.ShapeDtypeStruct
    outs, accs = rowmap(fn, [x2, pg, pp, target], [],
                        [sds(x2.shape, F32), sds(x2.shape, BF16), sds(x2.shape, BF16)], [sds((8, LANES), F32)], name)
    return outs[0], outs[1], outs[2], accs[0][0, 0]


def _attn_block(dot, q, kp, kc, vp, vc, bp, bc, prev_offset):
    blk = q.shape[0]
    qi = lax.broadcasted_iota(I32, (blk, blk), 0)
    ki = lax.broadcasted_iota(I32, (blk, blk), 1)
    mask_c = ki <= qi
    mask_p = ki >= qi + prev_offset
    scale = HEAD_DIM ** -0.5
    s_c = jnp.where(mask_c, dot(q, kc, 'nt', False) * scale + bc, NEG_INF)
    s_p = jnp.where(mask_p, dot(q, kp, 'nt', False) * scale + bp, NEG_INF)
    m = lax.stop_gradient(jnp.maximum(jnp.max(s_c, axis=1, keepdims=True), jnp.max(s_p, axis=1, keepdims=True)))
    e_c = jnp.where(mask_c, jnp.exp(s_c - m), 0.0)
    e_p = jnp.where(mask_p, jnp.exp(s_p - m), 0.0)
    l = jnp.sum(e_c, axis=1, keepdims=True) + jnp.sum(e_p, axis=1, keepdims=True)
    o = (dot(e_c, vc, 'nn', False) + dot(e_p, vp, 'nn', False)) / l
    return o, jnp.broadcast_to(m + jnp.log(l), o.shape)


def _attn_specs(blk, hd):
    cur = pl.BlockSpec((None, None, blk, hd), lambda h, r, n: (h, r, n, 0))
    prev = pl.BlockSpec((None, None, blk, hd), lambda h, r, n: (h, r, jnp.maximum(n - 1, 0), 0))
    bias = pl.BlockSpec((None, blk, blk), lambda h, r, n: (h, 0, 0))
    return cur, prev, bias


def attn_fwd(q, k, v, bp, bc, name):
    hg, d, length, hd = q.shape
    blk = ATTN_BLOCK
    cur, prev, bias = _attn_specs(blk, hd)

    def body(q_ref, kp_ref, kc_ref, vp_ref, vc_ref, bp_ref, bc_ref, o_ref, l_ref):
        off = jnp.where(pl.program_id(2) > 0, 0, blk)
        o, l = _attn_block(_dot, q_ref[...], kp_ref[...], kc_ref[...], vp_ref[...], vc_ref[...], bp_ref[...],
                           bc_ref[...], off)
        o_ref[...] = o
        l_ref[...] = l

    return pl.pallas_call(
        body, grid=(hg, d, length // blk), in_specs=[cur, prev, cur, prev, cur, bias, bias], out_specs=[cur, cur],
        out_shape=[jax.ShapeDtypeStruct(q.shape, F32)] * 2,
        compiler_params=_params(("parallel", "parallel", "parallel")), name=name,
    )(q, k, k, v, v, bp, bc)


def attn_bwd(q, k, v, bp, bc, do, dl, name):
    hg, d, length, hd = q.shape
    blk = ATTN_BLOCK
    nb = length // blk
    cur, prev, bias = _attn_specs(blk, hd)

    def body(q_ref, kp_ref, kc_ref, vp_ref, vc_ref, bp_ref, bc_ref, do_ref, dl_ref,
             dq_ref, dkp_ref, dkc_ref, dvp_ref, dvc_ref, dbp_ref, dbc_ref):
        off = jnp.where(pl.program_id(2) > 0, 0, blk)
        f = functools.partial(_attn_block, _dot_ad, prev_offset=off)
        _, vjp = jax.vjp(f, q_ref[...], kp_ref[...], kc_ref[...], vp_ref[...], vc_ref[...], bp_ref[...], bc_ref[...])
        dq, dkp, dkc, dvp, dvc, dbp, dbc = vjp((do_ref[...], dl_ref[...]))
        dq_ref[...] = dq
        dkp_ref[...] = dkp
        dkc_ref[...] = dkc
        dvp_ref[...] = dvp
        dvc_ref[...] = dvc
        first = jnp.logical_and(pl.program_id(1) == 0, pl.program_id(2) == 0)

        @pl.when(first)
        def _():
            dbp_ref[...] = dbp
            dbc_ref[...] = dbc

        @pl.when(jnp.logical_not(first))
        def _():
            dbp_ref[...] += dbp
            dbc_ref[...] += dbc

    blocks = jax.ShapeDtypeStruct(q.shape, F32)
    dq, dkp, dkc, dvp, dvc, dbp, dbc = pl.pallas_call(
        body, grid=(hg, d, nb), in_specs=[cur, prev, cur, prev, cur, bias, bias, cur, cur],
        out_specs=[cur] * 5 + [bias] * 2, out_shape=[blocks] * 5 + [jax.ShapeDtypeStruct(bp.shape, F32)] * 2,
        compiler_params=_params(("arbitrary", "arbitrary", "arbitrary")), name=name,
    )(q, k, k, v, v, bp, bc, do, dl)

    nxt = pl.BlockSpec((None, None, blk, hd), lambda h, r, n: (h, r, jnp.minimum(n + 1, nb - 1), 0))

    def add_body(kc_ref, kp_ref, vc_ref, vp_ref, dk_ref, dv_ref):
        has_next = (pl.program_id(2) + 1 < nb).astype(F32)
        dk_ref[...] = kc_ref[...] + kp_ref[...] * has_next
        dv_ref[...] = vc_ref[...] + vp_ref[...] * has_next

    dk, dv = pl.pallas_call(
        add_body, grid=(hg, d, nb), in_specs=[cur, nxt, cur, nxt], out_specs=[cur, cur], out_shape=[blocks] * 2,
        compiler_params=_params(("parallel", "parallel", "parallel")), name=name + "_kv",
    )(dkc, dkp, dvc, dvp)
    return dq, dk, dv, dbp, dbc


def _t5_bucket(dist):
    max_exact = N_BUCKETS // 2
    d_f = jnp.maximum(dist, 1).astype(F32)
    large = max_exact + (jnp.log(d_f / max_exact) / math.log(MAX_DISTANCE / max_exact)
                         * (N_BUCKETS - max_exact)).astype(I32)
    large = jnp.minimum(large, N_BUCKETS - 1)
    return jnp.where(dist < max_exact, dist, large)


def _bucket_tables():
    blk = ATTN_BLOCK
    qi = jnp.arange(blk)[:, None]
    ki = jnp.arange(blk)[None, :]
    out = []
    for _, dilation in DILATED_GROUPS:
        rel_p = jnp.maximum(blk + qi - ki, 0) * dilation
        rel_c = jnp.maximum(qi - ki, 0) * dilation
        out.append(jnp.stack([_t5_bucket(rel_p), _t5_bucket(rel_c)]))
    return jnp.stack(out).astype(I32)


def bias_fwd(table, buckets, name):
    blk = ATTN_BLOCK

    def body(tab_ref, bkt_ref, out_ref):
        for g in range(len(DILATED_GROUPS)):
            for half in range(2):
                bk = bkt_ref[g, half]
                for hh in range(HEADS_PER_GROUP):
                    h = g * HEADS_PER_GROUP + hh
                    acc = jnp.zeros((blk, blk), F32)
                    for b in range(N_BUCKETS):
                        acc = jnp.where(bk == b, tab_ref[b, h], acc)
                    out_ref[h, half] = acc

    return pl.pallas_call(
        body, in_specs=[pl.BlockSpec(memory_space=pltpu.SMEM), pl.BlockSpec(memory_space=pltpu.VMEM)],
        out_specs=pl.BlockSpec(memory_space=pltpu.VMEM),
        out_shape=jax.ShapeDtypeStruct((N_HEADS, 2, blk, blk), F32), name=name,
    )(table, buckets)


def bias_bwd(dbias, buckets, name):
    def body(db_ref, bkt_ref, out_ref):
        rows = lax.broadcasted_iota(I32, (N_BUCKETS, LANES), 0)
        cols = lax.broadcasted_iota(I32, (N_BUCKETS, LANES), 1)
        acc = jnp.zeros((N_BUCKETS, LANES), F32)
        for g in range(len(DILATED_GROUPS)):
            bk_p, bk_c = bkt_ref[g, 0], bkt_ref[g, 1]
            for hh in range(HEADS_PER_GROUP):
                h = g * HEADS_PER_GROUP + hh
                d_p, d_c = db_ref[h, 0], db_ref[h, 1]
                for b in range(N_BUCKETS):
                    s = jnp.sum(jnp.where(bk_p == b, d_p, 0.0)) + jnp.sum(jnp.where(bk_c == b, d_c, 0.0))
                    acc = jnp.where(jnp.logical_and(rows == b, cols == h), s, acc)
        out_ref[...] = acc

    return pl.pallas_call(
        body, in_specs=[pl.BlockSpec(memory_space=pltpu.VMEM)] * 2, out_specs=pl.BlockSpec(memory_space=pltpu.VMEM),
        out_shape=jax.ShapeDtypeStruct((N_BUCKETS, LANES), F32), name=name,
    )(dbias, buckets)


def _rwkv_chunk(dot, s0, r, lw, k, v, a, b):
    c = r.shape[0]
    ti = lax.broadcasted_iota(I32, (c, c), 0)
    si = lax.broadcasted_iota(I32, (c, c), 1)
    incl = si <= ti
    strict = si < ti
    cum = dot(incl.astype(F32), lw, 'nn', True)
    w_incl = jnp.exp(cum)
    w_prev = jnp.exp(cum - lw)
    w_inv = jnp.exp(-cum)
    w_end = jnp.exp(jnp.sum(lw, axis=0, keepdims=True))
    a_t, r_t, b_t, k_t = a * w_prev, r * w_incl, b * w_inv, k * w_inv
    l_ab = jnp.where(strict, dot(a_t, b_t, 'nt', True), 0.0)
    l_ak = jnp.where(strict, dot(a_t, k_t, 'nt', True), 0.0)
    u = dot(a_t, s0, 'nt', True) + dot(l_ak, v, 'nn', True)
    u = u + dot(l_ab, u, 'nn', True)
    power = l_ab
    for _ in range(int(math.log2(c)) - 1):
        power = dot(power, power, 'nn', True)
        u = u + dot(power, u, 'nn', True)
    m_rb = jnp.where(incl, dot(r_t, b_t, 'nt', True), 0.0)
    m_rk = jnp.where(incl, dot(r_t, k_t, 'nt', True), 0.0)
    y = dot(r_t, s0, 'nt', True) + dot(m_rb, u, 'nn', True) + dot(m_rk, v, 'nn', True)
    s1 = (s0 + dot(u, b_t, 'tn', True) + dot(v, k_t, 'tn', True)) * w_end
    return y, s1


def rwkv_fwd(r, lw, k, v, a, b, name, gather=()):
    h, t, n = r.shape
    c = RWKV_CHUNK
    nc = t // c
    hs = math.gcd(h, RWKV_HEADS_PER_STEP)
    steps = h // hs * nc
    ng = len(gather)
    row = pl.BlockSpec((hs, c, n), lambda i, j: (i, j, 0))

    def body(*refs):
        ins, x_refs = refs[:6], refs[6:6 + ng]
        y_ref, s0_ref = refs[6 + ng:8 + ng]
        out_refs, state, sems = refs[8 + ng:8 + 2 * ng], refs[8 + 2 * ng], refs[9 + 2 * ng:]
        step = pl.program_id(0) * nc + pl.program_id(1)
        if ng:
            start, forward, finish = _gather_schedule(x_refs, out_refs, *sems)
            pl.when(step == 0)(start)
            pl.when(step == steps // 2)(forward)

        @pl.when(pl.program_id(1) == 0)
        def _():
            state[...] = jnp.zeros_like(state)

        for q in range(hs):
            s0 = state[q]
            s0_ref[q] = s0
            y, s1 = _rwkv_chunk(_dot, s0, *[ref[q] for ref in ins])
            y_ref[q] = y
            state[q] = s1
        if ng:
            pl.when(step == steps - 1)(finish)

    return pl.pallas_call(
        body, grid=(h // hs, nc), in_specs=[row] * 6 + [_ANY] * ng,
        out_specs=[row, pl.BlockSpec((hs, None, n, n), lambda i, j: (i, j, 0, 0))] + [_ANY] * ng,
        out_shape=[jax.ShapeDtypeStruct((h, t, n), F32), jax.ShapeDtypeStruct((h, nc, n, n), F32)]
        + [jax.ShapeDtypeStruct((N_DEV,) + g.shape, g.dtype) for g in gather],
        scratch_shapes=[pltpu.VMEM((hs, n, n), F32)] + (_exchange_scratch(ng) if ng else []),
        compiler_params=_params(("arbitrary", "arbitrary")), name=name,
    )(r, lw, k, v, a, b, *gather)


def rwkv_bwd(r, lw, k, v, a, b, s0, dy, name, scatter=()):
    h, t, n = r.shape
    c = RWKV_CHUNK
    nc = t // c
    hs = math.gcd(h, RWKV_HEADS_PER_STEP)
    steps = h // hs * nc
    ns = len(scatter)
    row = pl.BlockSpec((hs, c, n), lambda i, j: (i, nc - 1 - j, 0))
    st = pl.BlockSpec((hs, None, n, n), lambda i, j: (i, nc - 1 - j, 0, 0))

    def body(*refs):
        ins, s0_ref, dy_ref, send_refs = refs[:6], refs[6], refs[7], refs[8:8 + ns]
        grad_refs, recv_refs = refs[8 + ns:14 + ns], refs[14 + ns:14 + 2 * ns]
        dstate, sems = refs[14 + 2 * ns], refs[15 + 2 * ns:]
        step = pl.program_id(0) * nc + pl.program_id(1)
        if ns:
            start, finish = _scatter_schedule(send_refs, recv_refs, *sems)
            pl.when(step == 0)(start)

        @pl.when(pl.program_id(1) == 0)
        def _():
            dstate[...] = jnp.zeros_like(dstate)

        for q in range(hs):
            _, vjp = jax.vjp(functools.partial(_rwkv_chunk, _dot_ad), s0_ref[q], *[ref[q] for ref in ins])
            grads = vjp((dy_ref[q], dstate[q]))
            dstate[q] = grads[0]
            for ref, g in zip(grad_refs, grads[1:]):
                ref[q] = g
        if ns:
            pl.when(step == steps - 1)(finish)

    return pl.pallas_call(
        body, grid=(h // hs, nc), in_specs=[row] * 6 + [st, row] + [_ANY] * ns, out_specs=[row] * 6 + [_ANY] * ns,
        out_shape=[jax.ShapeDtypeStruct((h, t, n), F32)] * 6 + [jax.ShapeDtypeStruct(s.shape, s.dtype) for s in scatter],
        scratch_shapes=[pltpu.VMEM((hs, n, n), F32)] + (_exchange_scratch(ns) if ns else []),
        compiler_params=_params(("arbitrary", "arbitrary")), name=name,
    )(r, lw, k, v, a, b, s0, dy, *scatter)


_ANY = pl.BlockSpec(memory_space=pl.ANY)


def _exchange_scratch(n_arrays):
    return [pltpu.SemaphoreType.DMA((n_arrays, N_DEV - 1)), pltpu.SemaphoreType.DMA((n_arrays, N_DEV - 1)),
            pltpu.SemaphoreType.DMA((n_arrays,))]


def _gather_schedule(x_refs, out_refs, send_sems, recv_sems, local_sems):
    x, y, c = lax.axis_index("x"), lax.axis_index("y"), lax.axis_index("c")
    me, sibling = (x, y, c), (x, y, 1 - c)
    chips = [(1 - x, y), (x, 1 - y), (1 - x, 1 - y)]
    arrays = range(len(x_refs))

    def slot(a, pos):
        return out_refs[a].at[4 * pos[0] + 2 * pos[1] + pos[2]]

    def copy(a, i, block, to, src=None):
        return pltpu.make_async_remote_copy(
            src_ref=slot(a, block) if src is None else src, dst_ref=slot(a, block), send_sem=send_sems.at[a, i],
            recv_sem=recv_sems.at[a, i], device_id=to, device_id_type=pl.DeviceIdType.MESH)

    mine = [pltpu.make_async_copy(x_refs[a], slot(a, me), local_sems.at[a]) for a in arrays]
    first = [[copy(a, 0, me, sibling, src=x_refs[a])]
             + [copy(a, 1 + j, me, (*chip, c), src=x_refs[a]) for j, chip in enumerate(chips)] for a in arrays]
    passed = [[copy(a, 4 + j, (*chip, c), sibling) for j, chip in enumerate(chips)] for a in arrays]

    def start():
        for a in arrays:
            mine[a].start()
            for cp in first[a]:
                cp.start()

    def forward():
        for j, chip in enumerate(chips):
            for a in arrays:
                copy(a, 1 + j, (*chip, c), me).wait_recv()
                passed[a][j].start()

    def finish():
        for a in arrays:
            copy(a, 0, sibling, me).wait_recv()
            for j, chip in enumerate(chips):
                copy(a, 4 + j, (*chip, 1 - c), me).wait_recv()
            for cp in first[a] + passed[a]:
                cp.wait_send()
            mine[a].wait()

    return start, forward, finish


def _scatter_schedule(in_refs, out_refs, send_sems, recv_sems, local_sems):
    x, y, c = lax.axis_index("x"), lax.axis_index("y"), lax.axis_index("c")
    me = 4 * x + 2 * y + c
    mine, remote = [], []
    for a, (src, dst) in enumerate(zip(in_refs, out_refs)):
        mine.append(pltpu.make_async_copy(src.at[me], dst.at[me], local_sems.at[a]))
        for i in range(1, N_DEV):
            px, py, pc = x ^ (i >> 2), y ^ ((i >> 1) & 1), c ^ (i & 1)
            remote.append(pltpu.make_async_remote_copy(
                src_ref=src.at[4 * px + 2 * py + pc], dst_ref=dst.at[me], send_sem=send_sems.at[a, i - 1],
                recv_sem=recv_sems.at[a, i - 1], device_id=(px, py, pc), device_id_type=pl.DeviceIdType.MESH))

    def start():
        for cp in mine + remote:
            cp.start()

    def finish():
        for cp in remote:
            cp.wait_recv()
        for cp in remote:
            cp.wait_send()
        for cp in mine:
            cp.wait()

    return start, finish


def all_gather_many(shards, name):
    n = len(shards)

    def body(*refs):
        start, forward, finish = _gather_schedule(refs[:n], refs[n:2 * n], *refs[2 * n:])
        start()
        forward()
        finish()

    return pl.pallas_call(
        body, in_specs=[_ANY] * n, out_specs=[_ANY] * n,
        out_shape=[jax.ShapeDtypeStruct((N_DEV,) + s.shape, s.dtype) for s in shards],
        scratch_shapes=_exchange_scratch(n), name=name,
    )(*shards)


def scatter_many(parts, name):
    n = len(parts)

    def body(*refs):
        start, finish = _scatter_schedule(refs[:n], refs[n:2 * n], *refs[2 * n:])
        start()
        finish()

    return pl.pallas_call(
        body, in_specs=[_ANY] * n, out_specs=[_ANY] * n,
        out_shape=[jax.ShapeDtypeStruct(s.shape, s.dtype) for s in parts],
        scratch_shapes=_exchange_scratch(n), name=name,
    )(*parts)


def scan_and_gather(scan_in, shards):
    y, s0, *gathered = rwkv_fwd(*scan_in, name="rwkv_scan", gather=shards)
    return y, s0, gathered


def scan_bwd_and_scatter(scan_in, s0, dy, parts):
    res = rwkv_bwd(*scan_in, s0, dy, name="rwkv_scan_bwd", scatter=parts)
    return res[:6], res[6:]


def adamw_shard(parts, w, m, v, name):
    _, k, n = w.shape
    tc = _pick(n, (2048, 1024, 512))
    tr = _row_tile(k, tc * (N_DEV * jnp.dtype(parts.dtype).itemsize + 7 * 4))

    def body(p_ref, w_ref, m_ref, v_ref, g_ref, d_ref, nm_ref, nv_ref):
        _adamw_block(p_ref, w_ref, m_ref, v_ref, g_ref, d_ref, nm_ref, nv_ref)

    one = pl.BlockSpec((None, tr, tc), lambda i, j: (0, i, j))
    return pl.pallas_call(
        body, grid=(k // tr, n // tc), in_specs=[pl.BlockSpec((N_DEV, tr, tc), lambda i, j: (0, i, j))] + [one] * 3,
        out_specs=[one] * 4, out_shape=[jax.ShapeDtypeStruct(w.shape, F32)] * 4,
        compiler_params=_params(("parallel", "parallel")), name=name,
    )(parts, w, m, v)


def _adamw_block(p_ref, w_ref, m_ref, v_ref, g_ref, d_ref, nm_ref, nv_ref):
    g = p_ref[0].astype(F32)
    for j in range(1, N_DEV):
        g = g + p_ref[j].astype(F32)
    new_m = ADAM_B1 * m_ref[...] + (1.0 - ADAM_B1) * g
    new_v = ADAM_B2 * v_ref[...] + (1.0 - ADAM_B2) * jnp.square(g)
    m_hat = new_m / (1.0 - ADAM_B1 ** ADAM_STEP)
    v_hat = new_v / (1.0 - ADAM_B2 ** ADAM_STEP)
    g_ref[...] = g
    d_ref[...] = -ADAM_LR * (m_hat / (jnp.sqrt(v_hat) + ADAM_EPS) + ADAM_WD * w_ref[...])
    nm_ref[...] = new_m
    nv_ref[...] = new_v


def adamw(parts, w, m, v, name):
    rows = w.shape[0]
    tile = _row_tile(rows, N_DEV * LANES * jnp.dtype(parts.dtype).itemsize + 7 * LANES * 4)

    def body(p_ref, w_ref, m_ref, v_ref, g_ref, d_ref, nm_ref, nv_ref):
        _adamw_block(p_ref, w_ref, m_ref, v_ref, g_ref, d_ref, nm_ref, nv_ref)

    flat = pl.BlockSpec((tile, LANES), lambda i: (i, 0))
    return pl.pallas_call(
        body, grid=(rows // tile,), in_specs=[pl.BlockSpec((N_DEV, tile, LANES), lambda i: (0, i, 0))] + [flat] * 3,
        out_specs=[flat] * 4, out_shape=[jax.ShapeDtypeStruct(w.shape, F32)] * 4,
        compiler_params=_params(("parallel",)), name=name,
    )(parts, w, m, v)


def _part_rows(n_elems):
    return -(-n_elems // (PACK_ROWS * LANES)) * PACK_ROWS


def _pack(arrays, dtype, lead=()):
    parts, layout, off = [], [], 0
    for arr in arrays:
        n = math.prod(arr.shape[len(lead):])
        rows = _part_rows(n)
        flat = arr.reshape(lead + (n,)).astype(dtype)
        flat = jnp.pad(flat, [(0, 0)] * len(lead) + [(0, rows * LANES - n)])
        parts.append(flat.reshape(lead + (rows, LANES)))
        layout.append((off, rows))
        off += rows
    total = -(-off // 1024) * 1024
    if total > off:
        parts.append(jnp.zeros(lead + (total - off, LANES), dtype))
    return jnp.concatenate(parts, axis=len(lead)), layout


def _unpack(buf, layout, shapes, lead=()):
    out = []
    for (off, rows), shape in zip(layout, shapes):
        n = math.prod(shape)
        piece = lax.slice_in_dim(buf, off, off + rows, axis=len(lead))
        out.append(piece.reshape(lead + (rows * LANES,))[..., :n].reshape(lead + tuple(shape)))
    return out


def _split_shards(full, axis):
    if axis == 0:
        return full.reshape((N_DEV, full.shape[0] // N_DEV, full.shape[1]))
    return full.reshape((full.shape[0], N_DEV, full.shape[1] // N_DEV)).transpose(1, 0, 2)


def _join_shards(shards, axis):
    if axis == 0:
        return shards.reshape((-1, shards.shape[2]))
    return shards.transpose(1, 0, 2).reshape((shards.shape[1], -1))


def _shift_down(t):
    return jnp.pad(t, ((1, 0), (0, 0)))[:-1]


def _shift_up(t):
    return jnp.pad(t, ((0, 1), (0, 0)))[1:]


def kernel(x, p, norm_mix, w_in, q_gain, k_gain, rel_bias, w_attn_up, shift_mix, w0, w_decay_up, a0, w_aaa_up, w_gate_up, k_k, k_a, r_k, gn_w, gn_b, w_rwkv_up, w_out, norm_mlp, w_mlp_in, w_mlp_out, norm_ple, w_ple_gate, w_ple_proj, loss_target, m_norm_mix, m_w_in, m_q_gain, m_k_gain, m_rel_bias, m_w_attn_up, m_shift_mix, m_w0, m_w_decay_up, m_a0, m_w_aaa_up, m_w_gate_up, m_k_k, m_k_a, m_r_k, m_gn_w, m_gn_b, m_w_rwkv_up, m_w_out, m_norm_mlp, m_w_mlp_in, m_w_mlp_out, m_norm_ple, m_w_ple_gate, m_w_ple_proj, v_norm_mix, v_w_in, v_q_gain, v_k_gain, v_rel_bias, v_w_attn_up, v_shift_mix, v_w0, v_w_decay_up, v_a0, v_w_aaa_up, v_w_gate_up, v_k_k, v_k_a, v_r_k, v_gn_w, v_gn_b, v_w_rwkv_up, v_w_out, v_norm_mlp, v_w_mlp_in, v_w_mlp_out, v_norm_ple, v_w_ple_gate, v_w_ple_proj):
    given = dict(locals())
    xs = x[0]
    t_len, d_model = xs.shape
    target = loss_target[0]
    p_bf = p[0, 0].astype(BF16)
    rw_width = w0.shape[1]
    n_rheads = rw_width // RWKV_HEAD
    lora_d, lora_a, lora_g = w_decay_up.shape[1], w_aaa_up.shape[1], w_gate_up.shape[1]
    z_width = shift_mix.shape[1]
    z_pad = -(-z_width // LANES) * LANES
    qkv_width = 3 * ATTN_WIDTH
    assert z_width == 3 * rw_width + lora_d + lora_a + lora_g
    assert N_DEV * w_in.shape[2] == qkv_width + z_width + 2 * d_model
    for window, dilation in DILATED_GROUPS:
        assert window // dilation == ATTN_BLOCK and t_len % (dilation * ATTN_BLOCK) == 0

    shard_bf = {n: given[n][0].astype(BF16) for n in BIG}
    early = ['w_in', 'w_decay_up', 'w_aaa_up', 'w_gate_up']
    late = [n for n in BIG if n not in early]
    full = {n: _join_shards(g, 1) for n, g in zip(early, all_gather_many([shard_bf[n] for n in early], "gather_w_in"))}
    w_qkv = full['w_in'][:, :qkv_width]
    w_z = jnp.pad(full['w_in'][:, qkv_width:qkv_width + z_width], ((0, 0), (0, z_pad - z_width)))
    w_g = full['w_in'][:, qkv_width + z_width:]

    (h_in,) = rowmap_fwd(lambda a, g: st_norm(a, g)[1:], [xs], [norm_mix], [BF16], "norm_in")
    qkv = matmul(h_in, w_qkv, 'nn', F32, "proj_qkv")
    z = matmul(h_in, w_z, 'nn', F32, "proj_z")
    gates = matmul(h_in, w_g, 'nn', F32, "proj_gates")
    gate_a, gate_r = gates[:, :d_model], gates[:, d_model:]

    q_raw = qkv[:, :ATTN_WIDTH].reshape(t_len * N_HEADS, HEAD_DIM)
    k_raw = qkv[:, ATTN_WIDTH:2 * ATTN_WIDTH].reshape(t_len * N_HEADS, HEAD_DIM)
    v_att = qkv[:, 2 * ATTN_WIDTH:]
    q_n, k_n = rowmap_fwd(st_qk_norm, [q_raw, k_raw], [q_gain, k_gain], [F32, F32], "qk_norm")
    buckets = _bucket_tables()
    bias = bias_fwd(rel_bias, buckets, "attn_bias")

    def to_blocks(tok, g):
        dil = DILATED_GROUPS[g][1]
        tok = tok.reshape(t_len, N_HEADS, HEAD_DIM)[:, g * HEADS_PER_GROUP:(g + 1) * HEADS_PER_GROUP]
        return tok.reshape(t_len // dil, dil, HEADS_PER_GROUP, HEAD_DIM).transpose(2, 1, 0, 3)

    def from_blocks(blk):
        return blk.transpose(2, 1, 0, 3).reshape(t_len, ATTN_OUT)

    att_in, att_o, att_l = [], [], []
    for g in range(len(DILATED_GROUPS)):
        hs = slice(g * HEADS_PER_GROUP, (g + 1) * HEADS_PER_GROUP)
        ops = (to_blocks(q_n, g), to_blocks(k_n, g), to_blocks(v_att, g), bias[hs, 0], bias[hs, 1])
        o_g, l_g = attn_fwd(*ops, name=f"attn_fwd_{g}")
        att_in.append(ops)
        att_o.append(from_blocks(o_g))
        att_l.append(from_blocks(l_g))
    (attn,) = rowmap_fwd(st_merge, att_o + att_l, [], [BF16], "attn_merge")

    c0 = rw_width
    cuts = [0, c0, 2 * c0, 3 * c0, 3 * c0 + lora_d, 3 * c0 + lora_d + lora_a, z_width]
    z_parts = [z[:, lo:hi] for lo, hi in zip(cuts[:-1], cuts[1:])]
    z_prev = [_shift_down(t) for t in z_parts]
    mixes = [shift_mix[:, lo:hi] for lo, hi in zip(cuts[:-1], cuts[1:])]
    pre_params = mixes + [w0, full['w_decay_up'], a0, full['w_aaa_up'], full['w_gate_up'], k_k, k_a]
    pre_out = rowmap_fwd(functools.partial(_st_rwkv_pre, _dot), z_parts + z_prev, pre_params, [F32] * 7, "rwkv_pre")
    r_s, lw_s, k_s, v_s, kk0_s, a_s, g_s = pre_out

    def heads(tok):
        return tok.reshape(t_len, n_rheads, RWKV_HEAD)

    def head_major(tok):
        return heads(tok).transpose(1, 0, 2)

    aa_s, bb_s = rowmap_fwd(st_rwkv_kk, [heads(kk0_s), heads(a_s)], [], [F32, F32], "rwkv_kk")
    scan_in = [head_major(r_s), head_major(lw_s), head_major(k_s), head_major(v_s),
               aa_s.transpose(1, 0, 2), bb_s.transpose(1, 0, 2)]
    y_h, s0_h, late_gathered = scan_and_gather(scan_in, [shard_bf[n] for n in late])
    wt = {n: g if SHARD_AXIS[n] == 1 else g.reshape(-1, g.shape[2]) for n, g in zip(late, late_gathered)}
    y_s = y_h.transpose(1, 0, 2)
    post_params = [gn_w.reshape(1, n_rheads, RWKV_HEAD), gn_b.reshape(1, n_rheads, RWKV_HEAD), r_k]
    post_in = [y_s, heads(r_s), heads(k_s), heads(v_s), heads(g_s)]
    (rw,) = rowmap_fwd(st_rwkv_post, post_in, post_params, [BF16], "rwkv_post")
    rw = rw.reshape(t_len, rw_width)
    attn_d = matmul(attn, wt['w_attn_up'], 'nn', F32, "attn_up", b_shards=True)
    rwkv_d = matmul(rw, wt['w_rwkv_up'], 'nn', F32, "rwkv_up", b_shards=True)

    (merged,) = rowmap_fwd(st_gate, [gate_a, gate_r, attn_d, rwkv_d], [], [BF16], "gate_merge")
    mix_out = matmul(merged, wt['w_out'], 'nn', F32, "out_proj")
    x1, h_mlp = rowmap_fwd(st_res_norm, [xs, mix_out], [norm_mlp], [F32, BF16], "res_norm_mlp")
    u = matmul(h_mlp, wt['w_mlp_in'], 'nn', F32, "mlp_in", b_shards=True)
    (act,) = rowmap_fwd(st_relu2, [u], [], [BF16], "mlp_act")
    mlp_out = matmul(act, wt['w_mlp_out'], 'nn', F32, "mlp_out")
    x2, h_ple = rowmap_fwd(st_res_norm, [x1, mlp_out], [norm_ple], [F32, BF16], "res_norm_ple")
    pg = matmul(h_ple, wt['w_ple_gate'], 'nn', F32, "ple_gate")
    pp = matmul(p_bf, wt['w_ple_proj'], 'nn', F32, "ple_proj", b_shards=True)
    dy, d_pg, d_pp, loss_local = loss_head(x2, pg, pp, target, "loss_head")

    def row_cut(full_grad):
        return full_grad.reshape(N_DEV, full_grad.shape[0] // N_DEV, full_grad.shape[1])

    grads, sends = {}, {}
    sends['w_ple_gate'] = row_cut(matmul(h_ple, d_pg, 'tn', BF16, "d_w_ple_gate"))
    sends['w_ple_proj'] = matmul(p_bf, d_pp, 'tn', BF16, "d_w_ple_proj", out_shards=True)
    d_h_ple = matmul(d_pg, wt['w_ple_gate'], 'nt', F32, "d_h_ple")
    (d_x2, d_x2_bf), (grads['norm_ple'],) = rowmap_bwd(
        st_res_norm, [x1, mlp_out], [norm_ple], [dy, d_h_ple], [F32, BF16], "res_norm_ple_bwd")
    sends['w_mlp_out'] = row_cut(matmul(act, d_x2_bf, 'tn', BF16, "d_w_mlp_out"))
    d_act = matmul(d_x2_bf, wt['w_mlp_out'], 'nt', F32, "d_act")
    (d_u,), _ = rowmap_bwd(st_relu2, [u], [], [d_act], [BF16], "mlp_act_bwd")
    sends['w_mlp_in'] = matmul(h_mlp, d_u, 'tn', BF16, "d_w_mlp_in", out_shards=True)
    d_h_mlp = matmul(d_u, wt['w_mlp_in'], 'nt', F32, "d_h_mlp", b_shards=True)
    (d_x1, d_x1_bf), (grads['norm_mlp'],) = rowmap_bwd(
        st_res_norm, [xs, mix_out], [norm_mlp], [d_x2, d_h_mlp], [F32, BF16], "res_norm_mlp_bwd")

    sends['w_out'] = row_cut(matmul(merged, d_x1_bf, 'tn', BF16, "d_w_out"))
    d_merged = matmul(d_x1_bf, wt['w_out'], 'nt', F32, "d_merged")
    (d_gate_a, d_gate_r, d_attn_d, d_rwkv_d), _ = rowmap_bwd(
        st_gate, [gate_a, gate_r, attn_d, rwkv_d], [], [d_merged], [BF16] * 4, "gate_merge_bwd")
    sends['w_attn_up'] = matmul(attn, d_attn_d, 'tn', BF16, "d_w_attn_up", out_shards=True)
    sends['w_rwkv_up'] = matmul(rw, d_rwkv_d, 'tn', BF16, "d_w_rwkv_up", out_shards=True)
    d_attn = matmul(d_attn_d, wt['w_attn_up'], 'nt', F32, "d_attn", b_shards=True)
    d_rw = matmul(d_rwkv_d, wt['w_rwkv_up'], 'nt', F32, "d_rw", b_shards=True)

    (d_y, d_r1, d_k1, d_v1, d_g), (d_gn_w, d_gn_b, grads['r_k']) = rowmap_bwd(
        st_rwkv_post, post_in, post_params, [heads(d_rw)], [F32] * 5, "rwkv_post_bwd")
    grads['gn_w'], grads['gn_b'] = d_gn_w.reshape(1, rw_width), d_gn_b.reshape(1, rw_width)
    scan_grads, late_received = scan_bwd_and_scatter(scan_in, s0_h, d_y.transpose(1, 0, 2), [sends[n] for n in late])
    received = dict(zip(late, late_received))
    d_r2, d_lw, d_k2, d_v2, d_aa, d_bb = [t.transpose(1, 0, 2) for t in scan_grads]
    (d_kk0, d_a), _ = rowmap_bwd(st_rwkv_kk, [heads(kk0_s), heads(a_s)], [], [d_aa, d_bb], [F32, F32], "rwkv_kk_bwd")

    def flat(tok):
        return tok.reshape(t_len, rw_width)

    pre_cts = [[flat(d_r1), flat(d_r2)], flat(d_lw), [flat(d_k1), flat(d_k2)], [flat(d_v1), flat(d_v2)],
               flat(d_kk0), flat(d_a), flat(d_g)]
    d_zp, d_pre = rowmap_bwd(functools.partial(_st_rwkv_pre, _dot_ad), z_parts + z_prev, pre_params, pre_cts,
                             [F32] * 12, "rwkv_pre_bwd")
    grads['shift_mix'] = jnp.concatenate(d_pre[:6], axis=1)
    grads['w0'], d_w_decay, grads['a0'], d_w_aaa, d_w_gate, grads['k_k'], grads['k_a'] = d_pre[6:]
    for name, full_grad in (('w_decay_up', d_w_decay), ('w_aaa_up', d_w_aaa), ('w_gate_up', d_w_gate)):
        sends[name] = _split_shards(full_grad, 1).astype(BF16)
    z_fill = [jnp.zeros((t_len, z_pad - z_width), F32)] if z_pad > z_width else []
    d_z_cur = jnp.concatenate(d_zp[:6] + z_fill, axis=1)
    d_z_prev = _shift_up(jnp.concatenate(d_zp[6:] + z_fill, axis=1))
    (d_z,) = rowmap_fwd(st_add, [d_z_cur, d_z_prev], [], [BF16], "d_z_sum")

    d_merge, _ = rowmap_bwd(st_merge, att_o + att_l, [], [d_attn], [F32] * 6, "attn_merge_bwd")
    d_qn, d_kn, d_vs, d_bias = [], [], [], []
    for g in range(len(DILATED_GROUPS)):
        tok_g = functools.partial(to_blocks_grad, t_len=t_len, dil=DILATED_GROUPS[g][1])
        dq, dk, dv, dbp, dbc = attn_bwd(*att_in[g], tok_g(d_merge[g]), tok_g(d_merge[3 + g]), name=f"attn_bwd_{g}")
        d_qn.append(from_blocks(dq))
        d_kn.append(from_blocks(dk))
        d_vs.append(from_blocks(dv))
        d_bias.append(jnp.stack([dbp, dbc], axis=1))
    d_table = bias_bwd(jnp.concatenate(d_bias, axis=0), buckets, "attn_bias_bwd")
    grads['rel_bias'] = d_table[:, :N_HEADS]
    d_qn = jnp.concatenate(d_qn, axis=1).reshape(t_len * N_HEADS, HEAD_DIM)
    d_kn = jnp.concatenate(d_kn, axis=1).reshape(t_len * N_HEADS, HEAD_DIM)
    (d_q, d_k), (grads['q_gain'], grads['k_gain']) = rowmap_bwd(
        st_qk_norm, [q_raw, k_raw], [q_gain, k_gain], [d_qn, d_kn], [BF16, BF16], "qk_norm_bwd")
    d_qkv = jnp.concatenate([d_q.reshape(t_len, ATTN_WIDTH), d_k.reshape(t_len, ATTN_WIDTH)]
                            + [t.astype(BF16) for t in d_vs], axis=1)

    d_gates = jnp.concatenate([d_gate_a, d_gate_r], axis=1)
    sends['w_in'] = _split_shards(jnp.concatenate([
        matmul(h_in, d_qkv, 'tn', BF16, "d_w_qkv"),
        matmul(h_in, d_z, 'tn', BF16, "d_w_z")[:, :z_width],
        matmul(h_in, d_gates, 'tn', BF16, "d_w_gates")], axis=1), 1)
    d_h_in = [matmul(d_qkv, w_qkv, 'nt', F32, "d_h_qkv"), matmul(d_z, w_z, 'nt', F32, "d_h_z"),
              matmul(d_gates, w_g, 'nt', F32, "d_h_gates")]
    (grad_x,), (grads['norm_mix'],) = rowmap_bwd(st_norm, [xs], [norm_mix], [d_x1, d_h_in], [F32], "norm_in_bwd")

    received.update(zip(early, scatter_many([sends[n] for n in early], "scatter_grads_w_in")))
    by_name = {n: adamw_shard(received[n], given[n], given['m_' + n], given['v_' + n], "adamw_" + n) for n in BIG}
    pk = lambda prefix: _pack([given[prefix + n] for n in SMALL], F32)[0]
    small_buf, small_layout = _pack([grads[n].reshape(given[n].shape) for n in SMALL], F32)
    (small_all,) = all_gather_many([small_buf], "gather_small_grads")
    small_out = adamw(small_all, pk(''), pk('m_'), pk('v_'), "adamw_replicated")
    small_shapes = [given[n].shape for n in SMALL]
    by_name.update(zip(SMALL, zip(*[_unpack(buf, small_layout, small_shapes) for buf in small_out])))
    loss = lax.psum(loss_local, MESH_AXES)
    return (loss, grad_x[None], *[by_name[n][0] for n in WEIGHTS], *[by_name[n][1] for n in WEIGHTS],
            *[by_name[n][2] for n in WEIGHTS], *[by_name[n][3] for n in WEIGHTS])


def to_blocks_grad(tok, t_len, dil):
    return tok.reshape(t_len // dil, dil, HEADS_PER_GROUP, HEAD_DIM).transpose(2, 1, 0, 3)
```

```python
import functools
import math

import jax
import jax.numpy as jnp
from jax import lax
from jax.experimental import pallas as pl
from jax.experimental.pallas import tpu as pltpu

F32 = jnp.float32
BF16 = jnp.bfloat16
I32 = jnp.int32

N_DEV = 8
N_CHIPS = 4
MESH_AXES = ("x", "y", "c")
LANES = 128
PACK_ROWS = 16
VMEM_LIMIT_BYTES = 48 * 2**20
ROW_BLOCK_BYTES = 3 * 2**20

HEAD_DIM = 128
ATTN_BLOCK = 128
HEADS_PER_GROUP = 4
DILATED_GROUPS = ((128, 1), (512, 4), (2048, 16))
N_HEADS = HEADS_PER_GROUP * len(DILATED_GROUPS)
ATTN_WIDTH = N_HEADS * HEAD_DIM
ATTN_OUT = HEADS_PER_GROUP * HEAD_DIM
N_BUCKETS = 32
MAX_DISTANCE = 2048
RWKV_HEAD = 64
RWKV_CHUNK = 64
RWKV_HEADS_PER_STEP = 8
RMS_EPS = 1e-6
GN_EPS = 64e-5
NEG_INF = -1e30

ADAM_LR = 0.001
ADAM_B1 = 0.9
ADAM_B2 = 0.999
ADAM_EPS = 1e-08
ADAM_WD = 0.01
ADAM_STEP = 10

WEIGHTS = ['norm_mix', 'w_in', 'q_gain', 'k_gain', 'rel_bias', 'w_attn_up', 'shift_mix', 'w0', 'w_decay_up', 'a0',
           'w_aaa_up', 'w_gate_up', 'k_k', 'k_a', 'r_k', 'gn_w', 'gn_b', 'w_rwkv_up', 'w_out', 'norm_mlp', 'w_mlp_in',
           'w_mlp_out', 'norm_ple', 'w_ple_gate', 'w_ple_proj']
SHARD_AXIS = {'w_in': 1, 'w_attn_up': 1, 'w_decay_up': 1, 'w_aaa_up': 1, 'w_gate_up': 1, 'w_rwkv_up': 1, 'w_out': 0,
              'w_mlp_in': 1, 'w_mlp_out': 0, 'w_ple_gate': 0, 'w_ple_proj': 1}
BIG = [n for n in WEIGHTS if n in SHARD_AXIS]
SMALL = [n for n in WEIGHTS if n not in SHARD_AXIS]


def _params(sem):
    return pltpu.CompilerParams(dimension_semantics=sem, vmem_limit_bytes=VMEM_LIMIT_BYTES)


_DN = {'nn': (((1,), (0,)), ((), ())), 'nt': (((1,), (1,)), ((), ())), 'tn': (((0,), (0,)), ((), ()))}


def _dot(a, b, mode, exact):
    if exact:
        return lax.dot_general(a, b, _DN[mode], precision=lax.Precision.HIGH, preferred_element_type=F32)
    return lax.dot_general(a.astype(BF16), b.astype(BF16), _DN[mode], preferred_element_type=F32)


@functools.partial(jax.custom_vjp, nondiff_argnums=(2, 3))
def _dot_ad(a, b, mode, exact):
    return _dot(a, b, mode, exact)


def _dot_ad_fwd(a, b, mode, exact):
    return _dot(a, b, mode, exact), (a, b)


def _dot_ad_bwd(mode, exact, res, g):
    a, b = res
    if mode == 'nn':
        return _dot(g, b, 'nt', exact), _dot(a, g, 'tn', exact)
    if mode == 'nt':
        return _dot(g, b, 'nn', exact), _dot(g, a, 'tn', exact)
    return _dot(b, g, 'nt', exact), _dot(a, g, 'nn', exact)


_dot_ad.defvjp(_dot_ad_fwd, _dot_ad_bwd)


def _pick(n, cands):
    for c in cands:
        if n % c == 0:
            return c
    return n


def matmul(a, b, mode, out_dtype, name, b_shards=False, out_shards=False):
    assert a.dtype == BF16 and b.dtype == BF16, (name, a.dtype, b.dtype)
    if mode == 'tn':
        k, m = a.shape
    else:
        m, k = a.shape
    b_rows, b_cols = (b.shape[1], N_DEV * b.shape[2]) if b_shards else b.shape
    n, kb = (b_rows, b_cols) if mode == 'nt' else (b_cols, b_rows)
    assert kb == k, (name, a.shape, b.shape)
    tm = _pick(m, (1024, 512, 256, 128))
    tn = _pick(n // N_DEV if (out_shards or (b_shards and mode != 'nt')) else n, (1024, 512, 256, 128))
    tk = _pick(k // N_DEV if (b_shards and mode == 'nt') else k, (1024, 512, 256, 128))
    nk = k // tk

    def body(a_ref, b_ref, o_ref, acc_ref):
        kk = pl.program_id(2)

        @pl.when(kk == 0)
        def _():
            acc_ref[...] = jnp.zeros_like(acc_ref)

        acc_ref[...] += lax.dot_general(a_ref[...], b_ref[...], _DN[mode], preferred_element_type=F32)

        @pl.when(kk == nk - 1)
        def _():
            o_ref[...] = acc_ref[...].astype(o_ref.dtype)

    if mode == 'tn':
        a_spec = pl.BlockSpec((tk, tm), lambda i, j, kk: (kk, i))
    else:
        a_spec = pl.BlockSpec((tm, tk), lambda i, j, kk: (i, kk))
    if mode == 'nt':
        if b_shards:
            per = b.shape[2] // tk
            b_spec = pl.BlockSpec((None, tn, tk), lambda i, j, kk: (kk // per, j, kk % per))
        else:
            b_spec = pl.BlockSpec((tn, tk), lambda i, j, kk: (j, kk))
    else:
        if b_shards:
            per = b.shape[2] // tn
            b_spec = pl.BlockSpec((None, tk, tn), lambda i, j, kk: (j // per, kk, j % per))
        else:
            b_spec = pl.BlockSpec((tk, tn), lambda i, j, kk: (kk, j))
    if out_shards:
        per_o = n // N_DEV // tn
        o_spec = pl.BlockSpec((None, tm, tn), lambda i, j, kk: (j // per_o, i, j % per_o))
        o_shape = jax.ShapeDtypeStruct((N_DEV, m, n // N_DEV), out_dtype)
    else:
        o_spec = pl.BlockSpec((tm, tn), lambda i, j, kk: (i, j))
        o_shape = jax.ShapeDtypeStruct((m, n), out_dtype)
    return pl.pallas_call(
        body, grid=(m // tm, n // tn, nk), in_specs=[a_spec, b_spec], out_specs=o_spec, out_shape=o_shape,
        scratch_shapes=[pltpu.VMEM((tm, tn), F32)],
        compiler_params=_params(("parallel", "parallel", "arbitrary")), name=name,
    )(a, b)


def _row_bytes(shape, dtype):
    dims = list(shape[1:])
    dims[-1] = -(-dims[-1] // LANES) * LANES
    return math.prod(dims) * jnp.dtype(dtype).itemsize


def _row_tile(n, row_bytes):
    t = 1024
    while t > 16 and (n % t or t * row_bytes > ROW_BLOCK_BYTES):
        t //= 2
    assert n % t == 0, (n, t)
    return t


def rowmap(fn, tiled, bcast, out_tiled, out_acc, name):
    n = tiled[0].shape[0]
    tile = _row_tile(n, sum(_row_bytes(t.shape, t.dtype) for t in list(tiled) + list(out_tiled)))
    n_in, n_out = len(tiled) + len(bcast), len(out_tiled)

    def body(*refs):
        outs, accs = fn(*[r[...] for r in refs[:n_in]])
        assert len(outs) == n_out and len(accs) == len(out_acc), name
        for r, v in zip(refs[n_in:n_in + n_out], outs):
            r[...] = v.astype(r.dtype)
        acc_refs = refs[n_in + n_out:]
        if acc_refs:
            @pl.when(pl.program_id(0) == 0)
            def _():
                for r, v in zip(acc_refs, accs):
                    r[...] = v.astype(r.dtype)

            @pl.when(pl.program_id(0) != 0)
            def _():
                for r, v in zip(acc_refs, accs):
                    r[...] += v.astype(r.dtype)

    def tspec(s):
        nd = len(s.shape)
        return pl.BlockSpec((tile,) + tuple(s.shape[1:]), lambda i, nd=nd: (i,) + (0,) * (nd - 1))

    def bspec(s):
        nd = len(s.shape)
        return pl.BlockSpec(tuple(s.shape), lambda i, nd=nd: (0,) * nd)

    res = pl.pallas_call(
        body, grid=(n // tile,),
        in_specs=[tspec(t) for t in tiled] + [bspec(t) for t in bcast],
        out_specs=[tspec(t) for t in out_tiled] + [bspec(t) for t in out_acc],
        out_shape=list(out_tiled) + list(out_acc),
        compiler_params=_params(("arbitrary",)), name=name,
    )(*tiled, *bcast)
    return list(res[:n_out]), list(res[n_out:])


def rowmap_fwd(fwd, tiled, bcast, out_dtypes, name):
    shapes = jax.eval_shape(fwd, *tiled, *bcast)
    out_tiled = [jax.ShapeDtypeStruct(s.shape, d) for s, d in zip(shapes, out_dtypes)]
    outs, _ = rowmap(lambda *blk: (fwd(*[b.astype(F32) for b in blk]), ()), tiled, bcast, out_tiled, [], name)
    return outs


def rowmap_bwd(fwd, tiled, bcast, cts, want, name):
    cts = [[] if c is None else (list(c) if isinstance(c, (list, tuple)) else [c]) for c in cts]
    flat_cts = [c for group in cts for c in group]
    nt_, nc_ = len(tiled), len(flat_cts)

    def fn(*blk):
        ins = [b.astype(F32) for b in blk[:nt_]] + [b.astype(F32) for b in blk[nt_ + nc_:]]
        ctb = list(blk[nt_:nt_ + nc_])
        outs, vjp = jax.vjp(fwd, *ins)
        full = []
        for o, group in zip(outs, cts):
            acc = jnp.zeros_like(o)
            for _ in group:
                acc = acc + ctb.pop(0).astype(F32)
            full.append(acc)
        g = vjp(tuple(full))
        return [g[i] for i in range(nt_) if want[i] is not None], list(g[nt_:])

    out_tiled = [jax.ShapeDtypeStruct(t.shape, w) for t, w in zip(tiled, want) if w is not None]
    out_acc = [jax.ShapeDtypeStruct(b.shape, F32) for b in bcast]
    return rowmap(fn, list(tiled) + flat_cts, bcast, out_tiled, out_acc, name)


def _rms(x, gain):
    return x * lax.rsqrt(jnp.mean(jnp.square(x), axis=-1, keepdims=True) + RMS_EPS) * gain


def _sigmoid(x):
    return 1.0 / (1.0 + jnp.exp(-x))


def _softplus(x):
    return jnp.maximum(x, 0.0) + jnp.log(1.0 + jnp.exp(-jnp.abs(x)))


def st_norm(x, gain):
    return x, _rms(x, gain)


def st_res_norm(x, delta, gain):
    y = x + delta
    return y, _rms(y, gain)


def st_qk_norm(q, k, q_gain, k_gain):
    return _rms(q, q_gain), _rms(k, k_gain)


def st_merge(o0, o1, o2, l0, l1, l2):
    m = jnp.maximum(jnp.maximum(l0, l1), l2)
    e0, e1, e2 = jnp.exp(l0 - m), jnp.exp(l1 - m), jnp.exp(l2 - m)
    return ((e0 * o0 + e1 * o1 + e2 * o2) / (e0 + e1 + e2),)


def _st_rwkv_pre(dot, zr, zk, zv, xw, xa, xg, pr, pk, pv, pw, pa, pg, mr, mk, mv, mw, ma, mg,
                 w0, w_decay, a0, w_aaa, w_gate, k_k, k_a):
    def shift(cur, prev, mix):
        return cur + mix * (prev - cur)

    r, k, v = shift(zr, pr, mr), shift(zk, pk, mk), shift(zv, pv, mv)
    xw, xa, xg = shift(xw, pw, mw), shift(xa, pa, ma), shift(xg, pg, mg)
    w = -_softplus(-(w0 + dot(jnp.tanh(xw), w_decay, 'nn', False))) - 0.5
    a = _sigmoid(a0 + dot(xa, w_aaa, 'nn', False))
    g = dot(_sigmoid(xg), w_gate, 'nn', False)
    log_decay = -jnp.exp(w)
    return r, log_decay, k * (1.0 + (a - 1.0) * k_a), v, k * k_k, a, g


def st_rwkv_kk(kk0, a):
    kk = kk0 / jnp.maximum(jnp.sqrt(jnp.sum(jnp.square(kk0), axis=-1, keepdims=True)), 1e-12)
    return -kk, kk * a


def st_rwkv_post(y, r, k, v, g, gn_w, gn_b, r_k):
    mu = jnp.mean(y, axis=-1, keepdims=True)
    var = jnp.mean(jnp.square(y - mu), axis=-1, keepdims=True)
    out = (y - mu) * lax.rsqrt(var + GN_EPS) * gn_w + gn_b
    out = out + jnp.sum(r * k * r_k, axis=-1, keepdims=True) * v
    return (out * g,)


def st_gate(g0, g1, attn_d, rwkv_d):
    return (_sigmoid(g0) * attn_d + _sigmoid(g1) * rwkv_d,)


def st_relu2(u):
    return (jnp.square(jnp.maximum(u, 0.0)),)


def st_add(a, b):
    return (a + b,)


def loss_head(x2, pg, pp, target, name):
    d_model = x2.shape[1]

    def fn(x2, pg, pp, tgt):
        s = _sigmoid(pg)
        err = x2 + s * pp - tgt
        dy = err * (1.0 / d_model)
        part = 0.5 * jnp.sum(jnp.square(err)) * (1.0 / d_model)
        return [dy, dy * pp * s * (1.0 - s), dy * s], [jnp.full((8, LANES), part, F32)]

    sds = jax.ShapeDtypeStruct
    outs, accs = rowmap(fn, [x2, pg, pp, target], [],
                        [sds(x2.shape, F32), sds(x2.shape, BF16), sds(x2.shape, BF16)], [sds((8, LANES), F32)], name)
    return outs[0], outs[1], outs[2], accs[0][0, 0]


def _attn_block(dot, q, kp, kc, vp, vc, bp, bc, prev_offset):
    blk = q.shape[0]
    qi = lax.broadcasted_iota(I32, (blk, blk), 0)
    ki = lax.broadcasted_iota(I32, (blk, blk), 1)
    mask_c = ki <= qi
    mask_p = ki >= qi + prev_offset
    scale = HEAD_DIM ** -0.5
    s_c = jnp.where(mask_c, dot(q, kc, 'nt', False) * scale + bc, NEG_INF)
    s_p = jnp.where(mask_p, dot(q, kp, 'nt', False) * scale + bp, NEG_INF)
    m = lax.stop_gradient(jnp.maximum(jnp.max(s_c, axis=1, keepdims=True), jnp.max(s_p, axis=1, keepdims=True)))
    e_c = jnp.where(mask_c, jnp.exp(s_c - m), 0.0)
    e_p = jnp.where(mask_p, jnp.exp(s_p - m), 0.0)
    l = jnp.sum(e_c, axis=1, keepdims=True) + jnp.sum(e_p, axis=1, keepdims=True)
    o = (dot(e_c, vc, 'nn', False) + dot(e_p, vp, 'nn', False)) / l
    return o, jnp.broadcast_to(m + jnp.log(l), o.shape)


def _attn_specs(blk, hd):
    cur = pl.BlockSpec((None, None, blk, hd), lambda h, r, n: (h, r, n, 0))
    prev = pl.BlockSpec((None, None, blk, hd), lambda h, r, n: (h, r, jnp.maximum(n - 1, 0), 0))
    bias = pl.BlockSpec((None, blk, blk), lambda h, r, n: (h, 0, 0))
    return cur, prev, bias


def attn_fwd(q, k, v, bp, bc, name):
    hg, d, length, hd = q.shape
    blk = ATTN_BLOCK
    cur, prev, bias = _attn_specs(blk, hd)

    def body(q_ref, kp_ref, kc_ref, vp_ref, vc_ref, bp_ref, bc_ref, o_ref, l_ref):
        off = jnp.where(pl.program_id(2) > 0, 0, blk)
        o, l = _attn_block(_dot, q_ref[...], kp_ref[...], kc_ref[...], vp_ref[...], vc_ref[...], bp_ref[...],
                           bc_ref[...], off)
        o_ref[...] = o
        l_ref[...] = l

    return pl.pallas_call(
        body, grid=(hg, d, length // blk), in_specs=[cur, prev, cur, prev, cur, bias, bias], out_specs=[cur, cur],
        out_shape=[jax.ShapeDtypeStruct(q.shape, F32)] * 2,
        compiler_params=_params(("parallel", "parallel", "parallel")), name=name,
    )(q, k, k, v, v, bp, bc)


def attn_bwd(q, k, v, bp, bc, do, dl, name):
    hg, d, length, hd = q.shape
    blk = ATTN_BLOCK
    nb = length // blk
    cur, prev, bias = _attn_specs(blk, hd)

    def body(q_ref, kp_ref, kc_ref, vp_ref, vc_ref, bp_ref, bc_ref, do_ref, dl_ref,
             dq_ref, dkp_ref, dkc_ref, dvp_ref, dvc_ref, dbp_ref, dbc_ref):
        off = jnp.where(pl.program_id(2) > 0, 0, blk)
        f = functools.partial(_attn_block, _dot_ad, prev_offset=off)
        _, vjp = jax.vjp(f, q_ref[...], kp_ref[...], kc_ref[...], vp_ref[...], vc_ref[...], bp_ref[...], bc_ref[...])
        dq, dkp, dkc, dvp, dvc, dbp, dbc = vjp((do_ref[...], dl_ref[...]))
        dq_ref[...] = dq
        dkp_ref[...] = dkp
        dkc_ref[...] = dkc
        dvp_ref[...] = dvp
        dvc_ref[...] = dvc
        first = jnp.logical_and(pl.program_id(1) == 0, pl.program_id(2) == 0)

        @pl.when(first)
        def _():
            dbp_ref[...] = dbp
            dbc_ref[...] = dbc

        @pl.when(jnp.logical_not(first))
        def _():
            dbp_ref[...] += dbp
            dbc_ref[...] += dbc

    blocks = jax.ShapeDtypeStruct(q.shape, F32)
    dq, dkp, dkc, dvp, dvc, dbp, dbc = pl.pallas_call(
        body, grid=(hg, d, nb), in_specs=[cur, prev, cur, prev, cur, bias, bias, cur, cur],
        out_specs=[cur] * 5 + [bias] * 2, out_shape=[blocks] * 5 + [jax.ShapeDtypeStruct(bp.shape, F32)] * 2,
        compiler_params=_params(("arbitrary", "arbitrary", "arbitrary")), name=name,
    )(q, k, k, v, v, bp, bc, do, dl)

    nxt = pl.BlockSpec((None, None, blk, hd), lambda h, r, n: (h, r, jnp.minimum(n + 1, nb - 1), 0))

    def add_body(kc_ref, kp_ref, vc_ref, vp_ref, dk_ref, dv_ref):
        has_next = (pl.program_id(2) + 1 < nb).astype(F32)
        dk_ref[...] = kc_ref[...] + kp_ref[...] * has_next
        dv_ref[...] = vc_ref[...] + vp_ref[...] * has_next

    dk, dv = pl.pallas_call(
        add_body, grid=(hg, d, nb), in_specs=[cur, nxt, cur, nxt], out_specs=[cur, cur], out_shape=[blocks] * 2,
        compiler_params=_params(("parallel", "parallel", "parallel")), name=name + "_kv",
    )(dkc, dkp, dvc, dvp)
    return dq, dk, dv, dbp, dbc


def _t5_bucket(dist):
    max_exact = N_BUCKETS // 2
    d_f = jnp.maximum(dist, 1).astype(F32)
    large = max_exact + (jnp.log(d_f / max_exact) / math.log(MAX_DISTANCE / max_exact)
                         * (N_BUCKETS - max_exact)).astype(I32)
    large = jnp.minimum(large, N_BUCKETS - 1)
    return jnp.where(dist < max_exact, dist, large)


def _bucket_tables():
    blk = ATTN_BLOCK
    qi = jnp.arange(blk)[:, None]
    ki = jnp.arange(blk)[None, :]
    out = []
    for _, dilation in DILATED_GROUPS:
        rel_p = jnp.maximum(blk + qi - ki, 0) * dilation
        rel_c = jnp.maximum(qi - ki, 0) * dilation
        out.append(jnp.stack([_t5_bucket(rel_p), _t5_bucket(rel_c)]))
    return jnp.stack(out).astype(I32)


def bias_fwd(table, buckets, name):
    blk = ATTN_BLOCK

    def body(tab_ref, bkt_ref, out_ref):
        for g in range(len(DILATED_GROUPS)):
            for half in range(2):
                bk = bkt_ref[g, half]
                for hh in range(HEADS_PER_GROUP):
                    h = g * HEADS_PER_GROUP + hh
                    acc = jnp.zeros((blk, blk), F32)
                    for b in range(N_BUCKETS):
                        acc = jnp.where(bk == b, tab_ref[b, h], acc)
                    out_ref[h, half] = acc

    return pl.pallas_call(
        body, in_specs=[pl.BlockSpec(memory_space=pltpu.SMEM), pl.BlockSpec(memory_space=pltpu.VMEM)],
        out_specs=pl.BlockSpec(memory_space=pltpu.VMEM),
        out_shape=jax.ShapeDtypeStruct((N_HEADS, 2, blk, blk), F32), name=name,
    )(table, buckets)


def bias_bwd(dbias, buckets, name):
    def body(db_ref, bkt_ref, out_ref):
        rows = lax.broadcasted_iota(I32, (N_BUCKETS, LANES), 0)
        cols = lax.broadcasted_iota(I32, (N_BUCKETS, LANES), 1)
        acc = jnp.zeros((N_BUCKETS, LANES), F32)
        for g in range(len(DILATED_GROUPS)):
            bk_p, bk_c = bkt_ref[g, 0], bkt_ref[g, 1]
            for hh in range(HEADS_PER_GROUP):
                h = g * HEADS_PER_GROUP + hh
                d_p, d_c = db_ref[h, 0], db_ref[h, 1]
                for b in range(N_BUCKETS):
                    s = jnp.sum(jnp.where(bk_p == b, d_p, 0.0)) + jnp.sum(jnp.where(bk_c == b, d_c, 0.0))
                    acc = jnp.where(jnp.logical_and(rows == b, cols == h), s, acc)
        out_ref[...] = acc

    return pl.pallas_call(
        body, in_specs=[pl.BlockSpec(memory_space=pltpu.VMEM)] * 2, out_specs=pl.BlockSpec(memory_space=pltpu.VMEM),
        out_shape=jax.ShapeDtypeStruct((N_BUCKETS, LANES), F32), name=name,
    )(dbias, buckets)


def _rwkv_chunk(dot, s0, r, lw, k, v, a, b):
    def each(f, *lists):
        return [f(*xs) for xs in zip(*lists)]

    def mm(mode):
        return lambda p, q: dot(p, q, mode, True)

    def mul(p, q):
        return p * q

    def add(p, q):
        return p + q

    c = r[0].shape[0]
    ti = lax.broadcasted_iota(I32, (c, c), 0)
    si = lax.broadcasted_iota(I32, (c, c), 1)
    incl = si <= ti
    strict = si < ti
    ones_incl = incl.astype(F32)
    cum = each(lambda x: dot(ones_incl, x, 'nn', True), lw)
    w_incl = each(jnp.exp, cum)
    w_prev = each(lambda cu, x: jnp.exp(cu - x), cum, lw)
    w_inv = each(lambda cu: jnp.exp(-cu), cum)
    w_end = each(lambda x: jnp.exp(jnp.sum(x, axis=0, keepdims=True)), lw)
    a_t, r_t, b_t, k_t = each(mul, a, w_prev), each(mul, r, w_incl), each(mul, b, w_inv), each(mul, k, w_inv)
    l_ab = each(lambda p, q: jnp.where(strict, dot(p, q, 'nt', True), 0.0), a_t, b_t)
    l_ak = each(lambda p, q: jnp.where(strict, dot(p, q, 'nt', True), 0.0), a_t, k_t)
    u = each(add, each(mm('nt'), a_t, s0), each(mm('nn'), l_ak, v))
    u = each(add, u, each(mm('nn'), l_ab, u))
    power = l_ab
    for _ in range(int(math.log2(c)) - 1):
        power = each(mm('nn'), power, power)
        u = each(add, u, each(mm('nn'), power, u))
    m_rb = each(lambda p, q: jnp.where(incl, dot(p, q, 'nt', True), 0.0), r_t, b_t)
    m_rk = each(lambda p, q: jnp.where(incl, dot(p, q, 'nt', True), 0.0), r_t, k_t)
    y = each(add, each(add, each(mm('nt'), r_t, s0), each(mm('nn'), m_rb, u)), each(mm('nn'), m_rk, v))
    s1 = each(add, each(add, s0, each(mm('tn'), u, b_t)), each(mm('tn'), v, k_t))
    return y, each(mul, s1, w_end)


def rwkv_fwd(r, lw, k, v, a, b, name, gather=()):
    h, t, n = r.shape
    c = RWKV_CHUNK
    nc = t // c
    hs = math.gcd(h, RWKV_HEADS_PER_STEP)
    steps = h // hs * nc
    ng = len(gather)
    row = pl.BlockSpec((hs, c, n), lambda i, j: (i, j, 0))

    def body(*refs):
        ins, x_refs = refs[:6], refs[6:6 + ng]
        y_ref, s0_ref = refs[6 + ng:8 + ng]
        out_refs, state, sems = refs[8 + ng:8 + 2 * ng], refs[8 + 2 * ng], refs[9 + 2 * ng:]
        step = pl.program_id(0) * nc + pl.program_id(1)
        if ng:
            start, forward, finish = _gather_schedule(x_refs, out_refs, *sems)
            pl.when(step == 0)(start)
            pl.when(step == steps // 2)(forward)

        @pl.when(pl.program_id(1) == 0)
        def _():
            state[...] = jnp.zeros_like(state)

        s0 = [state[q] for q in range(hs)]
        y, s1 = _rwkv_chunk(_dot, s0, *[[ref[q] for q in range(hs)] for ref in ins])
        for q in range(hs):
            s0_ref[q] = s0[q]
            y_ref[q] = y[q]
            state[q] = s1[q]
        if ng:
            pl.when(step == steps - 1)(finish)

    return pl.pallas_call(
        body, grid=(h // hs, nc), in_specs=[row] * 6 + [_ANY] * ng,
        out_specs=[row, pl.BlockSpec((hs, None, n, n), lambda i, j: (i, j, 0, 0))] + [_ANY] * ng,
        out_shape=[jax.ShapeDtypeStruct((h, t, n), F32), jax.ShapeDtypeStruct((h, nc, n, n), F32)]
        + [jax.ShapeDtypeStruct((N_DEV,) + g.shape, g.dtype) for g in gather],
        scratch_shapes=[pltpu.VMEM((hs, n, n), F32)] + (_exchange_scratch(ng) if ng else []),
        compiler_params=_params(("arbitrary", "arbitrary")), name=name,
    )(r, lw, k, v, a, b, *gather)


def rwkv_bwd(r, lw, k, v, a, b, s0, dy, name, scatter=()):
    h, t, n = r.shape
    c = RWKV_CHUNK
    nc = t // c
    hs = math.gcd(h, RWKV_HEADS_PER_STEP)
    steps = h // hs * nc
    ns = len(scatter)
    row = pl.BlockSpec((hs, c, n), lambda i, j: (i, nc - 1 - j, 0))
    st = pl.BlockSpec((hs, None, n, n), lambda i, j: (i, nc - 1 - j, 0, 0))

    def body(*refs):
        ins, s0_ref, dy_ref, send_refs = refs[:6], refs[6], refs[7], refs[8:8 + ns]
        grad_refs, recv_refs = refs[8 + ns:14 + ns], refs[14 + ns:14 + 2 * ns]
        dstate, sems = refs[14 + 2 * ns], refs[15 + 2 * ns:]
        step = pl.program_id(0) * nc + pl.program_id(1)
        if ns:
            start, finish = _scatter_schedule(send_refs, recv_refs, *sems)
            pl.when(step == 0)(start)

        @pl.when(pl.program_id(1) == 0)
        def _():
            dstate[...] = jnp.zeros_like(dstate)

        per_head = range(hs)
        _, vjp = jax.vjp(functools.partial(_rwkv_chunk, _dot_ad), [s0_ref[q] for q in per_head],
                         *[[ref[q] for q in per_head] for ref in ins])
        grads = vjp(([dy_ref[q] for q in per_head], [dstate[q] for q in per_head]))
        for q in per_head:
            dstate[q] = grads[0][q]
            for ref, g in zip(grad_refs, grads[1:]):
                ref[q] = g[q]
        if ns:
            pl.when(step == steps - 1)(finish)

    return pl.pallas_call(
        body, grid=(h // hs, nc), in_specs=[row] * 6 + [st, row] + [_ANY] * ns, out_specs=[row] * 6 + [_ANY] * ns,
        out_shape=[jax.ShapeDtypeStruct((h, t, n), F32)] * 6 + [jax.ShapeDtypeStruct(s.shape, s.dtype) for s in scatter],
        scratch_shapes=[pltpu.VMEM((hs, n, n), F32)] + (_exchange_scratch(ns) if ns else []),
        compiler_params=_params(("arbitrary", "arbitrary")), name=name,
    )(r, lw, k, v, a, b, s0, dy, *scatter)


_ANY = pl.BlockSpec(memory_space=pl.ANY)


def _exchange_scratch(n_arrays):
    return [pltpu.SemaphoreType.DMA((n_arrays, N_DEV - 1)), pltpu.SemaphoreType.DMA((n_arrays, N_DEV - 1)),
            pltpu.SemaphoreType.DMA((n_arrays,))]


def _gather_schedule(x_refs, out_refs, send_sems, recv_sems, local_sems):
    x, y, c = lax.axis_index("x"), lax.axis_index("y"), lax.axis_index("c")
    me, sibling = (x, y, c), (x, y, 1 - c)
    chips = [(1 - x, y), (x, 1 - y), (1 - x, 1 - y)]
    arrays = range(len(x_refs))

    def slot(a, pos):
        return out_refs[a].at[4 * pos[0] + 2 * pos[1] + pos[2]]

    def copy(a, i, block, to, src=None):
        return pltpu.make_async_remote_copy(
            src_ref=slot(a, block) if src is None else src, dst_ref=slot(a, block), send_sem=send_sems.at[a, i],
            recv_sem=recv_sems.at[a, i], device_id=to, device_id_type=pl.DeviceIdType.MESH)

    mine = [pltpu.make_async_copy(x_refs[a], slot(a, me), local_sems.at[a]) for a in arrays]
    first = [[copy(a, 0, me, sibling, src=x_refs[a])]
             + [copy(a, 1 + j, me, (*chip, c), src=x_refs[a]) for j, chip in enumerate(chips)] for a in arrays]
    passed = [[copy(a, 4 + j, (*chip, c), sibling) for j, chip in enumerate(chips)] for a in arrays]

    def start():
        for a in arrays:
            mine[a].start()
            for cp in first[a]:
                cp.start()

    def forward():
        for j, chip in enumerate(chips):
            for a in arrays:
                copy(a, 1 + j, (*chip, c), me).wait_recv()
                passed[a][j].start()

    def finish():
        for a in arrays:
            copy(a, 0, sibling, me).wait_recv()
            for j, chip in enumerate(chips):
                copy(a, 4 + j, (*chip, 1 - c), me).wait_recv()
            for cp in first[a] + passed[a]:
                cp.wait_send()
            mine[a].wait()

    return start, forward, finish


def _scatter_schedule(in_refs, out_refs, send_sems, recv_sems, local_sems):
    x, y, c = lax.axis_index("x"), lax.axis_index("y"), lax.axis_index("c")
    my_chip = 2 * x + y
    mine, remote = [], []
    for a, (src, dst) in enumerate(zip(in_refs, out_refs)):
        mine.append(pltpu.make_async_copy(src.at[my_chip], dst.at[my_chip], local_sems.at[a]))
        for i in range(1, N_CHIPS):
            px, py = x ^ (i >> 1), y ^ (i & 1)
            remote.append(pltpu.make_async_remote_copy(
                src_ref=src.at[2 * px + py], dst_ref=dst.at[my_chip], send_sem=send_sems.at[a, i - 1],
                recv_sem=recv_sems.at[a, i - 1], device_id=(px, py, c), device_id_type=pl.DeviceIdType.MESH))

    def start():
        for cp in mine + remote:
            cp.start()

    def finish():
        for cp in remote:
            cp.wait_recv()
        for cp in remote:
            cp.wait_send()
        for cp in mine:
            cp.wait()

    return start, finish


def pair_exchange(parts, name):
    n = len(parts)

    def body(*refs):
        x, y, c = lax.axis_index("x"), lax.axis_index("y"), lax.axis_index("c")
        send_sems, recv_sems = refs[2 * n:]
        copies = [pltpu.make_async_remote_copy(
            src_ref=refs[a].at[q, 1 - c], dst_ref=refs[n + a].at[q], send_sem=send_sems.at[a, q],
            recv_sem=recv_sems.at[a, q], device_id=(x, y, 1 - c), device_id_type=pl.DeviceIdType.MESH)
            for a in range(n) for q in range(N_CHIPS)]
        for cp in copies:
            cp.start()
        for cp in copies:
            cp.wait_recv()
        for cp in copies:
            cp.wait_send()

    return pl.pallas_call(
        body, in_specs=[_ANY] * n, out_specs=[_ANY] * n,
        out_shape=[jax.ShapeDtypeStruct((N_CHIPS,) + s.shape[2:], s.dtype) for s in parts],
        scratch_shapes=[pltpu.SemaphoreType.DMA((n, N_CHIPS)), pltpu.SemaphoreType.DMA((n, N_CHIPS))], name=name,
    )(*parts)


def pair_add(mine, theirs, core, name):
    _, _, k, n = mine.shape
    tc = _pick(n, (2048, 1024, 512))
    tr = _row_tile(k, tc * 3 * jnp.dtype(mine.dtype).itemsize)

    def body(core_ref, a_ref, b_ref, o_ref):
        o_ref[...] = (a_ref[...].astype(F32) + b_ref[...].astype(F32)).astype(o_ref.dtype)

    one = pl.BlockSpec((None, tr, tc), lambda q, i, j, core_ref: (q, i, j))
    grid_spec = pltpu.PrefetchScalarGridSpec(
        num_scalar_prefetch=1, grid=(N_CHIPS, k // tr, n // tc),
        in_specs=[pl.BlockSpec((None, None, tr, tc), lambda q, i, j, core_ref: (q, core_ref[0], i, j)), one],
        out_specs=one)
    return pl.pallas_call(
        body, grid_spec=grid_spec, out_shape=jax.ShapeDtypeStruct(theirs.shape, BF16),
        compiler_params=_params(("parallel", "parallel", "parallel")), name=name,
    )(core, mine, theirs)


def all_gather_many(shards, name):
    n = len(shards)

    def body(*refs):
        start, forward, finish = _gather_schedule(refs[:n], refs[n:2 * n], *refs[2 * n:])
        start()
        forward()
        finish()

    return pl.pallas_call(
        body, in_specs=[_ANY] * n, out_specs=[_ANY] * n,
        out_shape=[jax.ShapeDtypeStruct((N_DEV,) + s.shape, s.dtype) for s in shards],
        scratch_shapes=_exchange_scratch(n), name=name,
    )(*shards)


def scatter_many(parts, name):
    n = len(parts)

    def body(*refs):
        start, finish = _scatter_schedule(refs[:n], refs[n:2 * n], *refs[2 * n:])
        start()
        finish()

    return pl.pallas_call(
        body, in_specs=[_ANY] * n, out_specs=[_ANY] * n,
        out_shape=[jax.ShapeDtypeStruct(s.shape, s.dtype) for s in parts],
        scratch_shapes=_exchange_scratch(n), name=name,
    )(*parts)


def scan_and_gather(scan_in, shards):
    y, s0, *gathered = rwkv_fwd(*scan_in, name="rwkv_scan", gather=shards)
    return y, s0, gathered


def scan_bwd_and_scatter(scan_in, s0, dy, parts):
    res = rwkv_bwd(*scan_in, s0, dy, name="rwkv_scan_bwd", scatter=parts)
    return res[:6], res[6:]


def adamw_shard(parts, w, m, v, name):
    _, k, n = w.shape
    slots = parts.shape[0]
    tc = _pick(n, (2048, 1024, 512))
    tr = _row_tile(k, tc * (slots * jnp.dtype(parts.dtype).itemsize + 7 * 4))

    def body(p_ref, w_ref, m_ref, v_ref, g_ref, d_ref, nm_ref, nv_ref):
        _adamw_block(p_ref, w_ref, m_ref, v_ref, g_ref, d_ref, nm_ref, nv_ref)

    one = pl.BlockSpec((None, tr, tc), lambda i, j: (0, i, j))
    return pl.pallas_call(
        body, grid=(k // tr, n // tc), in_specs=[pl.BlockSpec((slots, tr, tc), lambda i, j: (0, i, j))] + [one] * 3,
        out_specs=[one] * 4, out_shape=[jax.ShapeDtypeStruct(w.shape, F32)] * 4,
        compiler_params=_params(("parallel", "parallel")), name=name,
    )(parts, w, m, v)


def _adamw_block(p_ref, w_ref, m_ref, v_ref, g_ref, d_ref, nm_ref, nv_ref):
    g = p_ref[0].astype(F32)
    for j in range(1, p_ref.shape[0]):
        g = g + p_ref[j].astype(F32)
    new_m = ADAM_B1 * m_ref[...] + (1.0 - ADAM_B1) * g
    new_v = ADAM_B2 * v_ref[...] + (1.0 - ADAM_B2) * jnp.square(g)
    m_hat = new_m / (1.0 - ADAM_B1 ** ADAM_STEP)
    v_hat = new_v / (1.0 - ADAM_B2 ** ADAM_STEP)
    g_ref[...] = g
    d_ref[...] = -ADAM_LR * (m_hat / (jnp.sqrt(v_hat) + ADAM_EPS) + ADAM_WD * w_ref[...])
    nm_ref[...] = new_m
    nv_ref[...] = new_v


def adamw(parts, w, m, v, name):
    rows = w.shape[0]
    tile = _row_tile(rows, N_DEV * LANES * jnp.dtype(parts.dtype).itemsize + 7 * LANES * 4)

    def body(p_ref, w_ref, m_ref, v_ref, g_ref, d_ref, nm_ref, nv_ref):
        _adamw_block(p_ref, w_ref, m_ref, v_ref, g_ref, d_ref, nm_ref, nv_ref)

    flat = pl.BlockSpec((tile, LANES), lambda i: (i, 0))
    return pl.pallas_call(
        body, grid=(rows // tile,), in_specs=[pl.BlockSpec((N_DEV, tile, LANES), lambda i: (0, i, 0))] + [flat] * 3,
        out_specs=[flat] * 4, out_shape=[jax.ShapeDtypeStruct(w.shape, F32)] * 4,
        compiler_params=_params(("parallel",)), name=name,
    )(parts, w, m, v)


def _part_rows(n_elems):
    return -(-n_elems // (PACK_ROWS * LANES)) * PACK_ROWS


def _pack(arrays, dtype, lead=()):
    parts, layout, off = [], [], 0
    for arr in arrays:
        n = math.prod(arr.shape[len(lead):])
        rows = _part_rows(n)
        flat = arr.reshape(lead + (n,)).astype(dtype)
        flat = jnp.pad(flat, [(0, 0)] * len(lead) + [(0, rows * LANES - n)])
        parts.append(flat.reshape(lead + (rows, LANES)))
        layout.append((off, rows))
        off += rows
    total = -(-off // 1024) * 1024
    if total > off:
        parts.append(jnp.zeros(lead + (total - off, LANES), dtype))
    return jnp.concatenate(parts, axis=len(lead)), layout


def _unpack(buf, layout, shapes, lead=()):
    out = []
    for (off, rows), shape in zip(layout, shapes):
        n = math.prod(shape)
        piece = lax.slice_in_dim(buf, off, off + rows, axis=len(lead))
        out.append(piece.reshape(lead + (rows * LANES,))[..., :n].reshape(lead + tuple(shape)))
    return out


def _split_shards(full, axis):
    if axis == 0:
        return full.reshape((N_DEV, full.shape[0] // N_DEV, full.shape[1]))
    return full.reshape((full.shape[0], N_DEV, full.shape[1] // N_DEV)).transpose(1, 0, 2)


def _join_shards(shards, axis):
    if axis == 0:
        return shards.reshape((-1, shards.shape[2]))
    return shards.transpose(1, 0, 2).reshape((shards.shape[1], -1))


def _shift_down(t):
    return jnp.pad(t, ((1, 0), (0, 0)))[:-1]


def _shift_up(t):
    return jnp.pad(t, ((0, 1), (0, 0)))[1:]


def kernel(x, p, norm_mix, w_in, q_gain, k_gain, rel_bias, w_attn_up, shift_mix, w0, w_decay_up, a0, w_aaa_up, w_gate_up, k_k, k_a, r_k, gn_w, gn_b, w_rwkv_up, w_out, norm_mlp, w_mlp_in, w_mlp_out, norm_ple, w_ple_gate, w_ple_proj, loss_target, m_norm_mix, m_w_in, m_q_gain, m_k_gain, m_rel_bias, m_w_attn_up, m_shift_mix, m_w0, m_w_decay_up, m_a0, m_w_aaa_up, m_w_gate_up, m_k_k, m_k_a, m_r_k, m_gn_w, m_gn_b, m_w_rwkv_up, m_w_out, m_norm_mlp, m_w_mlp_in, m_w_mlp_out, m_norm_ple, m_w_ple_gate, m_w_ple_proj, v_norm_mix, v_w_in, v_q_gain, v_k_gain, v_rel_bias, v_w_attn_up, v_shift_mix, v_w0, v_w_decay_up, v_a0, v_w_aaa_up, v_w_gate_up, v_k_k, v_k_a, v_r_k, v_gn_w, v_gn_b, v_w_rwkv_up, v_w_out, v_norm_mlp, v_w_mlp_in, v_w_mlp_out, v_norm_ple, v_w_ple_gate, v_w_ple_proj):
    given = dict(locals())
    xs = x[0]
    t_len, d_model = xs.shape
    target = loss_target[0]
    p_bf = p[0, 0].astype(BF16)
    rw_width = w0.shape[1]
    n_rheads = rw_width // RWKV_HEAD
    lora_d, lora_a, lora_g = w_decay_up.shape[1], w_aaa_up.shape[1], w_gate_up.shape[1]
    z_width = shift_mix.shape[1]
    z_pad = -(-z_width // LANES) * LANES
    qkv_width = 3 * ATTN_WIDTH
    assert z_width == 3 * rw_width + lora_d + lora_a + lora_g
    assert N_DEV * w_in.shape[2] == qkv_width + z_width + 2 * d_model
    for window, dilation in DILATED_GROUPS:
        assert window // dilation == ATTN_BLOCK and t_len % (dilation * ATTN_BLOCK) == 0

    shard_bf = {n: given[n][0].astype(BF16) for n in BIG}
    early = ['w_in', 'w_decay_up', 'w_aaa_up', 'w_gate_up']
    late = [n for n in BIG if n not in early]
    full = {n: _join_shards(g, 1) for n, g in zip(early, all_gather_many([shard_bf[n] for n in early], "gather_w_in"))}
    w_qkv = full['w_in'][:, :qkv_width]
    w_z = jnp.pad(full['w_in'][:, qkv_width:qkv_width + z_width], ((0, 0), (0, z_pad - z_width)))
    w_g = full['w_in'][:, qkv_width + z_width:]

    (h_in,) = rowmap_fwd(lambda a, g: st_norm(a, g)[1:], [xs], [norm_mix], [BF16], "norm_in")
    qkv = matmul(h_in, w_qkv, 'nn', F32, "proj_qkv")
    z = matmul(h_in, w_z, 'nn', F32, "proj_z")
    gates = matmul(h_in, w_g, 'nn', F32, "proj_gates")
    gate_a, gate_r = gates[:, :d_model], gates[:, d_model:]

    q_raw = qkv[:, :ATTN_WIDTH].reshape(t_len * N_HEADS, HEAD_DIM)
    k_raw = qkv[:, ATTN_WIDTH:2 * ATTN_WIDTH].reshape(t_len * N_HEADS, HEAD_DIM)
    v_att = qkv[:, 2 * ATTN_WIDTH:]
    q_n, k_n = rowmap_fwd(st_qk_norm, [q_raw, k_raw], [q_gain, k_gain], [F32, F32], "qk_norm")
    buckets = _bucket_tables()
    bias = bias_fwd(rel_bias, buckets, "attn_bias")

    def to_blocks(tok, g):
        dil = DILATED_GROUPS[g][1]
        tok = tok.reshape(t_len, N_HEADS, HEAD_DIM)[:, g * HEADS_PER_GROUP:(g + 1) * HEADS_PER_GROUP]
        return tok.reshape(t_len // dil, dil, HEADS_PER_GROUP, HEAD_DIM).transpose(2, 1, 0, 3)

    def from_blocks(blk):
        return blk.transpose(2, 1, 0, 3).reshape(t_len, ATTN_OUT)

    att_in, att_o, att_l = [], [], []
    for g in range(len(DILATED_GROUPS)):
        hs = slice(g * HEADS_PER_GROUP, (g + 1) * HEADS_PER_GROUP)
        ops = (to_blocks(q_n, g), to_blocks(k_n, g), to_blocks(v_att, g), bias[hs, 0], bias[hs, 1])
        o_g, l_g = attn_fwd(*ops, name=f"attn_fwd_{g}")
        att_in.append(ops)
        att_o.append(from_blocks(o_g))
        att_l.append(from_blocks(l_g))
    (attn,) = rowmap_fwd(st_merge, att_o + att_l, [], [BF16], "attn_merge")

    c0 = rw_width
    cuts = [0, c0, 2 * c0, 3 * c0, 3 * c0 + lora_d, 3 * c0 + lora_d + lora_a, z_width]
    z_parts = [z[:, lo:hi] for lo, hi in zip(cuts[:-1], cuts[1:])]
    z_prev = [_shift_down(t) for t in z_parts]
    mixes = [shift_mix[:, lo:hi] for lo, hi in zip(cuts[:-1], cuts[1:])]
    pre_params = mixes + [w0, full['w_decay_up'], a0, full['w_aaa_up'], full['w_gate_up'], k_k, k_a]
    pre_out = rowmap_fwd(functools.partial(_st_rwkv_pre, _dot), z_parts + z_prev, pre_params, [F32] * 7, "rwkv_pre")
    r_s, lw_s, k_s, v_s, kk0_s, a_s, g_s = pre_out

    def heads(tok):
        return tok.reshape(t_len, n_rheads, RWKV_HEAD)

    def head_major(tok):
        return heads(tok).transpose(1, 0, 2)

    aa_s, bb_s = rowmap_fwd(st_rwkv_kk, [heads(kk0_s), heads(a_s)], [], [F32, F32], "rwkv_kk")
    scan_in = [head_major(r_s), head_major(lw_s), head_major(k_s), head_major(v_s),
               aa_s.transpose(1, 0, 2), bb_s.transpose(1, 0, 2)]
    y_h, s0_h, late_gathered = scan_and_gather(scan_in, [shard_bf[n] for n in late])
    wt = {n: g if SHARD_AXIS[n] == 1 else g.reshape(-1, g.shape[2]) for n, g in zip(late, late_gathered)}
    y_s = y_h.transpose(1, 0, 2)
    post_params = [gn_w.reshape(1, n_rheads, RWKV_HEAD), gn_b.reshape(1, n_rheads, RWKV_HEAD), r_k]
    post_in = [y_s, heads(r_s), heads(k_s), heads(v_s), heads(g_s)]
    (rw,) = rowmap_fwd(st_rwkv_post, post_in, post_params, [BF16], "rwkv_post")
    rw = rw.reshape(t_len, rw_width)
    attn_d = matmul(attn, wt['w_attn_up'], 'nn', F32, "attn_up", b_shards=True)
    rwkv_d = matmul(rw, wt['w_rwkv_up'], 'nn', F32, "rwkv_up", b_shards=True)

    (merged,) = rowmap_fwd(st_gate, [gate_a, gate_r, attn_d, rwkv_d], [], [BF16], "gate_merge")
    mix_out = matmul(merged, wt['w_out'], 'nn', F32, "out_proj")
    x1, h_mlp = rowmap_fwd(st_res_norm, [xs, mix_out], [norm_mlp], [F32, BF16], "res_norm_mlp")
    u = matmul(h_mlp, wt['w_mlp_in'], 'nn', F32, "mlp_in", b_shards=True)
    (act,) = rowmap_fwd(st_relu2, [u], [], [BF16], "mlp_act")
    mlp_out = matmul(act, wt['w_mlp_out'], 'nn', F32, "mlp_out")
    x2, h_ple = rowmap_fwd(st_res_norm, [x1, mlp_out], [norm_ple], [F32, BF16], "res_norm_ple")
    pg = matmul(h_ple, wt['w_ple_gate'], 'nn', F32, "ple_gate")
    pp = matmul(p_bf, wt['w_ple_proj'], 'nn', F32, "ple_proj", b_shards=True)
    dy, d_pg, d_pp, loss_local = loss_head(x2, pg, pp, target, "loss_head")

    def row_cut(full_grad):
        return full_grad.reshape(N_DEV, full_grad.shape[0] // N_DEV, full_grad.shape[1])

    grads, sends = {}, {}
    sends['w_ple_gate'] = row_cut(matmul(h_ple, d_pg, 'tn', BF16, "d_w_ple_gate"))
    sends['w_ple_proj'] = matmul(p_bf, d_pp, 'tn', BF16, "d_w_ple_proj", out_shards=True)
    d_h_ple = matmul(d_pg, wt['w_ple_gate'], 'nt', F32, "d_h_ple")
    (d_x2, d_x2_bf), (grads['norm_ple'],) = rowmap_bwd(
        st_res_norm, [x1, mlp_out], [norm_ple], [dy, d_h_ple], [F32, BF16], "res_norm_ple_bwd")
    sends['w_mlp_out'] = row_cut(matmul(act, d_x2_bf, 'tn', BF16, "d_w_mlp_out"))
    d_act = matmul(d_x2_bf, wt['w_mlp_out'], 'nt', F32, "d_act")
    (d_u,), _ = rowmap_bwd(st_relu2, [u], [], [d_act], [BF16], "mlp_act_bwd")
    sends['w_mlp_in'] = matmul(h_mlp, d_u, 'tn', BF16, "d_w_mlp_in", out_shards=True)
    d_h_mlp = matmul(d_u, wt['w_mlp_in'], 'nt', F32, "d_h_mlp", b_shards=True)
    (d_x1, d_x1_bf), (grads['norm_mlp'],) = rowmap_bwd(
        st_res_norm, [xs, mix_out], [norm_mlp], [d_x2, d_h_mlp], [F32, BF16], "res_norm_mlp_bwd")

    sends['w_out'] = row_cut(matmul(merged, d_x1_bf, 'tn', BF16, "d_w_out"))
    d_merged = matmul(d_x1_bf, wt['w_out'], 'nt', F32, "d_merged")
    (d_gate_a, d_gate_r, d_attn_d, d_rwkv_d), _ = rowmap_bwd(
        st_gate, [gate_a, gate_r, attn_d, rwkv_d], [], [d_merged], [BF16] * 4, "gate_merge_bwd")
    sends['w_attn_up'] = matmul(attn, d_attn_d, 'tn', BF16, "d_w_attn_up", out_shards=True)
    sends['w_rwkv_up'] = matmul(rw, d_rwkv_d, 'tn', BF16, "d_w_rwkv_up", out_shards=True)
    d_attn = matmul(d_attn_d, wt['w_attn_up'], 'nt', F32, "d_attn", b_shards=True)
    d_rw = matmul(d_rwkv_d, wt['w_rwkv_up'], 'nt', F32, "d_rw", b_shards=True)

    (d_y, d_r1, d_k1, d_v1, d_g), (d_gn_w, d_gn_b, grads['r_k']) = rowmap_bwd(
        st_rwkv_post, post_in, post_params, [heads(d_rw)], [F32] * 5, "rwkv_post_bwd")
    grads['gn_w'], grads['gn_b'] = d_gn_w.reshape(1, rw_width), d_gn_b.reshape(1, rw_width)
    core = lax.axis_index("c").astype(I32).reshape(1)

    def chip_partials(names, tag):
        mine = [sends[n].reshape((N_CHIPS, 2) + sends[n].shape[1:]) for n in names]
        theirs = pair_exchange(mine, "pair_grads_" + tag)
        return [pair_add(m, t, core, "pair_add_" + n) for n, m, t in zip(names, mine, theirs)]

    scan_grads, late_received = scan_bwd_and_scatter(scan_in, s0_h, d_y.transpose(1, 0, 2),
                                                     chip_partials(late, "late"))
    received = dict(zip(late, late_received))
    d_r2, d_lw, d_k2, d_v2, d_aa, d_bb = [t.transpose(1, 0, 2) for t in scan_grads]
    (d_kk0, d_a), _ = rowmap_bwd(st_rwkv_kk, [heads(kk0_s), heads(a_s)], [], [d_aa, d_bb], [F32, F32], "rwkv_kk_bwd")

    def flat(tok):
        return tok.reshape(t_len, rw_width)

    pre_cts = [[flat(d_r1), flat(d_r2)], flat(d_lw), [flat(d_k1), flat(d_k2)], [flat(d_v1), flat(d_v2)],
               flat(d_kk0), flat(d_a), flat(d_g)]
    d_zp, d_pre = rowmap_bwd(functools.partial(_st_rwkv_pre, _dot_ad), z_parts + z_prev, pre_params, pre_cts,
                             [F32] * 12, "rwkv_pre_bwd")
    grads['shift_mix'] = jnp.concatenate(d_pre[:6], axis=1)
    grads['w0'], d_w_decay, grads['a0'], d_w_aaa, d_w_gate, grads['k_k'], grads['k_a'] = d_pre[6:]
    for name, full_grad in (('w_decay_up', d_w_decay), ('w_aaa_up', d_w_aaa), ('w_gate_up', d_w_gate)):
        sends[name] = _split_shards(full_grad, 1).astype(BF16)
    z_fill = [jnp.zeros((t_len, z_pad - z_width), F32)] if z_pad > z_width else []
    d_z_cur = jnp.concatenate(d_zp[:6] + z_fill, axis=1)
    d_z_prev = _shift_up(jnp.concatenate(d_zp[6:] + z_fill, axis=1))
    (d_z,) = rowmap_fwd(st_add, [d_z_cur, d_z_prev], [], [BF16], "d_z_sum")

    d_merge, _ = rowmap_bwd(st_merge, att_o + att_l, [], [d_attn], [F32] * 6, "attn_merge_bwd")
    d_qn, d_kn, d_vs, d_bias = [], [], [], []
    for g in range(len(DILATED_GROUPS)):
        tok_g = functools.partial(to_blocks_grad, t_len=t_len, dil=DILATED_GROUPS[g][1])
        dq, dk, dv, dbp, dbc = attn_bwd(*att_in[g], tok_g(d_merge[g]), tok_g(d_merge[3 + g]), name=f"attn_bwd_{g}")
        d_qn.append(from_blocks(dq))
        d_kn.append(from_blocks(dk))
        d_vs.append(from_blocks(dv))
        d_bias.append(jnp.stack([dbp, dbc], axis=1))
    d_table = bias_bwd(jnp.concatenate(d_bias, axis=0), buckets, "attn_bias_bwd")
    grads['rel_bias'] = d_table[:, :N_HEADS]
    d_qn = jnp.concatenate(d_qn, axis=1).reshape(t_len * N_HEADS, HEAD_DIM)
    d_kn = jnp.concatenate(d_kn, axis=1).reshape(t_len * N_HEADS, HEAD_DIM)
    (d_q, d_k), (grads['q_gain'], grads['k_gain']) = rowmap_bwd(
        st_qk_norm, [q_raw, k_raw], [q_gain, k_gain], [d_qn, d_kn], [BF16, BF16], "qk_norm_bwd")
    d_qkv = jnp.concatenate([d_q.reshape(t_len, ATTN_WIDTH), d_k.reshape(t_len, ATTN_WIDTH)]
                            + [t.astype(BF16) for t in d_vs], axis=1)

    d_gates = jnp.concatenate([d_gate_a, d_gate_r], axis=1)
    sends['w_in'] = _split_shards(jnp.concatenate([
        matmul(h_in, d_qkv, 'tn', BF16, "d_w_qkv"),
        matmul(h_in, d_z, 'tn', BF16, "d_w_z")[:, :z_width],
        matmul(h_in, d_gates, 'tn', BF16, "d_w_gates")], axis=1), 1)
    d_h_in = [matmul(d_qkv, w_qkv, 'nt', F32, "d_h_qkv"), matmul(d_z, w_z, 'nt', F32, "d_h_z"),
              matmul(d_gates, w_g, 'nt', F32, "d_h_gates")]
    (grad_x,), (grads['norm_mix'],) = rowmap_bwd(st_norm, [xs], [norm_mix], [d_x1, d_h_in], [F32], "norm_in_bwd")

    received.update(zip(early, scatter_many(chip_partials(early, "early"), "scatter_grads_w_in")))
    by_name = {n: adamw_shard(received[n], given[n], given['m_' + n], given['v_' + n], "adamw_" + n) for n in BIG}
    pk = lambda prefix: _pack([given[prefix + n] for n in SMALL], F32)[0]
    small_buf, small_layout = _pack([grads[n].reshape(given[n].shape) for n in SMALL], F32)
    (small_all,) = all_gather_many([small_buf], "gather_small_grads")
    small_out = adamw(small_all, pk(''), pk('m_'), pk('v_'), "adamw_replicated")
    small_shapes = [given[n].shape for n in SMALL]
    by_name.update(zip(SMALL, zip(*[_unpack(buf, small_layout, small_shapes) for buf in small_out])))
    loss = lax.psum(loss_local, MESH_AXES)
    return (loss, grad_x[None], *[by_name[n][0] for n in WEIGHTS], *[by_name[n][1] for n in WEIGHTS],
            *[by_name[n][2] for n in WEIGHTS], *[by_name[n][3] for n in WEIGHTS])


def to_blocks_grad(tok, t_len, dil):
    return tok.reshape(t_len // dil, dil, HEADS_PER_GROUP, HEAD_DIM).transpose(2, 1, 0, 3)
```

```python
import functools
import math

import jax
import jax.numpy as jnp
from jax import lax
from jax.experimental import pallas as pl
from jax.experimental.pallas import tpu as pltpu

F32 = jnp.float32
BF16 = jnp.bfloat16
I32 = jnp.int32

N_DEV = 8
N_CHIPS = 4
MESH_AXES = ("x", "y", "c")
LANES = 128
PACK_ROWS = 16
VMEM_LIMIT_BYTES = 48 * 2**20
ROW_BLOCK_BYTES = 3 * 2**20

HEAD_DIM = 128
ATTN_BLOCK = 128
HEADS_PER_GROUP = 4
DILATED_GROUPS = ((128, 1), (512, 4), (2048, 16))
N_HEADS = HEADS_PER_GROUP * len(DILATED_GROUPS)
ATTN_WIDTH = N_HEADS * HEAD_DIM
ATTN_OUT = HEADS_PER_GROUP * HEAD_DIM
N_BUCKETS = 32
MAX_DISTANCE = 2048
RWKV_HEAD = 64
RWKV_CHUNK = 64
RWKV_HEADS_PER_STEP = 8
RMS_EPS = 1e-6
GN_EPS = 64e-5
NEG_INF = -1e30

ADAM_LR = 0.001
ADAM_B1 = 0.9
ADAM_B2 = 0.999
ADAM_EPS = 1e-08
ADAM_WD = 0.01
ADAM_STEP = 10

WEIGHTS = ['norm_mix', 'w_in', 'q_gain', 'k_gain', 'rel_bias', 'w_attn_up', 'shift_mix', 'w0', 'w_decay_up', 'a0',
           'w_aaa_up', 'w_gate_up', 'k_k', 'k_a', 'r_k', 'gn_w', 'gn_b', 'w_rwkv_up', 'w_out', 'norm_mlp', 'w_mlp_in',
           'w_mlp_out', 'norm_ple', 'w_ple_gate', 'w_ple_proj']
SHARD_AXIS = {'w_in': 1, 'w_attn_up': 1, 'w_decay_up': 1, 'w_aaa_up': 1, 'w_gate_up': 1, 'w_rwkv_up': 1, 'w_out': 0,
              'w_mlp_in': 1, 'w_mlp_out': 0, 'w_ple_gate': 0, 'w_ple_proj': 1}
BIG = [n for n in WEIGHTS if n in SHARD_AXIS]
SMALL = [n for n in WEIGHTS if n not in SHARD_AXIS]


def _params(sem):
    return pltpu.CompilerParams(dimension_semantics=sem, vmem_limit_bytes=VMEM_LIMIT_BYTES)


_DN = {'nn': (((1,), (0,)), ((), ())), 'nt': (((1,), (1,)), ((), ())), 'tn': (((0,), (0,)), ((), ()))}


def _dot(a, b, mode, exact):
    if exact:
        return lax.dot_general(a, b, _DN[mode], precision=lax.Precision.HIGH, preferred_element_type=F32)
    return lax.dot_general(a.astype(BF16), b.astype(BF16), _DN[mode], preferred_element_type=F32)


@functools.partial(jax.custom_vjp, nondiff_argnums=(2, 3))
def _dot_ad(a, b, mode, exact):
    return _dot(a, b, mode, exact)


def _dot_ad_fwd(a, b, mode, exact):
    return _dot(a, b, mode, exact), (a, b)


def _dot_ad_bwd(mode, exact, res, g):
    a, b = res
    if mode == 'nn':
        return _dot(g, b, 'nt', exact), _dot(a, g, 'tn', exact)
    if mode == 'nt':
        return _dot(g, b, 'nn', exact), _dot(g, a, 'tn', exact)
    return _dot(b, g, 'nt', exact), _dot(a, g, 'nn', exact)


_dot_ad.defvjp(_dot_ad_fwd, _dot_ad_bwd)


def _pick(n, cands):
    for c in cands:
        if n % c == 0:
            return c
    return n


def matmul(a, b, mode, out_dtype, name, b_shards=False, out_shards=False, b_col_offsets=None, epilogue=None,
           extras=(), exchange=None):
    a_list = list(a) if isinstance(a, (list, tuple)) else [a]
    b_list = list(b) if isinstance(b, (list, tuple)) else [b]
    seg = len(a_list)
    assert all(t.dtype == BF16 for t in a_list + b_list), name
    assert seg == 1 or (mode == 'nt' and not b_shards), name
    m = a_list[0].shape[1] if mode == 'tn' else a_list[0].shape[0]
    ks = [t.shape[0] if mode == 'tn' else t.shape[1] for t in a_list]
    b0 = b_list[0]
    b_rows, b_cols = (b0.shape[1], N_DEV * b0.shape[2]) if b_shards else b0.shape
    n = b_rows if mode == 'nt' else b_cols
    if seg == 1 and b_col_offsets is None:
        assert (b_cols if mode == 'nt' else b_rows) == ks[0], (name, a_list[0].shape, b0.shape)
    offsets = list(b_col_offsets) if b_col_offsets is not None else [0] * seg
    tm = _pick(m, (1024, 512, 256, 128))
    tn = _pick(n // N_DEV if (out_shards or (b_shards and mode != 'nt')) else n, (1024, 512, 256, 128))
    k_units = [kk // N_DEV if (b_shards and mode == 'nt') else kk for kk in ks] + [o for o in offsets if o]
    k_cands = (1024, 512, 256, 128) if (epilogue is not None or extras) else (2048, 1024, 512, 256, 128)
    tk = next((c for c in k_cands if all(u % c == 0 for u in k_units)), k_units[0])
    nks = [kk // tk for kk in ks]
    starts = [sum(nks[:s]) for s in range(seg)]
    nk = sum(nks)
    grid = (m // tm, n // tn, nk)
    total_steps = grid[0] * grid[1] * nk
    kind, moved = exchange if exchange is not None else (None, [])
    nx, ne = len(moved), len(extras)
    out_dtypes = list(out_dtype) if isinstance(out_dtype, (list, tuple)) else [out_dtype]
    no = len(out_dtypes)

    def body(*refs):
        a_refs, b_refs, x_refs = refs[:seg], refs[seg:2 * seg], refs[2 * seg:2 * seg + ne]
        pos = 2 * seg + ne
        moved_in, o_refs = refs[pos:pos + nx], refs[pos + nx:pos + nx + no]
        pos += nx + no
        moved_out, acc_ref, sems = refs[pos:pos + nx], refs[pos + nx], refs[pos + nx + 1:]
        kk = pl.program_id(2)
        step = (pl.program_id(0) * grid[1] + pl.program_id(1)) * nk + kk
        if kind == 'gather':
            start, forward, finish = _gather_schedule(moved_in, moved_out, *sems)
            pl.when(step == 0)(start)
            pl.when(step == total_steps // 2)(forward)
        elif kind == 'scatter':
            start, finish = _scatter_schedule(moved_in, moved_out, *sems)
            pl.when(step == 0)(start)

        @pl.when(kk == 0)
        def _():
            acc_ref[...] = jnp.zeros_like(acc_ref)

        for s in range(seg):
            def accumulate(s=s):
                acc_ref[...] += lax.dot_general(a_refs[s][...], b_refs[s][...], _DN[mode], preferred_element_type=F32)

            if seg == 1:
                accumulate()
            else:
                pl.when(jnp.logical_and(kk >= starts[s], kk < starts[s] + nks[s]))(accumulate)

        @pl.when(kk == nk - 1)
        def _():
            acc = acc_ref[...]
            outs = (acc,) if epilogue is None else epilogue(acc, *[x[...] for x in x_refs])
            for r, v in zip(o_refs, outs):
                r[...] = v.astype(r.dtype)

        if kind is not None:
            pl.when(step == total_steps - 1)(finish)

    def k_of(kk, s):
        return kk if seg == 1 else jnp.clip(kk - starts[s], 0, nks[s] - 1)

    a_specs, b_specs = [], []
    for s in range(seg):
        off = offsets[s] // tk
        if mode == 'tn':
            a_specs.append(pl.BlockSpec((tk, tm), lambda i, j, kk, s=s: (k_of(kk, s), i)))
        else:
            a_specs.append(pl.BlockSpec((tm, tk), lambda i, j, kk, s=s: (i, k_of(kk, s))))
        if mode == 'nt':
            if b_shards:
                per = b0.shape[2] // tk
                b_specs.append(pl.BlockSpec((None, tn, tk), lambda i, j, kk: (kk // per, j, kk % per)))
            else:
                b_specs.append(pl.BlockSpec((tn, tk), lambda i, j, kk, s=s, off=off: (j, off + k_of(kk, s))))
        else:
            if b_shards:
                per = b0.shape[2] // tn
                b_specs.append(pl.BlockSpec((None, tk, tn), lambda i, j, kk: (j // per, kk, j % per)))
            else:
                b_specs.append(pl.BlockSpec((tk, tn), lambda i, j, kk: (kk, j)))
    tile = pl.BlockSpec((tm, tn), lambda i, j, kk: (i, j))
    if out_shards:
        assert epilogue is None
        per_o = n // N_DEV // tn
        o_specs = [pl.BlockSpec((None, tm, tn), lambda i, j, kk: (j // per_o, i, j % per_o))]
        o_shapes = [jax.ShapeDtypeStruct((N_DEV, m, n // N_DEV), out_dtypes[0])]
    else:
        o_specs = [tile] * no
        o_shapes = [jax.ShapeDtypeStruct((m, n), d) for d in out_dtypes]
    moved_shapes = [jax.ShapeDtypeStruct(((N_DEV,) + t.shape) if kind == 'gather' else t.shape, t.dtype) for t in moved]
    res = pl.pallas_call(
        body, grid=grid, in_specs=a_specs + b_specs + [tile] * ne + [_ANY] * nx,
        out_specs=o_specs + [_ANY] * nx, out_shape=o_shapes + moved_shapes,
        scratch_shapes=[pltpu.VMEM((tm, tn), F32)] + (_exchange_scratch(nx) if nx else []),
        compiler_params=_params(("arbitrary",) * 3 if nx else ("parallel", "parallel", "arbitrary")), name=name,
    )(*a_list, *b_list, *extras, *moved)
    result = res[0] if (epilogue is None) else list(res[:no])
    return (result, list(res[no:])) if nx else result


def _row_bytes(shape, dtype):
    dims = list(shape[1:])
    dims[-1] = -(-dims[-1] // LANES) * LANES
    return math.prod(dims) * jnp.dtype(dtype).itemsize


def _row_tile(n, row_bytes):
    t = 1024
    while t > 16 and (n % t or t * row_bytes > ROW_BLOCK_BYTES):
        t //= 2
    assert n % t == 0, (n, t)
    return t


def rowmap(fn, tiled, bcast, out_tiled, out_acc, name):
    n = tiled[0].shape[0]
    tile = _row_tile(n, sum(_row_bytes(t.shape, t.dtype) for t in list(tiled) + list(out_tiled)))
    n_in, n_out = len(tiled) + len(bcast), len(out_tiled)

    def body(*refs):
        outs, accs = fn(*[r[...] for r in refs[:n_in]])
        assert len(outs) == n_out and len(accs) == len(out_acc), name
        for r, v in zip(refs[n_in:n_in + n_out], outs):
            r[...] = v.astype(r.dtype)
        acc_refs = refs[n_in + n_out:]
        if acc_refs:
            @pl.when(pl.program_id(0) == 0)
            def _():
                for r, v in zip(acc_refs, accs):
                    r[...] = v.astype(r.dtype)

            @pl.when(pl.program_id(0) != 0)
            def _():
                for r, v in zip(acc_refs, accs):
                    r[...] += v.astype(r.dtype)

    def tspec(s):
        nd = len(s.shape)
        return pl.BlockSpec((tile,) + tuple(s.shape[1:]), lambda i, nd=nd: (i,) + (0,) * (nd - 1))

    def bspec(s):
        nd = len(s.shape)
        return pl.BlockSpec(tuple(s.shape), lambda i, nd=nd: (0,) * nd)

    res = pl.pallas_call(
        body, grid=(n // tile,),
        in_specs=[tspec(t) for t in tiled] + [bspec(t) for t in bcast],
        out_specs=[tspec(t) for t in out_tiled] + [bspec(t) for t in out_acc],
        out_shape=list(out_tiled) + list(out_acc),
        compiler_params=_params(("arbitrary",)), name=name,
    )(*tiled, *bcast)
    return list(res[:n_out]), list(res[n_out:])


def rowmap_fwd(fwd, tiled, bcast, out_dtypes, name):
    shapes = jax.eval_shape(fwd, *tiled, *bcast)
    out_tiled = [jax.ShapeDtypeStruct(s.shape, d) for s, d in zip(shapes, out_dtypes)]
    outs, _ = rowmap(lambda *blk: (fwd(*[b.astype(F32) for b in blk]), ()), tiled, bcast, out_tiled, [], name)
    return outs


def rowmap_bwd(fwd, tiled, bcast, cts, want, name):
    cts = [[] if c is None else (list(c) if isinstance(c, (list, tuple)) else [c]) for c in cts]
    flat_cts = [c for group in cts for c in group]
    nt_, nc_ = len(tiled), len(flat_cts)

    def fn(*blk):
        ins = [b.astype(F32) for b in blk[:nt_]] + [b.astype(F32) for b in blk[nt_ + nc_:]]
        ctb = list(blk[nt_:nt_ + nc_])
        outs, vjp = jax.vjp(fwd, *ins)
        full = []
        for o, group in zip(outs, cts):
            acc = jnp.zeros_like(o)
            for _ in group:
                acc = acc + ctb.pop(0).astype(F32)
            full.append(acc)
        g = vjp(tuple(full))
        return [g[i] for i in range(nt_) if want[i] is not None], list(g[nt_:])

    out_tiled = [jax.ShapeDtypeStruct(t.shape, w) for t, w in zip(tiled, want) if w is not None]
    out_acc = [jax.ShapeDtypeStruct(b.shape, F32) for b in bcast]
    return rowmap(fn, list(tiled) + flat_cts, bcast, out_tiled, out_acc, name)


def _rms(x, gain):
    return x * lax.rsqrt(jnp.mean(jnp.square(x), axis=-1, keepdims=True) + RMS_EPS) * gain


def _sigmoid(x):
    return 1.0 / (1.0 + jnp.exp(-x))


def _softplus(x):
    return jnp.maximum(x, 0.0) + jnp.log(1.0 + jnp.exp(-jnp.abs(x)))


def st_norm(x, gain):
    return x, _rms(x, gain)


def st_res_norm(x, delta, gain):
    y = x + delta
    return y, _rms(y, gain)


def st_qk_norm(q, k, q_gain, k_gain):
    return _rms(q, q_gain), _rms(k, k_gain)


def st_merge(o0, o1, o2, l0, l1, l2):
    m = jnp.maximum(jnp.maximum(l0, l1), l2)
    e0, e1, e2 = jnp.exp(l0 - m), jnp.exp(l1 - m), jnp.exp(l2 - m)
    return ((e0 * o0 + e1 * o1 + e2 * o2) / (e0 + e1 + e2),)


def _st_rwkv_pre(dot, zr, zk, zv, xw, xa, xg, pr, pk, pv, pw, pa, pg, mr, mk, mv, mw, ma, mg,
                 w0, w_decay, a0, w_aaa, w_gate, k_k, k_a):
    def shift(cur, prev, mix):
        return cur + mix * (prev - cur)

    r, k, v = shift(zr, pr, mr), shift(zk, pk, mk), shift(zv, pv, mv)
    xw, xa, xg = shift(xw, pw, mw), shift(xa, pa, ma), shift(xg, pg, mg)
    w = -_softplus(-(w0 + dot(jnp.tanh(xw), w_decay, 'nn', False))) - 0.5
    a = _sigmoid(a0 + dot(xa, w_aaa, 'nn', False))
    g = dot(_sigmoid(xg), w_gate, 'nn', False)
    log_decay = -jnp.exp(w)
    return r, log_decay, k * (1.0 + (a - 1.0) * k_a), v, k * k_k, a, g


def st_rwkv_kk(kk0, a):
    kk = kk0 / jnp.maximum(jnp.sqrt(jnp.sum(jnp.square(kk0), axis=-1, keepdims=True)), 1e-12)
    return -kk, kk * a


def st_rwkv_post(y, r, k, v, g, gn_w, gn_b, r_k):
    mu = jnp.mean(y, axis=-1, keepdims=True)
    var = jnp.mean(jnp.square(y - mu), axis=-1, keepdims=True)
    out = (y - mu) * lax.rsqrt(var + GN_EPS) * gn_w + gn_b
    out = out + jnp.sum(r * k * r_k, axis=-1, keepdims=True) * v
    return (out * g,)


def st_gate(g0, g1, attn_d, rwkv_d):
    return (_sigmoid(g0) * attn_d + _sigmoid(g1) * rwkv_d,)


def st_relu2(u):
    return (jnp.square(jnp.maximum(u, 0.0)),)


def st_add(a, b):
    return (a + b,)


def loss_head(x2, pg, pp, target, name):
    d_model = x2.shape[1]

    def fn(x2, pg, pp, tgt):
        s = _sigmoid(pg)
        err = x2 + s * pp - tgt
        dy = err * (1.0 / d_model)
        part = 0.5 * jnp.sum(jnp.square(err)) * (1.0 / d_model)
        return [dy, dy * pp * s * (1.0 - s), dy * s], [jnp.full((8, LANES), part, F32)]

    sds = jax.ShapeDtypeStruct
    outs, accs = rowmap(fn, [x2, pg, pp, target], [],
                        [sds(x2.shape, F32), sds(x2.shape, BF16), sds(x2.shape, BF16)], [sds((8, LANES), F32)], name)
    return outs[0], outs[1], outs[2], accs[0][0, 0]


def _attn_block(dot, q, kp, kc, vp, vc, bp, bc, prev_offset):
    blk = q.shape[0]
    qi = lax.broadcasted_iota(I32, (blk, blk), 0)
    ki = lax.broadcasted_iota(I32, (blk, blk), 1)
    mask_c = ki <= qi
    mask_p = ki >= qi + prev_offset
    scale = HEAD_DIM ** -0.5
    s_c = jnp.where(mask_c, dot(q, kc, 'nt', False) * scale + bc, NEG_INF)
    s_p = jnp.where(mask_p, dot(q, kp, 'nt', False) * scale + bp, NEG_INF)
    m = lax.stop_gradient(jnp.maximum(jnp.max(s_c, axis=1, keepdims=True), jnp.max(s_p, axis=1, keepdims=True)))
    e_c = jnp.where(mask_c, jnp.exp(s_c - m), 0.0)
    e_p = jnp.where(mask_p, jnp.exp(s_p - m), 0.0)
    l = jnp.sum(e_c, axis=1, keepdims=True) + jnp.sum(e_p, axis=1, keepdims=True)
    o = (dot(e_c, vc, 'nn', False) + dot(e_p, vp, 'nn', False)) / l
    return o, jnp.broadcast_to(m + jnp.log(l), o.shape)


def _attn_specs(blk, hd):
    cur = pl.BlockSpec((None, None, blk, hd), lambda h, r, n: (h, r, n, 0))
    prev = pl.BlockSpec((None, None, blk, hd), lambda h, r, n: (h, r, jnp.maximum(n - 1, 0), 0))
    bias = pl.BlockSpec((None, blk, blk), lambda h, r, n: (h, 0, 0))
    return cur, prev, bias


def attn_fwd(q, k, v, bp, bc, name):
    hg, d, length, hd = q.shape
    blk = ATTN_BLOCK
    cur, prev, bias = _attn_specs(blk, hd)

    def body(q_ref, kp_ref, kc_ref, vp_ref, vc_ref, bp_ref, bc_ref, o_ref, l_ref):
        off = jnp.where(pl.program_id(2) > 0, 0, blk)
        o, l = _attn_block(_dot, q_ref[...], kp_ref[...], kc_ref[...], vp_ref[...], vc_ref[...], bp_ref[...],
                           bc_ref[...], off)
        o_ref[...] = o
        l_ref[...] = l

    return pl.pallas_call(
        body, grid=(hg, d, length // blk), in_specs=[cur, prev, cur, prev, cur, bias, bias], out_specs=[cur, cur],
        out_shape=[jax.ShapeDtypeStruct(q.shape, F32)] * 2,
        compiler_params=_params(("parallel", "parallel", "parallel")), name=name,
    )(q, k, k, v, v, bp, bc)


def attn_bwd(q, k, v, bp, bc, do, dl, name):
    hg, d, length, hd = q.shape
    blk = ATTN_BLOCK
    nb = length // blk
    cur, prev, bias = _attn_specs(blk, hd)

    def body(q_ref, kp_ref, kc_ref, vp_ref, vc_ref, bp_ref, bc_ref, do_ref, dl_ref,
             dq_ref, dkp_ref, dkc_ref, dvp_ref, dvc_ref, dbp_ref, dbc_ref):
        off = jnp.where(pl.program_id(2) > 0, 0, blk)
        f = functools.partial(_attn_block, _dot_ad, prev_offset=off)
        _, vjp = jax.vjp(f, q_ref[...], kp_ref[...], kc_ref[...], vp_ref[...], vc_ref[...], bp_ref[...], bc_ref[...])
        dq, dkp, dkc, dvp, dvc, dbp, dbc = vjp((do_ref[...], dl_ref[...]))
        dq_ref[...] = dq
        dkp_ref[...] = dkp
        dkc_ref[...] = dkc
        dvp_ref[...] = dvp
        dvc_ref[...] = dvc
        first = jnp.logical_and(pl.program_id(1) == 0, pl.program_id(2) == 0)

        @pl.when(first)
        def _():
            dbp_ref[...] = dbp
            dbc_ref[...] = dbc

        @pl.when(jnp.logical_not(first))
        def _():
            dbp_ref[...] += dbp
            dbc_ref[...] += dbc

    blocks = jax.ShapeDtypeStruct(q.shape, F32)
    dq, dkp, dkc, dvp, dvc, dbp, dbc = pl.pallas_call(
        body, grid=(hg, d, nb), in_specs=[cur, prev, cur, prev, cur, bias, bias, cur, cur],
        out_specs=[cur] * 5 + [bias] * 2, out_shape=[blocks] * 5 + [jax.ShapeDtypeStruct(bp.shape, F32)] * 2,
        compiler_params=_params(("arbitrary", "arbitrary", "arbitrary")), name=name,
    )(q, k, k, v, v, bp, bc, do, dl)

    nxt = pl.BlockSpec((None, None, blk, hd), lambda h, r, n: (h, r, jnp.minimum(n + 1, nb - 1), 0))

    def add_body(kc_ref, kp_ref, vc_ref, vp_ref, dk_ref, dv_ref):
        has_next = (pl.program_id(2) + 1 < nb).astype(F32)
        dk_ref[...] = kc_ref[...] + kp_ref[...] * has_next
        dv_ref[...] = vc_ref[...] + vp_ref[...] * has_next

    dk, dv = pl.pallas_call(
        add_body, grid=(hg, d, nb), in_specs=[cur, nxt, cur, nxt], out_specs=[cur, cur], out_shape=[blocks] * 2,
        compiler_params=_params(("parallel", "parallel", "parallel")), name=name + "_kv",
    )(dkc, dkp, dvc, dvp)
    return dq, dk, dv, dbp, dbc


def _t5_bucket(dist):
    max_exact = N_BUCKETS // 2
    d_f = jnp.maximum(dist, 1).astype(F32)
    large = max_exact + (jnp.log(d_f / max_exact) / math.log(MAX_DISTANCE / max_exact)
                         * (N_BUCKETS - max_exact)).astype(I32)
    large = jnp.minimum(large, N_BUCKETS - 1)
    return jnp.where(dist < max_exact, dist, large)


def _bucket_tables():
    blk = ATTN_BLOCK
    qi = jnp.arange(blk)[:, None]
    ki = jnp.arange(blk)[None, :]
    out = []
    for _, dilation in DILATED_GROUPS:
        rel_p = jnp.maximum(blk + qi - ki, 0) * dilation
        rel_c = jnp.maximum(qi - ki, 0) * dilation
        out.append(jnp.stack([_t5_bucket(rel_p), _t5_bucket(rel_c)]))
    return jnp.stack(out).astype(I32)


def bias_fwd(table, buckets, name):
    blk = ATTN_BLOCK

    def body(tab_ref, bkt_ref, out_ref):
        for g in range(len(DILATED_GROUPS)):
            for half in range(2):
                bk = bkt_ref[g, half]
                for hh in range(HEADS_PER_GROUP):
                    h = g * HEADS_PER_GROUP + hh
                    acc = jnp.zeros((blk, blk), F32)
                    for b in range(N_BUCKETS):
                        acc = jnp.where(bk == b, tab_ref[b, h], acc)
                    out_ref[h, half] = acc

    return pl.pallas_call(
        body, in_specs=[pl.BlockSpec(memory_space=pltpu.SMEM), pl.BlockSpec(memory_space=pltpu.VMEM)],
        out_specs=pl.BlockSpec(memory_space=pltpu.VMEM),
        out_shape=jax.ShapeDtypeStruct((N_HEADS, 2, blk, blk), F32), name=name,
    )(table, buckets)


def bias_bwd(dbias, buckets, name):
    def body(db_ref, bkt_ref, out_ref):
        rows = lax.broadcasted_iota(I32, (N_BUCKETS, LANES), 0)
        cols = lax.broadcasted_iota(I32, (N_BUCKETS, LANES), 1)
        acc = jnp.zeros((N_BUCKETS, LANES), F32)
        for g in range(len(DILATED_GROUPS)):
            bk_p, bk_c = bkt_ref[g, 0], bkt_ref[g, 1]
            for hh in range(HEADS_PER_GROUP):
                h = g * HEADS_PER_GROUP + hh
                d_p, d_c = db_ref[h, 0], db_ref[h, 1]
                for b in range(N_BUCKETS):
                    s = jnp.sum(jnp.where(bk_p == b, d_p, 0.0)) + jnp.sum(jnp.where(bk_c == b, d_c, 0.0))
                    acc = jnp.where(jnp.logical_and(rows == b, cols == h), s, acc)
        out_ref[...] = acc

    return pl.pallas_call(
        body, in_specs=[pl.BlockSpec(memory_space=pltpu.VMEM)] * 2, out_specs=pl.BlockSpec(memory_space=pltpu.VMEM),
        out_shape=jax.ShapeDtypeStruct((N_BUCKETS, LANES), F32), name=name,
    )(dbias, buckets)


def _rwkv_chunk(dot, s0, r, lw, k, v, a, b):
    def each(f, *lists):
        return [f(*xs) for xs in zip(*lists)]

    def mm(mode):
        return lambda p, q: dot(p, q, mode, True)

    def mul(p, q):
        return p * q

    def add(p, q):
        return p + q

    c = r[0].shape[0]
    ti = lax.broadcasted_iota(I32, (c, c), 0)
    si = lax.broadcasted_iota(I32, (c, c), 1)
    incl = si <= ti
    strict = si < ti
    ones_incl = incl.astype(F32)
    cum = each(lambda x: dot(ones_incl, x, 'nn', True), lw)
    w_incl = each(jnp.exp, cum)
    w_prev = each(lambda cu, x: jnp.exp(cu - x), cum, lw)
    w_inv = each(lambda cu: jnp.exp(-cu), cum)
    w_end = each(lambda x: jnp.exp(jnp.sum(x, axis=0, keepdims=True)), lw)
    a_t, r_t, b_t, k_t = each(mul, a, w_prev), each(mul, r, w_incl), each(mul, b, w_inv), each(mul, k, w_inv)
    l_ab = each(lambda p, q: jnp.where(strict, dot(p, q, 'nt', True), 0.0), a_t, b_t)
    l_ak = each(lambda p, q: jnp.where(strict, dot(p, q, 'nt', True), 0.0), a_t, k_t)
    u = each(add, each(mm('nt'), a_t, s0), each(mm('nn'), l_ak, v))
    u = each(add, u, each(mm('nn'), l_ab, u))
    power = l_ab
    for _ in range(int(math.log2(c)) - 1):
        power = each(mm('nn'), power, power)
        u = each(add, u, each(mm('nn'), power, u))
    m_rb = each(lambda p, q: jnp.where(incl, dot(p, q, 'nt', True), 0.0), r_t, b_t)
    m_rk = each(lambda p, q: jnp.where(incl, dot(p, q, 'nt', True), 0.0), r_t, k_t)
    y = each(add, each(add, each(mm('nt'), r_t, s0), each(mm('nn'), m_rb, u)), each(mm('nn'), m_rk, v))
    s1 = each(add, each(add, s0, each(mm('tn'), u, b_t)), each(mm('tn'), v, k_t))
    return y, each(mul, s1, w_end)


def rwkv_fwd(r, lw, k, v, a, b, name, gather=()):
    h, t, n = r.shape
    c = RWKV_CHUNK
    nc = t // c
    hs = math.gcd(h, RWKV_HEADS_PER_STEP)
    steps = h // hs * nc
    ng = len(gather)
    row = pl.BlockSpec((hs, c, n), lambda i, j: (i, j, 0))

    def body(*refs):
        ins, x_refs = refs[:6], refs[6:6 + ng]
        y_ref, s0_ref = refs[6 + ng:8 + ng]
        out_refs, state, sems = refs[8 + ng:8 + 2 * ng], refs[8 + 2 * ng], refs[9 + 2 * ng:]
        step = pl.program_id(0) * nc + pl.program_id(1)
        if ng:
            start, forward, finish = _gather_schedule(x_refs, out_refs, *sems)
            pl.when(step == 0)(start)
            pl.when(step == steps // 2)(forward)

        @pl.when(pl.program_id(1) == 0)
        def _():
            state[...] = jnp.zeros_like(state)

        s0 = [state[q] for q in range(hs)]
        y, s1 = _rwkv_chunk(_dot, s0, *[[ref[q] for q in range(hs)] for ref in ins])
        for q in range(hs):
            s0_ref[q] = s0[q]
            y_ref[q] = y[q]
            state[q] = s1[q]
        if ng:
            pl.when(step == steps - 1)(finish)

    return pl.pallas_call(
        body, grid=(h // hs, nc), in_specs=[row] * 6 + [_ANY] * ng,
        out_specs=[row, pl.BlockSpec((hs, None, n, n), lambda i, j: (i, j, 0, 0))] + [_ANY] * ng,
        out_shape=[jax.ShapeDtypeStruct((h, t, n), F32), jax.ShapeDtypeStruct((h, nc, n, n), F32)]
        + [jax.ShapeDtypeStruct((N_DEV,) + g.shape, g.dtype) for g in gather],
        scratch_shapes=[pltpu.VMEM((hs, n, n), F32)] + (_exchange_scratch(ng) if ng else []),
        compiler_params=_params(("arbitrary", "arbitrary")), name=name,
    )(r, lw, k, v, a, b, *gather)


def rwkv_bwd(r, lw, k, v, a, b, s0, dy, name, scatter=()):
    h, t, n = r.shape
    c = RWKV_CHUNK
    nc = t // c
    hs = math.gcd(h, RWKV_HEADS_PER_STEP)
    steps = h // hs * nc
    ns = len(scatter)
    row = pl.BlockSpec((hs, c, n), lambda i, j: (i, nc - 1 - j, 0))
    st = pl.BlockSpec((hs, None, n, n), lambda i, j: (i, nc - 1 - j, 0, 0))

    def body(*refs):
        ins, s0_ref, dy_ref, send_refs = refs[:6], refs[6], refs[7], refs[8:8 + ns]
        grad_refs, recv_refs = refs[8 + ns:14 + ns], refs[14 + ns:14 + 2 * ns]
        dstate, sems = refs[14 + 2 * ns], refs[15 + 2 * ns:]
        step = pl.program_id(0) * nc + pl.program_id(1)
        if ns:
            start, finish = _scatter_schedule(send_refs, recv_refs, *sems)
            pl.when(step == 0)(start)

        @pl.when(pl.program_id(1) == 0)
        def _():
            dstate[...] = jnp.zeros_like(dstate)

        per_head = range(hs)
        _, vjp = jax.vjp(functools.partial(_rwkv_chunk, _dot_ad), [s0_ref[q] for q in per_head],
                         *[[ref[q] for q in per_head] for ref in ins])
        grads = vjp(([dy_ref[q] for q in per_head], [dstate[q] for q in per_head]))
        for q in per_head:
            dstate[q] = grads[0][q]
            for ref, g in zip(grad_refs, grads[1:]):
                ref[q] = g[q]
        if ns:
            pl.when(step == steps - 1)(finish)

    return pl.pallas_call(
        body, grid=(h // hs, nc), in_specs=[row] * 6 + [st, row] + [_ANY] * ns, out_specs=[row] * 6 + [_ANY] * ns,
        out_shape=[jax.ShapeDtypeStruct((h, t, n), F32)] * 6 + [jax.ShapeDtypeStruct(s.shape, s.dtype) for s in scatter],
        scratch_shapes=[pltpu.VMEM((hs, n, n), F32)] + (_exchange_scratch(ns) if ns else []),
        compiler_params=_params(("arbitrary", "arbitrary")), name=name,
    )(r, lw, k, v, a, b, s0, dy, *scatter)


_ANY = pl.BlockSpec(memory_space=pl.ANY)


def _exchange_scratch(n_arrays):
    return [pltpu.SemaphoreType.DMA((n_arrays, N_DEV - 1)), pltpu.SemaphoreType.DMA((n_arrays, N_DEV - 1)),
            pltpu.SemaphoreType.DMA((n_arrays,))]


def _gather_schedule(x_refs, out_refs, send_sems, recv_sems, local_sems):
    x, y, c = lax.axis_index("x"), lax.axis_index("y"), lax.axis_index("c")
    me, sibling = (x, y, c), (x, y, 1 - c)
    chips = [(1 - x, y), (x, 1 - y), (1 - x, 1 - y)]
    arrays = range(len(x_refs))

    def slot(a, pos):
        return out_refs[a].at[4 * pos[0] + 2 * pos[1] + pos[2]]

    def copy(a, i, block, to, src=None):
        return pltpu.make_async_remote_copy(
            src_ref=slot(a, block) if src is None else src, dst_ref=slot(a, block), send_sem=send_sems.at[a, i],
            recv_sem=recv_sems.at[a, i], device_id=to, device_id_type=pl.DeviceIdType.MESH)

    mine = [pltpu.make_async_copy(x_refs[a], slot(a, me), local_sems.at[a]) for a in arrays]
    first = [[copy(a, 0, me, sibling, src=x_refs[a])]
             + [copy(a, 1 + j, me, (*chip, c), src=x_refs[a]) for j, chip in enumerate(chips)] for a in arrays]
    passed = [[copy(a, 4 + j, (*chip, c), sibling) for j, chip in enumerate(chips)] for a in arrays]

    def start():
        for a in arrays:
            mine[a].start()
            for cp in first[a]:
                cp.start()

    def forward():
        for j, chip in enumerate(chips):
            for a in arrays:
                copy(a, 1 + j, (*chip, c), me).wait_recv()
                passed[a][j].start()

    def finish():
        for a in arrays:
            copy(a, 0, sibling, me).wait_recv()
            for j, chip in enumerate(chips):
                copy(a, 4 + j, (*chip, 1 - c), me).wait_recv()
            for cp in first[a] + passed[a]:
                cp.wait_send()
            mine[a].wait()

    return start, forward, finish


def _scatter_schedule(in_refs, out_refs, send_sems, recv_sems, local_sems):
    x, y, c = lax.axis_index("x"), lax.axis_index("y"), lax.axis_index("c")
    my_chip = 2 * x + y
    mine, remote = [], []
    for a, (src, dst) in enumerate(zip(in_refs, out_refs)):
        mine.append(pltpu.make_async_copy(src.at[my_chip], dst.at[my_chip], local_sems.at[a]))
        for i in range(1, N_CHIPS):
            px, py = x ^ (i >> 1), y ^ (i & 1)
            remote.append(pltpu.make_async_remote_copy(
                src_ref=src.at[2 * px + py], dst_ref=dst.at[my_chip], send_sem=send_sems.at[a, i - 1],
                recv_sem=recv_sems.at[a, i - 1], device_id=(px, py, c), device_id_type=pl.DeviceIdType.MESH))

    def start():
        for cp in mine + remote:
            cp.start()

    def finish():
        for cp in remote:
            cp.wait_recv()
        for cp in remote:
            cp.wait_send()
        for cp in mine:
            cp.wait()

    return start, finish


def pair_exchange(parts, name):
    n = len(parts)

    def body(*refs):
        x, y, c = lax.axis_index("x"), lax.axis_index("y"), lax.axis_index("c")
        send_sems, recv_sems = refs[2 * n:]
        copies = [pltpu.make_async_remote_copy(
            src_ref=refs[a].at[q, 1 - c], dst_ref=refs[n + a].at[q], send_sem=send_sems.at[a, q],
            recv_sem=recv_sems.at[a, q], device_id=(x, y, 1 - c), device_id_type=pl.DeviceIdType.MESH)
            for a in range(n) for q in range(N_CHIPS)]
        for cp in copies:
            cp.start()
        for cp in copies:
            cp.wait_recv()
        for cp in copies:
            cp.wait_send()

    return pl.pallas_call(
        body, in_specs=[_ANY] * n, out_specs=[_ANY] * n,
        out_shape=[jax.ShapeDtypeStruct((N_CHIPS,) + s.shape[2:], s.dtype) for s in parts],
        scratch_shapes=[pltpu.SemaphoreType.DMA((n, N_CHIPS)), pltpu.SemaphoreType.DMA((n, N_CHIPS))], name=name,
    )(*parts)


def pair_add(mine, theirs, core, name):
    _, _, k, n = mine.shape
    tc = _pick(n, (2048, 1024, 512))
    tr = _row_tile(k, tc * 3 * jnp.dtype(mine.dtype).itemsize)

    def body(core_ref, a_ref, b_ref, o_ref):
        o_ref[...] = (a_ref[...].astype(F32) + b_ref[...].astype(F32)).astype(o_ref.dtype)

    one = pl.BlockSpec((None, tr, tc), lambda q, i, j, core_ref: (q, i, j))
    grid_spec = pltpu.PrefetchScalarGridSpec(
        num_scalar_prefetch=1, grid=(N_CHIPS, k // tr, n // tc),
        in_specs=[pl.BlockSpec((None, None, tr, tc), lambda q, i, j, core_ref: (q, core_ref[0], i, j)), one],
        out_specs=one)
    return pl.pallas_call(
        body, grid_spec=grid_spec, out_shape=jax.ShapeDtypeStruct(theirs.shape, BF16),
        compiler_params=_params(("parallel", "parallel", "parallel")), name=name,
    )(core, mine, theirs)


def all_gather_many(shards, name):
    n = len(shards)

    def body(*refs):
        start, forward, finish = _gather_schedule(refs[:n], refs[n:2 * n], *refs[2 * n:])
        start()
        forward()
        finish()

    return pl.pallas_call(
        body, in_specs=[_ANY] * n, out_specs=[_ANY] * n,
        out_shape=[jax.ShapeDtypeStruct((N_DEV,) + s.shape, s.dtype) for s in shards],
        scratch_shapes=_exchange_scratch(n), name=name,
    )(*shards)


def project_and_gather(h, w_all, shards):
    return matmul(h, w_all, 'nn', F32, "proj_in", exchange=('gather', shards))


def input_grad_and_scatter(d_parts, w_all, offsets, parts):
    return matmul(d_parts, [w_all] * len(d_parts), 'nt', F32, "d_h_in", b_col_offsets=offsets,
                  exchange=('scatter', parts))


def scan_and_gather(scan_in, shards):
    y, s0, *gathered = rwkv_fwd(*scan_in, name="rwkv_scan", gather=shards)
    return y, s0, gathered


def scan_bwd_and_scatter(scan_in, s0, dy, parts):
    res = rwkv_bwd(*scan_in, s0, dy, name="rwkv_scan_bwd", scatter=parts)
    return res[:6], res[6:]


def adamw_shard(parts, w, m, v, name):
    _, k, n = w.shape
    slots = parts.shape[0]
    tc = _pick(n, (2048, 1024, 512))
    tr = _row_tile(k, tc * (slots * jnp.dtype(parts.dtype).itemsize + 7 * 4))

    def body(p_ref, w_ref, m_ref, v_ref, g_ref, d_ref, nm_ref, nv_ref):
        _adamw_block(p_ref, w_ref, m_ref, v_ref, g_ref, d_ref, nm_ref, nv_ref)

    one = pl.BlockSpec((None, tr, tc), lambda i, j: (0, i, j))
    return pl.pallas_call(
        body, grid=(k // tr, n // tc), in_specs=[pl.BlockSpec((slots, tr, tc), lambda i, j: (0, i, j))] + [one] * 3,
        out_specs=[one] * 4, out_shape=[jax.ShapeDtypeStruct(w.shape, F32)] * 4,
        compiler_params=_params(("parallel", "parallel")), name=name,
    )(parts, w, m, v)


def _adamw_block(p_ref, w_ref, m_ref, v_ref, g_ref, d_ref, nm_ref, nv_ref):
    g = p_ref[0].astype(F32)
    for j in range(1, p_ref.shape[0]):
        g = g + p_ref[j].astype(F32)
    new_m = ADAM_B1 * m_ref[...] + (1.0 - ADAM_B1) * g
    new_v = ADAM_B2 * v_ref[...] + (1.0 - ADAM_B2) * jnp.square(g)
    m_hat = new_m / (1.0 - ADAM_B1 ** ADAM_STEP)
    v_hat = new_v / (1.0 - ADAM_B2 ** ADAM_STEP)
    g_ref[...] = g
    d_ref[...] = -ADAM_LR * (m_hat / (jnp.sqrt(v_hat) + ADAM_EPS) + ADAM_WD * w_ref[...])
    nm_ref[...] = new_m
    nv_ref[...] = new_v


def adamw(parts, w, m, v, name):
    rows = w.shape[0]
    tile = _row_tile(rows, N_DEV * LANES * jnp.dtype(parts.dtype).itemsize + 7 * LANES * 4)

    def body(p_ref, w_ref, m_ref, v_ref, g_ref, d_ref, nm_ref, nv_ref):
        _adamw_block(p_ref, w_ref, m_ref, v_ref, g_ref, d_ref, nm_ref, nv_ref)

    flat = pl.BlockSpec((tile, LANES), lambda i: (i, 0))
    return pl.pallas_call(
        body, grid=(rows // tile,), in_specs=[pl.BlockSpec((N_DEV, tile, LANES), lambda i: (0, i, 0))] + [flat] * 3,
        out_specs=[flat] * 4, out_shape=[jax.ShapeDtypeStruct(w.shape, F32)] * 4,
        compiler_params=_params(("parallel",)), name=name,
    )(parts, w, m, v)


def _part_rows(n_elems):
    return -(-n_elems // (PACK_ROWS * LANES)) * PACK_ROWS


def _pack(arrays, dtype, lead=()):
    parts, layout, off = [], [], 0
    for arr in arrays:
        n = math.prod(arr.shape[len(lead):])
        rows = _part_rows(n)
        flat = arr.reshape(lead + (n,)).astype(dtype)
        flat = jnp.pad(flat, [(0, 0)] * len(lead) + [(0, rows * LANES - n)])
        parts.append(flat.reshape(lead + (rows, LANES)))
        layout.append((off, rows))
        off += rows
    total = -(-off // 1024) * 1024
    if total > off:
        parts.append(jnp.zeros(lead + (total - off, LANES), dtype))
    return jnp.concatenate(parts, axis=len(lead)), layout


def _unpack(buf, layout, shapes, lead=()):
    out = []
    for (off, rows), shape in zip(layout, shapes):
        n = math.prod(shape)
        piece = lax.slice_in_dim(buf, off, off + rows, axis=len(lead))
        out.append(piece.reshape(lead + (rows * LANES,))[..., :n].reshape(lead + tuple(shape)))
    return out


def _split_shards(full, axis):
    if axis == 0:
        return full.reshape((N_DEV, full.shape[0] // N_DEV, full.shape[1]))
    return full.reshape((full.shape[0], N_DEV, full.shape[1] // N_DEV)).transpose(1, 0, 2)


def _join_shards(shards, axis):
    if axis == 0:
        return shards.reshape((-1, shards.shape[2]))
    return shards.transpose(1, 0, 2).reshape((shards.shape[1], -1))


def _shift_down(t):
    return jnp.pad(t, ((1, 0), (0, 0)))[:-1]


def _shift_up(t):
    return jnp.pad(t, ((0, 1), (0, 0)))[1:]


def kernel(x, p, norm_mix, w_in, q_gain, k_gain, rel_bias, w_attn_up, shift_mix, w0, w_decay_up, a0, w_aaa_up, w_gate_up, k_k, k_a, r_k, gn_w, gn_b, w_rwkv_up, w_out, norm_mlp, w_mlp_in, w_mlp_out, norm_ple, w_ple_gate, w_ple_proj, loss_target, m_norm_mix, m_w_in, m_q_gain, m_k_gain, m_rel_bias, m_w_attn_up, m_shift_mix, m_w0, m_w_decay_up, m_a0, m_w_aaa_up, m_w_gate_up, m_k_k, m_k_a, m_r_k, m_gn_w, m_gn_b, m_w_rwkv_up, m_w_out, m_norm_mlp, m_w_mlp_in, m_w_mlp_out, m_norm_ple, m_w_ple_gate, m_w_ple_proj, v_norm_mix, v_w_in, v_q_gain, v_k_gain, v_rel_bias, v_w_attn_up, v_shift_mix, v_w0, v_w_decay_up, v_a0, v_w_aaa_up, v_w_gate_up, v_k_k, v_k_a, v_r_k, v_gn_w, v_gn_b, v_w_rwkv_up, v_w_out, v_norm_mlp, v_w_mlp_in, v_w_mlp_out, v_norm_ple, v_w_ple_gate, v_w_ple_proj):
    given = dict(locals())
    xs = x[0]
    t_len, d_model = xs.shape
    target = loss_target[0]
    p_bf = p[0, 0].astype(BF16)
    rw_width = w0.shape[1]
    n_rheads = rw_width // RWKV_HEAD
    lora_d, lora_a, lora_g = w_decay_up.shape[1], w_aaa_up.shape[1], w_gate_up.shape[1]
    z_width = shift_mix.shape[1]
    z_pad = -(-z_width // LANES) * LANES
    qkv_width = 3 * ATTN_WIDTH
    assert z_width == 3 * rw_width + lora_d + lora_a + lora_g
    assert N_DEV * w_in.shape[2] == qkv_width + z_width + 2 * d_model
    for window, dilation in DILATED_GROUPS:
        assert window // dilation == ATTN_BLOCK and t_len % (dilation * ATTN_BLOCK) == 0

    shard_bf = {n: given[n][0].astype(BF16) for n in BIG}
    early = ['w_in', 'w_decay_up', 'w_aaa_up', 'w_gate_up']
    during_proj = ['w_mlp_out']
    during_scan = [n for n in BIG if n not in early + during_proj]
    late = during_proj + during_scan
    full = {n: _join_shards(g, 1) for n, g in zip(early, all_gather_many([shard_bf[n] for n in early], "gather_w_in"))}
    z_end = qkv_width + z_width
    w_all = jnp.concatenate([full['w_in'][:, :z_end], jnp.zeros((d_model, z_pad - z_width), BF16),
                             full['w_in'][:, z_end:]], axis=1)
    gates_at = qkv_width + z_pad

    (h_in,) = rowmap_fwd(lambda a, g: st_norm(a, g)[1:], [xs], [norm_mix], [BF16], "norm_in")
    proj, proj_gathered = project_and_gather(h_in, w_all, [shard_bf[n] for n in during_proj])
    qkv, z = proj[:, :qkv_width], proj[:, qkv_width:gates_at]
    gate_a, gate_r = proj[:, gates_at:gates_at + d_model], proj[:, gates_at + d_model:]

    q_raw = qkv[:, :ATTN_WIDTH].reshape(t_len * N_HEADS, HEAD_DIM)
    k_raw = qkv[:, ATTN_WIDTH:2 * ATTN_WIDTH].reshape(t_len * N_HEADS, HEAD_DIM)
    v_att = qkv[:, 2 * ATTN_WIDTH:]
    q_n, k_n = rowmap_fwd(st_qk_norm, [q_raw, k_raw], [q_gain, k_gain], [F32, F32], "qk_norm")
    buckets = _bucket_tables()
    bias = bias_fwd(rel_bias, buckets, "attn_bias")

    def to_blocks(tok, g):
        dil = DILATED_GROUPS[g][1]
        tok = tok.reshape(t_len, N_HEADS, HEAD_DIM)[:, g * HEADS_PER_GROUP:(g + 1) * HEADS_PER_GROUP]
        return tok.reshape(t_len // dil, dil, HEADS_PER_GROUP, HEAD_DIM).transpose(2, 1, 0, 3)

    def from_blocks(blk):
        return blk.transpose(2, 1, 0, 3).reshape(t_len, ATTN_OUT)

    att_in, att_o, att_l = [], [], []
    for g in range(len(DILATED_GROUPS)):
        hs = slice(g * HEADS_PER_GROUP, (g + 1) * HEADS_PER_GROUP)
        ops = (to_blocks(q_n, g), to_blocks(k_n, g), to_blocks(v_att, g), bias[hs, 0], bias[hs, 1])
        o_g, l_g = attn_fwd(*ops, name=f"attn_fwd_{g}")
        att_in.append(ops)
        att_o.append(from_blocks(o_g))
        att_l.append(from_blocks(l_g))
    (attn,) = rowmap_fwd(st_merge, att_o + att_l, [], [BF16], "attn_merge")

    c0 = rw_width
    cuts = [0, c0, 2 * c0, 3 * c0, 3 * c0 + lora_d, 3 * c0 + lora_d + lora_a, z_width]
    z_parts = [z[:, lo:hi] for lo, hi in zip(cuts[:-1], cuts[1:])]
    z_prev = [_shift_down(t) for t in z_parts]
    mixes = [shift_mix[:, lo:hi] for lo, hi in zip(cuts[:-1], cuts[1:])]
    pre_params = mixes + [w0, full['w_decay_up'], a0, full['w_aaa_up'], full['w_gate_up'], k_k, k_a]
    pre_out = rowmap_fwd(functools.partial(_st_rwkv_pre, _dot), z_parts + z_prev, pre_params, [F32] * 7, "rwkv_pre")
    r_s, lw_s, k_s, v_s, kk0_s, a_s, g_s = pre_out

    def heads(tok):
        return tok.reshape(t_len, n_rheads, RWKV_HEAD)

    def head_major(tok):
        return heads(tok).transpose(1, 0, 2)

    aa_s, bb_s = rowmap_fwd(st_rwkv_kk, [heads(kk0_s), heads(a_s)], [], [F32, F32], "rwkv_kk")
    scan_in = [head_major(r_s), head_major(lw_s), head_major(k_s), head_major(v_s),
               aa_s.transpose(1, 0, 2), bb_s.transpose(1, 0, 2)]
    y_h, s0_h, scan_gathered = scan_and_gather(scan_in, [shard_bf[n] for n in during_scan])
    late_gathered = list(proj_gathered) + list(scan_gathered)
    wt = {n: g if SHARD_AXIS[n] == 1 else g.reshape(-1, g.shape[2]) for n, g in zip(late, late_gathered)}
    y_s = y_h.transpose(1, 0, 2)
    post_params = [gn_w.reshape(1, n_rheads, RWKV_HEAD), gn_b.reshape(1, n_rheads, RWKV_HEAD), r_k]
    post_in = [y_s, heads(r_s), heads(k_s), heads(v_s), heads(g_s)]
    (rw,) = rowmap_fwd(st_rwkv_post, post_in, post_params, [BF16], "rwkv_post")
    rw = rw.reshape(t_len, rw_width)
    attn_d = matmul(attn, wt['w_attn_up'], 'nn', F32, "attn_up", b_shards=True)
    rwkv_d = matmul(rw, wt['w_rwkv_up'], 'nn', F32, "rwkv_up", b_shards=True)

    (merged,) = rowmap_fwd(st_gate, [gate_a, gate_r, attn_d, rwkv_d], [], [BF16], "gate_merge")
    mix_out = matmul(merged, wt['w_out'], 'nn', F32, "out_proj")
    x1, h_mlp = rowmap_fwd(st_res_norm, [xs, mix_out], [norm_mlp], [F32, BF16], "res_norm_mlp")
    u, act = matmul(h_mlp, wt['w_mlp_in'], 'nn', [F32, BF16], "mlp_in", b_shards=True,
                    epilogue=lambda acc: (acc,) + st_relu2(acc))
    mlp_out = matmul(act, wt['w_mlp_out'], 'nn', F32, "mlp_out")
    x2, h_ple = rowmap_fwd(st_res_norm, [x1, mlp_out], [norm_ple], [F32, BF16], "res_norm_ple")
    pg = matmul(h_ple, wt['w_ple_gate'], 'nn', F32, "ple_gate")
    pp = matmul(p_bf, wt['w_ple_proj'], 'nn', F32, "ple_proj", b_shards=True)
    dy, d_pg, d_pp, loss_local = loss_head(x2, pg, pp, target, "loss_head")

    def row_cut(full_grad):
        return full_grad.reshape(N_DEV, full_grad.shape[0] // N_DEV, full_grad.shape[1])

    grads, sends = {}, {}
    sends['w_ple_gate'] = row_cut(matmul(h_ple, d_pg, 'tn', BF16, "d_w_ple_gate"))
    sends['w_ple_proj'] = matmul(p_bf, d_pp, 'tn', BF16, "d_w_ple_proj", out_shards=True)
    d_h_ple = matmul(d_pg, wt['w_ple_gate'], 'nt', F32, "d_h_ple")
    (d_x2, d_x2_bf), (grads['norm_ple'],) = rowmap_bwd(
        st_res_norm, [x1, mlp_out], [norm_ple], [dy, d_h_ple], [F32, BF16], "res_norm_ple_bwd")
    sends['w_mlp_out'] = row_cut(matmul(act, d_x2_bf, 'tn', BF16, "d_w_mlp_out"))
    (d_u,) = matmul(d_x2_bf, wt['w_mlp_out'], 'nt', [BF16], "d_mlp_act", extras=[u],
                    epilogue=lambda d_act, u_blk: (d_act * (2.0 * jnp.maximum(u_blk, 0.0)),))
    sends['w_mlp_in'] = matmul(h_mlp, d_u, 'tn', BF16, "d_w_mlp_in", out_shards=True)
    d_h_mlp = matmul(d_u, wt['w_mlp_in'], 'nt', F32, "d_h_mlp", b_shards=True)
    (d_x1, d_x1_bf), (grads['norm_mlp'],) = rowmap_bwd(
        st_res_norm, [xs, mix_out], [norm_mlp], [d_x2, d_h_mlp], [F32, BF16], "res_norm_mlp_bwd")

    sends['w_out'] = row_cut(matmul(merged, d_x1_bf, 'tn', BF16, "d_w_out"))
    d_merged = matmul(d_x1_bf, wt['w_out'], 'nt', F32, "d_merged")
    (d_gate_a, d_gate_r, d_attn_d, d_rwkv_d), _ = rowmap_bwd(
        st_gate, [gate_a, gate_r, attn_d, rwkv_d], [], [d_merged], [BF16] * 4, "gate_merge_bwd")
    sends['w_attn_up'] = matmul(attn, d_attn_d, 'tn', BF16, "d_w_attn_up", out_shards=True)
    sends['w_rwkv_up'] = matmul(rw, d_rwkv_d, 'tn', BF16, "d_w_rwkv_up", out_shards=True)
    d_attn = matmul(d_attn_d, wt['w_attn_up'], 'nt', F32, "d_attn", b_shards=True)
    d_rw = matmul(d_rwkv_d, wt['w_rwkv_up'], 'nt', F32, "d_rw", b_shards=True)

    (d_y, d_r1, d_k1, d_v1, d_g), (d_gn_w, d_gn_b, grads['r_k']) = rowmap_bwd(
        st_rwkv_post, post_in, post_params, [heads(d_rw)], [F32] * 5, "rwkv_post_bwd")
    grads['gn_w'], grads['gn_b'] = d_gn_w.reshape(1, rw_width), d_gn_b.reshape(1, rw_width)
    core = lax.axis_index("c").astype(I32).reshape(1)

    def chip_partials(names, tag):
        mine = [sends[n].reshape((N_CHIPS, 2) + sends[n].shape[1:]) for n in names]
        theirs = pair_exchange(mine, "pair_grads_" + tag)
        return [pair_add(m, t, core, "pair_add_" + n) for n, m, t in zip(names, mine, theirs)]

    scan_grads, late_received = scan_bwd_and_scatter(scan_in, s0_h, d_y.transpose(1, 0, 2),
                                                     chip_partials(late, "late"))
    received = dict(zip(late, late_received))
    d_r2, d_lw, d_k2, d_v2, d_aa, d_bb = [t.transpose(1, 0, 2) for t in scan_grads]
    (d_kk0, d_a), _ = rowmap_bwd(st_rwkv_kk, [heads(kk0_s), heads(a_s)], [], [d_aa, d_bb], [F32, F32], "rwkv_kk_bwd")

    def flat(tok):
        return tok.reshape(t_len, rw_width)

    pre_cts = [[flat(d_r1), flat(d_r2)], flat(d_lw), [flat(d_k1), flat(d_k2)], [flat(d_v1), flat(d_v2)],
               flat(d_kk0), flat(d_a), flat(d_g)]
    d_zp, d_pre = rowmap_bwd(functools.partial(_st_rwkv_pre, _dot_ad), z_parts + z_prev, pre_params, pre_cts,
                             [F32] * 12, "rwkv_pre_bwd")
    grads['shift_mix'] = jnp.concatenate(d_pre[:6], axis=1)
    grads['w0'], d_w_decay, grads['a0'], d_w_aaa, d_w_gate, grads['k_k'], grads['k_a'] = d_pre[6:]
    for name, full_grad in (('w_decay_up', d_w_decay), ('w_aaa_up', d_w_aaa), ('w_gate_up', d_w_gate)):
        sends[name] = _split_shards(full_grad, 1).astype(BF16)
    z_fill = [jnp.zeros((t_len, z_pad - z_width), F32)] if z_pad > z_width else []
    d_z_cur = jnp.concatenate(d_zp[:6] + z_fill, axis=1)
    d_z_prev = _shift_up(jnp.concatenate(d_zp[6:] + z_fill, axis=1))
    (d_z,) = rowmap_fwd(st_add, [d_z_cur, d_z_prev], [], [BF16], "d_z_sum")

    d_merge, _ = rowmap_bwd(st_merge, att_o + att_l, [], [d_attn], [F32] * 6, "attn_merge_bwd")
    d_qn, d_kn, d_vs, d_bias = [], [], [], []
    for g in range(len(DILATED_GROUPS)):
        tok_g = functools.partial(to_blocks_grad, t_len=t_len, dil=DILATED_GROUPS[g][1])
        dq, dk, dv, dbp, dbc = attn_bwd(*att_in[g], tok_g(d_merge[g]), tok_g(d_merge[3 + g]), name=f"attn_bwd_{g}")
        d_qn.append(from_blocks(dq))
        d_kn.append(from_blocks(dk))
        d_vs.append(from_blocks(dv))
        d_bias.append(jnp.stack([dbp, dbc], axis=1))
    d_table = bias_bwd(jnp.concatenate(d_bias, axis=0), buckets, "attn_bias_bwd")
    grads['rel_bias'] = d_table[:, :N_HEADS]
    d_qn = jnp.concatenate(d_qn, axis=1).reshape(t_len * N_HEADS, HEAD_DIM)
    d_kn = jnp.concatenate(d_kn, axis=1).reshape(t_len * N_HEADS, HEAD_DIM)
    (d_q, d_k), (grads['q_gain'], grads['k_gain']) = rowmap_bwd(
        st_qk_norm, [q_raw, k_raw], [q_gain, k_gain], [d_qn, d_kn], [BF16, BF16], "qk_norm_bwd")
    d_qkv = jnp.concatenate([d_q.reshape(t_len, ATTN_WIDTH), d_k.reshape(t_len, ATTN_WIDTH)]
                            + [t.astype(BF16) for t in d_vs], axis=1)

    d_gates = jnp.concatenate([d_gate_a, d_gate_r], axis=1)
    sends['w_in'] = _split_shards(jnp.concatenate([
        matmul(h_in, d_qkv, 'tn', BF16, "d_w_qkv"),
        matmul(h_in, d_z, 'tn', BF16, "d_w_z")[:, :z_width],
        matmul(h_in, d_gates, 'tn', BF16, "d_w_gates")], axis=1), 1)
    d_h_in, early_received = input_grad_and_scatter([d_qkv, d_z, d_gates], w_all, [0, qkv_width, gates_at],
                                                    chip_partials(early, "early"))
    received.update(zip(early, early_received))
    (grad_x,), (grads['norm_mix'],) = rowmap_bwd(st_norm, [xs], [norm_mix], [d_x1, d_h_in], [F32], "norm_in_bwd")

    by_name = {n: adamw_shard(received[n], given[n], given['m_' + n], given['v_' + n], "adamw_" + n) for n in BIG}
    pk = lambda prefix: _pack([given[prefix + n] for n in SMALL], F32)[0]
    small_buf, small_layout = _pack([grads[n].reshape(given[n].shape) for n in SMALL], F32)
    (small_all,) = all_gather_many([small_buf], "gather_small_grads")
    small_out = adamw(small_all, pk(''), pk('m_'), pk('v_'), "adamw_replicated")
    small_shapes = [given[n].shape for n in SMALL]
    by_name.update(zip(SMALL, zip(*[_unpack(buf, small_layout, small_shapes) for buf in small_out])))
    loss = lax.psum(loss_local, MESH_AXES)
    return (loss, grad_x[None], *[by_name[n][0] for n in WEIGHTS], *[by_name[n][1] for n in WEIGHTS],
            *[by_name[n][2] for n in WEIGHTS], *[by_name[n][3] for n in WEIGHTS])


def to_blocks_grad(tok, t_len, dil):
    return tok.reshape(t_len // dil, dil, HEADS_PER_GROUP, HEAD_DIM).transpose(2, 1, 0, 3)
```

```python
import functools
import math

import jax
import jax.numpy as jnp
from jax import lax
from jax.experimental import pallas as pl
from jax.experimental.pallas import tpu as pltpu

F32 = jnp.float32
BF16 = jnp.bfloat16
I32 = jnp.int32

N_DEV = 8
N_CHIPS = 4
MESH_AXES = ("x", "y", "c")
LANES = 128
PACK_ROWS = 16
VMEM_LIMIT_BYTES = 48 * 2**20
ROW_BLOCK_BYTES = 3 * 2**20

HEAD_DIM = 128
ATTN_BLOCK = 128
HEADS_PER_GROUP = 4
DILATED_GROUPS = ((128, 1), (512, 4), (2048, 16))
N_HEADS = HEADS_PER_GROUP * len(DILATED_GROUPS)
ATTN_WIDTH = N_HEADS * HEAD_DIM
ATTN_OUT = HEADS_PER_GROUP * HEAD_DIM
N_BUCKETS = 32
MAX_DISTANCE = 2048
RWKV_HEAD = 64
RWKV_CHUNK = 64
RWKV_HEADS_PER_STEP = 8
RMS_EPS = 1e-6
GN_EPS = 64e-5
NEG_INF = -1e30

ADAM_LR = 0.001
ADAM_B1 = 0.9
ADAM_B2 = 0.999
ADAM_EPS = 1e-08
ADAM_WD = 0.01
ADAM_STEP = 10

WEIGHTS = ['norm_mix', 'w_in', 'q_gain', 'k_gain', 'rel_bias', 'w_attn_up', 'shift_mix', 'w0', 'w_decay_up', 'a0',
           'w_aaa_up', 'w_gate_up', 'k_k', 'k_a', 'r_k', 'gn_w', 'gn_b', 'w_rwkv_up', 'w_out', 'norm_mlp', 'w_mlp_in',
           'w_mlp_out', 'norm_ple', 'w_ple_gate', 'w_ple_proj']
SHARD_AXIS = {'w_in': 1, 'w_attn_up': 1, 'w_decay_up': 1, 'w_aaa_up': 1, 'w_gate_up': 1, 'w_rwkv_up': 1, 'w_out': 0,
              'w_mlp_in': 1, 'w_mlp_out': 0, 'w_ple_gate': 0, 'w_ple_proj': 1}
BIG = [n for n in WEIGHTS if n in SHARD_AXIS]
SMALL = [n for n in WEIGHTS if n not in SHARD_AXIS]


def _params(sem):
    return pltpu.CompilerParams(dimension_semantics=sem, vmem_limit_bytes=VMEM_LIMIT_BYTES)


_DN = {'nn': (((1,), (0,)), ((), ())), 'nt': (((1,), (1,)), ((), ())), 'tn': (((0,), (0,)), ((), ()))}


def _dot(a, b, mode, exact):
    if exact:
        return lax.dot_general(a, b, _DN[mode], precision=lax.Precision.HIGH, preferred_element_type=F32)
    return lax.dot_general(a.astype(BF16), b.astype(BF16), _DN[mode], preferred_element_type=F32)


@functools.partial(jax.custom_vjp, nondiff_argnums=(2, 3))
def _dot_ad(a, b, mode, exact):
    return _dot(a, b, mode, exact)


def _dot_ad_fwd(a, b, mode, exact):
    return _dot(a, b, mode, exact), (a, b)


def _dot_ad_bwd(mode, exact, res, g):
    a, b = res
    if mode == 'nn':
        return _dot(g, b, 'nt', exact), _dot(a, g, 'tn', exact)
    if mode == 'nt':
        return _dot(g, b, 'nn', exact), _dot(g, a, 'tn', exact)
    return _dot(b, g, 'nt', exact), _dot(a, g, 'nn', exact)


_dot_ad.defvjp(_dot_ad_fwd, _dot_ad_bwd)


def _pick(n, cands):
    for c in cands:
        if n % c == 0:
            return c
    return n


def matmul(a, b, mode, out_dtype, name, b_shards=False, out_shards=False, b_col_offsets=None, epilogue=None,
           extras=(), exchange=None):
    a_list = list(a) if isinstance(a, (list, tuple)) else [a]
    b_list = list(b) if isinstance(b, (list, tuple)) else [b]
    seg = len(a_list)
    assert all(t.dtype == BF16 for t in a_list + b_list), name
    assert seg == 1 or (mode == 'nt' and not b_shards), name
    m = a_list[0].shape[1] if mode == 'tn' else a_list[0].shape[0]
    ks = [t.shape[0] if mode == 'tn' else t.shape[1] for t in a_list]
    b0 = b_list[0]
    b_rows, b_cols = (b0.shape[1], N_DEV * b0.shape[2]) if b_shards else b0.shape
    n = b_rows if mode == 'nt' else b_cols
    if seg == 1 and b_col_offsets is None:
        assert (b_cols if mode == 'nt' else b_rows) == ks[0], (name, a_list[0].shape, b0.shape)
    offsets = list(b_col_offsets) if b_col_offsets is not None else [0] * seg
    tm = _pick(m, (1024, 512, 256, 128))
    tn = _pick(n // N_DEV if (out_shards or (b_shards and mode != 'nt')) else n, (1024, 512, 256, 128))
    k_units = [kk // N_DEV if (b_shards and mode == 'nt') else kk for kk in ks] + [o for o in offsets if o]
    k_cands = (1024, 512, 256, 128) if (epilogue is not None or extras) else (2048, 1024, 512, 256, 128)
    tk = next((c for c in k_cands if all(u % c == 0 for u in k_units)), k_units[0])
    nks = [kk // tk for kk in ks]
    starts = [sum(nks[:s]) for s in range(seg)]
    nk = sum(nks)
    grid = (m // tm, n // tn, nk)
    total_steps = grid[0] * grid[1] * nk
    kind, moved = exchange if exchange is not None else (None, [])
    nx, ne = len(moved), len(extras)
    out_dtypes = list(out_dtype) if isinstance(out_dtype, (list, tuple)) else [out_dtype]
    no = len(out_dtypes)

    def body(*refs):
        a_refs, b_refs, x_refs = refs[:seg], refs[seg:2 * seg], refs[2 * seg:2 * seg + ne]
        pos = 2 * seg + ne
        moved_in, o_refs = refs[pos:pos + nx], refs[pos + nx:pos + nx + no]
        pos += nx + no
        moved_out, acc_ref, sems = refs[pos:pos + nx], refs[pos + nx], refs[pos + nx + 1:]
        kk = pl.program_id(2)
        step = (pl.program_id(0) * grid[1] + pl.program_id(1)) * nk + kk
        if kind == 'gather':
            start, forward, finish = _gather_schedule(moved_in, moved_out, *sems)
            pl.when(step == 0)(start)
            pl.when(step == total_steps - 1 - total_steps // 8)(forward)
        elif kind == 'scatter':
            start, finish = _scatter_schedule(moved_in, moved_out, *sems)
            pl.when(step == 0)(start)

        @pl.when(kk == 0)
        def _():
            acc_ref[...] = jnp.zeros_like(acc_ref)

        for s in range(seg):
            def accumulate(s=s):
                acc_ref[...] += lax.dot_general(a_refs[s][...], b_refs[s][...], _DN[mode], preferred_element_type=F32)

            if seg == 1:
                accumulate()
            else:
                pl.when(jnp.logical_and(kk >= starts[s], kk < starts[s] + nks[s]))(accumulate)

        @pl.when(kk == nk - 1)
        def _():
            acc = acc_ref[...]
            outs = (acc,) if epilogue is None else epilogue(acc, *[x[...] for x in x_refs])
            for r, v in zip(o_refs, outs):
                r[...] = v.astype(r.dtype)

        if kind is not None:
            pl.when(step == total_steps - 1)(finish)

    def k_of(kk, s):
        return kk if seg == 1 else jnp.clip(kk - starts[s], 0, nks[s] - 1)

    a_specs, b_specs = [], []
    for s in range(seg):
        off = offsets[s] // tk
        if mode == 'tn':
            a_specs.append(pl.BlockSpec((tk, tm), lambda i, j, kk, s=s: (k_of(kk, s), i)))
        else:
            a_specs.append(pl.BlockSpec((tm, tk), lambda i, j, kk, s=s: (i, k_of(kk, s))))
        if mode == 'nt':
            if b_shards:
                per = b0.shape[2] // tk
                b_specs.append(pl.BlockSpec((None, tn, tk), lambda i, j, kk: (kk // per, j, kk % per)))
            else:
                b_specs.append(pl.BlockSpec((tn, tk), lambda i, j, kk, s=s, off=off: (j, off + k_of(kk, s))))
        else:
            if b_shards:
                per = b0.shape[2] // tn
                b_specs.append(pl.BlockSpec((None, tk, tn), lambda i, j, kk: (j // per, kk, j % per)))
            else:
                b_specs.append(pl.BlockSpec((tk, tn), lambda i, j, kk: (kk, j)))
    tile = pl.BlockSpec((tm, tn), lambda i, j, kk: (i, j))
    if out_shards:
        assert epilogue is None
        per_o = n // N_DEV // tn
        o_specs = [pl.BlockSpec((None, tm, tn), lambda i, j, kk: (j // per_o, i, j % per_o))]
        o_shapes = [jax.ShapeDtypeStruct((N_DEV, m, n // N_DEV), out_dtypes[0])]
    else:
        o_specs = [tile] * no
        o_shapes = [jax.ShapeDtypeStruct((m, n), d) for d in out_dtypes]
    moved_shapes = [jax.ShapeDtypeStruct(((N_DEV,) + t.shape) if kind == 'gather' else t.shape, t.dtype) for t in moved]
    res = pl.pallas_call(
        body, grid=grid, in_specs=a_specs + b_specs + [tile] * ne + [_ANY] * nx,
        out_specs=o_specs + [_ANY] * nx, out_shape=o_shapes + moved_shapes,
        scratch_shapes=[pltpu.VMEM((tm, tn), F32)] + (_exchange_scratch(nx) if nx else []),
        compiler_params=_params(("arbitrary",) * 3 if nx else ("parallel", "parallel", "arbitrary")), name=name,
    )(*a_list, *b_list, *extras, *moved)
    result = res[0] if (epilogue is None) else list(res[:no])
    return (result, list(res[no:])) if nx else result


def _row_bytes(shape, dtype):
    dims = list(shape[1:])
    dims[-1] = -(-dims[-1] // LANES) * LANES
    return math.prod(dims) * jnp.dtype(dtype).itemsize


def _row_tile(n, row_bytes):
    t = 1024
    while t > 16 and (n % t or t * row_bytes > ROW_BLOCK_BYTES):
        t //= 2
    assert n % t == 0, (n, t)
    return t


def rowmap(fn, tiled, bcast, out_tiled, out_acc, name):
    n = tiled[0].shape[0]
    tile = _row_tile(n, sum(_row_bytes(t.shape, t.dtype) for t in list(tiled) + list(out_tiled)))
    n_in, n_out = len(tiled) + len(bcast), len(out_tiled)

    def body(*refs):
        outs, accs = fn(*[r[...] for r in refs[:n_in]])
        assert len(outs) == n_out and len(accs) == len(out_acc), name
        for r, v in zip(refs[n_in:n_in + n_out], outs):
            r[...] = v.astype(r.dtype)
        acc_refs = refs[n_in + n_out:]
        if acc_refs:
            @pl.when(pl.program_id(0) == 0)
            def _():
                for r, v in zip(acc_refs, accs):
                    r[...] = v.astype(r.dtype)

            @pl.when(pl.program_id(0) != 0)
            def _():
                for r, v in zip(acc_refs, accs):
                    r[...] += v.astype(r.dtype)

    def tspec(s):
        nd = len(s.shape)
        return pl.BlockSpec((tile,) + tuple(s.shape[1:]), lambda i, nd=nd: (i,) + (0,) * (nd - 1))

    def bspec(s):
        nd = len(s.shape)
        return pl.BlockSpec(tuple(s.shape), lambda i, nd=nd: (0,) * nd)

    res = pl.pallas_call(
        body, grid=(n // tile,),
        in_specs=[tspec(t) for t in tiled] + [bspec(t) for t in bcast],
        out_specs=[tspec(t) for t in out_tiled] + [bspec(t) for t in out_acc],
        out_shape=list(out_tiled) + list(out_acc),
        compiler_params=_params(("arbitrary",)), name=name,
    )(*tiled, *bcast)
    return list(res[:n_out]), list(res[n_out:])


def rowmap_fwd(fwd, tiled, bcast, out_dtypes, name):
    shapes = jax.eval_shape(fwd, *tiled, *bcast)
    out_tiled = [jax.ShapeDtypeStruct(s.shape, d) for s, d in zip(shapes, out_dtypes)]
    outs, _ = rowmap(lambda *blk: (fwd(*[b.astype(F32) for b in blk]), ()), tiled, bcast, out_tiled, [], name)
    return outs


def rowmap_bwd(fwd, tiled, bcast, cts, want, name):
    cts = [[] if c is None else (list(c) if isinstance(c, (list, tuple)) else [c]) for c in cts]
    flat_cts = [c for group in cts for c in group]
    nt_, nc_ = len(tiled), len(flat_cts)

    def fn(*blk):
        ins = [b.astype(F32) for b in blk[:nt_]] + [b.astype(F32) for b in blk[nt_ + nc_:]]
        ctb = list(blk[nt_:nt_ + nc_])
        outs, vjp = jax.vjp(fwd, *ins)
        full = []
        for o, group in zip(outs, cts):
            acc = jnp.zeros_like(o)
            for _ in group:
                acc = acc + ctb.pop(0).astype(F32)
            full.append(acc)
        g = vjp(tuple(full))
        return [g[i] for i in range(nt_) if want[i] is not None], list(g[nt_:])

    out_tiled = [jax.ShapeDtypeStruct(t.shape, w) for t, w in zip(tiled, want) if w is not None]
    out_acc = [jax.ShapeDtypeStruct(b.shape, F32) for b in bcast]
    return rowmap(fn, list(tiled) + flat_cts, bcast, out_tiled, out_acc, name)


def _rms(x, gain):
    return x * lax.rsqrt(jnp.mean(jnp.square(x), axis=-1, keepdims=True) + RMS_EPS) * gain


def _sigmoid(x):
    return 1.0 / (1.0 + jnp.exp(-x))


def _softplus(x):
    return jnp.maximum(x, 0.0) + jnp.log(1.0 + jnp.exp(-jnp.abs(x)))


def st_norm(x, gain):
    return x, _rms(x, gain)


def st_res_norm(x, delta, gain):
    y = x + delta
    return y, _rms(y, gain)


def st_qk_norm(q, k, q_gain, k_gain):
    return _rms(q, q_gain), _rms(k, k_gain)


def st_merge(o0, o1, o2, l0, l1, l2):
    m = jnp.maximum(jnp.maximum(l0, l1), l2)
    e0, e1, e2 = jnp.exp(l0 - m), jnp.exp(l1 - m), jnp.exp(l2 - m)
    return ((e0 * o0 + e1 * o1 + e2 * o2) / (e0 + e1 + e2),)


def _st_rwkv_pre(dot, zr, zk, zv, xw, xa, xg, pr, pk, pv, pw, pa, pg, mr, mk, mv, mw, ma, mg,
                 w0, w_decay, a0, w_aaa, w_gate, k_k, k_a):
    def shift(cur, prev, mix):
        return cur + mix * (prev - cur)

    r, k, v = shift(zr, pr, mr), shift(zk, pk, mk), shift(zv, pv, mv)
    xw, xa, xg = shift(xw, pw, mw), shift(xa, pa, ma), shift(xg, pg, mg)
    w = -_softplus(-(w0 + dot(jnp.tanh(xw), w_decay, 'nn', False))) - 0.5
    a = _sigmoid(a0 + dot(xa, w_aaa, 'nn', False))
    g = dot(_sigmoid(xg), w_gate, 'nn', False)
    log_decay = -jnp.exp(w)
    return r, log_decay, k * (1.0 + (a - 1.0) * k_a), v, k * k_k, a, g


def st_rwkv_kk(kk0, a):
    kk = kk0 / jnp.maximum(jnp.sqrt(jnp.sum(jnp.square(kk0), axis=-1, keepdims=True)), 1e-12)
    return -kk, kk * a


def st_rwkv_post(y, r, k, v, g, gn_w, gn_b, r_k):
    mu = jnp.mean(y, axis=-1, keepdims=True)
    var = jnp.mean(jnp.square(y - mu), axis=-1, keepdims=True)
    out = (y - mu) * lax.rsqrt(var + GN_EPS) * gn_w + gn_b
    out = out + jnp.sum(r * k * r_k, axis=-1, keepdims=True) * v
    return (out * g,)


def st_gate(g0, g1, attn_d, rwkv_d):
    return (_sigmoid(g0) * attn_d + _sigmoid(g1) * rwkv_d,)


def st_relu2(u):
    return (jnp.square(jnp.maximum(u, 0.0)),)


def st_add(a, b):
    return (a + b,)


def loss_head(x2, pg, pp, target, name):
    d_model = x2.shape[1]

    def fn(x2, pg, pp, tgt):
        s = _sigmoid(pg)
        err = x2 + s * pp - tgt
        dy = err * (1.0 / d_model)
        part = 0.5 * jnp.sum(jnp.square(err)) * (1.0 / d_model)
        return [dy, dy * pp * s * (1.0 - s), dy * s], [jnp.full((8, LANES), part, F32)]

    sds = jax.ShapeDtypeStruct
    outs, accs = rowmap(fn, [x2, pg, pp, target], [],
                        [sds(x2.shape, F32), sds(x2.shape, BF16), sds(x2.shape, BF16)], [sds((8, LANES), F32)], name)
    return outs[0], outs[1], outs[2], accs[0][0, 0]


def _attn_block(dot, q, kp, kc, vp, vc, bp, bc, prev_offset):
    blk = q.shape[0]
    qi = lax.broadcasted_iota(I32, (blk, blk), 0)
    ki = lax.broadcasted_iota(I32, (blk, blk), 1)
    mask_c = ki <= qi
    mask_p = ki >= qi + prev_offset
    scale = HEAD_DIM ** -0.5
    s_c = jnp.where(mask_c, dot(q, kc, 'nt', False) * scale + bc, NEG_INF)
    s_p = jnp.where(mask_p, dot(q, kp, 'nt', False) * scale + bp, NEG_INF)
    m = lax.stop_gradient(jnp.maximum(jnp.max(s_c, axis=1, keepdims=True), jnp.max(s_p, axis=1, keepdims=True)))
    e_c = jnp.where(mask_c, jnp.exp(s_c - m), 0.0)
    e_p = jnp.where(mask_p, jnp.exp(s_p - m), 0.0)
    l = jnp.sum(e_c, axis=1, keepdims=True) + jnp.sum(e_p, axis=1, keepdims=True)
    o = (dot(e_c, vc, 'nn', False) + dot(e_p, vp, 'nn', False)) / l
    return o, jnp.broadcast_to(m + jnp.log(l), o.shape)


class _ClassView:
    def __init__(self, tokens, dilation, first_col):
        self.view = tokens.reshape(tokens.shape[0] // dilation, dilation * tokens.shape[1])
        self.tiles, self.first = tokens.shape[1] // HEAD_DIM, first_col // HEAD_DIM

    def spec(self, shift, nb):
        tiles, first = self.tiles, self.first
        return pl.BlockSpec((ATTN_BLOCK, HEAD_DIM),
                            lambda h, r, n: (jnp.clip(n + shift, 0, nb - 1), r * tiles + first + h))


_BIAS_SPEC = pl.BlockSpec((None, ATTN_BLOCK, ATTN_BLOCK), lambda h, r, n: (h, 0, 0))


def attn_fwd(q, k, v, bp, bc, dilation, name):
    t_len = q[0].shape[0]
    nb = t_len // dilation // ATTN_BLOCK
    qv, kv, vv = (_ClassView(arr, dilation, col) for arr, col in (q, k, v))
    o_shape = jax.ShapeDtypeStruct((t_len // dilation, dilation * ATTN_OUT), F32)
    o_spec = pl.BlockSpec((ATTN_BLOCK, HEAD_DIM), lambda h, r, n: (n, r * HEADS_PER_GROUP + h))

    def body(q_ref, kp_ref, kc_ref, vp_ref, vc_ref, bp_ref, bc_ref, o_ref, l_ref):
        off = jnp.where(pl.program_id(2) > 0, 0, ATTN_BLOCK)
        o, l = _attn_block(_dot, q_ref[...], kp_ref[...], kc_ref[...], vp_ref[...], vc_ref[...], bp_ref[...],
                           bc_ref[...], off)
        o_ref[...] = o
        l_ref[...] = l

    o, l = pl.pallas_call(
        body, grid=(HEADS_PER_GROUP, dilation, nb),
        in_specs=[qv.spec(0, nb), kv.spec(-1, nb), kv.spec(0, nb), vv.spec(-1, nb), vv.spec(0, nb),
                  _BIAS_SPEC, _BIAS_SPEC],
        out_specs=[o_spec, o_spec], out_shape=[o_shape] * 2,
        compiler_params=_params(("parallel", "parallel", "parallel")), name=name,
    )(qv.view, kv.view, kv.view, vv.view, vv.view, bp, bc)
    return o.reshape(t_len, ATTN_OUT), l.reshape(t_len, ATTN_OUT)


def attn_bwd(q, k, v, bp, bc, do, dl, dilation, name):
    t_len = q[0].shape[0]
    blk = ATTN_BLOCK
    nb = t_len // dilation // blk
    qv, kv, vv = (_ClassView(arr, dilation, col) for arr, col in (q, k, v))
    dov, dlv = _ClassView(do, dilation, 0), _ClassView(dl, dilation, 0)
    cur, bias = dov.spec(0, nb), _BIAS_SPEC

    def body(q_ref, kp_ref, kc_ref, vp_ref, vc_ref, bp_ref, bc_ref, do_ref, dl_ref,
             dq_ref, dkp_ref, dkc_ref, dvp_ref, dvc_ref, dbp_ref, dbc_ref):
        off = jnp.where(pl.program_id(2) > 0, 0, blk)
        f = functools.partial(_attn_block, _dot_ad, prev_offset=off)
        _, vjp = jax.vjp(f, q_ref[...], kp_ref[...], kc_ref[...], vp_ref[...], vc_ref[...], bp_ref[...], bc_ref[...])
        dq, dkp, dkc, dvp, dvc, dbp, dbc = vjp((do_ref[...], dl_ref[...]))
        dq_ref[...] = dq
        dkp_ref[...] = dkp
        dkc_ref[...] = dkc
        dvp_ref[...] = dvp
        dvc_ref[...] = dvc
        first = jnp.logical_and(pl.program_id(1) == 0, pl.program_id(2) == 0)

        @pl.when(first)
        def _():
            dbp_ref[...] = dbp
            dbc_ref[...] = dbc

        @pl.when(jnp.logical_not(first))
        def _():
            dbp_ref[...] += dbp
            dbc_ref[...] += dbc

    blocks = jax.ShapeDtypeStruct(dov.view.shape, F32)
    grid = (HEADS_PER_GROUP, dilation, nb)
    dq, dkp, dkc, dvp, dvc, dbp, dbc = pl.pallas_call(
        body, grid=grid,
        in_specs=[qv.spec(0, nb), kv.spec(-1, nb), kv.spec(0, nb), vv.spec(-1, nb), vv.spec(0, nb), bias, bias,
                  cur, dlv.spec(0, nb)],
        out_specs=[cur] * 5 + [bias] * 2, out_shape=[blocks] * 5 + [jax.ShapeDtypeStruct(bp.shape, F32)] * 2,
        compiler_params=_params(("arbitrary", "arbitrary", "arbitrary")), name=name,
    )(qv.view, kv.view, kv.view, vv.view, vv.view, bp, bc, dov.view, dlv.view)

    nxt = dov.spec(1, nb)

    def add_body(kc_ref, kp_ref, vc_ref, vp_ref, dk_ref, dv_ref):
        has_next = (pl.program_id(2) + 1 < nb).astype(F32)
        dk_ref[...] = kc_ref[...] + kp_ref[...] * has_next
        dv_ref[...] = vc_ref[...] + vp_ref[...] * has_next

    dk, dv = pl.pallas_call(
        add_body, grid=grid, in_specs=[cur, nxt, cur, nxt], out_specs=[cur, cur], out_shape=[blocks] * 2,
        compiler_params=_params(("parallel", "parallel", "parallel")), name=name + "_kv",
    )(dkc, dkp, dvc, dvp)
    return [t.reshape(t_len, ATTN_OUT) for t in (dq, dk, dv)] + [dbp, dbc]


def _t5_bucket(dist):
    max_exact = N_BUCKETS // 2
    d_f = jnp.maximum(dist, 1).astype(F32)
    large = max_exact + (jnp.log(d_f / max_exact) / math.log(MAX_DISTANCE / max_exact)
                         * (N_BUCKETS - max_exact)).astype(I32)
    large = jnp.minimum(large, N_BUCKETS - 1)
    return jnp.where(dist < max_exact, dist, large)


def _bucket_tables():
    blk = ATTN_BLOCK
    qi = jnp.arange(blk)[:, None]
    ki = jnp.arange(blk)[None, :]
    out = []
    for _, dilation in DILATED_GROUPS:
        rel_p = jnp.maximum(blk + qi - ki, 0) * dilation
        rel_c = jnp.maximum(qi - ki, 0) * dilation
        out.append(jnp.stack([_t5_bucket(rel_p), _t5_bucket(rel_c)]))
    return jnp.stack(out).astype(I32)


def bias_fwd(table, buckets, name):
    blk = ATTN_BLOCK

    def body(tab_ref, bkt_ref, out_ref):
        for g in range(len(DILATED_GROUPS)):
            for half in range(2):
                bk = bkt_ref[g, half]
                for hh in range(HEADS_PER_GROUP):
                    h = g * HEADS_PER_GROUP + hh
                    acc = jnp.zeros((blk, blk), F32)
                    for b in range(N_BUCKETS):
                        acc = jnp.where(bk == b, tab_ref[b, h], acc)
                    out_ref[h, half] = acc

    return pl.pallas_call(
        body, in_specs=[pl.BlockSpec(memory_space=pltpu.SMEM), pl.BlockSpec(memory_space=pltpu.VMEM)],
        out_specs=pl.BlockSpec(memory_space=pltpu.VMEM),
        out_shape=jax.ShapeDtypeStruct((N_HEADS, 2, blk, blk), F32), name=name,
    )(table, buckets)


def bias_bwd(dbias, buckets, name):
    def body(db_ref, bkt_ref, out_ref):
        rows = lax.broadcasted_iota(I32, (N_BUCKETS, LANES), 0)
        cols = lax.broadcasted_iota(I32, (N_BUCKETS, LANES), 1)
        acc = jnp.zeros((N_BUCKETS, LANES), F32)
        for g in range(len(DILATED_GROUPS)):
            bk_p, bk_c = bkt_ref[g, 0], bkt_ref[g, 1]
            for hh in range(HEADS_PER_GROUP):
                h = g * HEADS_PER_GROUP + hh
                d_p, d_c = db_ref[h, 0], db_ref[h, 1]
                for b in range(N_BUCKETS):
                    s = jnp.sum(jnp.where(bk_p == b, d_p, 0.0)) + jnp.sum(jnp.where(bk_c == b, d_c, 0.0))
                    acc = jnp.where(jnp.logical_and(rows == b, cols == h), s, acc)
        out_ref[...] = acc

    return pl.pallas_call(
        body, in_specs=[pl.BlockSpec(memory_space=pltpu.VMEM)] * 2, out_specs=pl.BlockSpec(memory_space=pltpu.VMEM),
        out_shape=jax.ShapeDtypeStruct((N_BUCKETS, LANES), F32), name=name,
    )(dbias, buckets)


def _rwkv_chunk(dot, s0, r, lw, k, v, a, b):
    def each(f, *lists):
        return [f(*xs) for xs in zip(*lists)]

    def mm(mode):
        return lambda p, q: dot(p, q, mode, True)

    def mul(p, q):
        return p * q

    def add(p, q):
        return p + q

    c = r[0].shape[0]
    ti = lax.broadcasted_iota(I32, (c, c), 0)
    si = lax.broadcasted_iota(I32, (c, c), 1)
    incl = si <= ti
    strict = si < ti
    ones_incl = incl.astype(F32)
    cum = each(lambda x: dot(ones_incl, x, 'nn', True), lw)
    w_incl = each(jnp.exp, cum)
    w_prev = each(lambda cu, x: jnp.exp(cu - x), cum, lw)
    w_inv = each(lambda cu: jnp.exp(-cu), cum)
    w_end = each(lambda x: jnp.exp(jnp.sum(x, axis=0, keepdims=True)), lw)
    a_t, r_t, b_t, k_t = each(mul, a, w_prev), each(mul, r, w_incl), each(mul, b, w_inv), each(mul, k, w_inv)
    l_ab = each(lambda p, q: jnp.where(strict, dot(p, q, 'nt', True), 0.0), a_t, b_t)
    l_ak = each(lambda p, q: jnp.where(strict, dot(p, q, 'nt', True), 0.0), a_t, k_t)
    u = each(add, each(mm('nt'), a_t, s0), each(mm('nn'), l_ak, v))
    u = each(add, u, each(mm('nn'), l_ab, u))
    power = l_ab
    for _ in range(int(math.log2(c)) - 1):
        power = each(mm('nn'), power, power)
        u = each(add, u, each(mm('nn'), power, u))
    m_rb = each(lambda p, q: jnp.where(incl, dot(p, q, 'nt', True), 0.0), r_t, b_t)
    m_rk = each(lambda p, q: jnp.where(incl, dot(p, q, 'nt', True), 0.0), r_t, k_t)
    y = each(add, each(add, each(mm('nt'), r_t, s0), each(mm('nn'), m_rb, u)), each(mm('nn'), m_rk, v))
    s1 = each(add, each(add, s0, each(mm('tn'), u, b_t)), each(mm('tn'), v, k_t))
    return y, each(mul, s1, w_end)


def rwkv_fwd(r, lw, k, v, a, b, name, gather=()):
    h, t, n = r.shape
    c = RWKV_CHUNK
    nc = t // c
    hs = math.gcd(h, RWKV_HEADS_PER_STEP)
    steps = h // hs * nc
    ng = len(gather)
    row = pl.BlockSpec((hs, c, n), lambda i, j: (i, j, 0))

    def body(*refs):
        ins, x_refs = refs[:6], refs[6:6 + ng]
        y_ref, s0_ref = refs[6 + ng:8 + ng]
        out_refs, state, sems = refs[8 + ng:8 + 2 * ng], refs[8 + 2 * ng], refs[9 + 2 * ng:]
        step = pl.program_id(0) * nc + pl.program_id(1)
        if ng:
            start, forward, finish = _gather_schedule(x_refs, out_refs, *sems)
            pl.when(step == 0)(start)
            pl.when(step == steps - 1 - steps // 8)(forward)

        @pl.when(pl.program_id(1) == 0)
        def _():
            state[...] = jnp.zeros_like(state)

        s0 = [state[q] for q in range(hs)]
        y, s1 = _rwkv_chunk(_dot, s0, *[[ref[q] for q in range(hs)] for ref in ins])
        for q in range(hs):
            s0_ref[q] = s0[q]
            y_ref[q] = y[q]
            state[q] = s1[q]
        if ng:
            pl.when(step == steps - 1)(finish)

    return pl.pallas_call(
        body, grid=(h // hs, nc), in_specs=[row] * 6 + [_ANY] * ng,
        out_specs=[row, pl.BlockSpec((hs, None, n, n), lambda i, j: (i, j, 0, 0))] + [_ANY] * ng,
        out_shape=[jax.ShapeDtypeStruct((h, t, n), F32), jax.ShapeDtypeStruct((h, nc, n, n), F32)]
        + [jax.ShapeDtypeStruct((N_DEV,) + g.shape, g.dtype) for g in gather],
        scratch_shapes=[pltpu.VMEM((hs, n, n), F32)] + (_exchange_scratch(ng) if ng else []),
        compiler_params=_params(("arbitrary", "arbitrary")), name=name,
    )(r, lw, k, v, a, b, *gather)


def rwkv_bwd(r, lw, k, v, a, b, s0, dy, name, scatter=()):
    h, t, n = r.shape
    c = RWKV_CHUNK
    nc = t // c
    hs = math.gcd(h, RWKV_HEADS_PER_STEP)
    steps = h // hs * nc
    ns = len(scatter)
    row = pl.BlockSpec((hs, c, n), lambda i, j: (i, nc - 1 - j, 0))
    st = pl.BlockSpec((hs, None, n, n), lambda i, j: (i, nc - 1 - j, 0, 0))

    def body(*refs):
        ins, s0_ref, dy_ref, send_refs = refs[:6], refs[6], refs[7], refs[8:8 + ns]
        grad_refs, recv_refs = refs[8 + ns:14 + ns], refs[14 + ns:14 + 2 * ns]
        dstate, sems = refs[14 + 2 * ns], refs[15 + 2 * ns:]
        step = pl.program_id(0) * nc + pl.program_id(1)
        if ns:
            start, finish = _scatter_schedule(send_refs, recv_refs, *sems)
            pl.when(step == 0)(start)

        @pl.when(pl.program_id(1) == 0)
        def _():
            dstate[...] = jnp.zeros_like(dstate)

        per_head = range(hs)
        _, vjp = jax.vjp(functools.partial(_rwkv_chunk, _dot_ad), [s0_ref[q] for q in per_head],
                         *[[ref[q] for q in per_head] for ref in ins])
        grads = vjp(([dy_ref[q] for q in per_head], [dstate[q] for q in per_head]))
        for q in per_head:
            dstate[q] = grads[0][q]
            for ref, g in zip(grad_refs, grads[1:]):
                ref[q] = g[q]
        if ns:
            pl.when(step == steps - 1)(finish)

    return pl.pallas_call(
        body, grid=(h // hs, nc), in_specs=[row] * 6 + [st, row] + [_ANY] * ns, out_specs=[row] * 6 + [_ANY] * ns,
        out_shape=[jax.ShapeDtypeStruct((h, t, n), F32)] * 6 + [jax.ShapeDtypeStruct(s.shape, s.dtype) for s in scatter],
        scratch_shapes=[pltpu.VMEM((hs, n, n), F32)] + (_exchange_scratch(ns) if ns else []),
        compiler_params=_params(("arbitrary", "arbitrary")), name=name,
    )(r, lw, k, v, a, b, s0, dy, *scatter)


_ANY = pl.BlockSpec(memory_space=pl.ANY)


def _exchange_scratch(n_arrays):
    return [pltpu.SemaphoreType.DMA((n_arrays, N_DEV - 1)), pltpu.SemaphoreType.DMA((n_arrays, N_DEV - 1)),
            pltpu.SemaphoreType.DMA((n_arrays,))]


def _gather_schedule(x_refs, out_refs, send_sems, recv_sems, local_sems):
    x, y, c = lax.axis_index("x"), lax.axis_index("y"), lax.axis_index("c")
    me, sibling = (x, y, c), (x, y, 1 - c)
    chips = [(1 - x, y), (x, 1 - y), (1 - x, 1 - y)]
    arrays = range(len(x_refs))

    def slot(a, pos):
        return out_refs[a].at[4 * pos[0] + 2 * pos[1] + pos[2]]

    def copy(a, i, block, to, src=None):
        return pltpu.make_async_remote_copy(
            src_ref=slot(a, block) if src is None else src, dst_ref=slot(a, block), send_sem=send_sems.at[a, i],
            recv_sem=recv_sems.at[a, i], device_id=to, device_id_type=pl.DeviceIdType.MESH)

    mine = [pltpu.make_async_copy(x_refs[a], slot(a, me), local_sems.at[a]) for a in arrays]
    first = [[copy(a, 0, me, sibling, src=x_refs[a])]
             + [copy(a, 1 + j, me, (*chip, c), src=x_refs[a]) for j, chip in enumerate(chips)] for a in arrays]
    passed = [[copy(a, 4 + j, (*chip, c), sibling) for j, chip in enumerate(chips)] for a in arrays]

    def start():
        for a in arrays:
            mine[a].start()
            for cp in first[a]:
                cp.start()

    def forward():
        for j, chip in enumerate(chips):
            for a in arrays:
                copy(a, 1 + j, (*chip, c), me).wait_recv()
                passed[a][j].start()

    def finish():
        for a in arrays:
            copy(a, 0, sibling, me).wait_recv()
            for j, chip in enumerate(chips):
                copy(a, 4 + j, (*chip, 1 - c), me).wait_recv()
            for cp in first[a] + passed[a]:
                cp.wait_send()
            mine[a].wait()

    return start, forward, finish


def _scatter_schedule(in_refs, out_refs, send_sems, recv_sems, local_sems):
    x, y, c = lax.axis_index("x"), lax.axis_index("y"), lax.axis_index("c")
    my_chip = 2 * x + y
    mine, remote = [], []
    for a, (src, dst) in enumerate(zip(in_refs, out_refs)):
        mine.append(pltpu.make_async_copy(src.at[my_chip], dst.at[my_chip], local_sems.at[a]))
        for i in range(1, N_CHIPS):
            px, py = x ^ (i >> 1), y ^ (i & 1)
            remote.append(pltpu.make_async_remote_copy(
                src_ref=src.at[2 * px + py], dst_ref=dst.at[my_chip], send_sem=send_sems.at[a, i - 1],
                recv_sem=recv_sems.at[a, i - 1], device_id=(px, py, c), device_id_type=pl.DeviceIdType.MESH))

    def start():
        for cp in mine + remote:
            cp.start()

    def finish():
        for cp in remote:
            cp.wait_recv()
        for cp in remote:
            cp.wait_send()
        for cp in mine:
            cp.wait()

    return start, finish


def pair_exchange(parts, name):
    n = len(parts)

    def body(*refs):
        x, y, c = lax.axis_index("x"), lax.axis_index("y"), lax.axis_index("c")
        send_sems, recv_sems = refs[2 * n:]
        copies = [pltpu.make_async_remote_copy(
            src_ref=refs[a].at[q, 1 - c], dst_ref=refs[n + a].at[q], send_sem=send_sems.at[a, q],
            recv_sem=recv_sems.at[a, q], device_id=(x, y, 1 - c), device_id_type=pl.DeviceIdType.MESH)
            for a in range(n) for q in range(N_CHIPS)]
        for cp in copies:
            cp.start()
        for cp in copies:
            cp.wait_recv()
        for cp in copies:
            cp.wait_send()

    return pl.pallas_call(
        body, in_specs=[_ANY] * n, out_specs=[_ANY] * n,
        out_shape=[jax.ShapeDtypeStruct((N_CHIPS,) + s.shape[2:], s.dtype) for s in parts],
        scratch_shapes=[pltpu.SemaphoreType.DMA((n, N_CHIPS)), pltpu.SemaphoreType.DMA((n, N_CHIPS))], name=name,
    )(*parts)


def pair_add(mine, theirs, core, name):
    _, _, k, n = mine.shape
    tc = _pick(n, (2048, 1024, 512))
    tr = _row_tile(k, tc * 3 * jnp.dtype(mine.dtype).itemsize)

    def body(core_ref, a_ref, b_ref, o_ref):
        o_ref[...] = (a_ref[...].astype(F32) + b_ref[...].astype(F32)).astype(o_ref.dtype)

    one = pl.BlockSpec((None, tr, tc), lambda q, i, j, core_ref: (q, i, j))
    grid_spec = pltpu.PrefetchScalarGridSpec(
        num_scalar_prefetch=1, grid=(N_CHIPS, k // tr, n // tc),
        in_specs=[pl.BlockSpec((None, None, tr, tc), lambda q, i, j, core_ref: (q, core_ref[0], i, j)), one],
        out_specs=one)
    return pl.pallas_call(
        body, grid_spec=grid_spec, out_shape=jax.ShapeDtypeStruct(theirs.shape, BF16),
        compiler_params=_params(("parallel", "parallel", "parallel")), name=name,
    )(core, mine, theirs)


def all_gather_many(shards, name):
    n = len(shards)

    def body(*refs):
        start, forward, finish = _gather_schedule(refs[:n], refs[n:2 * n], *refs[2 * n:])
        start()
        forward()
        finish()

    return pl.pallas_call(
        body, in_specs=[_ANY] * n, out_specs=[_ANY] * n,
        out_shape=[jax.ShapeDtypeStruct((N_DEV,) + s.shape, s.dtype) for s in shards],
        scratch_shapes=_exchange_scratch(n), name=name,
    )(*shards)


def project_and_gather(h, w_all, shards):
    return matmul(h, w_all, 'nn', F32, "proj_in", exchange=('gather', shards))


def input_grad_and_scatter(d_parts, w_all, offsets, parts):
    return matmul(d_parts, [w_all] * len(d_parts), 'nt', F32, "d_h_in", b_col_offsets=offsets,
                  exchange=('scatter', parts))


def scan_and_gather(scan_in, shards):
    y, s0, *gathered = rwkv_fwd(*scan_in, name="rwkv_scan", gather=shards)
    return y, s0, gathered


def scan_bwd_and_scatter(scan_in, s0, dy, parts):
    res = rwkv_bwd(*scan_in, s0, dy, name="rwkv_scan_bwd", scatter=parts)
    return res[:6], res[6:]


def adamw_shard(parts, w, m, v, name):
    _, k, n = w.shape
    slots = parts.shape[0]
    tc = _pick(n, (2048, 1024, 512))
    tr = _row_tile(k, tc * (slots * jnp.dtype(parts.dtype).itemsize + 7 * 4))

    def body(p_ref, w_ref, m_ref, v_ref, g_ref, d_ref, nm_ref, nv_ref):
        _adamw_block(p_ref, w_ref, m_ref, v_ref, g_ref, d_ref, nm_ref, nv_ref)

    one = pl.BlockSpec((None, tr, tc), lambda i, j: (0, i, j))
    return pl.pallas_call(
        body, grid=(k // tr, n // tc), in_specs=[pl.BlockSpec((slots, tr, tc), lambda i, j: (0, i, j))] + [one] * 3,
        out_specs=[one] * 4, out_shape=[jax.ShapeDtypeStruct(w.shape, F32)] * 4,
        compiler_params=_params(("parallel", "parallel")), name=name,
    )(parts, w, m, v)


def _adamw_block(p_ref, w_ref, m_ref, v_ref, g_ref, d_ref, nm_ref, nv_ref):
    g = p_ref[0].astype(F32)
    for j in range(1, p_ref.shape[0]):
        g = g + p_ref[j].astype(F32)
    new_m = ADAM_B1 * m_ref[...] + (1.0 - ADAM_B1) * g
    new_v = ADAM_B2 * v_ref[...] + (1.0 - ADAM_B2) * jnp.square(g)
    m_hat = new_m / (1.0 - ADAM_B1 ** ADAM_STEP)
    v_hat = new_v / (1.0 - ADAM_B2 ** ADAM_STEP)
    g_ref[...] = g
    d_ref[...] = -ADAM_LR * (m_hat / (jnp.sqrt(v_hat) + ADAM_EPS) + ADAM_WD * w_ref[...])
    nm_ref[...] = new_m
    nv_ref[...] = new_v


def adamw(parts, w, m, v, name):
    rows = w.shape[0]
    tile = _row_tile(rows, N_DEV * LANES * jnp.dtype(parts.dtype).itemsize + 7 * LANES * 4)

    def body(p_ref, w_ref, m_ref, v_ref, g_ref, d_ref, nm_ref, nv_ref):
        _adamw_block(p_ref, w_ref, m_ref, v_ref, g_ref, d_ref, nm_ref, nv_ref)

    flat = pl.BlockSpec((tile, LANES), lambda i: (i, 0))
    return pl.pallas_call(
        body, grid=(rows // tile,), in_specs=[pl.BlockSpec((N_DEV, tile, LANES), lambda i: (0, i, 0))] + [flat] * 3,
        out_specs=[flat] * 4, out_shape=[jax.ShapeDtypeStruct(w.shape, F32)] * 4,
        compiler_params=_params(("parallel",)), name=name,
    )(parts, w, m, v)


def _part_rows(n_elems):
    return -(-n_elems // (PACK_ROWS * LANES)) * PACK_ROWS


def _pack(arrays, dtype, lead=()):
    parts, layout, off = [], [], 0
    for arr in arrays:
        n = math.prod(arr.shape[len(lead):])
        rows = _part_rows(n)
        flat = arr.reshape(lead + (n,)).astype(dtype)
        flat = jnp.pad(flat, [(0, 0)] * len(lead) + [(0, rows * LANES - n)])
        parts.append(flat.reshape(lead + (rows, LANES)))
        layout.append((off, rows))
        off += rows
    total = -(-off // 1024) * 1024
    if total > off:
        parts.append(jnp.zeros(lead + (total - off, LANES), dtype))
    return jnp.concatenate(parts, axis=len(lead)), layout


def _unpack(buf, layout, shapes, lead=()):
    out = []
    for (off, rows), shape in zip(layout, shapes):
        n = math.prod(shape)
        piece = lax.slice_in_dim(buf, off, off + rows, axis=len(lead))
        out.append(piece.reshape(lead + (rows * LANES,))[..., :n].reshape(lead + tuple(shape)))
    return out


def _split_shards(full, axis):
    if axis == 0:
        return full.reshape((N_DEV, full.shape[0] // N_DEV, full.shape[1]))
    return full.reshape((full.shape[0], N_DEV, full.shape[1] // N_DEV)).transpose(1, 0, 2)


def _join_shards(shards, axis):
    if axis == 0:
        return shards.reshape((-1, shards.shape[2]))
    return shards.transpose(1, 0, 2).reshape((shards.shape[1], -1))


def _shift_down(t):
    return jnp.pad(t, ((1, 0), (0, 0)))[:-1]


def _shift_up(t):
    return jnp.pad(t, ((0, 1), (0, 0)))[1:]


def kernel(x, p, norm_mix, w_in, q_gain, k_gain, rel_bias, w_attn_up, shift_mix, w0, w_decay_up, a0, w_aaa_up, w_gate_up, k_k, k_a, r_k, gn_w, gn_b, w_rwkv_up, w_out, norm_mlp, w_mlp_in, w_mlp_out, norm_ple, w_ple_gate, w_ple_proj, loss_target, m_norm_mix, m_w_in, m_q_gain, m_k_gain, m_rel_bias, m_w_attn_up, m_shift_mix, m_w0, m_w_decay_up, m_a0, m_w_aaa_up, m_w_gate_up, m_k_k, m_k_a, m_r_k, m_gn_w, m_gn_b, m_w_rwkv_up, m_w_out, m_norm_mlp, m_w_mlp_in, m_w_mlp_out, m_norm_ple, m_w_ple_gate, m_w_ple_proj, v_norm_mix, v_w_in, v_q_gain, v_k_gain, v_rel_bias, v_w_attn_up, v_shift_mix, v_w0, v_w_decay_up, v_a0, v_w_aaa_up, v_w_gate_up, v_k_k, v_k_a, v_r_k, v_gn_w, v_gn_b, v_w_rwkv_up, v_w_out, v_norm_mlp, v_w_mlp_in, v_w_mlp_out, v_norm_ple, v_w_ple_gate, v_w_ple_proj):
    given = dict(locals())
    xs = x[0]
    t_len, d_model = xs.shape
    target = loss_target[0]
    p_bf = p[0, 0].astype(BF16)
    rw_width = w0.shape[1]
    n_rheads = rw_width // RWKV_HEAD
    lora_d, lora_a, lora_g = w_decay_up.shape[1], w_aaa_up.shape[1], w_gate_up.shape[1]
    z_width = shift_mix.shape[1]
    z_pad = -(-z_width // LANES) * LANES
    qkv_width = 3 * ATTN_WIDTH
    assert z_width == 3 * rw_width + lora_d + lora_a + lora_g
    assert N_DEV * w_in.shape[2] == qkv_width + z_width + 2 * d_model
    for window, dilation in DILATED_GROUPS:
        assert window // dilation == ATTN_BLOCK and t_len % (dilation * ATTN_BLOCK) == 0

    shard_bf = {n: given[n][0].astype(BF16) for n in BIG}
    early = ['w_in', 'w_decay_up', 'w_aaa_up', 'w_gate_up']
    during_proj = ['w_mlp_out']
    during_scan = [n for n in BIG if n not in early + during_proj]
    late = during_proj + during_scan
    full = {n: _join_shards(g, 1) for n, g in zip(early, all_gather_many([shard_bf[n] for n in early], "gather_w_in"))}
    z_end = qkv_width + z_width
    w_all = jnp.concatenate([full['w_in'][:, :z_end], jnp.zeros((d_model, z_pad - z_width), BF16),
                             full['w_in'][:, z_end:]], axis=1)
    gates_at = qkv_width + z_pad

    (h_in,) = rowmap_fwd(lambda a, g: st_norm(a, g)[1:], [xs], [norm_mix], [BF16], "norm_in")
    proj, proj_gathered = project_and_gather(h_in, w_all, [shard_bf[n] for n in during_proj])
    qkv, z = proj[:, :qkv_width], proj[:, qkv_width:gates_at]
    gate_a, gate_r = proj[:, gates_at:gates_at + d_model], proj[:, gates_at + d_model:]

    q_raw = qkv[:, :ATTN_WIDTH].reshape(t_len * N_HEADS, HEAD_DIM)
    k_raw = qkv[:, ATTN_WIDTH:2 * ATTN_WIDTH].reshape(t_len * N_HEADS, HEAD_DIM)
    q_n, k_n = rowmap_fwd(st_qk_norm, [q_raw, k_raw], [q_gain, k_gain], [F32, F32], "qk_norm")
    q_n, k_n = q_n.reshape(t_len, ATTN_WIDTH), k_n.reshape(t_len, ATTN_WIDTH)
    buckets = _bucket_tables()
    bias = bias_fwd(rel_bias, buckets, "attn_bias")

    att_in, att_o, att_l = [], [], []
    for g, (_, dilation) in enumerate(DILATED_GROUPS):
        hs = slice(g * HEADS_PER_GROUP, (g + 1) * HEADS_PER_GROUP)
        ops = ((q_n, g * ATTN_OUT), (k_n, g * ATTN_OUT), (proj, 2 * ATTN_WIDTH + g * ATTN_OUT), bias[hs, 0], bias[hs, 1])
        o_g, l_g = attn_fwd(*ops, dilation, name=f"attn_fwd_{g}")
        att_in.append(ops)
        att_o.append(o_g)
        att_l.append(l_g)
    (attn,) = rowmap_fwd(st_merge, att_o + att_l, [], [BF16], "attn_merge")

    c0 = rw_width
    cuts = [0, c0, 2 * c0, 3 * c0, 3 * c0 + lora_d, 3 * c0 + lora_d + lora_a, z_width]
    z_parts = [z[:, lo:hi] for lo, hi in zip(cuts[:-1], cuts[1:])]
    z_prev = [_shift_down(t) for t in z_parts]
    mixes = [shift_mix[:, lo:hi] for lo, hi in zip(cuts[:-1], cuts[1:])]
    pre_params = mixes + [w0, full['w_decay_up'], a0, full['w_aaa_up'], full['w_gate_up'], k_k, k_a]
    pre_out = rowmap_fwd(functools.partial(_st_rwkv_pre, _dot), z_parts + z_prev, pre_params, [F32] * 7, "rwkv_pre")
    r_s, lw_s, k_s, v_s, kk0_s, a_s, g_s = pre_out

    def heads(tok):
        return tok.reshape(t_len, n_rheads, RWKV_HEAD)

    def head_major(tok):
        return heads(tok).transpose(1, 0, 2)

    aa_s, bb_s = rowmap_fwd(st_rwkv_kk, [heads(kk0_s), heads(a_s)], [], [F32, F32], "rwkv_kk")
    scan_in = [head_major(r_s), head_major(lw_s), head_major(k_s), head_major(v_s),
               aa_s.transpose(1, 0, 2), bb_s.transpose(1, 0, 2)]
    y_h, s0_h, scan_gathered = scan_and_gather(scan_in, [shard_bf[n] for n in during_scan])
    late_gathered = list(proj_gathered) + list(scan_gathered)
    wt = {n: g if SHARD_AXIS[n] == 1 else g.reshape(-1, g.shape[2]) for n, g in zip(late, late_gathered)}
    y_s = y_h.transpose(1, 0, 2)
    post_params = [gn_w.reshape(1, n_rheads, RWKV_HEAD), gn_b.reshape(1, n_rheads, RWKV_HEAD), r_k]
    post_in = [y_s, heads(r_s), heads(k_s), heads(v_s), heads(g_s)]
    (rw,) = rowmap_fwd(st_rwkv_post, post_in, post_params, [BF16], "rwkv_post")
    rw = rw.reshape(t_len, rw_width)
    attn_d = matmul(attn, wt['w_attn_up'], 'nn', F32, "attn_up", b_shards=True)
    rwkv_d = matmul(rw, wt['w_rwkv_up'], 'nn', F32, "rwkv_up", b_shards=True)

    (merged,) = rowmap_fwd(st_gate, [gate_a, gate_r, attn_d, rwkv_d], [], [BF16], "gate_merge")
    mix_out = matmul(merged, wt['w_out'], 'nn', F32, "out_proj")
    x1, h_mlp = rowmap_fwd(st_res_norm, [xs, mix_out], [norm_mlp], [F32, BF16], "res_norm_mlp")
    u, act = matmul(h_mlp, wt['w_mlp_in'], 'nn', [F32, BF16], "mlp_in", b_shards=True,
                    epilogue=lambda acc: (acc,) + st_relu2(acc))
    mlp_out = matmul(act, wt['w_mlp_out'], 'nn', F32, "mlp_out")
    x2, h_ple = rowmap_fwd(st_res_norm, [x1, mlp_out], [norm_ple], [F32, BF16], "res_norm_ple")
    pg = matmul(h_ple, wt['w_ple_gate'], 'nn', F32, "ple_gate")
    pp = matmul(p_bf, wt['w_ple_proj'], 'nn', F32, "ple_proj", b_shards=True)
    dy, d_pg, d_pp, loss_local = loss_head(x2, pg, pp, target, "loss_head")

    def row_cut(full_grad):
        return full_grad.reshape(N_DEV, full_grad.shape[0] // N_DEV, full_grad.shape[1])

    grads, sends = {}, {}
    sends['w_ple_gate'] = row_cut(matmul(h_ple, d_pg, 'tn', BF16, "d_w_ple_gate"))
    sends['w_ple_proj'] = matmul(p_bf, d_pp, 'tn', BF16, "d_w_ple_proj", out_shards=True)
    d_h_ple = matmul(d_pg, wt['w_ple_gate'], 'nt', F32, "d_h_ple")
    (d_x2, d_x2_bf), (grads['norm_ple'],) = rowmap_bwd(
        st_res_norm, [x1, mlp_out], [norm_ple], [dy, d_h_ple], [F32, BF16], "res_norm_ple_bwd")
    sends['w_mlp_out'] = row_cut(matmul(act, d_x2_bf, 'tn', BF16, "d_w_mlp_out"))
    (d_u,) = matmul(d_x2_bf, wt['w_mlp_out'], 'nt', [BF16], "d_mlp_act", extras=[u],
                    epilogue=lambda d_act, u_blk: (d_act * (2.0 * jnp.maximum(u_blk, 0.0)),))
    sends['w_mlp_in'] = matmul(h_mlp, d_u, 'tn', BF16, "d_w_mlp_in", out_shards=True)
    d_h_mlp = matmul(d_u, wt['w_mlp_in'], 'nt', F32, "d_h_mlp", b_shards=True)
    (d_x1, d_x1_bf), (grads['norm_mlp'],) = rowmap_bwd(
        st_res_norm, [xs, mix_out], [norm_mlp], [d_x2, d_h_mlp], [F32, BF16], "res_norm_mlp_bwd")

    sends['w_out'] = row_cut(matmul(merged, d_x1_bf, 'tn', BF16, "d_w_out"))
    d_merged = matmul(d_x1_bf, wt['w_out'], 'nt', F32, "d_merged")
    (d_gate_a, d_gate_r, d_attn_d, d_rwkv_d), _ = rowmap_bwd(
        st_gate, [gate_a, gate_r, attn_d, rwkv_d], [], [d_merged], [BF16] * 4, "gate_merge_bwd")
    sends['w_attn_up'] = matmul(attn, d_attn_d, 'tn', BF16, "d_w_attn_up", out_shards=True)
    sends['w_rwkv_up'] = matmul(rw, d_rwkv_d, 'tn', BF16, "d_w_rwkv_up", out_shards=True)
    d_attn = matmul(d_attn_d, wt['w_attn_up'], 'nt', F32, "d_attn", b_shards=True)
    d_rw = matmul(d_rwkv_d, wt['w_rwkv_up'], 'nt', F32, "d_rw", b_shards=True)

    (d_y, d_r1, d_k1, d_v1, d_g), (d_gn_w, d_gn_b, grads['r_k']) = rowmap_bwd(
        st_rwkv_post, post_in, post_params, [heads(d_rw)], [F32] * 5, "rwkv_post_bwd")
    grads['gn_w'], grads['gn_b'] = d_gn_w.reshape(1, rw_width), d_gn_b.reshape(1, rw_width)
    core = lax.axis_index("c").astype(I32).reshape(1)

    def chip_partials(names, tag):
        mine = [sends[n].reshape((N_CHIPS, 2) + sends[n].shape[1:]) for n in names]
        theirs = pair_exchange(mine, "pair_grads_" + tag)
        return [pair_add(m, t, core, "pair_add_" + n) for n, m, t in zip(names, mine, theirs)]

    scan_grads, late_received = scan_bwd_and_scatter(scan_in, s0_h, d_y.transpose(1, 0, 2),
                                                     chip_partials(late, "late"))
    received = dict(zip(late, late_received))
    d_r2, d_lw, d_k2, d_v2, d_aa, d_bb = [t.transpose(1, 0, 2) for t in scan_grads]
    (d_kk0, d_a), _ = rowmap_bwd(st_rwkv_kk, [heads(kk0_s), heads(a_s)], [], [d_aa, d_bb], [F32, F32], "rwkv_kk_bwd")

    def flat(tok):
        return tok.reshape(t_len, rw_width)

    pre_cts = [[flat(d_r1), flat(d_r2)], flat(d_lw), [flat(d_k1), flat(d_k2)], [flat(d_v1), flat(d_v2)],
               flat(d_kk0), flat(d_a), flat(d_g)]
    d_zp, d_pre = rowmap_bwd(functools.partial(_st_rwkv_pre, _dot_ad), z_parts + z_prev, pre_params, pre_cts,
                             [F32] * 12, "rwkv_pre_bwd")
    grads['shift_mix'] = jnp.concatenate(d_pre[:6], axis=1)
    grads['w0'], d_w_decay, grads['a0'], d_w_aaa, d_w_gate, grads['k_k'], grads['k_a'] = d_pre[6:]
    for name, full_grad in (('w_decay_up', d_w_decay), ('w_aaa_up', d_w_aaa), ('w_gate_up', d_w_gate)):
        sends[name] = _split_shards(full_grad, 1).astype(BF16)
    z_fill = [jnp.zeros((t_len, z_pad - z_width), F32)] if z_pad > z_width else []
    d_z_cur = jnp.concatenate(d_zp[:6] + z_fill, axis=1)
    d_z_prev = _shift_up(jnp.concatenate(d_zp[6:] + z_fill, axis=1))
    (d_z,) = rowmap_fwd(st_add, [d_z_cur, d_z_prev], [], [BF16], "d_z_sum")

    d_merge, _ = rowmap_bwd(st_merge, att_o + att_l, [], [d_attn], [F32] * 6, "attn_merge_bwd")
    d_qn, d_kn, d_vs, d_bias = [], [], [], []
    for g, (_, dilation) in enumerate(DILATED_GROUPS):
        dq, dk, dv, dbp, dbc = attn_bwd(*att_in[g], d_merge[g], d_merge[3 + g], dilation, name=f"attn_bwd_{g}")
        d_qn.append(dq)
        d_kn.append(dk)
        d_vs.append(dv)
        d_bias.append(jnp.stack([dbp, dbc], axis=1))
    d_table = bias_bwd(jnp.concatenate(d_bias, axis=0), buckets, "attn_bias_bwd")
    grads['rel_bias'] = d_table[:, :N_HEADS]
    d_qn = jnp.concatenate(d_qn, axis=1).reshape(t_len * N_HEADS, HEAD_DIM)
    d_kn = jnp.concatenate(d_kn, axis=1).reshape(t_len * N_HEADS, HEAD_DIM)
    (d_q, d_k), (grads['q_gain'], grads['k_gain']) = rowmap_bwd(
        st_qk_norm, [q_raw, k_raw], [q_gain, k_gain], [d_qn, d_kn], [BF16, BF16], "qk_norm_bwd")
    d_qkv = jnp.concatenate([d_q.reshape(t_len, ATTN_WIDTH), d_k.reshape(t_len, ATTN_WIDTH)]
                            + [t.astype(BF16) for t in d_vs], axis=1)

    d_gates = jnp.concatenate([d_gate_a, d_gate_r], axis=1)
    sends['w_in'] = _split_shards(jnp.concatenate([
        matmul(h_in, d_qkv, 'tn', BF16, "d_w_qkv"),
        matmul(h_in, d_z, 'tn', BF16, "d_w_z")[:, :z_width],
        matmul(h_in, d_gates, 'tn', BF16, "d_w_gates")], axis=1), 1)
    d_h_in, early_received = input_grad_and_scatter([d_qkv, d_z, d_gates], w_all, [0, qkv_width, gates_at],
                                                    chip_partials(early, "early"))
    received.update(zip(early, early_received))
    (grad_x,), (grads['norm_mix'],) = rowmap_bwd(st_norm, [xs], [norm_mix], [d_x1, d_h_in], [F32], "norm_in_bwd")

    by_name = {n: adamw_shard(received[n], given[n], given['m_' + n], given['v_' + n], "adamw_" + n) for n in BIG}
    pk = lambda prefix: _pack([given[prefix + n] for n in SMALL], F32)[0]
    small_buf, small_layout = _pack([grads[n].reshape(given[n].shape) for n in SMALL], F32)
    (small_all,) = all_gather_many([small_buf], "gather_small_grads")
    small_out = adamw(small_all, pk(''), pk('m_'), pk('v_'), "adamw_replicated")
    small_shapes = [given[n].shape for n in SMALL]
    by_name.update(zip(SMALL, zip(*[_unpack(buf, small_layout, small_shapes) for buf in small_out])))
    loss = lax.psum(loss_local, MESH_AXES)
    return (loss, grad_x[None], *[by_name[n][0] for n in WEIGHTS], *[by_name[n][1] for n in WEIGHTS],
            *[by_name[n][2] for n in WEIGHTS], *[by_name[n][3] for n in WEIGHTS])
```

```python
import functools
import math

import jax
import jax.numpy as jnp
from jax import lax
from jax.experimental import pallas as pl
from jax.experimental.pallas import tpu as pltpu

F32 = jnp.float32
BF16 = jnp.bfloat16
I32 = jnp.int32

N_DEV = 8
N_CHIPS = 4
MESH_AXES = ("x", "y", "c")
LANES = 128
PACK_ROWS = 16
VMEM_LIMIT_BYTES = 48 * 2**20
ROW_BLOCK_BYTES = 3 * 2**20

HEAD_DIM = 128
ATTN_BLOCK = 128
HEADS_PER_GROUP = 4
DILATED_GROUPS = ((128, 1), (512, 4), (2048, 16))
N_HEADS = HEADS_PER_GROUP * len(DILATED_GROUPS)
ATTN_WIDTH = N_HEADS * HEAD_DIM
ATTN_OUT = HEADS_PER_GROUP * HEAD_DIM
N_BUCKETS = 32
MAX_DISTANCE = 2048
RWKV_HEAD = 64
RWKV_CHUNK = 64
RWKV_PAIRS_PER_STEP = 8
RMS_EPS = 1e-6
GN_EPS = 64e-5
NEG_INF = -1e30

ADAM_LR = 0.001
ADAM_B1 = 0.9
ADAM_B2 = 0.999
ADAM_EPS = 1e-08
ADAM_WD = 0.01
ADAM_STEP = 10

WEIGHTS = ['norm_mix', 'w_in', 'q_gain', 'k_gain', 'rel_bias', 'w_attn_up', 'shift_mix', 'w0', 'w_decay_up', 'a0',
           'w_aaa_up', 'w_gate_up', 'k_k', 'k_a', 'r_k', 'gn_w', 'gn_b', 'w_rwkv_up', 'w_out', 'norm_mlp', 'w_mlp_in',
           'w_mlp_out', 'norm_ple', 'w_ple_gate', 'w_ple_proj']
SHARD_AXIS = {'w_in': 1, 'w_attn_up': 1, 'w_decay_up': 1, 'w_aaa_up': 1, 'w_gate_up': 1, 'w_rwkv_up': 1, 'w_out': 0,
              'w_mlp_in': 1, 'w_mlp_out': 0, 'w_ple_gate': 0, 'w_ple_proj': 1}
BIG = [n for n in WEIGHTS if n in SHARD_AXIS]
SMALL = [n for n in WEIGHTS if n not in SHARD_AXIS]


def _params(sem):
    return pltpu.CompilerParams(dimension_semantics=sem, vmem_limit_bytes=VMEM_LIMIT_BYTES)


_DN = {'nn': (((1,), (0,)), ((), ())), 'nt': (((1,), (1,)), ((), ())), 'tn': (((0,), (0,)), ((), ()))}


def _dot(a, b, mode, exact):
    if exact:
        return lax.dot_general(a, b, _DN[mode], precision=lax.Precision.HIGH, preferred_element_type=F32)
    return lax.dot_general(a.astype(BF16), b.astype(BF16), _DN[mode], preferred_element_type=F32)


@functools.partial(jax.custom_vjp, nondiff_argnums=(2, 3))
def _dot_ad(a, b, mode, exact):
    return _dot(a, b, mode, exact)


def _dot_ad_fwd(a, b, mode, exact):
    return _dot(a, b, mode, exact), (a, b)


def _dot_ad_bwd(mode, exact, res, g):
    a, b = res
    if mode == 'nn':
        return _dot(g, b, 'nt', exact), _dot(a, g, 'tn', exact)
    if mode == 'nt':
        return _dot(g, b, 'nn', exact), _dot(g, a, 'tn', exact)
    return _dot(b, g, 'nt', exact), _dot(a, g, 'nn', exact)


_dot_ad.defvjp(_dot_ad_fwd, _dot_ad_bwd)


def _pick(n, cands):
    for c in cands:
        if n % c == 0:
            return c
    return n


def matmul(a, b, mode, out_dtype, name, b_shards=False, out_shards=False, b_col_offsets=None, epilogue=None,
           extras=(), exchange=None):
    a_list = list(a) if isinstance(a, (list, tuple)) else [a]
    b_list = list(b) if isinstance(b, (list, tuple)) else [b]
    seg = len(a_list)
    assert all(t.dtype == BF16 for t in a_list + b_list), name
    assert seg == 1 or (mode == 'nt' and not b_shards), name
    m = a_list[0].shape[1] if mode == 'tn' else a_list[0].shape[0]
    ks = [t.shape[0] if mode == 'tn' else t.shape[1] for t in a_list]
    b0 = b_list[0]
    b_rows, b_cols = (b0.shape[1], N_DEV * b0.shape[2]) if b_shards else b0.shape
    n = b_rows if mode == 'nt' else b_cols
    if seg == 1 and b_col_offsets is None:
        assert (b_cols if mode == 'nt' else b_rows) == ks[0], (name, a_list[0].shape, b0.shape)
    offsets = list(b_col_offsets) if b_col_offsets is not None else [0] * seg
    tm = _pick(m, (1024, 512, 256, 128))
    tn = _pick(n // N_DEV if (out_shards or (b_shards and mode != 'nt')) else n, (1024, 512, 256, 128))
    k_units = [kk // N_DEV if (b_shards and mode == 'nt') else kk for kk in ks] + [o for o in offsets if o]
    k_cands = (1024, 512, 256, 128) if (epilogue is not None or extras) else (2048, 1024, 512, 256, 128)
    tk = next((c for c in k_cands if all(u % c == 0 for u in k_units)), k_units[0])
    nks = [kk // tk for kk in ks]
    starts = [sum(nks[:s]) for s in range(seg)]
    nk = sum(nks)
    grid = (m // tm, n // tn, nk)
    total_steps = grid[0] * grid[1] * nk
    kind, moved = exchange if exchange is not None else (None, [])
    nx, ne = len(moved), len(extras)
    out_dtypes = list(out_dtype) if isinstance(out_dtype, (list, tuple)) else [out_dtype]
    no = len(out_dtypes)

    def body(*refs):
        a_refs, b_refs, x_refs = refs[:seg], refs[seg:2 * seg], refs[2 * seg:2 * seg + ne]
        pos = 2 * seg + ne
        moved_in, o_refs = refs[pos:pos + nx], refs[pos + nx:pos + nx + no]
        pos += nx + no
        moved_out, acc_ref, sems = refs[pos:pos + nx], refs[pos + nx], refs[pos + nx + 1:]
        kk = pl.program_id(2)
        step = (pl.program_id(0) * grid[1] + pl.program_id(1)) * nk + kk
        if kind == 'gather':
            start, forward, finish = _gather_schedule(moved_in, moved_out, *sems)
            pl.when(step == 0)(start)
            pl.when(step == total_steps - 1 - total_steps // 8)(forward)
        elif kind == 'scatter':
            start, finish = _scatter_schedule(moved_in, moved_out, *sems)
            pl.when(step == 0)(start)

        @pl.when(kk == 0)
        def _():
            acc_ref[...] = jnp.zeros_like(acc_ref)

        for s in range(seg):
            def accumulate(s=s):
                acc_ref[...] += lax.dot_general(a_refs[s][...], b_refs[s][...], _DN[mode], preferred_element_type=F32)

            if seg == 1:
                accumulate()
            else:
                pl.when(jnp.logical_and(kk >= starts[s], kk < starts[s] + nks[s]))(accumulate)

        @pl.when(kk == nk - 1)
        def _():
            acc = acc_ref[...]
            outs = (acc,) if epilogue is None else epilogue(acc, *[x[...] for x in x_refs])
            for r, v in zip(o_refs, outs):
                r[...] = v.astype(r.dtype)

        if kind is not None:
            pl.when(step == total_steps - 1)(finish)

    def k_of(kk, s):
        return kk if seg == 1 else jnp.clip(kk - starts[s], 0, nks[s] - 1)

    a_specs, b_specs = [], []
    for s in range(seg):
        off = offsets[s] // tk
        if mode == 'tn':
            a_specs.append(pl.BlockSpec((tk, tm), lambda i, j, kk, s=s: (k_of(kk, s), i)))
        else:
            a_specs.append(pl.BlockSpec((tm, tk), lambda i, j, kk, s=s: (i, k_of(kk, s))))
        if mode == 'nt':
            if b_shards:
                per = b0.shape[2] // tk
                b_specs.append(pl.BlockSpec((None, tn, tk), lambda i, j, kk: (kk // per, j, kk % per)))
            else:
                b_specs.append(pl.BlockSpec((tn, tk), lambda i, j, kk, s=s, off=off: (j, off + k_of(kk, s))))
        else:
            if b_shards:
                per = b0.shape[2] // tn
                b_specs.append(pl.BlockSpec((None, tk, tn), lambda i, j, kk: (j // per, kk, j % per)))
            else:
                b_specs.append(pl.BlockSpec((tk, tn), lambda i, j, kk: (kk, j)))
    tile = pl.BlockSpec((tm, tn), lambda i, j, kk: (i, j))
    if out_shards:
        assert epilogue is None
        per_o = n // N_DEV // tn
        o_specs = [pl.BlockSpec((None, tm, tn), lambda i, j, kk: (j // per_o, i, j % per_o))]
        o_shapes = [jax.ShapeDtypeStruct((N_DEV, m, n // N_DEV), out_dtypes[0])]
    else:
        o_specs = [tile] * no
        o_shapes = [jax.ShapeDtypeStruct((m, n), d) for d in out_dtypes]
    moved_shapes = [jax.ShapeDtypeStruct(((N_DEV,) + t.shape) if kind == 'gather' else t.shape, t.dtype) for t in moved]
    res = pl.pallas_call(
        body, grid=grid, in_specs=a_specs + b_specs + [tile] * ne + [_ANY] * nx,
        out_specs=o_specs + [_ANY] * nx, out_shape=o_shapes + moved_shapes,
        scratch_shapes=[pltpu.VMEM((tm, tn), F32)] + (_exchange_scratch(nx) if nx else []),
        compiler_params=_params(("arbitrary",) * 3 if nx else ("parallel", "parallel", "arbitrary")), name=name,
    )(*a_list, *b_list, *extras, *moved)
    result = res[0] if (epilogue is None) else list(res[:no])
    return (result, list(res[no:])) if nx else result


def _row_bytes(shape, dtype):
    dims = list(shape[1:])
    dims[-1] = -(-dims[-1] // LANES) * LANES
    return math.prod(dims) * jnp.dtype(dtype).itemsize


def _row_tile(n, row_bytes):
    t = 1024
    while t > 16 and (n % t or t * row_bytes > ROW_BLOCK_BYTES):
        t //= 2
    assert n % t == 0, (n, t)
    return t


def rowmap(fn, tiled, bcast, out_tiled, out_acc, name):
    n = tiled[0].shape[0]
    tile = _row_tile(n, sum(_row_bytes(t.shape, t.dtype) for t in list(tiled) + list(out_tiled)))
    n_in, n_out = len(tiled) + len(bcast), len(out_tiled)

    def body(*refs):
        outs, accs = fn(*[r[...] for r in refs[:n_in]])
        assert len(outs) == n_out and len(accs) == len(out_acc), name
        for r, v in zip(refs[n_in:n_in + n_out], outs):
            r[...] = v.astype(r.dtype)
        acc_refs = refs[n_in + n_out:]
        if acc_refs:
            @pl.when(pl.program_id(0) == 0)
            def _():
                for r, v in zip(acc_refs, accs):
                    r[...] = v.astype(r.dtype)

            @pl.when(pl.program_id(0) != 0)
            def _():
                for r, v in zip(acc_refs, accs):
                    r[...] += v.astype(r.dtype)

    def tspec(s):
        nd = len(s.shape)
        return pl.BlockSpec((tile,) + tuple(s.shape[1:]), lambda i, nd=nd: (i,) + (0,) * (nd - 1))

    def bspec(s):
        nd = len(s.shape)
        return pl.BlockSpec(tuple(s.shape), lambda i, nd=nd: (0,) * nd)

    res = pl.pallas_call(
        body, grid=(n // tile,),
        in_specs=[tspec(t) for t in tiled] + [bspec(t) for t in bcast],
        out_specs=[tspec(t) for t in out_tiled] + [bspec(t) for t in out_acc],
        out_shape=list(out_tiled) + list(out_acc),
        compiler_params=_params(("arbitrary",)), name=name,
    )(*tiled, *bcast)
    return list(res[:n_out]), list(res[n_out:])


def rowmap_fwd(fwd, tiled, bcast, out_dtypes, name):
    shapes = jax.eval_shape(fwd, *tiled, *bcast)
    out_tiled = [jax.ShapeDtypeStruct(s.shape, d) for s, d in zip(shapes, out_dtypes)]
    outs, _ = rowmap(lambda *blk: (fwd(*[b.astype(F32) for b in blk]), ()), tiled, bcast, out_tiled, [], name)
    return outs


def rowmap_bwd(fwd, tiled, bcast, cts, want, name):
    cts = [[] if c is None else (list(c) if isinstance(c, (list, tuple)) else [c]) for c in cts]
    flat_cts = [c for group in cts for c in group]
    nt_, nc_ = len(tiled), len(flat_cts)

    def fn(*blk):
        ins = [b.astype(F32) for b in blk[:nt_]] + [b.astype(F32) for b in blk[nt_ + nc_:]]
        ctb = list(blk[nt_:nt_ + nc_])
        outs, vjp = jax.vjp(fwd, *ins)
        full = []
        for o, group in zip(outs, cts):
            acc = jnp.zeros_like(o)
            for _ in group:
                acc = acc + ctb.pop(0).astype(F32)
            full.append(acc)
        g = vjp(tuple(full))
        return [g[i] for i in range(nt_) if want[i] is not None], list(g[nt_:])

    out_tiled = [jax.ShapeDtypeStruct(t.shape, w) for t, w in zip(tiled, want) if w is not None]
    out_acc = [jax.ShapeDtypeStruct(b.shape, F32) for b in bcast]
    return rowmap(fn, list(tiled) + flat_cts, bcast, out_tiled, out_acc, name)


def _rms(x, gain):
    return x * lax.rsqrt(jnp.mean(jnp.square(x), axis=-1, keepdims=True) + RMS_EPS) * gain


def _sigmoid(x):
    return 1.0 / (1.0 + jnp.exp(-x))


def _softplus(x):
    return jnp.maximum(x, 0.0) + jnp.log(1.0 + jnp.exp(-jnp.abs(x)))


def st_norm(x, gain):
    return x, _rms(x, gain)


def st_res_norm(x, delta, gain):
    y = x + delta
    return y, _rms(y, gain)


def st_qk_norm(q, k, q_gain, k_gain):
    return _rms(q, q_gain), _rms(k, k_gain)


def st_merge(o0, o1, o2, l0, l1, l2):
    m = jnp.maximum(jnp.maximum(l0, l1), l2)
    e0, e1, e2 = jnp.exp(l0 - m), jnp.exp(l1 - m), jnp.exp(l2 - m)
    return ((e0 * o0 + e1 * o1 + e2 * o2) / (e0 + e1 + e2),)


def _st_rwkv_pre(dot, zr, zk, zv, xw, xa, xg, pr, pk, pv, pw, pa, pg, mr, mk, mv, mw, ma, mg,
                 w0, w_decay, a0, w_aaa, w_gate, k_k, k_a):
    def shift(cur, prev, mix):
        return cur + mix * (prev - cur)

    r, k, v = shift(zr, pr, mr), shift(zk, pk, mk), shift(zv, pv, mv)
    xw, xa, xg = shift(xw, pw, mw), shift(xa, pa, ma), shift(xg, pg, mg)
    w = -_softplus(-(w0 + dot(jnp.tanh(xw), w_decay, 'nn', False))) - 0.5
    a = _sigmoid(a0 + dot(xa, w_aaa, 'nn', False))
    g = dot(_sigmoid(xg), w_gate, 'nn', False)
    log_decay = -jnp.exp(w)
    return r, log_decay, k * (1.0 + (a - 1.0) * k_a), v, k * k_k, a, g


def st_rwkv_kk(kk0, a):
    kk = kk0 / jnp.maximum(jnp.sqrt(jnp.sum(jnp.square(kk0), axis=-1, keepdims=True)), 1e-12)
    return -kk, kk * a


def st_rwkv_post(y, r, k, v, g, gn_w, gn_b, r_k):
    mu = jnp.mean(y, axis=-1, keepdims=True)
    var = jnp.mean(jnp.square(y - mu), axis=-1, keepdims=True)
    out = (y - mu) * lax.rsqrt(var + GN_EPS) * gn_w + gn_b
    out = out + jnp.sum(r * k * r_k, axis=-1, keepdims=True) * v
    return (out * g,)


def st_gate(g0, g1, attn_d, rwkv_d):
    return (_sigmoid(g0) * attn_d + _sigmoid(g1) * rwkv_d,)


def st_relu2(u):
    return (jnp.square(jnp.maximum(u, 0.0)),)


def st_add(a, b):
    return (a + b,)


def loss_head(x2, pg, pp, target, name):
    d_model = x2.shape[1]

    def fn(x2, pg, pp, tgt):
        s = _sigmoid(pg)
        err = x2 + s * pp - tgt
        dy = err * (1.0 / d_model)
        part = 0.5 * jnp.sum(jnp.square(err)) * (1.0 / d_model)
        return [dy, dy * pp * s * (1.0 - s), dy * s], [jnp.full((8, LANES), part, F32)]

    sds = jax.ShapeDtypeStruct
    outs, accs = rowmap(fn, [x2, pg, pp, target], [],
                        [sds(x2.shape, F32), sds(x2.shape, BF16), sds(x2.shape, BF16)], [sds((8, LANES), F32)], name)
    return outs[0], outs[1], outs[2], accs[0][0, 0]


def _attn_block(dot, q, kp, kc, vp, vc, bp, bc, prev_offset):
    blk = q.shape[0]
    qi = lax.broadcasted_iota(I32, (blk, blk), 0)
    ki = lax.broadcasted_iota(I32, (blk, blk), 1)
    mask_c = ki <= qi
    mask_p = ki >= qi + prev_offset
    scale = HEAD_DIM ** -0.5
    s_c = jnp.where(mask_c, dot(q, kc, 'nt', False) * scale + bc, NEG_INF)
    s_p = jnp.where(mask_p, dot(q, kp, 'nt', False) * scale + bp, NEG_INF)
    m = lax.stop_gradient(jnp.maximum(jnp.max(s_c, axis=1, keepdims=True), jnp.max(s_p, axis=1, keepdims=True)))
    e_c = jnp.where(mask_c, jnp.exp(s_c - m), 0.0)
    e_p = jnp.where(mask_p, jnp.exp(s_p - m), 0.0)
    l = jnp.sum(e_c, axis=1, keepdims=True) + jnp.sum(e_p, axis=1, keepdims=True)
    o = (dot(e_c, vc, 'nn', False) + dot(e_p, vp, 'nn', False)) / l
    return o, jnp.broadcast_to(m + jnp.log(l), o.shape)


class _ClassView:
    def __init__(self, tokens, dilation, first_col):
        self.view = tokens.reshape(tokens.shape[0] // dilation, dilation * tokens.shape[1])
        self.tiles, self.first = tokens.shape[1] // HEAD_DIM, first_col // HEAD_DIM

    def spec(self, shift, nb):
        tiles, first = self.tiles, self.first
        return pl.BlockSpec((ATTN_BLOCK, HEAD_DIM),
                            lambda h, r, n: (jnp.clip(n + shift, 0, nb - 1), r * tiles + first + h))


_BIAS_SPEC = pl.BlockSpec((None, ATTN_BLOCK, ATTN_BLOCK), lambda h, r, n: (h, 0, 0))


def attn_fwd(q, k, v, bp, bc, dilation, name):
    t_len = q[0].shape[0]
    nb = t_len // dilation // ATTN_BLOCK
    qv, kv, vv = (_ClassView(arr, dilation, col) for arr, col in (q, k, v))
    o_shape = jax.ShapeDtypeStruct((t_len // dilation, dilation * ATTN_OUT), F32)
    o_spec = pl.BlockSpec((ATTN_BLOCK, HEAD_DIM), lambda h, r, n: (n, r * HEADS_PER_GROUP + h))

    def body(q_ref, kp_ref, kc_ref, vp_ref, vc_ref, bp_ref, bc_ref, o_ref, l_ref):
        off = jnp.where(pl.program_id(2) > 0, 0, ATTN_BLOCK)
        o, l = _attn_block(_dot, q_ref[...], kp_ref[...], kc_ref[...], vp_ref[...], vc_ref[...], bp_ref[...],
                           bc_ref[...], off)
        o_ref[...] = o
        l_ref[...] = l

    o, l = pl.pallas_call(
        body, grid=(HEADS_PER_GROUP, dilation, nb),
        in_specs=[qv.spec(0, nb), kv.spec(-1, nb), kv.spec(0, nb), vv.spec(-1, nb), vv.spec(0, nb),
                  _BIAS_SPEC, _BIAS_SPEC],
        out_specs=[o_spec, o_spec], out_shape=[o_shape] * 2,
        compiler_params=_params(("parallel", "parallel", "parallel")), name=name,
    )(qv.view, kv.view, kv.view, vv.view, vv.view, bp, bc)
    return o.reshape(t_len, ATTN_OUT), l.reshape(t_len, ATTN_OUT)


def attn_bwd(q, k, v, bp, bc, do, dl, dilation, name):
    t_len = q[0].shape[0]
    blk = ATTN_BLOCK
    nb = t_len // dilation // blk
    qv, kv, vv = (_ClassView(arr, dilation, col) for arr, col in (q, k, v))
    dov, dlv = _ClassView(do, dilation, 0), _ClassView(dl, dilation, 0)
    cur, bias = dov.spec(0, nb), _BIAS_SPEC

    def body(q_ref, kp_ref, kc_ref, vp_ref, vc_ref, bp_ref, bc_ref, do_ref, dl_ref,
             dq_ref, dkp_ref, dkc_ref, dvp_ref, dvc_ref, dbp_ref, dbc_ref):
        off = jnp.where(pl.program_id(2) > 0, 0, blk)
        f = functools.partial(_attn_block, _dot_ad, prev_offset=off)
        _, vjp = jax.vjp(f, q_ref[...], kp_ref[...], kc_ref[...], vp_ref[...], vc_ref[...], bp_ref[...], bc_ref[...])
        dq, dkp, dkc, dvp, dvc, dbp, dbc = vjp((do_ref[...], dl_ref[...]))
        dq_ref[...] = dq
        dkp_ref[...] = dkp
        dkc_ref[...] = dkc
        dvp_ref[...] = dvp
        dvc_ref[...] = dvc
        first = jnp.logical_and(pl.program_id(1) == 0, pl.program_id(2) == 0)

        @pl.when(first)
        def _():
            dbp_ref[...] = dbp
            dbc_ref[...] = dbc

        @pl.when(jnp.logical_not(first))
        def _():
            dbp_ref[...] += dbp
            dbc_ref[...] += dbc

    blocks = jax.ShapeDtypeStruct(dov.view.shape, F32)
    grid = (HEADS_PER_GROUP, dilation, nb)
    dq, dkp, dkc, dvp, dvc, dbp, dbc = pl.pallas_call(
        body, grid=grid,
        in_specs=[qv.spec(0, nb), kv.spec(-1, nb), kv.spec(0, nb), vv.spec(-1, nb), vv.spec(0, nb), bias, bias,
                  cur, dlv.spec(0, nb)],
        out_specs=[cur] * 5 + [bias] * 2, out_shape=[blocks] * 5 + [jax.ShapeDtypeStruct(bp.shape, F32)] * 2,
        compiler_params=_params(("arbitrary", "arbitrary", "arbitrary")), name=name,
    )(qv.view, kv.view, kv.view, vv.view, vv.view, bp, bc, dov.view, dlv.view)

    nxt = dov.spec(1, nb)

    def add_body(kc_ref, kp_ref, vc_ref, vp_ref, dk_ref, dv_ref):
        has_next = (pl.program_id(2) + 1 < nb).astype(F32)
        dk_ref[...] = kc_ref[...] + kp_ref[...] * has_next
        dv_ref[...] = vc_ref[...] + vp_ref[...] * has_next

    dk, dv = pl.pallas_call(
        add_body, grid=grid, in_specs=[cur, nxt, cur, nxt], out_specs=[cur, cur], out_shape=[blocks] * 2,
        compiler_params=_params(("parallel", "parallel", "parallel")), name=name + "_kv",
    )(dkc, dkp, dvc, dvp)
    return [t.reshape(t_len, ATTN_OUT) for t in (dq, dk, dv)] + [dbp, dbc]


def _t5_bucket(dist):
    max_exact = N_BUCKETS // 2
    d_f = jnp.maximum(dist, 1).astype(F32)
    large = max_exact + (jnp.log(d_f / max_exact) / math.log(MAX_DISTANCE / max_exact)
                         * (N_BUCKETS - max_exact)).astype(I32)
    large = jnp.minimum(large, N_BUCKETS - 1)
    return jnp.where(dist < max_exact, dist, large)


def _bucket_tables():
    blk = ATTN_BLOCK
    qi = jnp.arange(blk)[:, None]
    ki = jnp.arange(blk)[None, :]
    out = []
    for _, dilation in DILATED_GROUPS:
        rel_p = jnp.maximum(blk + qi - ki, 0) * dilation
        rel_c = jnp.maximum(qi - ki, 0) * dilation
        out.append(jnp.stack([_t5_bucket(rel_p), _t5_bucket(rel_c)]))
    return jnp.stack(out).astype(I32)


def bias_fwd(table, buckets, name):
    blk = ATTN_BLOCK

    def body(tab_ref, bkt_ref, out_ref):
        for g in range(len(DILATED_GROUPS)):
            for half in range(2):
                bk = bkt_ref[g, half]
                for hh in range(HEADS_PER_GROUP):
                    h = g * HEADS_PER_GROUP + hh
                    acc = jnp.zeros((blk, blk), F32)
                    for b in range(N_BUCKETS):
                        acc = jnp.where(bk == b, tab_ref[b, h], acc)
                    out_ref[h, half] = acc

    return pl.pallas_call(
        body, in_specs=[pl.BlockSpec(memory_space=pltpu.SMEM), pl.BlockSpec(memory_space=pltpu.VMEM)],
        out_specs=pl.BlockSpec(memory_space=pltpu.VMEM),
        out_shape=jax.ShapeDtypeStruct((N_HEADS, 2, blk, blk), F32), name=name,
    )(table, buckets)


def bias_bwd(dbias, buckets, name):
    def body(db_ref, bkt_ref, out_ref):
        rows = lax.broadcasted_iota(I32, (N_BUCKETS, LANES), 0)
        cols = lax.broadcasted_iota(I32, (N_BUCKETS, LANES), 1)
        acc = jnp.zeros((N_BUCKETS, LANES), F32)
        for g in range(len(DILATED_GROUPS)):
            bk_p, bk_c = bkt_ref[g, 0], bkt_ref[g, 1]
            for hh in range(HEADS_PER_GROUP):
                h = g * HEADS_PER_GROUP + hh
                d_p, d_c = db_ref[h, 0], db_ref[h, 1]
                for b in range(N_BUCKETS):
                    s = jnp.sum(jnp.where(bk_p == b, d_p, 0.0)) + jnp.sum(jnp.where(bk_c == b, d_c, 0.0))
                    acc = jnp.where(jnp.logical_and(rows == b, cols == h), s, acc)
        out_ref[...] = acc

    return pl.pallas_call(
        body, in_specs=[pl.BlockSpec(memory_space=pltpu.VMEM)] * 2, out_specs=pl.BlockSpec(memory_space=pltpu.VMEM),
        out_shape=jax.ShapeDtypeStruct((N_BUCKETS, LANES), F32), name=name,
    )(dbias, buckets)


def _rwkv_chunk(dot, s0, r, lw, k, v, a, b):
    def each(f, *lists):
        return [f(*xs) for xs in zip(*lists)]

    def mm(mode):
        return lambda p, q: dot(p, q, mode, True)

    def mul(p, q):
        return p * q

    def add(p, q):
        return p + q

    c, width = r[0].shape
    lane_a = (lax.broadcasted_iota(I32, (1, width), 1) < width // 2).astype(F32)
    time_a = (lax.broadcasted_iota(I32, (1, 2 * c), 1) < c).astype(F32)
    ti = lax.broadcasted_iota(I32, (c, 2 * c), 0)
    si = lax.broadcasted_iota(I32, (c, 2 * c), 1)
    si = jnp.where(si < c, si, si - c)
    incl, strict = si <= ti, si < ti
    ones_incl = (lax.broadcasted_iota(I32, (c, c), 1) <= lax.broadcasted_iota(I32, (c, c), 0)).astype(F32)
    same_head = _pair_block_mask(width)

    def by_head(x):
        return jnp.concatenate([x * lane_a, x * (1.0 - lane_a)], axis=0)

    def by_block(p):
        return jnp.concatenate([p * time_a, p * (1.0 - time_a)], axis=0)

    cum = each(lambda x: dot(ones_incl, x, 'nn', True), lw)
    w_incl = each(jnp.exp, cum)
    w_prev = each(lambda cu, x: jnp.exp(cu - x), cum, lw)
    w_inv = each(lambda cu: jnp.exp(-cu), cum)
    w_end = each(lambda x: jnp.exp(jnp.sum(x, axis=0, keepdims=True)), lw)
    a_t, r_t, b_t, k_t = each(mul, a, w_prev), each(mul, r, w_incl), each(mul, b, w_inv), each(mul, k, w_inv)
    b_h, k_h, v_h = each(by_head, b_t), each(by_head, k_t), each(by_head, v)
    l_ab = each(lambda p, q: jnp.where(strict, dot(p, q, 'nt', True), 0.0), a_t, b_h)
    l_ak = each(lambda p, q: jnp.where(strict, dot(p, q, 'nt', True), 0.0), a_t, k_h)
    u = each(add, each(mm('nt'), a_t, s0), each(mm('nn'), l_ak, v_h))
    u = each(add, u, each(mm('nn'), l_ab, each(by_head, u)))
    power = l_ab
    for _ in range(int(math.log2(c)) - 1):
        power = each(mm('nn'), power, each(by_block, power))
        u = each(add, u, each(mm('nn'), power, each(by_head, u)))
    m_rb = each(lambda p, q: jnp.where(incl, dot(p, q, 'nt', True), 0.0), r_t, b_h)
    m_rk = each(lambda p, q: jnp.where(incl, dot(p, q, 'nt', True), 0.0), r_t, k_h)
    y = each(add, each(add, each(mm('nt'), r_t, s0), each(mm('nn'), m_rb, each(by_head, u))), each(mm('nn'), m_rk, v_h))
    outer = each(lambda uu, vv, bb, kk: dot(jnp.concatenate([uu, vv], axis=0), jnp.concatenate([bb, kk], axis=0),
                                             'tn', True), u, v, b_t, k_t)
    s1 = each(lambda s, o, we: (s + jnp.where(same_head, o, 0.0)) * we, s0, outer, w_end)
    return y, s1


def _pair_block_mask(width):
    rows = lax.broadcasted_iota(I32, (width, width), 0) < width // 2
    cols = lax.broadcasted_iota(I32, (width, width), 1) < width // 2
    return rows == cols


def _rwkv_tiling(t, width):
    pair = 2 * RWKV_HEAD
    n_pairs = width // pair
    assert n_pairs * pair == width and t % RWKV_CHUNK == 0
    return pair, n_pairs, math.gcd(n_pairs, RWKV_PAIRS_PER_STEP), t // RWKV_CHUNK


def rwkv_fwd(r, lw, k, v, a, b, name, gather=()):
    t, width = r.shape
    pair, n_pairs, ps, nc = _rwkv_tiling(t, width)
    c = RWKV_CHUNK
    steps = n_pairs // ps * nc
    ng = len(gather)
    row = pl.BlockSpec((c, ps * pair), lambda i, j: (j, i))
    cols = [slice(q * pair, (q + 1) * pair) for q in range(ps)]

    def body(*refs):
        ins, x_refs = refs[:6], refs[6:6 + ng]
        y_ref, s0_ref = refs[6 + ng:8 + ng]
        out_refs, state, sems = refs[8 + ng:8 + 2 * ng], refs[8 + 2 * ng], refs[9 + 2 * ng:]
        step = pl.program_id(0) * nc + pl.program_id(1)
        if ng:
            start, forward, finish = _gather_schedule(x_refs, out_refs, *sems)
            pl.when(step == 0)(start)
            pl.when(step == steps - 1 - steps // 8)(forward)

        @pl.when(pl.program_id(1) == 0)
        def _():
            state[...] = jnp.zeros_like(state)

        s0 = [state[q] for q in range(ps)]
        y, s1 = _rwkv_chunk(_dot, s0, *[[ref[:, cs] for cs in cols] for ref in ins])
        for q in range(ps):
            s0_ref[q] = s0[q]
            y_ref[:, cols[q]] = y[q]
            state[q] = s1[q]
        if ng:
            pl.when(step == steps - 1)(finish)

    return pl.pallas_call(
        body, grid=(n_pairs // ps, nc), in_specs=[row] * 6 + [_ANY] * ng,
        out_specs=[row, pl.BlockSpec((ps, None, pair, pair), lambda i, j: (i, j, 0, 0))] + [_ANY] * ng,
        out_shape=[jax.ShapeDtypeStruct((t, width), F32), jax.ShapeDtypeStruct((n_pairs, nc, pair, pair), F32)]
        + [jax.ShapeDtypeStruct((N_DEV,) + g.shape, g.dtype) for g in gather],
        scratch_shapes=[pltpu.VMEM((ps, pair, pair), F32)] + (_exchange_scratch(ng) if ng else []),
        compiler_params=_params(("arbitrary", "arbitrary")), name=name,
    )(r, lw, k, v, a, b, *gather)


def rwkv_bwd(r, lw, k, v, a, b, s0, dy, name, scatter=()):
    t, width = r.shape
    pair, n_pairs, ps, nc = _rwkv_tiling(t, width)
    c = RWKV_CHUNK
    steps = n_pairs // ps * nc
    ns = len(scatter)
    row = pl.BlockSpec((c, ps * pair), lambda i, j: (nc - 1 - j, i))
    st = pl.BlockSpec((ps, None, pair, pair), lambda i, j: (i, nc - 1 - j, 0, 0))
    cols = [slice(q * pair, (q + 1) * pair) for q in range(ps)]

    def body(*refs):
        ins, s0_ref, dy_ref, send_refs = refs[:6], refs[6], refs[7], refs[8:8 + ns]
        grad_refs, recv_refs = refs[8 + ns:14 + ns], refs[14 + ns:14 + 2 * ns]
        dstate, sems = refs[14 + 2 * ns], refs[15 + 2 * ns:]
        step = pl.program_id(0) * nc + pl.program_id(1)
        if ns:
            start, finish = _scatter_schedule(send_refs, recv_refs, *sems)
            pl.when(step == 0)(start)

        @pl.when(pl.program_id(1) == 0)
        def _():
            dstate[...] = jnp.zeros_like(dstate)

        pairs = range(ps)
        _, vjp = jax.vjp(functools.partial(_rwkv_chunk, _dot_ad), [s0_ref[q] for q in pairs],
                         *[[ref[:, cs] for cs in cols] for ref in ins])
        grads = vjp(([dy_ref[:, cs] for cs in cols], [dstate[q] for q in pairs]))
        same_head = _pair_block_mask(pair)
        for q in pairs:
            dstate[q] = jnp.where(same_head, grads[0][q], 0.0)
            for ref, g in zip(grad_refs, grads[1:]):
                ref[:, cols[q]] = g[q]
        if ns:
            pl.when(step == steps - 1)(finish)

    return pl.pallas_call(
        body, grid=(n_pairs // ps, nc), in_specs=[row] * 6 + [st, row] + [_ANY] * ns,
        out_specs=[row] * 6 + [_ANY] * ns,
        out_shape=[jax.ShapeDtypeStruct((t, width), F32)] * 6 + [jax.ShapeDtypeStruct(s.shape, s.dtype) for s in scatter],
        scratch_shapes=[pltpu.VMEM((ps, pair, pair), F32)] + (_exchange_scratch(ns) if ns else []),
        compiler_params=_params(("arbitrary", "arbitrary")), name=name,
    )(r, lw, k, v, a, b, s0, dy, *scatter)


_ANY = pl.BlockSpec(memory_space=pl.ANY)


def _exchange_scratch(n_arrays):
    return [pltpu.SemaphoreType.DMA((n_arrays, N_DEV - 1)), pltpu.SemaphoreType.DMA((n_arrays, N_DEV - 1)),
            pltpu.SemaphoreType.DMA((n_arrays,))]


def _gather_schedule(x_refs, out_refs, send_sems, recv_sems, local_sems):
    x, y, c = lax.axis_index("x"), lax.axis_index("y"), lax.axis_index("c")
    me, sibling = (x, y, c), (x, y, 1 - c)
    chips = [(1 - x, y), (x, 1 - y), (1 - x, 1 - y)]
    arrays = range(len(x_refs))

    def slot(a, pos):
        return out_refs[a].at[4 * pos[0] + 2 * pos[1] + pos[2]]

    def copy(a, i, block, to, src=None):
        return pltpu.make_async_remote_copy(
            src_ref=slot(a, block) if src is None else src, dst_ref=slot(a, block), send_sem=send_sems.at[a, i],
            recv_sem=recv_sems.at[a, i], device_id=to, device_id_type=pl.DeviceIdType.MESH)

    mine = [pltpu.make_async_copy(x_refs[a], slot(a, me), local_sems.at[a]) for a in arrays]
    first = [[copy(a, 0, me, sibling, src=x_refs[a])]
             + [copy(a, 1 + j, me, (*chip, c), src=x_refs[a]) for j, chip in enumerate(chips)] for a in arrays]
    passed = [[copy(a, 4 + j, (*chip, c), sibling) for j, chip in enumerate(chips)] for a in arrays]

    def start():
        for a in arrays:
            mine[a].start()
            for cp in first[a]:
                cp.start()

    def forward():
        for j, chip in enumerate(chips):
            for a in arrays:
                copy(a, 1 + j, (*chip, c), me).wait_recv()
                passed[a][j].start()

    def finish():
        for a in arrays:
            copy(a, 0, sibling, me).wait_recv()
            for j, chip in enumerate(chips):
                copy(a, 4 + j, (*chip, 1 - c), me).wait_recv()
            for cp in first[a] + passed[a]:
                cp.wait_send()
            mine[a].wait()

    return start, forward, finish


def _scatter_schedule(in_refs, out_refs, send_sems, recv_sems, local_sems):
    x, y, c = lax.axis_index("x"), lax.axis_index("y"), lax.axis_index("c")
    my_chip = 2 * x + y
    mine, remote = [], []
    for a, (src, dst) in enumerate(zip(in_refs, out_refs)):
        mine.append(pltpu.make_async_copy(src.at[my_chip], dst.at[my_chip], local_sems.at[a]))
        for i in range(1, N_CHIPS):
            px, py = x ^ (i >> 1), y ^ (i & 1)
            remote.append(pltpu.make_async_remote_copy(
                src_ref=src.at[2 * px + py], dst_ref=dst.at[my_chip], send_sem=send_sems.at[a, i - 1],
                recv_sem=recv_sems.at[a, i - 1], device_id=(px, py, c), device_id_type=pl.DeviceIdType.MESH))

    def start():
        for cp in mine + remote:
            cp.start()

    def finish():
        for cp in remote:
            cp.wait_recv()
        for cp in remote:
            cp.wait_send()
        for cp in mine:
            cp.wait()

    return start, finish


def pair_exchange(parts, name):
    n = len(parts)

    def body(*refs):
        x, y, c = lax.axis_index("x"), lax.axis_index("y"), lax.axis_index("c")
        send_sems, recv_sems = refs[2 * n:]
        copies = [pltpu.make_async_remote_copy(
            src_ref=refs[a].at[q, 1 - c], dst_ref=refs[n + a].at[q], send_sem=send_sems.at[a, q],
            recv_sem=recv_sems.at[a, q], device_id=(x, y, 1 - c), device_id_type=pl.DeviceIdType.MESH)
            for a in range(n) for q in range(N_CHIPS)]
        for cp in copies:
            cp.start()
        for cp in copies:
            cp.wait_recv()
        for cp in copies:
            cp.wait_send()

    return pl.pallas_call(
        body, in_specs=[_ANY] * n, out_specs=[_ANY] * n,
        out_shape=[jax.ShapeDtypeStruct((N_CHIPS,) + s.shape[2:], s.dtype) for s in parts],
        scratch_shapes=[pltpu.SemaphoreType.DMA((n, N_CHIPS)), pltpu.SemaphoreType.DMA((n, N_CHIPS))], name=name,
    )(*parts)


def pair_add(mine, theirs, core, name):
    _, _, k, n = mine.shape
    tc = _pick(n, (2048, 1024, 512))
    tr = _row_tile(k, tc * 3 * jnp.dtype(mine.dtype).itemsize)

    def body(core_ref, a_ref, b_ref, o_ref):
        o_ref[...] = (a_ref[...].astype(F32) + b_ref[...].astype(F32)).astype(o_ref.dtype)

    one = pl.BlockSpec((None, tr, tc), lambda q, i, j, core_ref: (q, i, j))
    grid_spec = pltpu.PrefetchScalarGridSpec(
        num_scalar_prefetch=1, grid=(N_CHIPS, k // tr, n // tc),
        in_specs=[pl.BlockSpec((None, None, tr, tc), lambda q, i, j, core_ref: (q, core_ref[0], i, j)), one],
        out_specs=one)
    return pl.pallas_call(
        body, grid_spec=grid_spec, out_shape=jax.ShapeDtypeStruct(theirs.shape, BF16),
        compiler_params=_params(("parallel", "parallel", "parallel")), name=name,
    )(core, mine, theirs)


def all_gather_many(shards, name):
    n = len(shards)

    def body(*refs):
        start, forward, finish = _gather_schedule(refs[:n], refs[n:2 * n], *refs[2 * n:])
        start()
        forward()
        finish()

    return pl.pallas_call(
        body, in_specs=[_ANY] * n, out_specs=[_ANY] * n,
        out_shape=[jax.ShapeDtypeStruct((N_DEV,) + s.shape, s.dtype) for s in shards],
        scratch_shapes=_exchange_scratch(n), name=name,
    )(*shards)


def project_and_gather(h, w_all, shards):
    return matmul(h, w_all, 'nn', F32, "proj_in", exchange=('gather', shards))


def input_grad_and_scatter(d_parts, w_all, offsets, parts):
    return matmul(d_parts, [w_all] * len(d_parts), 'nt', F32, "d_h_in", b_col_offsets=offsets,
                  exchange=('scatter', parts))


def scan_and_gather(scan_in, shards):
    y, s0, *gathered = rwkv_fwd(*scan_in, name="rwkv_scan", gather=shards)
    return y, s0, gathered


def scan_bwd_and_scatter(scan_in, s0, dy, parts):
    res = rwkv_bwd(*scan_in, s0, dy, name="rwkv_scan_bwd", scatter=parts)
    return res[:6], res[6:]


def adamw_shard(parts, w, m, v, name):
    _, k, n = w.shape
    slots = parts.shape[0]
    tc = _pick(n, (2048, 1024, 512))
    tr = _row_tile(k, tc * (slots * jnp.dtype(parts.dtype).itemsize + 7 * 4))

    def body(p_ref, w_ref, m_ref, v_ref, g_ref, d_ref, nm_ref, nv_ref):
        _adamw_block(p_ref, w_ref, m_ref, v_ref, g_ref, d_ref, nm_ref, nv_ref)

    one = pl.BlockSpec((None, tr, tc), lambda i, j: (0, i, j))
    return pl.pallas_call(
        body, grid=(k // tr, n // tc), in_specs=[pl.BlockSpec((slots, tr, tc), lambda i, j: (0, i, j))] + [one] * 3,
        out_specs=[one] * 4, out_shape=[jax.ShapeDtypeStruct(w.shape, F32)] * 4,
        compiler_params=_params(("parallel", "parallel")), name=name,
    )(parts, w, m, v)


def _adamw_block(p_ref, w_ref, m_ref, v_ref, g_ref, d_ref, nm_ref, nv_ref):
    g = p_ref[0].astype(F32)
    for j in range(1, p_ref.shape[0]):
        g = g + p_ref[j].astype(F32)
    new_m = ADAM_B1 * m_ref[...] + (1.0 - ADAM_B1) * g
    new_v = ADAM_B2 * v_ref[...] + (1.0 - ADAM_B2) * jnp.square(g)
    m_hat = new_m / (1.0 - ADAM_B1 ** ADAM_STEP)
    v_hat = new_v / (1.0 - ADAM_B2 ** ADAM_STEP)
    g_ref[...] = g
    d_ref[...] = -ADAM_LR * (m_hat / (jnp.sqrt(v_hat) + ADAM_EPS) + ADAM_WD * w_ref[...])
    nm_ref[...] = new_m
    nv_ref[...] = new_v


def adamw(parts, w, m, v, name):
    rows = w.shape[0]
    tile = _row_tile(rows, N_DEV * LANES * jnp.dtype(parts.dtype).itemsize + 7 * LANES * 4)

    def body(p_ref, w_ref, m_ref, v_ref, g_ref, d_ref, nm_ref, nv_ref):
        _adamw_block(p_ref, w_ref, m_ref, v_ref, g_ref, d_ref, nm_ref, nv_ref)

    flat = pl.BlockSpec((tile, LANES), lambda i: (i, 0))
    return pl.pallas_call(
        body, grid=(rows // tile,), in_specs=[pl.BlockSpec((N_DEV, tile, LANES), lambda i: (0, i, 0))] + [flat] * 3,
        out_specs=[flat] * 4, out_shape=[jax.ShapeDtypeStruct(w.shape, F32)] * 4,
        compiler_params=_params(("parallel",)), name=name,
    )(parts, w, m, v)


def _part_rows(n_elems):
    return -(-n_elems // (PACK_ROWS * LANES)) * PACK_ROWS


def _pack(arrays, dtype, lead=()):
    parts, layout, off = [], [], 0
    for arr in arrays:
        n = math.prod(arr.shape[len(lead):])
        rows = _part_rows(n)
        flat = arr.reshape(lead + (n,)).astype(dtype)
        flat = jnp.pad(flat, [(0, 0)] * len(lead) + [(0, rows * LANES - n)])
        parts.append(flat.reshape(lead + (rows, LANES)))
        layout.append((off, rows))
        off += rows
    total = -(-off // 1024) * 1024
    if total > off:
        parts.append(jnp.zeros(lead + (total - off, LANES), dtype))
    return jnp.concatenate(parts, axis=len(lead)), layout


def _unpack(buf, layout, shapes, lead=()):
    out = []
    for (off, rows), shape in zip(layout, shapes):
        n = math.prod(shape)
        piece = lax.slice_in_dim(buf, off, off + rows, axis=len(lead))
        out.append(piece.reshape(lead + (rows * LANES,))[..., :n].reshape(lead + tuple(shape)))
    return out


def _split_shards(full, axis):
    if axis == 0:
        return full.reshape((N_DEV, full.shape[0] // N_DEV, full.shape[1]))
    return full.reshape((full.shape[0], N_DEV, full.shape[1] // N_DEV)).transpose(1, 0, 2)


def _join_shards(shards, axis):
    if axis == 0:
        return shards.reshape((-1, shards.shape[2]))
    return shards.transpose(1, 0, 2).reshape((shards.shape[1], -1))


def _shift_down(t):
    return jnp.pad(t, ((1, 0), (0, 0)))[:-1]


def _shift_up(t):
    return jnp.pad(t, ((0, 1), (0, 0)))[1:]


def kernel(x, p, norm_mix, w_in, q_gain, k_gain, rel_bias, w_attn_up, shift_mix, w0, w_decay_up, a0, w_aaa_up, w_gate_up, k_k, k_a, r_k, gn_w, gn_b, w_rwkv_up, w_out, norm_mlp, w_mlp_in, w_mlp_out, norm_ple, w_ple_gate, w_ple_proj, loss_target, m_norm_mix, m_w_in, m_q_gain, m_k_gain, m_rel_bias, m_w_attn_up, m_shift_mix, m_w0, m_w_decay_up, m_a0, m_w_aaa_up, m_w_gate_up, m_k_k, m_k_a, m_r_k, m_gn_w, m_gn_b, m_w_rwkv_up, m_w_out, m_norm_mlp, m_w_mlp_in, m_w_mlp_out, m_norm_ple, m_w_ple_gate, m_w_ple_proj, v_norm_mix, v_w_in, v_q_gain, v_k_gain, v_rel_bias, v_w_attn_up, v_shift_mix, v_w0, v_w_decay_up, v_a0, v_w_aaa_up, v_w_gate_up, v_k_k, v_k_a, v_r_k, v_gn_w, v_gn_b, v_w_rwkv_up, v_w_out, v_norm_mlp, v_w_mlp_in, v_w_mlp_out, v_norm_ple, v_w_ple_gate, v_w_ple_proj):
    given = dict(locals())
    xs = x[0]
    t_len, d_model = xs.shape
    target = loss_target[0]
    p_bf = p[0, 0].astype(BF16)
    rw_width = w0.shape[1]
    n_rheads = rw_width // RWKV_HEAD
    lora_d, lora_a, lora_g = w_decay_up.shape[1], w_aaa_up.shape[1], w_gate_up.shape[1]
    z_width = shift_mix.shape[1]
    z_pad = -(-z_width // LANES) * LANES
    qkv_width = 3 * ATTN_WIDTH
    assert z_width == 3 * rw_width + lora_d + lora_a + lora_g
    assert N_DEV * w_in.shape[2] == qkv_width + z_width + 2 * d_model
    for window, dilation in DILATED_GROUPS:
        assert window // dilation == ATTN_BLOCK and t_len % (dilation * ATTN_BLOCK) == 0

    shard_bf = {n: given[n][0].astype(BF16) for n in BIG}
    early = ['w_in', 'w_decay_up', 'w_aaa_up', 'w_gate_up']
    during_proj = ['w_mlp_out']
    during_scan = [n for n in BIG if n not in early + during_proj]
    late = during_proj + during_scan
    full = {n: _join_shards(g, 1) for n, g in zip(early, all_gather_many([shard_bf[n] for n in early], "gather_w_in"))}
    z_end = qkv_width + z_width
    w_all = jnp.concatenate([full['w_in'][:, :z_end], jnp.zeros((d_model, z_pad - z_width), BF16),
                             full['w_in'][:, z_end:]], axis=1)
    gates_at = qkv_width + z_pad

    (h_in,) = rowmap_fwd(lambda a, g: st_norm(a, g)[1:], [xs], [norm_mix], [BF16], "norm_in")
    proj, proj_gathered = project_and_gather(h_in, w_all, [shard_bf[n] for n in during_proj])
    qkv, z = proj[:, :qkv_width], proj[:, qkv_width:gates_at]
    gate_a, gate_r = proj[:, gates_at:gates_at + d_model], proj[:, gates_at + d_model:]

    q_raw = qkv[:, :ATTN_WIDTH].reshape(t_len * N_HEADS, HEAD_DIM)
    k_raw = qkv[:, ATTN_WIDTH:2 * ATTN_WIDTH].reshape(t_len * N_HEADS, HEAD_DIM)
    q_n, k_n = rowmap_fwd(st_qk_norm, [q_raw, k_raw], [q_gain, k_gain], [F32, F32], "qk_norm")
    q_n, k_n = q_n.reshape(t_len, ATTN_WIDTH), k_n.reshape(t_len, ATTN_WIDTH)
    buckets = _bucket_tables()
    bias = bias_fwd(rel_bias, buckets, "attn_bias")

    att_in, att_o, att_l = [], [], []
    for g, (_, dilation) in enumerate(DILATED_GROUPS):
        hs = slice(g * HEADS_PER_GROUP, (g + 1) * HEADS_PER_GROUP)
        cols = slice(g * ATTN_OUT, (g + 1) * ATTN_OUT)
        v_g = qkv[:, 2 * ATTN_WIDTH + g * ATTN_OUT:2 * ATTN_WIDTH + (g + 1) * ATTN_OUT]
        ops = ((q_n[:, cols], 0), (k_n[:, cols], 0), (v_g, 0), bias[hs, 0], bias[hs, 1])
        o_g, l_g = attn_fwd(*ops, dilation, name=f"attn_fwd_{g}")
        att_in.append(ops)
        att_o.append(o_g)
        att_l.append(l_g)
    (attn,) = rowmap_fwd(st_merge, att_o + att_l, [], [BF16], "attn_merge")

    c0 = rw_width
    cuts = [0, c0, 2 * c0, 3 * c0, 3 * c0 + lora_d, 3 * c0 + lora_d + lora_a, z_width]
    z_parts = [z[:, lo:hi] for lo, hi in zip(cuts[:-1], cuts[1:])]
    z_prev = [_shift_down(t) for t in z_parts]
    mixes = [shift_mix[:, lo:hi] for lo, hi in zip(cuts[:-1], cuts[1:])]
    pre_params = mixes + [w0, full['w_decay_up'], a0, full['w_aaa_up'], full['w_gate_up'], k_k, k_a]
    pre_out = rowmap_fwd(functools.partial(_st_rwkv_pre, _dot), z_parts + z_prev, pre_params, [F32] * 7, "rwkv_pre")
    r_s, lw_s, k_s, v_s, kk0_s, a_s, g_s = pre_out

    def heads(tok):
        return tok.reshape(t_len, n_rheads, RWKV_HEAD)

    def flat(tok):
        return tok.reshape(t_len, rw_width)

    aa_s, bb_s = rowmap_fwd(st_rwkv_kk, [heads(kk0_s), heads(a_s)], [], [F32, F32], "rwkv_kk")
    scan_in = [r_s, lw_s, k_s, v_s, flat(aa_s), flat(bb_s)]
    y_t, s0_h, scan_gathered = scan_and_gather(scan_in, [shard_bf[n] for n in during_scan])
    late_gathered = list(proj_gathered) + list(scan_gathered)
    wt = {n: g if SHARD_AXIS[n] == 1 else g.reshape(-1, g.shape[2]) for n, g in zip(late, late_gathered)}
    y_s = heads(y_t)
    post_params = [gn_w.reshape(1, n_rheads, RWKV_HEAD), gn_b.reshape(1, n_rheads, RWKV_HEAD), r_k]
    post_in = [y_s, heads(r_s), heads(k_s), heads(v_s), heads(g_s)]
    (rw,) = rowmap_fwd(st_rwkv_post, post_in, post_params, [BF16], "rwkv_post")
    rw = rw.reshape(t_len, rw_width)
    attn_d = matmul(attn, wt['w_attn_up'], 'nn', F32, "attn_up", b_shards=True)
    rwkv_d = matmul(rw, wt['w_rwkv_up'], 'nn', F32, "rwkv_up", b_shards=True)

    (merged,) = rowmap_fwd(st_gate, [gate_a, gate_r, attn_d, rwkv_d], [], [BF16], "gate_merge")
    mix_out = matmul(merged, wt['w_out'], 'nn', F32, "out_proj")
    x1, h_mlp = rowmap_fwd(st_res_norm, [xs, mix_out], [norm_mlp], [F32, BF16], "res_norm_mlp")
    u, act = matmul(h_mlp, wt['w_mlp_in'], 'nn', [F32, BF16], "mlp_in", b_shards=True,
                    epilogue=lambda acc: (acc,) + st_relu2(acc))
    mlp_out = matmul(act, wt['w_mlp_out'], 'nn', F32, "mlp_out")
    x2, h_ple = rowmap_fwd(st_res_norm, [x1, mlp_out], [norm_ple], [F32, BF16], "res_norm_ple")
    pg = matmul(h_ple, wt['w_ple_gate'], 'nn', F32, "ple_gate")
    pp = matmul(p_bf, wt['w_ple_proj'], 'nn', F32, "ple_proj", b_shards=True)
    dy, d_pg, d_pp, loss_local = loss_head(x2, pg, pp, target, "loss_head")

    def row_cut(full_grad):
        return full_grad.reshape(N_DEV, full_grad.shape[0] // N_DEV, full_grad.shape[1])

    grads, sends = {}, {}
    sends['w_ple_gate'] = row_cut(matmul(h_ple, d_pg, 'tn', BF16, "d_w_ple_gate"))
    sends['w_ple_proj'] = matmul(p_bf, d_pp, 'tn', BF16, "d_w_ple_proj", out_shards=True)
    d_h_ple = matmul(d_pg, wt['w_ple_gate'], 'nt', F32, "d_h_ple")
    (d_x2, d_x2_bf), (grads['norm_ple'],) = rowmap_bwd(
        st_res_norm, [x1, mlp_out], [norm_ple], [dy, d_h_ple], [F32, BF16], "res_norm_ple_bwd")
    sends['w_mlp_out'] = row_cut(matmul(act, d_x2_bf, 'tn', BF16, "d_w_mlp_out"))
    (d_u,) = matmul(d_x2_bf, wt['w_mlp_out'], 'nt', [BF16], "d_mlp_act", extras=[u],
                    epilogue=lambda d_act, u_blk: (d_act * (2.0 * jnp.maximum(u_blk, 0.0)),))
    sends['w_mlp_in'] = matmul(h_mlp, d_u, 'tn', BF16, "d_w_mlp_in", out_shards=True)
    d_h_mlp = matmul(d_u, wt['w_mlp_in'], 'nt', F32, "d_h_mlp", b_shards=True)
    (d_x1, d_x1_bf), (grads['norm_mlp'],) = rowmap_bwd(
        st_res_norm, [xs, mix_out], [norm_mlp], [d_x2, d_h_mlp], [F32, BF16], "res_norm_mlp_bwd")

    sends['w_out'] = row_cut(matmul(merged, d_x1_bf, 'tn', BF16, "d_w_out"))
    d_merged = matmul(d_x1_bf, wt['w_out'], 'nt', F32, "d_merged")
    (d_gate_a, d_gate_r, d_attn_d, d_rwkv_d), _ = rowmap_bwd(
        st_gate, [gate_a, gate_r, attn_d, rwkv_d], [], [d_merged], [BF16] * 4, "gate_merge_bwd")
    sends['w_attn_up'] = matmul(attn, d_attn_d, 'tn', BF16, "d_w_attn_up", out_shards=True)
    sends['w_rwkv_up'] = matmul(rw, d_rwkv_d, 'tn', BF16, "d_w_rwkv_up", out_shards=True)
    d_attn = matmul(d_attn_d, wt['w_attn_up'], 'nt', F32, "d_attn", b_shards=True)
    d_rw = matmul(d_rwkv_d, wt['w_rwkv_up'], 'nt', F32, "d_rw", b_shards=True)

    (d_y, d_r1, d_k1, d_v1, d_g), (d_gn_w, d_gn_b, grads['r_k']) = rowmap_bwd(
        st_rwkv_post, post_in, post_params, [heads(d_rw)], [F32] * 5, "rwkv_post_bwd")
    grads['gn_w'], grads['gn_b'] = d_gn_w.reshape(1, rw_width), d_gn_b.reshape(1, rw_width)
    core = lax.axis_index("c").astype(I32).reshape(1)

    def chip_partials(names, tag):
        mine = [sends[n].reshape((N_CHIPS, 2) + sends[n].shape[1:]) for n in names]
        theirs = pair_exchange(mine, "pair_grads_" + tag)
        return [pair_add(m, t, core, "pair_add_" + n) for n, m, t in zip(names, mine, theirs)]

    scan_grads, late_received = scan_bwd_and_scatter(scan_in, s0_h, flat(d_y), chip_partials(late, "late"))
    received = dict(zip(late, late_received))
    d_r2, d_lw, d_k2, d_v2, d_aa, d_bb = scan_grads
    (d_kk0, d_a), _ = rowmap_bwd(st_rwkv_kk, [heads(kk0_s), heads(a_s)], [], [heads(d_aa), heads(d_bb)], [F32, F32],
                                 "rwkv_kk_bwd")
    pre_cts = [[flat(d_r1), d_r2], d_lw, [flat(d_k1), d_k2], [flat(d_v1), d_v2], flat(d_kk0), flat(d_a), flat(d_g)]
    d_zp, d_pre = rowmap_bwd(functools.partial(_st_rwkv_pre, _dot_ad), z_parts + z_prev, pre_params, pre_cts,
                             [F32] * 12, "rwkv_pre_bwd")
    grads['shift_mix'] = jnp.concatenate(d_pre[:6], axis=1)
    grads['w0'], d_w_decay, grads['a0'], d_w_aaa, d_w_gate, grads['k_k'], grads['k_a'] = d_pre[6:]
    for name, full_grad in (('w_decay_up', d_w_decay), ('w_aaa_up', d_w_aaa), ('w_gate_up', d_w_gate)):
        sends[name] = _split_shards(full_grad, 1).astype(BF16)
    z_fill = [jnp.zeros((t_len, z_pad - z_width), F32)] if z_pad > z_width else []
    d_z_cur = jnp.concatenate(d_zp[:6] + z_fill, axis=1)
    d_z_prev = _shift_up(jnp.concatenate(d_zp[6:] + z_fill, axis=1))
    (d_z,) = rowmap_fwd(st_add, [d_z_cur, d_z_prev], [], [BF16], "d_z_sum")

    d_merge, _ = rowmap_bwd(st_merge, att_o + att_l, [], [d_attn], [F32] * 6, "attn_merge_bwd")
    d_qn, d_kn, d_vs, d_bias = [], [], [], []
    for g, (_, dilation) in enumerate(DILATED_GROUPS):
        dq, dk, dv, dbp, dbc = attn_bwd(*att_in[g], d_merge[g], d_merge[3 + g], dilation, name=f"attn_bwd_{g}")
        d_qn.append(dq)
        d_kn.append(dk)
        d_vs.append(dv)
        d_bias.append(jnp.stack([dbp, dbc], axis=1))
    d_table = bias_bwd(jnp.concatenate(d_bias, axis=0), buckets, "attn_bias_bwd")
    grads['rel_bias'] = d_table[:, :N_HEADS]
    d_qn = jnp.concatenate(d_qn, axis=1).reshape(t_len * N_HEADS, HEAD_DIM)
    d_kn = jnp.concatenate(d_kn, axis=1).reshape(t_len * N_HEADS, HEAD_DIM)
    (d_q, d_k), (grads['q_gain'], grads['k_gain']) = rowmap_bwd(
        st_qk_norm, [q_raw, k_raw], [q_gain, k_gain], [d_qn, d_kn], [BF16, BF16], "qk_norm_bwd")
    d_qkv = jnp.concatenate([d_q.reshape(t_len, ATTN_WIDTH), d_k.reshape(t_len, ATTN_WIDTH)]
                            + [t.astype(BF16) for t in d_vs], axis=1)

    d_gates = jnp.concatenate([d_gate_a, d_gate_r], axis=1)
    sends['w_in'] = _split_shards(jnp.concatenate([
        matmul(h_in, d_qkv, 'tn', BF16, "d_w_qkv"),
        matmul(h_in, d_z, 'tn', BF16, "d_w_z")[:, :z_width],
        matmul(h_in, d_gates, 'tn', BF16, "d_w_gates")], axis=1), 1)
    d_h_in, early_received = input_grad_and_scatter([d_qkv, d_z, d_gates], w_all, [0, qkv_width, gates_at],
                                                    chip_partials(early, "early"))
    received.update(zip(early, early_received))
    (grad_x,), (grads['norm_mix'],) = rowmap_bwd(st_norm, [xs], [norm_mix], [d_x1, d_h_in], [F32], "norm_in_bwd")

    by_name = {n: adamw_shard(received[n], given[n], given['m_' + n], given['v_' + n], "adamw_" + n) for n in BIG}
    pk = lambda prefix: _pack([given[prefix + n] for n in SMALL], F32)[0]
    small_buf, small_layout = _pack([grads[n].reshape(given[n].shape) for n in SMALL], F32)
    (small_all,) = all_gather_many([small_buf], "gather_small_grads")
    small_out = adamw(small_all, pk(''), pk('m_'), pk('v_'), "adamw_replicated")
    small_shapes = [given[n].shape for n in SMALL]
    by_name.update(zip(SMALL, zip(*[_unpack(buf, small_layout, small_shapes) for buf in small_out])))
    loss = lax.psum(loss_local, MESH_AXES)
    return (loss, grad_x[None], *[by_name[n][0] for n in WEIGHTS], *[by_name[n][1] for n in WEIGHTS],
            *[by_name[n][2] for n in WEIGHTS], *[by_name[n][3] for n in WEIGHTS])
```

```python
import functools
import math

import jax
import jax.numpy as jnp
from jax import lax
from jax.experimental import pallas as pl
from jax.experimental.pallas import tpu as pltpu

F32 = jnp.float32
BF16 = jnp.bfloat16
I32 = jnp.int32

N_DEV = 8
N_CHIPS = 4
MESH_AXES = ("x", "y", "c")
LANES = 128
PACK_ROWS = 16
VMEM_LIMIT_BYTES = 48 * 2**20
ROW_BLOCK_BYTES = 3 * 2**20

HEAD_DIM = 128
ATTN_BLOCK = 128
HEADS_PER_GROUP = 4
DILATED_GROUPS = ((128, 1), (512, 4), (2048, 16))
N_HEADS = HEADS_PER_GROUP * len(DILATED_GROUPS)
ATTN_WIDTH = N_HEADS * HEAD_DIM
ATTN_OUT = HEADS_PER_GROUP * HEAD_DIM
N_BUCKETS = 32
MAX_DISTANCE = 2048
RWKV_HEAD = 64
RWKV_CHUNK = 64
RWKV_PAIRS_PER_STEP = 8
RMS_EPS = 1e-6
GN_EPS = 64e-5
NEG_INF = -1e30

ADAM_LR = 0.001
ADAM_B1 = 0.9
ADAM_B2 = 0.999
ADAM_EPS = 1e-08
ADAM_WD = 0.01
ADAM_STEP = 10

WEIGHTS = ['norm_mix', 'w_in', 'q_gain', 'k_gain', 'rel_bias', 'w_attn_up', 'shift_mix', 'w0', 'w_decay_up', 'a0',
           'w_aaa_up', 'w_gate_up', 'k_k', 'k_a', 'r_k', 'gn_w', 'gn_b', 'w_rwkv_up', 'w_out', 'norm_mlp', 'w_mlp_in',
           'w_mlp_out', 'norm_ple', 'w_ple_gate', 'w_ple_proj']
SHARD_AXIS = {'w_in': 1, 'w_attn_up': 1, 'w_decay_up': 1, 'w_aaa_up': 1, 'w_gate_up': 1, 'w_rwkv_up': 1, 'w_out': 0,
              'w_mlp_in': 1, 'w_mlp_out': 0, 'w_ple_gate': 0, 'w_ple_proj': 1}
BIG = [n for n in WEIGHTS if n in SHARD_AXIS]
SMALL = [n for n in WEIGHTS if n not in SHARD_AXIS]


def _params(sem):
    return pltpu.CompilerParams(dimension_semantics=sem, vmem_limit_bytes=VMEM_LIMIT_BYTES)


_DN = {'nn': (((1,), (0,)), ((), ())), 'nt': (((1,), (1,)), ((), ())), 'tn': (((0,), (0,)), ((), ()))}


def _dot(a, b, mode, exact):
    if exact:
        return lax.dot_general(a, b, _DN[mode], precision=lax.Precision.HIGH, preferred_element_type=F32)
    return lax.dot_general(a.astype(BF16), b.astype(BF16), _DN[mode], preferred_element_type=F32)


@functools.partial(jax.custom_vjp, nondiff_argnums=(2, 3))
def _dot_ad(a, b, mode, exact):
    return _dot(a, b, mode, exact)


def _dot_ad_fwd(a, b, mode, exact):
    return _dot(a, b, mode, exact), (a, b)


def _dot_ad_bwd(mode, exact, res, g):
    a, b = res
    if mode == 'nn':
        return _dot(g, b, 'nt', exact), _dot(a, g, 'tn', exact)
    if mode == 'nt':
        return _dot(g, b, 'nn', exact), _dot(g, a, 'tn', exact)
    return _dot(b, g, 'nt', exact), _dot(a, g, 'nn', exact)


_dot_ad.defvjp(_dot_ad_fwd, _dot_ad_bwd)


def _pick(n, cands):
    for c in cands:
        if n % c == 0:
            return c
    return n


def matmul(a, b, mode, out_dtype, name, b_shards=False, out_shards=False, b_col_offsets=None, epilogue=None,
           extras=(), exchange=None):
    a_list = list(a) if isinstance(a, (list, tuple)) else [a]
    b_list = list(b) if isinstance(b, (list, tuple)) else [b]
    seg = len(a_list)
    assert all(t.dtype == BF16 for t in a_list + b_list), name
    assert seg == 1 or (mode == 'nt' and not b_shards), name
    m = a_list[0].shape[1] if mode == 'tn' else a_list[0].shape[0]
    ks = [t.shape[0] if mode == 'tn' else t.shape[1] for t in a_list]
    b0 = b_list[0]
    b_rows, b_cols = (b0.shape[1], N_DEV * b0.shape[2]) if b_shards else b0.shape
    n = b_rows if mode == 'nt' else b_cols
    if seg == 1 and b_col_offsets is None:
        assert (b_cols if mode == 'nt' else b_rows) == ks[0], (name, a_list[0].shape, b0.shape)
    offsets = list(b_col_offsets) if b_col_offsets is not None else [0] * seg
    tm = _pick(m, (1024, 512, 256, 128))
    tn = _pick(n // N_DEV if (out_shards or (b_shards and mode != 'nt')) else n, (1024, 512, 256, 128))
    k_units = [kk // N_DEV if (b_shards and mode == 'nt') else kk for kk in ks] + [o for o in offsets if o]
    k_cands = (1024, 512, 256, 128) if (epilogue is not None or extras) else (2048, 1024, 512, 256, 128)
    tk = next((c for c in k_cands if all(u % c == 0 for u in k_units)), k_units[0])
    nks = [kk // tk for kk in ks]
    starts = [sum(nks[:s]) for s in range(seg)]
    nk = sum(nks)
    grid = (m // tm, n // tn, nk)
    total_steps = grid[0] * grid[1] * nk
    kind, moved = exchange if exchange is not None else (None, [])
    nx, ne = len(moved), len(extras)
    out_dtypes = list(out_dtype) if isinstance(out_dtype, (list, tuple)) else [out_dtype]
    no = len(out_dtypes)

    def body(*refs):
        a_refs, b_refs, x_refs = refs[:seg], refs[seg:2 * seg], refs[2 * seg:2 * seg + ne]
        pos = 2 * seg + ne
        moved_in, o_refs = refs[pos:pos + nx], refs[pos + nx:pos + nx + no]
        pos += nx + no
        moved_out, acc_ref, sems = refs[pos:pos + nx], refs[pos + nx], refs[pos + nx + 1:]
        kk = pl.program_id(2)
        step = (pl.program_id(0) * grid[1] + pl.program_id(1)) * nk + kk
        if kind == 'gather':
            start, forward, finish = _gather_schedule(moved_in, moved_out, *sems)
            pl.when(step == 0)(start)
            pl.when(step == total_steps - 1 - total_steps // 8)(forward)
        elif kind == 'scatter':
            start, finish = _scatter_schedule(moved_in, moved_out, *sems)
            pl.when(step == 0)(start)

        @pl.when(kk == 0)
        def _():
            acc_ref[...] = jnp.zeros_like(acc_ref)

        for s in range(seg):
            def accumulate(s=s):
                acc_ref[...] += lax.dot_general(a_refs[s][...], b_refs[s][...], _DN[mode], preferred_element_type=F32)

            if seg == 1:
                accumulate()
            else:
                pl.when(jnp.logical_and(kk >= starts[s], kk < starts[s] + nks[s]))(accumulate)

        @pl.when(kk == nk - 1)
        def _():
            acc = acc_ref[...]
            outs = (acc,) if epilogue is None else epilogue(acc, *[x[...] for x in x_refs])
            for r, v in zip(o_refs, outs):
                r[...] = v.astype(r.dtype)

        if kind is not None:
            pl.when(step == total_steps - 1)(finish)

    def k_of(kk, s):
        return kk if seg == 1 else jnp.clip(kk - starts[s], 0, nks[s] - 1)

    a_specs, b_specs = [], []
    for s in range(seg):
        off = offsets[s] // tk
        if mode == 'tn':
            a_specs.append(pl.BlockSpec((tk, tm), lambda i, j, kk, s=s: (k_of(kk, s), i)))
        else:
            a_specs.append(pl.BlockSpec((tm, tk), lambda i, j, kk, s=s: (i, k_of(kk, s))))
        if mode == 'nt':
            if b_shards:
                per = b0.shape[2] // tk
                b_specs.append(pl.BlockSpec((None, tn, tk), lambda i, j, kk: (kk // per, j, kk % per)))
            else:
                b_specs.append(pl.BlockSpec((tn, tk), lambda i, j, kk, s=s, off=off: (j, off + k_of(kk, s))))
        else:
            if b_shards:
                per = b0.shape[2] // tn
                b_specs.append(pl.BlockSpec((None, tk, tn), lambda i, j, kk: (j // per, kk, j % per)))
            else:
                b_specs.append(pl.BlockSpec((tk, tn), lambda i, j, kk: (kk, j)))
    tile = pl.BlockSpec((tm, tn), lambda i, j, kk: (i, j))
    if out_shards:
        assert epilogue is None
        per_o = n // N_DEV // tn
        o_specs = [pl.BlockSpec((None, tm, tn), lambda i, j, kk: (j // per_o, i, j % per_o))]
        o_shapes = [jax.ShapeDtypeStruct((N_DEV, m, n // N_DEV), out_dtypes[0])]
    else:
        o_specs = [tile] * no
        o_shapes = [jax.ShapeDtypeStruct((m, n), d) for d in out_dtypes]
    moved_shapes = [jax.ShapeDtypeStruct(((N_DEV,) + t.shape) if kind == 'gather' else t.shape, t.dtype) for t in moved]
    res = pl.pallas_call(
        body, grid=grid, in_specs=a_specs + b_specs + [tile] * ne + [_ANY] * nx,
        out_specs=o_specs + [_ANY] * nx, out_shape=o_shapes + moved_shapes,
        scratch_shapes=[pltpu.VMEM((tm, tn), F32)] + (_exchange_scratch(nx) if nx else []),
        compiler_params=_params(("arbitrary",) * 3 if nx else ("parallel", "parallel", "arbitrary")), name=name,
    )(*a_list, *b_list, *extras, *moved)
    result = res[0] if (epilogue is None) else list(res[:no])
    return (result, list(res[no:])) if nx else result


def _row_bytes(shape, dtype):
    dims = list(shape[1:])
    dims[-1] = -(-dims[-1] // LANES) * LANES
    return math.prod(dims) * jnp.dtype(dtype).itemsize


def _row_tile(n, row_bytes):
    t = 1024
    while t > 16 and (n % t or t * row_bytes > ROW_BLOCK_BYTES):
        t //= 2
    assert n % t == 0, (n, t)
    return t


def rowmap(fn, tiled, bcast, out_tiled, out_acc, name):
    n = tiled[0].shape[0]
    tile = _row_tile(n, sum(_row_bytes(t.shape, t.dtype) for t in list(tiled) + list(out_tiled)))
    n_in, n_out = len(tiled) + len(bcast), len(out_tiled)

    def body(*refs):
        outs, accs = fn(*[r[...] for r in refs[:n_in]])
        assert len(outs) == n_out and len(accs) == len(out_acc), name
        for r, v in zip(refs[n_in:n_in + n_out], outs):
            r[...] = v.astype(r.dtype)
        acc_refs = refs[n_in + n_out:]
        if acc_refs:
            @pl.when(pl.program_id(0) == 0)
            def _():
                for r, v in zip(acc_refs, accs):
                    r[...] = v.astype(r.dtype)

            @pl.when(pl.program_id(0) != 0)
            def _():
                for r, v in zip(acc_refs, accs):
                    r[...] += v.astype(r.dtype)

    def tspec(s):
        nd = len(s.shape)
        return pl.BlockSpec((tile,) + tuple(s.shape[1:]), lambda i, nd=nd: (i,) + (0,) * (nd - 1))

    def bspec(s):
        nd = len(s.shape)
        return pl.BlockSpec(tuple(s.shape), lambda i, nd=nd: (0,) * nd)

    res = pl.pallas_call(
        body, grid=(n // tile,),
        in_specs=[tspec(t) for t in tiled] + [bspec(t) for t in bcast],
        out_specs=[tspec(t) for t in out_tiled] + [bspec(t) for t in out_acc],
        out_shape=list(out_tiled) + list(out_acc),
        compiler_params=_params(("arbitrary",)), name=name,
    )(*tiled, *bcast)
    return list(res[:n_out]), list(res[n_out:])


def rowmap_fwd(fwd, tiled, bcast, out_dtypes, name):
    shapes = jax.eval_shape(fwd, *tiled, *bcast)
    out_tiled = [jax.ShapeDtypeStruct(s.shape, d) for s, d in zip(shapes, out_dtypes)]
    outs, _ = rowmap(lambda *blk: (fwd(*[b.astype(F32) for b in blk]), ()), tiled, bcast, out_tiled, [], name)
    return outs


def rowmap_bwd(fwd, tiled, bcast, cts, want, name):
    cts = [[] if c is None else (list(c) if isinstance(c, (list, tuple)) else [c]) for c in cts]
    flat_cts = [c for group in cts for c in group]
    nt_, nc_ = len(tiled), len(flat_cts)

    def fn(*blk):
        ins = [b.astype(F32) for b in blk[:nt_]] + [b.astype(F32) for b in blk[nt_ + nc_:]]
        ctb = list(blk[nt_:nt_ + nc_])
        outs, vjp = jax.vjp(fwd, *ins)
        full = []
        for o, group in zip(outs, cts):
            acc = jnp.zeros_like(o)
            for _ in group:
                acc = acc + ctb.pop(0).astype(F32)
            full.append(acc)
        g = vjp(tuple(full))
        return [g[i] for i in range(nt_) if want[i] is not None], list(g[nt_:])

    out_tiled = [jax.ShapeDtypeStruct(t.shape, w) for t, w in zip(tiled, want) if w is not None]
    out_acc = [jax.ShapeDtypeStruct(b.shape, F32) for b in bcast]
    return rowmap(fn, list(tiled) + flat_cts, bcast, out_tiled, out_acc, name)


def _rms(x, gain):
    return x * lax.rsqrt(jnp.mean(jnp.square(x), axis=-1, keepdims=True) + RMS_EPS) * gain


def _sigmoid(x):
    return 1.0 / (1.0 + jnp.exp(-x))


def _softplus(x):
    return jnp.maximum(x, 0.0) + jnp.log(1.0 + jnp.exp(-jnp.abs(x)))


def st_norm(x, gain):
    return x, _rms(x, gain)


def st_res_norm(x, delta, gain):
    y = x + delta
    return y, _rms(y, gain)


def st_qk_norm(q, k, q_gain, k_gain):
    return _rms(q, q_gain), _rms(k, k_gain)


def st_merge(o0, o1, o2, l0, l1, l2):
    m = jnp.maximum(jnp.maximum(l0, l1), l2)
    e0, e1, e2 = jnp.exp(l0 - m), jnp.exp(l1 - m), jnp.exp(l2 - m)
    return ((e0 * o0 + e1 * o1 + e2 * o2) / (e0 + e1 + e2),)


def _st_rwkv_pre(dot, zr, zk, zv, xw, xa, xg, pr, pk, pv, pw, pa, pg, mr, mk, mv, mw, ma, mg,
                 w0, w_decay, a0, w_aaa, w_gate, k_k, k_a):
    def shift(cur, prev, mix):
        return cur + mix * (prev - cur)

    r, k, v = shift(zr, pr, mr), shift(zk, pk, mk), shift(zv, pv, mv)
    xw, xa, xg = shift(xw, pw, mw), shift(xa, pa, ma), shift(xg, pg, mg)
    w = -_softplus(-(w0 + dot(jnp.tanh(xw), w_decay, 'nn', False))) - 0.5
    a = _sigmoid(a0 + dot(xa, w_aaa, 'nn', False))
    g = dot(_sigmoid(xg), w_gate, 'nn', False)
    log_decay = -jnp.exp(w)
    return r, log_decay, k * (1.0 + (a - 1.0) * k_a), v, k * k_k, a, g


def _head_sum(x):
    joins = _pair_block_mask(2 * RWKV_HEAD).astype(F32)
    tile = joins.shape[0]
    return jnp.concatenate([_dot(x[:, lo:lo + tile], joins, 'nn', True) for lo in range(0, x.shape[1], tile)], axis=1)


@jax.custom_vjp
def _head_sum_ad(x):
    return _head_sum(x)


_head_sum_ad.defvjp(lambda x: (_head_sum(x), None), lambda _, g: (_head_sum(g),))


def _st_rwkv_kk(head_sum, kk0, a):
    kk = kk0 / jnp.maximum(jnp.sqrt(head_sum(jnp.square(kk0))), 1e-12)
    return -kk, kk * a


def _st_rwkv_post(head_sum, y, r, k, v, g, gn_w, gn_b, r_k):
    mu = head_sum(y) * (1.0 / RWKV_HEAD)
    var = head_sum(jnp.square(y - mu)) * (1.0 / RWKV_HEAD)
    out = (y - mu) * lax.rsqrt(var + GN_EPS) * gn_w + gn_b
    out = out + head_sum(r * k * r_k) * v
    return (out * g,)


def st_gate(g0, g1, attn_d, rwkv_d):
    return (_sigmoid(g0) * attn_d + _sigmoid(g1) * rwkv_d,)


def st_relu2(u):
    return (jnp.square(jnp.maximum(u, 0.0)),)


def st_add(a, b):
    return (a + b,)


def loss_head(x2, pg, pp, target, name):
    d_model = x2.shape[1]

    def fn(x2, pg, pp, tgt):
        s = _sigmoid(pg)
        err = x2 + s * pp - tgt
        dy = err * (1.0 / d_model)
        part = 0.5 * jnp.sum(jnp.square(err)) * (1.0 / d_model)
        return [dy, dy * pp * s * (1.0 - s), dy * s], [jnp.full((8, LANES), part, F32)]

    sds = jax.ShapeDtypeStruct
    outs, accs = rowmap(fn, [x2, pg, pp, target], [],
                        [sds(x2.shape, F32), sds(x2.shape, BF16), sds(x2.shape, BF16)], [sds((8, LANES), F32)], name)
    return outs[0], outs[1], outs[2], accs[0][0, 0]


def _attn_block(dot, q, kp, kc, vp, vc, bp, bc, prev_offset):
    blk = q.shape[0]
    qi = lax.broadcasted_iota(I32, (blk, blk), 0)
    ki = lax.broadcasted_iota(I32, (blk, blk), 1)
    mask_c = ki <= qi
    mask_p = ki >= qi + prev_offset
    scale = HEAD_DIM ** -0.5
    s_c = jnp.where(mask_c, dot(q, kc, 'nt', False) * scale + bc, NEG_INF)
    s_p = jnp.where(mask_p, dot(q, kp, 'nt', False) * scale + bp, NEG_INF)
    m = lax.stop_gradient(jnp.maximum(jnp.max(s_c, axis=1, keepdims=True), jnp.max(s_p, axis=1, keepdims=True)))
    e_c = jnp.where(mask_c, jnp.exp(s_c - m), 0.0)
    e_p = jnp.where(mask_p, jnp.exp(s_p - m), 0.0)
    l = jnp.sum(e_c, axis=1, keepdims=True) + jnp.sum(e_p, axis=1, keepdims=True)
    o = (dot(e_c, vc, 'nn', False) + dot(e_p, vp, 'nn', False)) / l
    return o, jnp.broadcast_to(m + jnp.log(l), o.shape)


class _ClassView:
    def __init__(self, tokens, dilation, first_col):
        self.view = tokens.reshape(tokens.shape[0] // dilation, dilation * tokens.shape[1])
        self.tiles, self.first = tokens.shape[1] // HEAD_DIM, first_col // HEAD_DIM

    def spec(self, shift, nb):
        tiles, first = self.tiles, self.first
        return pl.BlockSpec((ATTN_BLOCK, HEAD_DIM),
                            lambda h, r, n: (jnp.clip(n + shift, 0, nb - 1), r * tiles + first + h))


_BIAS_SPEC = pl.BlockSpec((None, ATTN_BLOCK, ATTN_BLOCK), lambda h, r, n: (h, 0, 0))


def attn_fwd(q, k, v, bp, bc, dilation, name):
    t_len = q[0].shape[0]
    nb = t_len // dilation // ATTN_BLOCK
    qv, kv, vv = (_ClassView(arr, dilation, col) for arr, col in (q, k, v))
    o_shape = jax.ShapeDtypeStruct((t_len // dilation, dilation * ATTN_OUT), F32)
    o_spec = pl.BlockSpec((ATTN_BLOCK, HEAD_DIM), lambda h, r, n: (n, r * HEADS_PER_GROUP + h))

    def body(q_ref, kp_ref, kc_ref, vp_ref, vc_ref, bp_ref, bc_ref, o_ref, l_ref):
        off = jnp.where(pl.program_id(2) > 0, 0, ATTN_BLOCK)
        o, l = _attn_block(_dot, q_ref[...], kp_ref[...], kc_ref[...], vp_ref[...], vc_ref[...], bp_ref[...],
                           bc_ref[...], off)
        o_ref[...] = o
        l_ref[...] = l

    o, l = pl.pallas_call(
        body, grid=(HEADS_PER_GROUP, dilation, nb),
        in_specs=[qv.spec(0, nb), kv.spec(-1, nb), kv.spec(0, nb), vv.spec(-1, nb), vv.spec(0, nb),
                  _BIAS_SPEC, _BIAS_SPEC],
        out_specs=[o_spec, o_spec], out_shape=[o_shape] * 2,
        compiler_params=_params(("parallel", "parallel", "parallel")), name=name,
    )(qv.view, kv.view, kv.view, vv.view, vv.view, bp, bc)
    return o.reshape(t_len, ATTN_OUT), l.reshape(t_len, ATTN_OUT)


def attn_bwd(q, k, v, bp, bc, do, dl, dilation, name):
    t_len = q[0].shape[0]
    blk = ATTN_BLOCK
    nb = t_len // dilation // blk
    qv, kv, vv = (_ClassView(arr, dilation, col) for arr, col in (q, k, v))
    dov, dlv = _ClassView(do, dilation, 0), _ClassView(dl, dilation, 0)
    cur, bias = dov.spec(0, nb), _BIAS_SPEC

    def body(q_ref, kp_ref, kc_ref, vp_ref, vc_ref, bp_ref, bc_ref, do_ref, dl_ref,
             dq_ref, dkp_ref, dkc_ref, dvp_ref, dvc_ref, dbp_ref, dbc_ref):
        off = jnp.where(pl.program_id(2) > 0, 0, blk)
        f = functools.partial(_attn_block, _dot_ad, prev_offset=off)
        _, vjp = jax.vjp(f, q_ref[...], kp_ref[...], kc_ref[...], vp_ref[...], vc_ref[...], bp_ref[...], bc_ref[...])
        dq, dkp, dkc, dvp, dvc, dbp, dbc = vjp((do_ref[...], dl_ref[...]))
        dq_ref[...] = dq
        dkp_ref[...] = dkp
        dkc_ref[...] = dkc
        dvp_ref[...] = dvp
        dvc_ref[...] = dvc
        first = jnp.logical_and(pl.program_id(1) == 0, pl.program_id(2) == 0)

        @pl.when(first)
        def _():
            dbp_ref[...] = dbp
            dbc_ref[...] = dbc

        @pl.when(jnp.logical_not(first))
        def _():
            dbp_ref[...] += dbp
            dbc_ref[...] += dbc

    blocks = jax.ShapeDtypeStruct(dov.view.shape, F32)
    grid = (HEADS_PER_GROUP, dilation, nb)
    dq, dkp, dkc, dvp, dvc, dbp, dbc = pl.pallas_call(
        body, grid=grid,
        in_specs=[qv.spec(0, nb), kv.spec(-1, nb), kv.spec(0, nb), vv.spec(-1, nb), vv.spec(0, nb), bias, bias,
                  cur, dlv.spec(0, nb)],
        out_specs=[cur] * 5 + [bias] * 2, out_shape=[blocks] * 5 + [jax.ShapeDtypeStruct(bp.shape, F32)] * 2,
        compiler_params=_params(("arbitrary", "arbitrary", "arbitrary")), name=name,
    )(qv.view, kv.view, kv.view, vv.view, vv.view, bp, bc, dov.view, dlv.view)

    nxt = dov.spec(1, nb)

    def add_body(kc_ref, kp_ref, vc_ref, vp_ref, dk_ref, dv_ref):
        has_next = (pl.program_id(2) + 1 < nb).astype(F32)
        dk_ref[...] = kc_ref[...] + kp_ref[...] * has_next
        dv_ref[...] = vc_ref[...] + vp_ref[...] * has_next

    dk, dv = pl.pallas_call(
        add_body, grid=grid, in_specs=[cur, nxt, cur, nxt], out_specs=[cur, cur], out_shape=[blocks] * 2,
        compiler_params=_params(("parallel", "parallel", "parallel")), name=name + "_kv",
    )(dkc, dkp, dvc, dvp)
    return [t.reshape(t_len, ATTN_OUT) for t in (dq, dk, dv)] + [dbp, dbc]


def _t5_bucket(dist):
    max_exact = N_BUCKETS // 2
    d_f = jnp.maximum(dist, 1).astype(F32)
    large = max_exact + (jnp.log(d_f / max_exact) / math.log(MAX_DISTANCE / max_exact)
                         * (N_BUCKETS - max_exact)).astype(I32)
    large = jnp.minimum(large, N_BUCKETS - 1)
    return jnp.where(dist < max_exact, dist, large)


def _bucket_tables():
    blk = ATTN_BLOCK
    qi = jnp.arange(blk)[:, None]
    ki = jnp.arange(blk)[None, :]
    out = []
    for _, dilation in DILATED_GROUPS:
        rel_p = jnp.maximum(blk + qi - ki, 0) * dilation
        rel_c = jnp.maximum(qi - ki, 0) * dilation
        out.append(jnp.stack([_t5_bucket(rel_p), _t5_bucket(rel_c)]))
    return jnp.stack(out).astype(I32)


def bias_fwd(table, buckets, name):
    blk = ATTN_BLOCK

    def body(tab_ref, bkt_ref, out_ref):
        for g in range(len(DILATED_GROUPS)):
            for half in range(2):
                bk = bkt_ref[g, half]
                for hh in range(HEADS_PER_GROUP):
                    h = g * HEADS_PER_GROUP + hh
                    acc = jnp.zeros((blk, blk), F32)
                    for b in range(N_BUCKETS):
                        acc = jnp.where(bk == b, tab_ref[b, h], acc)
                    out_ref[h, half] = acc

    return pl.pallas_call(
        body, in_specs=[pl.BlockSpec(memory_space=pltpu.SMEM), pl.BlockSpec(memory_space=pltpu.VMEM)],
        out_specs=pl.BlockSpec(memory_space=pltpu.VMEM),
        out_shape=jax.ShapeDtypeStruct((N_HEADS, 2, blk, blk), F32), name=name,
    )(table, buckets)


def bias_bwd(dbias, buckets, name):
    def body(db_ref, bkt_ref, out_ref):
        rows = lax.broadcasted_iota(I32, (N_BUCKETS, LANES), 0)
        cols = lax.broadcasted_iota(I32, (N_BUCKETS, LANES), 1)
        acc = jnp.zeros((N_BUCKETS, LANES), F32)
        for g in range(len(DILATED_GROUPS)):
            bk_p, bk_c = bkt_ref[g, 0], bkt_ref[g, 1]
            for hh in range(HEADS_PER_GROUP):
                h = g * HEADS_PER_GROUP + hh
                d_p, d_c = db_ref[h, 0], db_ref[h, 1]
                for b in range(N_BUCKETS):
                    s = jnp.sum(jnp.where(bk_p == b, d_p, 0.0)) + jnp.sum(jnp.where(bk_c == b, d_c, 0.0))
                    acc = jnp.where(jnp.logical_and(rows == b, cols == h), s, acc)
        out_ref[...] = acc

    return pl.pallas_call(
        body, in_specs=[pl.BlockSpec(memory_space=pltpu.VMEM)] * 2, out_specs=pl.BlockSpec(memory_space=pltpu.VMEM),
        out_shape=jax.ShapeDtypeStruct((N_BUCKETS, LANES), F32), name=name,
    )(dbias, buckets)


def _rwkv_chunk(dot, s0, r, lw, k, v, a, b):
    def each(f, *lists):
        return [f(*xs) for xs in zip(*lists)]

    def mm(mode):
        return lambda p, q: dot(p, q, mode, True)

    def mul(p, q):
        return p * q

    def add(p, q):
        return p + q

    c, width = r[0].shape
    lane_a = (lax.broadcasted_iota(I32, (1, width), 1) < width // 2).astype(F32)
    time_a = (lax.broadcasted_iota(I32, (1, 2 * c), 1) < c).astype(F32)
    ti = lax.broadcasted_iota(I32, (c, 2 * c), 0)
    si = lax.broadcasted_iota(I32, (c, 2 * c), 1)
    si = jnp.where(si < c, si, si - c)
    incl, strict = si <= ti, si < ti
    ones_incl = (lax.broadcasted_iota(I32, (c, c), 1) <= lax.broadcasted_iota(I32, (c, c), 0)).astype(F32)
    same_head = _pair_block_mask(width)

    def by_head(x):
        return jnp.concatenate([x * lane_a, x * (1.0 - lane_a)], axis=0)

    def by_block(p):
        return jnp.concatenate([p * time_a, p * (1.0 - time_a)], axis=0)

    cum = each(lambda x: dot(ones_incl, x, 'nn', True), lw)
    w_incl = each(jnp.exp, cum)
    w_prev = each(lambda cu, x: jnp.exp(cu - x), cum, lw)
    w_inv = each(lambda cu: jnp.exp(-cu), cum)
    w_end = each(lambda x: jnp.exp(jnp.sum(x, axis=0, keepdims=True)), lw)
    a_t, r_t, b_t, k_t = each(mul, a, w_prev), each(mul, r, w_incl), each(mul, b, w_inv), each(mul, k, w_inv)
    b_h, k_h, v_h = each(by_head, b_t), each(by_head, k_t), each(by_head, v)
    l_ab = each(lambda p, q: jnp.where(strict, dot(p, q, 'nt', True), 0.0), a_t, b_h)
    l_ak = each(lambda p, q: jnp.where(strict, dot(p, q, 'nt', True), 0.0), a_t, k_h)
    u = each(add, each(mm('nt'), a_t, s0), each(mm('nn'), l_ak, v_h))
    u = each(add, u, each(mm('nn'), l_ab, each(by_head, u)))
    power = l_ab
    for _ in range(int(math.log2(c)) - 1):
        power = each(mm('nn'), power, each(by_block, power))
        u = each(add, u, each(mm('nn'), power, each(by_head, u)))
    m_rb = each(lambda p, q: jnp.where(incl, dot(p, q, 'nt', True), 0.0), r_t, b_h)
    m_rk = each(lambda p, q: jnp.where(incl, dot(p, q, 'nt', True), 0.0), r_t, k_h)
    y = each(add, each(add, each(mm('nt'), r_t, s0), each(mm('nn'), m_rb, each(by_head, u))), each(mm('nn'), m_rk, v_h))
    outer = each(lambda uu, vv, bb, kk: dot(jnp.concatenate([uu, vv], axis=0), jnp.concatenate([bb, kk], axis=0),
                                             'tn', True), u, v, b_t, k_t)
    s1 = each(lambda s, o, we: (s + jnp.where(same_head, o, 0.0)) * we, s0, outer, w_end)
    return y, s1


def _pair_block_mask(width):
    rows = lax.broadcasted_iota(I32, (width, width), 0) < width // 2
    cols = lax.broadcasted_iota(I32, (width, width), 1) < width // 2
    return rows == cols


def _rwkv_tiling(t, width):
    pair = 2 * RWKV_HEAD
    n_pairs = width // pair
    assert n_pairs * pair == width and t % RWKV_CHUNK == 0
    return pair, n_pairs, math.gcd(n_pairs, RWKV_PAIRS_PER_STEP), t // RWKV_CHUNK


def rwkv_fwd(r, lw, k, v, a, b, name, gather=()):
    t, width = r.shape
    pair, n_pairs, ps, nc = _rwkv_tiling(t, width)
    c = RWKV_CHUNK
    steps = n_pairs // ps * nc
    ng = len(gather)
    row = pl.BlockSpec((c, ps * pair), lambda i, j: (j, i))
    cols = [slice(q * pair, (q + 1) * pair) for q in range(ps)]

    def body(*refs):
        ins, x_refs = refs[:6], refs[6:6 + ng]
        y_ref, s0_ref = refs[6 + ng:8 + ng]
        out_refs, state, sems = refs[8 + ng:8 + 2 * ng], refs[8 + 2 * ng], refs[9 + 2 * ng:]
        step = pl.program_id(0) * nc + pl.program_id(1)
        if ng:
            start, forward, finish = _gather_schedule(x_refs, out_refs, *sems)
            pl.when(step == 0)(start)
            pl.when(step == steps - 1 - steps // 8)(forward)

        @pl.when(pl.program_id(1) == 0)
        def _():
            state[...] = jnp.zeros_like(state)

        s0 = [state[q] for q in range(ps)]
        y, s1 = _rwkv_chunk(_dot, s0, *[[ref[:, cs] for cs in cols] for ref in ins])
        for q in range(ps):
            s0_ref[q] = s0[q]
            y_ref[:, cols[q]] = y[q]
            state[q] = s1[q]
        if ng:
            pl.when(step == steps - 1)(finish)

    return pl.pallas_call(
        body, grid=(n_pairs // ps, nc), in_specs=[row] * 6 + [_ANY] * ng,
        out_specs=[row, pl.BlockSpec((ps, None, pair, pair), lambda i, j: (i, j, 0, 0))] + [_ANY] * ng,
        out_shape=[jax.ShapeDtypeStruct((t, width), F32), jax.ShapeDtypeStruct((n_pairs, nc, pair, pair), F32)]
        + [jax.ShapeDtypeStruct((N_DEV,) + g.shape, g.dtype) for g in gather],
        scratch_shapes=[pltpu.VMEM((ps, pair, pair), F32)] + (_exchange_scratch(ng) if ng else []),
        compiler_params=_params(("arbitrary", "arbitrary")), name=name,
    )(r, lw, k, v, a, b, *gather)


def rwkv_bwd(r, lw, k, v, a, b, s0, dy, name, scatter=()):
    t, width = r.shape
    pair, n_pairs, ps, nc = _rwkv_tiling(t, width)
    c = RWKV_CHUNK
    steps = n_pairs // ps * nc
    ns = len(scatter)
    row = pl.BlockSpec((c, ps * pair), lambda i, j: (nc - 1 - j, i))
    st = pl.BlockSpec((ps, None, pair, pair), lambda i, j: (i, nc - 1 - j, 0, 0))
    cols = [slice(q * pair, (q + 1) * pair) for q in range(ps)]

    def body(*refs):
        ins, s0_ref, dy_ref, send_refs = refs[:6], refs[6], refs[7], refs[8:8 + ns]
        grad_refs, recv_refs = refs[8 + ns:14 + ns], refs[14 + ns:14 + 2 * ns]
        dstate, sems = refs[14 + 2 * ns], refs[15 + 2 * ns:]
        step = pl.program_id(0) * nc + pl.program_id(1)
        if ns:
            start, finish = _scatter_schedule(send_refs, recv_refs, *sems)
            pl.when(step == 0)(start)

        @pl.when(pl.program_id(1) == 0)
        def _():
            dstate[...] = jnp.zeros_like(dstate)

        pairs = range(ps)
        _, vjp = jax.vjp(functools.partial(_rwkv_chunk, _dot_ad), [s0_ref[q] for q in pairs],
                         *[[ref[:, cs] for cs in cols] for ref in ins])
        grads = vjp(([dy_ref[:, cs] for cs in cols], [dstate[q] for q in pairs]))
        same_head = _pair_block_mask(pair)
        for q in pairs:
            dstate[q] = jnp.where(same_head, grads[0][q], 0.0)
            for ref, g in zip(grad_refs, grads[1:]):
                ref[:, cols[q]] = g[q]
        if ns:
            pl.when(step == steps - 1)(finish)

    return pl.pallas_call(
        body, grid=(n_pairs // ps, nc), in_specs=[row] * 6 + [st, row] + [_ANY] * ns,
        out_specs=[row] * 6 + [_ANY] * ns,
        out_shape=[jax.ShapeDtypeStruct((t, width), F32)] * 6 + [jax.ShapeDtypeStruct(s.shape, s.dtype) for s in scatter],
        scratch_shapes=[pltpu.VMEM((ps, pair, pair), F32)] + (_exchange_scratch(ns) if ns else []),
        compiler_params=_params(("arbitrary", "arbitrary")), name=name,
    )(r, lw, k, v, a, b, s0, dy, *scatter)


_ANY = pl.BlockSpec(memory_space=pl.ANY)


def _exchange_scratch(n_arrays):
    return [pltpu.SemaphoreType.DMA((n_arrays, N_DEV - 1)), pltpu.SemaphoreType.DMA((n_arrays, N_DEV - 1)),
            pltpu.SemaphoreType.DMA((n_arrays,))]


def _gather_schedule(x_refs, out_refs, send_sems, recv_sems, local_sems):
    x, y, c = lax.axis_index("x"), lax.axis_index("y"), lax.axis_index("c")
    me, sibling = (x, y, c), (x, y, 1 - c)
    chips = [(1 - x, y), (x, 1 - y), (1 - x, 1 - y)]
    arrays = range(len(x_refs))

    def slot(a, pos):
        return out_refs[a].at[4 * pos[0] + 2 * pos[1] + pos[2]]

    def copy(a, i, block, to, src=None):
        return pltpu.make_async_remote_copy(
            src_ref=slot(a, block) if src is None else src, dst_ref=slot(a, block), send_sem=send_sems.at[a, i],
            recv_sem=recv_sems.at[a, i], device_id=to, device_id_type=pl.DeviceIdType.MESH)

    mine = [pltpu.make_async_copy(x_refs[a], slot(a, me), local_sems.at[a]) for a in arrays]
    first = [[copy(a, 0, me, sibling, src=x_refs[a])]
             + [copy(a, 1 + j, me, (*chip, c), src=x_refs[a]) for j, chip in enumerate(chips)] for a in arrays]
    passed = [[copy(a, 4 + j, (*chip, c), sibling) for j, chip in enumerate(chips)] for a in arrays]

    def start():
        for a in arrays:
            mine[a].start()
            for cp in first[a]:
                cp.start()

    def forward():
        for j, chip in enumerate(chips):
            for a in arrays:
                copy(a, 1 + j, (*chip, c), me).wait_recv()
                passed[a][j].start()

    def finish():
        for a in arrays:
            copy(a, 0, sibling, me).wait_recv()
            for j, chip in enumerate(chips):
                copy(a, 4 + j, (*chip, 1 - c), me).wait_recv()
            for cp in first[a] + passed[a]:
                cp.wait_send()
            mine[a].wait()

    return start, forward, finish


def _scatter_schedule(in_refs, out_refs, send_sems, recv_sems, local_sems):
    x, y, c = lax.axis_index("x"), lax.axis_index("y"), lax.axis_index("c")
    my_chip = 2 * x + y
    mine, remote = [], []
    for a, (src, dst) in enumerate(zip(in_refs, out_refs)):
        mine.append(pltpu.make_async_copy(src.at[my_chip], dst.at[my_chip], local_sems.at[a]))
        for i in range(1, N_CHIPS):
            px, py = x ^ (i >> 1), y ^ (i & 1)
            remote.append(pltpu.make_async_remote_copy(
                src_ref=src.at[2 * px + py], dst_ref=dst.at[my_chip], send_sem=send_sems.at[a, i - 1],
                recv_sem=recv_sems.at[a, i - 1], device_id=(px, py, c), device_id_type=pl.DeviceIdType.MESH))

    def start():
        for cp in mine + remote:
            cp.start()

    def finish():
        for cp in remote:
            cp.wait_recv()
        for cp in remote:
            cp.wait_send()
        for cp in mine:
            cp.wait()

    return start, finish


def pair_exchange(parts, name):
    n = len(parts)

    def body(*refs):
        x, y, c = lax.axis_index("x"), lax.axis_index("y"), lax.axis_index("c")
        send_sems, recv_sems = refs[2 * n:]
        copies = [pltpu.make_async_remote_copy(
            src_ref=refs[a].at[q, 1 - c], dst_ref=refs[n + a].at[q], send_sem=send_sems.at[a, q],
            recv_sem=recv_sems.at[a, q], device_id=(x, y, 1 - c), device_id_type=pl.DeviceIdType.MESH)
            for a in range(n) for q in range(N_CHIPS)]
        for cp in copies:
            cp.start()
        for cp in copies:
            cp.wait_recv()
        for cp in copies:
            cp.wait_send()

    return pl.pallas_call(
        body, in_specs=[_ANY] * n, out_specs=[_ANY] * n,
        out_shape=[jax.ShapeDtypeStruct((N_CHIPS,) + s.shape[2:], s.dtype) for s in parts],
        scratch_shapes=[pltpu.SemaphoreType.DMA((n, N_CHIPS)), pltpu.SemaphoreType.DMA((n, N_CHIPS))], name=name,
    )(*parts)


def pair_add(mine, theirs, core, name):
    _, _, k, n = mine.shape
    tc = _pick(n, (2048, 1024, 512))
    tr = _row_tile(k, tc * 3 * jnp.dtype(mine.dtype).itemsize)

    def body(core_ref, a_ref, b_ref, o_ref):
        o_ref[...] = (a_ref[...].astype(F32) + b_ref[...].astype(F32)).astype(o_ref.dtype)

    one = pl.BlockSpec((None, tr, tc), lambda q, i, j, core_ref: (q, i, j))
    grid_spec = pltpu.PrefetchScalarGridSpec(
        num_scalar_prefetch=1, grid=(N_CHIPS, k // tr, n // tc),
        in_specs=[pl.BlockSpec((None, None, tr, tc), lambda q, i, j, core_ref: (q, core_ref[0], i, j)), one],
        out_specs=one)
    return pl.pallas_call(
        body, grid_spec=grid_spec, out_shape=jax.ShapeDtypeStruct(theirs.shape, BF16),
        compiler_params=_params(("parallel", "parallel", "parallel")), name=name,
    )(core, mine, theirs)


def all_gather_many(shards, name):
    n = len(shards)

    def body(*refs):
        start, forward, finish = _gather_schedule(refs[:n], refs[n:2 * n], *refs[2 * n:])
        start()
        forward()
        finish()

    return pl.pallas_call(
        body, in_specs=[_ANY] * n, out_specs=[_ANY] * n,
        out_shape=[jax.ShapeDtypeStruct((N_DEV,) + s.shape, s.dtype) for s in shards],
        scratch_shapes=_exchange_scratch(n), name=name,
    )(*shards)


def project_and_gather(h, w_all, shards):
    return matmul(h, w_all, 'nn', F32, "proj_in", exchange=('gather', shards))


def input_grad_and_scatter(d_parts, w_all, offsets, parts):
    return matmul(d_parts, [w_all] * len(d_parts), 'nt', F32, "d_h_in", b_col_offsets=offsets,
                  exchange=('scatter', parts))


def scan_and_gather(scan_in, shards):
    y, s0, *gathered = rwkv_fwd(*scan_in, name="rwkv_scan", gather=shards)
    return y, s0, gathered


def scan_bwd_and_scatter(scan_in, s0, dy, parts):
    res = rwkv_bwd(*scan_in, s0, dy, name="rwkv_scan_bwd", scatter=parts)
    return res[:6], res[6:]


def adamw_shard(parts, w, m, v, name):
    _, k, n = w.shape
    slots = parts.shape[0]
    tc = _pick(n, (2048, 1024, 512))
    tr = _row_tile(k, tc * (slots * jnp.dtype(parts.dtype).itemsize + 7 * 4))

    def body(p_ref, w_ref, m_ref, v_ref, g_ref, d_ref, nm_ref, nv_ref):
        _adamw_block(p_ref, w_ref, m_ref, v_ref, g_ref, d_ref, nm_ref, nv_ref)

    one = pl.BlockSpec((None, tr, tc), lambda i, j: (0, i, j))
    return pl.pallas_call(
        body, grid=(k // tr, n // tc), in_specs=[pl.BlockSpec((slots, tr, tc), lambda i, j: (0, i, j))] + [one] * 3,
        out_specs=[one] * 4, out_shape=[jax.ShapeDtypeStruct(w.shape, F32)] * 4,
        compiler_params=_params(("parallel", "parallel")), name=name,
    )(parts, w, m, v)


def _adamw_block(p_ref, w_ref, m_ref, v_ref, g_ref, d_ref, nm_ref, nv_ref):
    g = p_ref[0].astype(F32)
    for j in range(1, p_ref.shape[0]):
        g = g + p_ref[j].astype(F32)
    new_m = ADAM_B1 * m_ref[...] + (1.0 - ADAM_B1) * g
    new_v = ADAM_B2 * v_ref[...] + (1.0 - ADAM_B2) * jnp.square(g)
    m_hat = new_m / (1.0 - ADAM_B1 ** ADAM_STEP)
    v_hat = new_v / (1.0 - ADAM_B2 ** ADAM_STEP)
    g_ref[...] = g
    d_ref[...] = -ADAM_LR * (m_hat / (jnp.sqrt(v_hat) + ADAM_EPS) + ADAM_WD * w_ref[...])
    nm_ref[...] = new_m
    nv_ref[...] = new_v


def adamw(parts, w, m, v, name):
    rows = w.shape[0]
    tile = _row_tile(rows, N_DEV * LANES * jnp.dtype(parts.dtype).itemsize + 7 * LANES * 4)

    def body(p_ref, w_ref, m_ref, v_ref, g_ref, d_ref, nm_ref, nv_ref):
        _adamw_block(p_ref, w_ref, m_ref, v_ref, g_ref, d_ref, nm_ref, nv_ref)

    flat = pl.BlockSpec((tile, LANES), lambda i: (i, 0))
    return pl.pallas_call(
        body, grid=(rows // tile,), in_specs=[pl.BlockSpec((N_DEV, tile, LANES), lambda i: (0, i, 0))] + [flat] * 3,
        out_specs=[flat] * 4, out_shape=[jax.ShapeDtypeStruct(w.shape, F32)] * 4,
        compiler_params=_params(("parallel",)), name=name,
    )(parts, w, m, v)


def _part_rows(n_elems):
    return -(-n_elems // (PACK_ROWS * LANES)) * PACK_ROWS


def _pack(arrays, dtype, lead=()):
    parts, layout, off = [], [], 0
    for arr in arrays:
        n = math.prod(arr.shape[len(lead):])
        rows = _part_rows(n)
        flat = arr.reshape(lead + (n,)).astype(dtype)
        flat = jnp.pad(flat, [(0, 0)] * len(lead) + [(0, rows * LANES - n)])
        parts.append(flat.reshape(lead + (rows, LANES)))
        layout.append((off, rows))
        off += rows
    total = -(-off // 1024) * 1024
    if total > off:
        parts.append(jnp.zeros(lead + (total - off, LANES), dtype))
    return jnp.concatenate(parts, axis=len(lead)), layout


def _unpack(buf, layout, shapes, lead=()):
    out = []
    for (off, rows), shape in zip(layout, shapes):
        n = math.prod(shape)
        piece = lax.slice_in_dim(buf, off, off + rows, axis=len(lead))
        out.append(piece.reshape(lead + (rows * LANES,))[..., :n].reshape(lead + tuple(shape)))
    return out


def _split_shards(full, axis):
    if axis == 0:
        return full.reshape((N_DEV, full.shape[0] // N_DEV, full.shape[1]))
    return full.reshape((full.shape[0], N_DEV, full.shape[1] // N_DEV)).transpose(1, 0, 2)


def _join_shards(shards, axis):
    if axis == 0:
        return shards.reshape((-1, shards.shape[2]))
    return shards.transpose(1, 0, 2).reshape((shards.shape[1], -1))


def _shift_down(t):
    return jnp.pad(t, ((1, 0), (0, 0)))[:-1]


def _shift_up(t):
    return jnp.pad(t, ((0, 1), (0, 0)))[1:]


def kernel(x, p, norm_mix, w_in, q_gain, k_gain, rel_bias, w_attn_up, shift_mix, w0, w_decay_up, a0, w_aaa_up, w_gate_up, k_k, k_a, r_k, gn_w, gn_b, w_rwkv_up, w_out, norm_mlp, w_mlp_in, w_mlp_out, norm_ple, w_ple_gate, w_ple_proj, loss_target, m_norm_mix, m_w_in, m_q_gain, m_k_gain, m_rel_bias, m_w_attn_up, m_shift_mix, m_w0, m_w_decay_up, m_a0, m_w_aaa_up, m_w_gate_up, m_k_k, m_k_a, m_r_k, m_gn_w, m_gn_b, m_w_rwkv_up, m_w_out, m_norm_mlp, m_w_mlp_in, m_w_mlp_out, m_norm_ple, m_w_ple_gate, m_w_ple_proj, v_norm_mix, v_w_in, v_q_gain, v_k_gain, v_rel_bias, v_w_attn_up, v_shift_mix, v_w0, v_w_decay_up, v_a0, v_w_aaa_up, v_w_gate_up, v_k_k, v_k_a, v_r_k, v_gn_w, v_gn_b, v_w_rwkv_up, v_w_out, v_norm_mlp, v_w_mlp_in, v_w_mlp_out, v_norm_ple, v_w_ple_gate, v_w_ple_proj):
    given = dict(locals())
    xs = x[0]
    t_len, d_model = xs.shape
    target = loss_target[0]
    p_bf = p[0, 0].astype(BF16)
    rw_width = w0.shape[1]
    n_rheads = rw_width // RWKV_HEAD
    lora_d, lora_a, lora_g = w_decay_up.shape[1], w_aaa_up.shape[1], w_gate_up.shape[1]
    z_width = shift_mix.shape[1]
    z_pad = -(-z_width // LANES) * LANES
    qkv_width = 3 * ATTN_WIDTH
    assert z_width == 3 * rw_width + lora_d + lora_a + lora_g
    assert N_DEV * w_in.shape[2] == qkv_width + z_width + 2 * d_model
    for window, dilation in DILATED_GROUPS:
        assert window // dilation == ATTN_BLOCK and t_len % (dilation * ATTN_BLOCK) == 0

    shard_bf = {n: given[n][0].astype(BF16) for n in BIG}
    early = ['w_in', 'w_decay_up', 'w_aaa_up', 'w_gate_up']
    during_proj = ['w_mlp_out']
    during_scan = [n for n in BIG if n not in early + during_proj]
    late = during_proj + during_scan
    full = {n: _join_shards(g, 1) for n, g in zip(early, all_gather_many([shard_bf[n] for n in early], "gather_w_in"))}
    z_end = qkv_width + z_width
    w_all = jnp.concatenate([full['w_in'][:, :z_end], jnp.zeros((d_model, z_pad - z_width), BF16),
                             full['w_in'][:, z_end:]], axis=1)
    gates_at = qkv_width + z_pad

    (h_in,) = rowmap_fwd(lambda a, g: st_norm(a, g)[1:], [xs], [norm_mix], [BF16], "norm_in")
    proj, proj_gathered = project_and_gather(h_in, w_all, [shard_bf[n] for n in during_proj])
    qkv, z = proj[:, :qkv_width], proj[:, qkv_width:gates_at]
    gate_a, gate_r = proj[:, gates_at:gates_at + d_model], proj[:, gates_at + d_model:]

    q_raw = qkv[:, :ATTN_WIDTH].reshape(t_len * N_HEADS, HEAD_DIM)
    k_raw = qkv[:, ATTN_WIDTH:2 * ATTN_WIDTH].reshape(t_len * N_HEADS, HEAD_DIM)
    q_n, k_n = rowmap_fwd(st_qk_norm, [q_raw, k_raw], [q_gain, k_gain], [F32, F32], "qk_norm")
    q_n, k_n = q_n.reshape(t_len, ATTN_WIDTH), k_n.reshape(t_len, ATTN_WIDTH)
    buckets = _bucket_tables()
    bias = bias_fwd(rel_bias, buckets, "attn_bias")

    att_in, att_o, att_l = [], [], []
    for g, (_, dilation) in enumerate(DILATED_GROUPS):
        hs = slice(g * HEADS_PER_GROUP, (g + 1) * HEADS_PER_GROUP)
        cols = slice(g * ATTN_OUT, (g + 1) * ATTN_OUT)
        v_g = qkv[:, 2 * ATTN_WIDTH + g * ATTN_OUT:2 * ATTN_WIDTH + (g + 1) * ATTN_OUT]
        ops = ((q_n[:, cols], 0), (k_n[:, cols], 0), (v_g, 0), bias[hs, 0], bias[hs, 1])
        o_g, l_g = attn_fwd(*ops, dilation, name=f"attn_fwd_{g}")
        att_in.append(ops)
        att_o.append(o_g)
        att_l.append(l_g)
    (attn,) = rowmap_fwd(st_merge, att_o + att_l, [], [BF16], "attn_merge")

    c0 = rw_width
    cuts = [0, c0, 2 * c0, 3 * c0, 3 * c0 + lora_d, 3 * c0 + lora_d + lora_a, z_width]
    z_parts = [z[:, lo:hi] for lo, hi in zip(cuts[:-1], cuts[1:])]
    z_prev = [_shift_down(t) for t in z_parts]
    mixes = [shift_mix[:, lo:hi] for lo, hi in zip(cuts[:-1], cuts[1:])]
    pre_params = mixes + [w0, full['w_decay_up'], a0, full['w_aaa_up'], full['w_gate_up'], k_k, k_a]
    pre_out = rowmap_fwd(functools.partial(_st_rwkv_pre, _dot), z_parts + z_prev, pre_params, [F32] * 7, "rwkv_pre")
    r_s, lw_s, k_s, v_s, kk0_s, a_s, g_s = pre_out

    aa_s, bb_s = rowmap_fwd(functools.partial(_st_rwkv_kk, _head_sum), [kk0_s, a_s], [], [F32, F32], "rwkv_kk")
    scan_in = [r_s, lw_s, k_s, v_s, aa_s, bb_s]
    y_t, s0_h, scan_gathered = scan_and_gather(scan_in, [shard_bf[n] for n in during_scan])
    late_gathered = list(proj_gathered) + list(scan_gathered)
    wt = {n: g if SHARD_AXIS[n] == 1 else g.reshape(-1, g.shape[2]) for n, g in zip(late, late_gathered)}
    post_params = [gn_w, gn_b, r_k.reshape(1, rw_width)]
    post_in = [y_t, r_s, k_s, v_s, g_s]
    (rw,) = rowmap_fwd(functools.partial(_st_rwkv_post, _head_sum), post_in, post_params, [BF16], "rwkv_post")
    attn_d = matmul(attn, wt['w_attn_up'], 'nn', F32, "attn_up", b_shards=True)
    rwkv_d = matmul(rw, wt['w_rwkv_up'], 'nn', F32, "rwkv_up", b_shards=True)

    (merged,) = rowmap_fwd(st_gate, [gate_a, gate_r, attn_d, rwkv_d], [], [BF16], "gate_merge")
    mix_out = matmul(merged, wt['w_out'], 'nn', F32, "out_proj")
    x1, h_mlp = rowmap_fwd(st_res_norm, [xs, mix_out], [norm_mlp], [F32, BF16], "res_norm_mlp")
    u, act = matmul(h_mlp, wt['w_mlp_in'], 'nn', [F32, BF16], "mlp_in", b_shards=True,
                    epilogue=lambda acc: (acc,) + st_relu2(acc))
    mlp_out = matmul(act, wt['w_mlp_out'], 'nn', F32, "mlp_out")
    x2, h_ple = rowmap_fwd(st_res_norm, [x1, mlp_out], [norm_ple], [F32, BF16], "res_norm_ple")
    pg = matmul(h_ple, wt['w_ple_gate'], 'nn', F32, "ple_gate")
    pp = matmul(p_bf, wt['w_ple_proj'], 'nn', F32, "ple_proj", b_shards=True)
    dy, d_pg, d_pp, loss_local = loss_head(x2, pg, pp, target, "loss_head")

    def row_cut(full_grad):
        return full_grad.reshape(N_DEV, full_grad.shape[0] // N_DEV, full_grad.shape[1])

    grads, sends = {}, {}
    sends['w_ple_gate'] = row_cut(matmul(h_ple, d_pg, 'tn', BF16, "d_w_ple_gate"))
    sends['w_ple_proj'] = matmul(p_bf, d_pp, 'tn', BF16, "d_w_ple_proj", out_shards=True)
    d_h_ple = matmul(d_pg, wt['w_ple_gate'], 'nt', F32, "d_h_ple")
    (d_x2, d_x2_bf), (grads['norm_ple'],) = rowmap_bwd(
        st_res_norm, [x1, mlp_out], [norm_ple], [dy, d_h_ple], [F32, BF16], "res_norm_ple_bwd")
    sends['w_mlp_out'] = row_cut(matmul(act, d_x2_bf, 'tn', BF16, "d_w_mlp_out"))
    (d_u,) = matmul(d_x2_bf, wt['w_mlp_out'], 'nt', [BF16], "d_mlp_act", extras=[u],
                    epilogue=lambda d_act, u_blk: (d_act * (2.0 * jnp.maximum(u_blk, 0.0)),))
    sends['w_mlp_in'] = matmul(h_mlp, d_u, 'tn', BF16, "d_w_mlp_in", out_shards=True)
    d_h_mlp = matmul(d_u, wt['w_mlp_in'], 'nt', F32, "d_h_mlp", b_shards=True)
    (d_x1, d_x1_bf), (grads['norm_mlp'],) = rowmap_bwd(
        st_res_norm, [xs, mix_out], [norm_mlp], [d_x2, d_h_mlp], [F32, BF16], "res_norm_mlp_bwd")

    sends['w_out'] = row_cut(matmul(merged, d_x1_bf, 'tn', BF16, "d_w_out"))
    d_merged = matmul(d_x1_bf, wt['w_out'], 'nt', F32, "d_merged")
    (d_gate_a, d_gate_r, d_attn_d, d_rwkv_d), _ = rowmap_bwd(
        st_gate, [gate_a, gate_r, attn_d, rwkv_d], [], [d_merged], [BF16] * 4, "gate_merge_bwd")
    sends['w_attn_up'] = matmul(attn, d_attn_d, 'tn', BF16, "d_w_attn_up", out_shards=True)
    sends['w_rwkv_up'] = matmul(rw, d_rwkv_d, 'tn', BF16, "d_w_rwkv_up", out_shards=True)
    d_attn = matmul(d_attn_d, wt['w_attn_up'], 'nt', F32, "d_attn", b_shards=True)
    d_rw = matmul(d_rwkv_d, wt['w_rwkv_up'], 'nt', F32, "d_rw", b_shards=True)

    (d_y, d_r1, d_k1, d_v1, d_g), (grads['gn_w'], grads['gn_b'], grads['r_k']) = rowmap_bwd(
        functools.partial(_st_rwkv_post, _head_sum_ad), post_in, post_params, [d_rw], [F32] * 5, "rwkv_post_bwd")
    core = lax.axis_index("c").astype(I32).reshape(1)

    def chip_partials(names, tag):
        mine = [sends[n].reshape((N_CHIPS, 2) + sends[n].shape[1:]) for n in names]
        theirs = pair_exchange(mine, "pair_grads_" + tag)
        return [pair_add(m, t, core, "pair_add_" + n) for n, m, t in zip(names, mine, theirs)]

    scan_grads, late_received = scan_bwd_and_scatter(scan_in, s0_h, d_y, chip_partials(late, "late"))
    received = dict(zip(late, late_received))
    d_r2, d_lw, d_k2, d_v2, d_aa, d_bb = scan_grads
    (d_kk0, d_a), _ = rowmap_bwd(functools.partial(_st_rwkv_kk, _head_sum_ad), [kk0_s, a_s], [], [d_aa, d_bb],
                                 [F32, F32], "rwkv_kk_bwd")
    pre_cts = [[d_r1, d_r2], d_lw, [d_k1, d_k2], [d_v1, d_v2], d_kk0, d_a, d_g]
    d_zp, d_pre = rowmap_bwd(functools.partial(_st_rwkv_pre, _dot_ad), z_parts + z_prev, pre_params, pre_cts,
                             [F32] * 12, "rwkv_pre_bwd")
    grads['shift_mix'] = jnp.concatenate(d_pre[:6], axis=1)
    grads['w0'], d_w_decay, grads['a0'], d_w_aaa, d_w_gate, grads['k_k'], grads['k_a'] = d_pre[6:]
    for name, full_grad in (('w_decay_up', d_w_decay), ('w_aaa_up', d_w_aaa), ('w_gate_up', d_w_gate)):
        sends[name] = _split_shards(full_grad, 1).astype(BF16)
    z_fill = [jnp.zeros((t_len, z_pad - z_width), F32)] if z_pad > z_width else []
    d_z_cur = jnp.concatenate(d_zp[:6] + z_fill, axis=1)
    d_z_prev = _shift_up(jnp.concatenate(d_zp[6:] + z_fill, axis=1))
    (d_z,) = rowmap_fwd(st_add, [d_z_cur, d_z_prev], [], [BF16], "d_z_sum")

    d_merge, _ = rowmap_bwd(st_merge, att_o + att_l, [], [d_attn], [F32] * 6, "attn_merge_bwd")
    d_qn, d_kn, d_vs, d_bias = [], [], [], []
    for g, (_, dilation) in enumerate(DILATED_GROUPS):
        dq, dk, dv, dbp, dbc = attn_bwd(*att_in[g], d_merge[g], d_merge[3 + g], dilation, name=f"attn_bwd_{g}")
        d_qn.append(dq)
        d_kn.append(dk)
        d_vs.append(dv)
        d_bias.append(jnp.stack([dbp, dbc], axis=1))
    d_table = bias_bwd(jnp.concatenate(d_bias, axis=0), buckets, "attn_bias_bwd")
    grads['rel_bias'] = d_table[:, :N_HEADS]
    d_qn = jnp.concatenate(d_qn, axis=1).reshape(t_len * N_HEADS, HEAD_DIM)
    d_kn = jnp.concatenate(d_kn, axis=1).reshape(t_len * N_HEADS, HEAD_DIM)
    (d_q, d_k), (grads['q_gain'], grads['k_gain']) = rowmap_bwd(
        st_qk_norm, [q_raw, k_raw], [q_gain, k_gain], [d_qn, d_kn], [BF16, BF16], "qk_norm_bwd")

    d_pieces = ([d_q.reshape(t_len, ATTN_WIDTH), d_k.reshape(t_len, ATTN_WIDTH)] + [t.astype(BF16) for t in d_vs]
                + [d_z, d_gate_a, d_gate_r])
    piece_cols = [0, ATTN_WIDTH] + [2 * ATTN_WIDTH + g * ATTN_OUT for g in range(len(DILATED_GROUPS))]
    piece_cols += [qkv_width, gates_at, gates_at + d_model]
    d_w_pieces = [matmul(h_in, piece, 'tn', BF16, f"d_w_in_{i}") for i, piece in enumerate(d_pieces)]
    d_w_pieces[-3] = d_w_pieces[-3][:, :z_width]
    sends['w_in'] = _split_shards(jnp.concatenate(d_w_pieces, axis=1), 1)
    d_h_in, early_received = input_grad_and_scatter(d_pieces, w_all, piece_cols, chip_partials(early, "early"))
    received.update(zip(early, early_received))
    (grad_x,), (grads['norm_mix'],) = rowmap_bwd(st_norm, [xs], [norm_mix], [d_x1, d_h_in], [F32], "norm_in_bwd")

    by_name = {n: adamw_shard(received[n], given[n], given['m_' + n], given['v_' + n], "adamw_" + n) for n in BIG}
    pk = lambda prefix: _pack([given[prefix + n] for n in SMALL], F32)[0]
    small_buf, small_layout = _pack([grads[n].reshape(given[n].shape) for n in SMALL], F32)
    (small_all,) = all_gather_many([small_buf], "gather_small_grads")
    small_out = adamw(small_all, pk(''), pk('m_'), pk('v_'), "adamw_replicated")
    small_shapes = [given[n].shape for n in SMALL]
    by_name.update(zip(SMALL, zip(*[_unpack(buf, small_layout, small_shapes) for buf in small_out])))
    loss = lax.psum(loss_local, MESH_AXES)
    return (loss, grad_x[None], *[by_name[n][0] for n in WEIGHTS], *[by_name[n][1] for n in WEIGHTS],
            *[by_name[n][2] for n in WEIGHTS], *[by_name[n][3] for n in WEIGHTS])
```

```python
import functools
import math

import jax
import jax.numpy as jnp
from jax import lax
from jax.experimental import pallas as pl
from jax.experimental.pallas import tpu as pltpu

F32 = jnp.float32
BF16 = jnp.bfloat16
I32 = jnp.int32

N_DEV = 8
N_CHIPS = 4
MESH_AXES = ("x", "y", "c")
LANES = 128
PACK_ROWS = 16
VMEM_LIMIT_BYTES = 48 * 2**20
ROW_BLOCK_BYTES = 6 * 2**20

HEAD_DIM = 128
ATTN_BLOCK = 128
HEADS_PER_GROUP = 4
DILATED_GROUPS = ((128, 1), (512, 4), (2048, 16))
N_HEADS = HEADS_PER_GROUP * len(DILATED_GROUPS)
ATTN_WIDTH = N_HEADS * HEAD_DIM
ATTN_OUT = HEADS_PER_GROUP * HEAD_DIM
N_BUCKETS = 32
MAX_DISTANCE = 2048
RWKV_HEAD = 64
RWKV_CHUNK = 64
RWKV_PAIRS_PER_STEP = 8
RMS_EPS = 1e-6
GN_EPS = 64e-5
NEG_INF = -1e30

ADAM_LR = 0.001
ADAM_B1 = 0.9
ADAM_B2 = 0.999
ADAM_EPS = 1e-08
ADAM_WD = 0.01
ADAM_STEP = 10

WEIGHTS = ['norm_mix', 'w_in', 'q_gain', 'k_gain', 'rel_bias', 'w_attn_up', 'shift_mix', 'w0', 'w_decay_up', 'a0',
           'w_aaa_up', 'w_gate_up', 'k_k', 'k_a', 'r_k', 'gn_w', 'gn_b', 'w_rwkv_up', 'w_out', 'norm_mlp', 'w_mlp_in',
           'w_mlp_out', 'norm_ple', 'w_ple_gate', 'w_ple_proj']
SHARD_AXIS = {'w_in': 1, 'w_attn_up': 1, 'w_decay_up': 1, 'w_aaa_up': 1, 'w_gate_up': 1, 'w_rwkv_up': 1, 'w_out': 0,
              'w_mlp_in': 1, 'w_mlp_out': 0, 'w_ple_gate': 0, 'w_ple_proj': 1}
BIG = [n for n in WEIGHTS if n in SHARD_AXIS]
SMALL = [n for n in WEIGHTS if n not in SHARD_AXIS]


def _params(sem):
    return pltpu.CompilerParams(dimension_semantics=sem, vmem_limit_bytes=VMEM_LIMIT_BYTES)


_DN = {'nn': (((1,), (0,)), ((), ())), 'nt': (((1,), (1,)), ((), ())), 'tn': (((0,), (0,)), ((), ()))}


def _dot(a, b, mode, exact):
    if exact:
        return lax.dot_general(a, b, _DN[mode], precision=lax.Precision.HIGH, preferred_element_type=F32)
    return lax.dot_general(a.astype(BF16), b.astype(BF16), _DN[mode], preferred_element_type=F32)


@functools.partial(jax.custom_vjp, nondiff_argnums=(2, 3))
def _dot_ad(a, b, mode, exact):
    return _dot(a, b, mode, exact)


def _dot_ad_fwd(a, b, mode, exact):
    return _dot(a, b, mode, exact), (a, b)


def _dot_ad_bwd(mode, exact, res, g):
    a, b = res
    if mode == 'nn':
        return _dot(g, b, 'nt', exact), _dot(a, g, 'tn', exact)
    if mode == 'nt':
        return _dot(g, b, 'nn', exact), _dot(g, a, 'tn', exact)
    return _dot(b, g, 'nt', exact), _dot(a, g, 'nn', exact)


_dot_ad.defvjp(_dot_ad_fwd, _dot_ad_bwd)


def _pick(n, cands):
    for c in cands:
        if n % c == 0:
            return c
    return n


def matmul(a, b, mode, out_dtype, name, b_shards=False, out_shards=False, b_col_offsets=None, epilogue=None,
           extras=(), exchange=None):
    a_list = list(a) if isinstance(a, (list, tuple)) else [a]
    b_list = list(b) if isinstance(b, (list, tuple)) else [b]
    seg = len(a_list)
    assert all(t.dtype == BF16 for t in a_list + b_list), name
    assert seg == 1 or (mode == 'nt' and not b_shards), name
    m = a_list[0].shape[1] if mode == 'tn' else a_list[0].shape[0]
    ks = [t.shape[0] if mode == 'tn' else t.shape[1] for t in a_list]
    b0 = b_list[0]
    b_rows, b_cols = (b0.shape[1], N_DEV * b0.shape[2]) if b_shards else b0.shape
    n = b_rows if mode == 'nt' else b_cols
    if seg == 1 and b_col_offsets is None:
        assert (b_cols if mode == 'nt' else b_rows) == ks[0], (name, a_list[0].shape, b0.shape)
    offsets = list(b_col_offsets) if b_col_offsets is not None else [0] * seg
    tm = _pick(m, (1024, 512, 256, 128))
    tn = _pick(n // N_DEV if (out_shards or (b_shards and mode != 'nt')) else n, (1024, 512, 256, 128))
    k_units = [kk // N_DEV if (b_shards and mode == 'nt') else kk for kk in ks] + [o for o in offsets if o]
    tk = next((c for c in (2048, 1024, 512, 256, 128) if all(u % c == 0 for u in k_units)), k_units[0])
    nks = [kk // tk for kk in ks]
    starts = [sum(nks[:s]) for s in range(seg)]
    nk = sum(nks)
    grid = (m // tm, n // tn, nk)
    total_steps = grid[0] * grid[1] * nk
    kind, moved = exchange if exchange is not None else (None, [])
    nx, ne = len(moved), len(extras)
    out_dtypes = list(out_dtype) if isinstance(out_dtype, (list, tuple)) else [out_dtype]
    no = len(out_dtypes)

    def body(*refs):
        a_refs, b_refs, x_refs = refs[:seg], refs[seg:2 * seg], refs[2 * seg:2 * seg + ne]
        pos = 2 * seg + ne
        moved_in, o_refs = refs[pos:pos + nx], refs[pos + nx:pos + nx + no]
        pos += nx + no
        moved_out, acc_ref, sems = refs[pos:pos + nx], refs[pos + nx], refs[pos + nx + 1:]
        kk = pl.program_id(2)
        step = (pl.program_id(0) * grid[1] + pl.program_id(1)) * nk + kk
        if kind == 'gather':
            start, forward, finish = _gather_schedule(moved_in, moved_out, *sems)
            pl.when(step == 0)(start)
            pl.when(step == total_steps - 1 - total_steps // 8)(forward)
        elif kind == 'scatter':
            start, finish = _scatter_schedule(moved_in, moved_out, *sems)
            pl.when(step == 0)(start)

        @pl.when(kk == 0)
        def _():
            acc_ref[...] = jnp.zeros_like(acc_ref)

        for s in range(seg):
            def accumulate(s=s):
                acc_ref[...] += lax.dot_general(a_refs[s][...], b_refs[s][...], _DN[mode], preferred_element_type=F32)

            if seg == 1:
                accumulate()
            else:
                pl.when(jnp.logical_and(kk >= starts[s], kk < starts[s] + nks[s]))(accumulate)

        @pl.when(kk == nk - 1)
        def _():
            acc = acc_ref[...]
            outs = (acc,) if epilogue is None else epilogue(acc, *[x[...] for x in x_refs])
            for r, v in zip(o_refs, outs):
                r[...] = v.astype(r.dtype)

        if kind is not None:
            pl.when(step == total_steps - 1)(finish)

    def k_of(kk, s):
        return kk if seg == 1 else jnp.clip(kk - starts[s], 0, nks[s] - 1)

    a_specs, b_specs = [], []
    for s in range(seg):
        off = offsets[s] // tk
        if mode == 'tn':
            a_specs.append(pl.BlockSpec((tk, tm), lambda i, j, kk, s=s: (k_of(kk, s), i)))
        else:
            a_specs.append(pl.BlockSpec((tm, tk), lambda i, j, kk, s=s: (i, k_of(kk, s))))
        if mode == 'nt':
            if b_shards:
                per = b0.shape[2] // tk
                b_specs.append(pl.BlockSpec((None, tn, tk), lambda i, j, kk: (kk // per, j, kk % per)))
            else:
                b_specs.append(pl.BlockSpec((tn, tk), lambda i, j, kk, s=s, off=off: (j, off + k_of(kk, s))))
        else:
            if b_shards:
                per = b0.shape[2] // tn
                b_specs.append(pl.BlockSpec((None, tk, tn), lambda i, j, kk: (j // per, kk, j % per)))
            else:
                b_specs.append(pl.BlockSpec((tk, tn), lambda i, j, kk: (kk, j)))
    tile = pl.BlockSpec((tm, tn), lambda i, j, kk: (i, j))
    if out_shards:
        assert epilogue is None
        per_o = n // N_DEV // tn
        o_specs = [pl.BlockSpec((None, tm, tn), lambda i, j, kk: (j // per_o, i, j % per_o))]
        o_shapes = [jax.ShapeDtypeStruct((N_DEV, m, n // N_DEV), out_dtypes[0])]
    else:
        o_specs = [tile] * no
        o_shapes = [jax.ShapeDtypeStruct((m, n), d) for d in out_dtypes]
    moved_shapes = [jax.ShapeDtypeStruct(((N_DEV,) + t.shape) if kind == 'gather' else t.shape, t.dtype) for t in moved]
    res = pl.pallas_call(
        body, grid=grid, in_specs=a_specs + b_specs + [tile] * ne + [_ANY] * nx,
        out_specs=o_specs + [_ANY] * nx, out_shape=o_shapes + moved_shapes,
        scratch_shapes=[pltpu.VMEM((tm, tn), F32)] + (_exchange_scratch(nx) if nx else []),
        compiler_params=_params(("arbitrary",) * 3 if nx else ("parallel", "parallel", "arbitrary")), name=name,
    )(*a_list, *b_list, *extras, *moved)
    result = res[0] if (epilogue is None) else list(res[:no])
    return (result, list(res[no:])) if nx else result


def _row_bytes(shape, dtype):
    dims = list(shape[1:])
    dims[-1] = -(-dims[-1] // LANES) * LANES
    return math.prod(dims) * jnp.dtype(dtype).itemsize


def _row_tile(n, row_bytes):
    t = 1024
    while t > 16 and (n % t or t * row_bytes > ROW_BLOCK_BYTES):
        t //= 2
    assert n % t == 0, (n, t)
    return t


def rowmap(fn, tiled, bcast, out_tiled, out_acc, name):
    n = tiled[0].shape[0]
    tile = _row_tile(n, sum(_row_bytes(t.shape, t.dtype) for t in list(tiled) + list(out_tiled)))
    n_in, n_out = len(tiled) + len(bcast), len(out_tiled)

    def body(*refs):
        outs, accs = fn(*[r[...] for r in refs[:n_in]])
        assert len(outs) == n_out and len(accs) == len(out_acc), name
        for r, v in zip(refs[n_in:n_in + n_out], outs):
            r[...] = v.astype(r.dtype)
        acc_refs = refs[n_in + n_out:]
        if acc_refs:
            @pl.when(pl.program_id(0) == 0)
            def _():
                for r, v in zip(acc_refs, accs):
                    r[...] = v.astype(r.dtype)

            @pl.when(pl.program_id(0) != 0)
            def _():
                for r, v in zip(acc_refs, accs):
                    r[...] += v.astype(r.dtype)

    def tspec(s):
        nd = len(s.shape)
        return pl.BlockSpec((tile,) + tuple(s.shape[1:]), lambda i, nd=nd: (i,) + (0,) * (nd - 1))

    def bspec(s):
        nd = len(s.shape)
        return pl.BlockSpec(tuple(s.shape), lambda i, nd=nd: (0,) * nd)

    res = pl.pallas_call(
        body, grid=(n // tile,),
        in_specs=[tspec(t) for t in tiled] + [bspec(t) for t in bcast],
        out_specs=[tspec(t) for t in out_tiled] + [bspec(t) for t in out_acc],
        out_shape=list(out_tiled) + list(out_acc),
        compiler_params=_params(("arbitrary",)), name=name,
    )(*tiled, *bcast)
    return list(res[:n_out]), list(res[n_out:])


def rowmap_fwd(fwd, tiled, bcast, out_dtypes, name):
    shapes = jax.eval_shape(fwd, *tiled, *bcast)
    out_tiled = [jax.ShapeDtypeStruct(s.shape, d) for s, d in zip(shapes, out_dtypes)]
    outs, _ = rowmap(lambda *blk: (fwd(*[b.astype(F32) for b in blk]), ()), tiled, bcast, out_tiled, [], name)
    return outs


def rowmap_bwd(fwd, tiled, bcast, cts, want, name):
    cts = [[] if c is None else (list(c) if isinstance(c, (list, tuple)) else [c]) for c in cts]
    flat_cts = [c for group in cts for c in group]
    nt_, nc_ = len(tiled), len(flat_cts)

    def fn(*blk):
        ins = [b.astype(F32) for b in blk[:nt_]] + [b.astype(F32) for b in blk[nt_ + nc_:]]
        ctb = list(blk[nt_:nt_ + nc_])
        outs, vjp = jax.vjp(fwd, *ins)
        full = []
        for o, group in zip(outs, cts):
            acc = jnp.zeros_like(o)
            for _ in group:
                acc = acc + ctb.pop(0).astype(F32)
            full.append(acc)
        g = vjp(tuple(full))
        return [g[i] for i in range(nt_) if want[i] is not None], list(g[nt_:])

    out_tiled = [jax.ShapeDtypeStruct(t.shape, w) for t, w in zip(tiled, want) if w is not None]
    out_acc = [jax.ShapeDtypeStruct(b.shape, F32) for b in bcast]
    return rowmap(fn, list(tiled) + flat_cts, bcast, out_tiled, out_acc, name)


def _rms(x, gain):
    return x * lax.rsqrt(jnp.mean(jnp.square(x), axis=-1, keepdims=True) + RMS_EPS) * gain


def _sigmoid(x):
    return 1.0 / (1.0 + jnp.exp(-x))


def _softplus(x):
    return jnp.maximum(x, 0.0) + jnp.log(1.0 + jnp.exp(-jnp.abs(x)))


def st_norm(x, gain):
    return x, _rms(x, gain)


def st_res_norm(x, delta, gain):
    y = x + delta
    return y, _rms(y, gain)


def st_qk_norm(q, k, q_gain, k_gain):
    return _rms(q, q_gain), _rms(k, k_gain)


def st_merge(o0, o1, o2, l0, l1, l2):
    m = jnp.maximum(jnp.maximum(l0, l1), l2)
    e0, e1, e2 = jnp.exp(l0 - m), jnp.exp(l1 - m), jnp.exp(l2 - m)
    return ((e0 * o0 + e1 * o1 + e2 * o2) / (e0 + e1 + e2),)


def _st_rwkv_pre(dot, zr, zk, zv, xw, xa, xg, pr, pk, pv, pw, pa, pg, mr, mk, mv, mw, ma, mg,
                 w0, w_decay, a0, w_aaa, w_gate, k_k, k_a):
    def shift(cur, prev, mix):
        return cur + mix * (prev - cur)

    r, k, v = shift(zr, pr, mr), shift(zk, pk, mk), shift(zv, pv, mv)
    xw, xa, xg = shift(xw, pw, mw), shift(xa, pa, ma), shift(xg, pg, mg)
    w = -_softplus(-(w0 + dot(jnp.tanh(xw), w_decay, 'nn', False))) - 0.5
    a = _sigmoid(a0 + dot(xa, w_aaa, 'nn', False))
    g = dot(_sigmoid(xg), w_gate, 'nn', False)
    log_decay = -jnp.exp(w)
    return r, log_decay, k * (1.0 + (a - 1.0) * k_a), v, k * k_k, a, g


def _head_sum(x):
    joins = _pair_block_mask(2 * RWKV_HEAD).astype(F32)
    tile = joins.shape[0]
    return jnp.concatenate([_dot(x[:, lo:lo + tile], joins, 'nn', True) for lo in range(0, x.shape[1], tile)], axis=1)


@jax.custom_vjp
def _head_sum_ad(x):
    return _head_sum(x)


_head_sum_ad.defvjp(lambda x: (_head_sum(x), None), lambda _, g: (_head_sum(g),))


def _st_rwkv_kk(head_sum, kk0, a):
    kk = kk0 / jnp.maximum(jnp.sqrt(head_sum(jnp.square(kk0))), 1e-12)
    return -kk, kk * a


def _st_rwkv_post(head_sum, y, r, k, v, g, gn_w, gn_b, r_k):
    mu = head_sum(y) * (1.0 / RWKV_HEAD)
    var = head_sum(jnp.square(y - mu)) * (1.0 / RWKV_HEAD)
    out = (y - mu) * lax.rsqrt(var + GN_EPS) * gn_w + gn_b
    out = out + head_sum(r * k * r_k) * v
    return (out * g,)


def st_gate(g0, g1, attn_d, rwkv_d):
    return (_sigmoid(g0) * attn_d + _sigmoid(g1) * rwkv_d,)


def st_relu2(u):
    return (jnp.square(jnp.maximum(u, 0.0)),)


def st_add(a, b):
    return (a + b,)


def loss_head(x2, pg, pp, target, name):
    d_model = x2.shape[1]

    def fn(x2, pg, pp, tgt):
        s = _sigmoid(pg)
        err = x2 + s * pp - tgt
        dy = err * (1.0 / d_model)
        part = 0.5 * jnp.sum(jnp.square(err)) * (1.0 / d_model)
        return [dy, dy * pp * s * (1.0 - s), dy * s], [jnp.full((8, LANES), part, F32)]

    sds = jax.ShapeDtypeStruct
    outs, accs = rowmap(fn, [x2, pg, pp, target], [],
                        [sds(x2.shape, F32), sds(x2.shape, BF16), sds(x2.shape, BF16)], [sds((8, LANES), F32)], name)
    return outs[0], outs[1], outs[2], accs[0][0, 0]


def _attn_block(dot, q, kp, kc, vp, vc, bp, bc, prev_offset):
    blk = q.shape[0]
    qi = lax.broadcasted_iota(I32, (blk, blk), 0)
    ki = lax.broadcasted_iota(I32, (blk, blk), 1)
    mask_c = ki <= qi
    mask_p = ki >= qi + prev_offset
    scale = HEAD_DIM ** -0.5
    s_c = jnp.where(mask_c, dot(q, kc, 'nt', False) * scale + bc, NEG_INF)
    s_p = jnp.where(mask_p, dot(q, kp, 'nt', False) * scale + bp, NEG_INF)
    m = lax.stop_gradient(jnp.maximum(jnp.max(s_c, axis=1, keepdims=True), jnp.max(s_p, axis=1, keepdims=True)))
    e_c = jnp.where(mask_c, jnp.exp(s_c - m), 0.0)
    e_p = jnp.where(mask_p, jnp.exp(s_p - m), 0.0)
    l = jnp.sum(e_c, axis=1, keepdims=True) + jnp.sum(e_p, axis=1, keepdims=True)
    o = (dot(e_c, vc, 'nn', False) + dot(e_p, vp, 'nn', False)) / l
    return o, jnp.broadcast_to(m + jnp.log(l), o.shape)


class _ClassView:
    def __init__(self, tokens, dilation, first_col):
        self.view = tokens.reshape(tokens.shape[0] // dilation, dilation * tokens.shape[1])
        self.tiles, self.first = tokens.shape[1] // HEAD_DIM, first_col // HEAD_DIM

    def spec(self, shift, nb):
        tiles, first = self.tiles, self.first
        return pl.BlockSpec((ATTN_BLOCK, HEAD_DIM),
                            lambda h, r, n: (jnp.clip(n + shift, 0, nb - 1), r * tiles + first + h))


_BIAS_SPEC = pl.BlockSpec((None, ATTN_BLOCK, ATTN_BLOCK), lambda h, r, n: (h, 0, 0))


def attn_fwd(q, k, v, bp, bc, dilation, name):
    t_len = q[0].shape[0]
    nb = t_len // dilation // ATTN_BLOCK
    qv, kv, vv = (_ClassView(arr, dilation, col) for arr, col in (q, k, v))
    o_shape = jax.ShapeDtypeStruct((t_len // dilation, dilation * ATTN_OUT), F32)
    o_spec = pl.BlockSpec((ATTN_BLOCK, HEAD_DIM), lambda h, r, n: (n, r * HEADS_PER_GROUP + h))

    def body(q_ref, kp_ref, kc_ref, vp_ref, vc_ref, bp_ref, bc_ref, o_ref, l_ref):
        off = jnp.where(pl.program_id(2) > 0, 0, ATTN_BLOCK)
        o, l = _attn_block(_dot, q_ref[...], kp_ref[...], kc_ref[...], vp_ref[...], vc_ref[...], bp_ref[...],
                           bc_ref[...], off)
        o_ref[...] = o
        l_ref[...] = l

    o, l = pl.pallas_call(
        body, grid=(HEADS_PER_GROUP, dilation, nb),
        in_specs=[qv.spec(0, nb), kv.spec(-1, nb), kv.spec(0, nb), vv.spec(-1, nb), vv.spec(0, nb),
                  _BIAS_SPEC, _BIAS_SPEC],
        out_specs=[o_spec, o_spec], out_shape=[o_shape] * 2,
        compiler_params=_params(("parallel", "parallel", "parallel")), name=name,
    )(qv.view, kv.view, kv.view, vv.view, vv.view, bp, bc)
    return o.reshape(t_len, ATTN_OUT), l.reshape(t_len, ATTN_OUT)


def attn_bwd(q, k, v, bp, bc, do, dl, dilation, name):
    t_len = q[0].shape[0]
    blk = ATTN_BLOCK
    nb = t_len // dilation // blk
    qv, kv, vv = (_ClassView(arr, dilation, col) for arr, col in (q, k, v))
    dov, dlv = _ClassView(do, dilation, 0), _ClassView(dl, dilation, 0)
    cur, bias = dov.spec(0, nb), _BIAS_SPEC

    def body(q_ref, kp_ref, kc_ref, vp_ref, vc_ref, bp_ref, bc_ref, do_ref, dl_ref,
             dq_ref, dkp_ref, dkc_ref, dvp_ref, dvc_ref, dbp_ref, dbc_ref):
        off = jnp.where(pl.program_id(2) > 0, 0, blk)
        f = functools.partial(_attn_block, _dot_ad, prev_offset=off)
        _, vjp = jax.vjp(f, q_ref[...], kp_ref[...], kc_ref[...], vp_ref[...], vc_ref[...], bp_ref[...], bc_ref[...])
        dq, dkp, dkc, dvp, dvc, dbp, dbc = vjp((do_ref[...], dl_ref[...]))
        dq_ref[...] = dq
        dkp_ref[...] = dkp
        dkc_ref[...] = dkc
        dvp_ref[...] = dvp
        dvc_ref[...] = dvc
        first = jnp.logical_and(pl.program_id(1) == 0, pl.program_id(2) == 0)

        @pl.when(first)
        def _():
            dbp_ref[...] = dbp
            dbc_ref[...] = dbc

        @pl.when(jnp.logical_not(first))
        def _():
            dbp_ref[...] += dbp
            dbc_ref[...] += dbc

    blocks = jax.ShapeDtypeStruct(dov.view.shape, F32)
    grid = (HEADS_PER_GROUP, dilation, nb)
    dq, dkp, dkc, dvp, dvc, dbp, dbc = pl.pallas_call(
        body, grid=grid,
        in_specs=[qv.spec(0, nb), kv.spec(-1, nb), kv.spec(0, nb), vv.spec(-1, nb), vv.spec(0, nb), bias, bias,
                  cur, dlv.spec(0, nb)],
        out_specs=[cur] * 5 + [bias] * 2, out_shape=[blocks] * 5 + [jax.ShapeDtypeStruct(bp.shape, F32)] * 2,
        compiler_params=_params(("arbitrary", "arbitrary", "arbitrary")), name=name,
    )(qv.view, kv.view, kv.view, vv.view, vv.view, bp, bc, dov.view, dlv.view)

    nxt = dov.spec(1, nb)

    def add_body(kc_ref, kp_ref, vc_ref, vp_ref, dk_ref, dv_ref):
        has_next = (pl.program_id(2) + 1 < nb).astype(F32)
        dk_ref[...] = kc_ref[...] + kp_ref[...] * has_next
        dv_ref[...] = vc_ref[...] + vp_ref[...] * has_next

    dk, dv = pl.pallas_call(
        add_body, grid=grid, in_specs=[cur, nxt, cur, nxt], out_specs=[cur, cur], out_shape=[blocks] * 2,
        compiler_params=_params(("parallel", "parallel", "parallel")), name=name + "_kv",
    )(dkc, dkp, dvc, dvp)
    return [t.reshape(t_len, ATTN_OUT) for t in (dq, dk, dv)] + [dbp, dbc]


def _t5_bucket(dist):
    max_exact = N_BUCKETS // 2
    d_f = jnp.maximum(dist, 1).astype(F32)
    large = max_exact + (jnp.log(d_f / max_exact) / math.log(MAX_DISTANCE / max_exact)
                         * (N_BUCKETS - max_exact)).astype(I32)
    large = jnp.minimum(large, N_BUCKETS - 1)
    return jnp.where(dist < max_exact, dist, large)


def _bucket_tables():
    blk = ATTN_BLOCK
    qi = jnp.arange(blk)[:, None]
    ki = jnp.arange(blk)[None, :]
    out = []
    for _, dilation in DILATED_GROUPS:
        rel_p = jnp.maximum(blk + qi - ki, 0) * dilation
        rel_c = jnp.maximum(qi - ki, 0) * dilation
        out.append(jnp.stack([_t5_bucket(rel_p), _t5_bucket(rel_c)]))
    return jnp.stack(out).astype(I32)


def bias_fwd(table, buckets, name):
    blk = ATTN_BLOCK

    def body(tab_ref, bkt_ref, out_ref):
        for g in range(len(DILATED_GROUPS)):
            for half in range(2):
                bk = bkt_ref[g, half]
                for hh in range(HEADS_PER_GROUP):
                    h = g * HEADS_PER_GROUP + hh
                    acc = jnp.zeros((blk, blk), F32)
                    for b in range(N_BUCKETS):
                        acc = jnp.where(bk == b, tab_ref[b, h], acc)
                    out_ref[h, half] = acc

    return pl.pallas_call(
        body, in_specs=[pl.BlockSpec(memory_space=pltpu.SMEM), pl.BlockSpec(memory_space=pltpu.VMEM)],
        out_specs=pl.BlockSpec(memory_space=pltpu.VMEM),
        out_shape=jax.ShapeDtypeStruct((N_HEADS, 2, blk, blk), F32), name=name,
    )(table, buckets)


def bias_bwd(dbias, buckets, name):
    def body(db_ref, bkt_ref, out_ref):
        rows = lax.broadcasted_iota(I32, (N_BUCKETS, LANES), 0)
        cols = lax.broadcasted_iota(I32, (N_BUCKETS, LANES), 1)
        acc = jnp.zeros((N_BUCKETS, LANES), F32)
        for g in range(len(DILATED_GROUPS)):
            bk_p, bk_c = bkt_ref[g, 0], bkt_ref[g, 1]
            for hh in range(HEADS_PER_GROUP):
                h = g * HEADS_PER_GROUP + hh
                d_p, d_c = db_ref[h, 0], db_ref[h, 1]
                for b in range(N_BUCKETS):
                    s = jnp.sum(jnp.where(bk_p == b, d_p, 0.0)) + jnp.sum(jnp.where(bk_c == b, d_c, 0.0))
                    acc = jnp.where(jnp.logical_and(rows == b, cols == h), s, acc)
        out_ref[...] = acc

    return pl.pallas_call(
        body, in_specs=[pl.BlockSpec(memory_space=pltpu.VMEM)] * 2, out_specs=pl.BlockSpec(memory_space=pltpu.VMEM),
        out_shape=jax.ShapeDtypeStruct((N_BUCKETS, LANES), F32), name=name,
    )(dbias, buckets)


def _rwkv_chunk(dot, s0, r, lw, k, v, a, b):
    def each(f, *lists):
        return [f(*xs) for xs in zip(*lists)]

    def mm(mode):
        return lambda p, q: dot(p, q, mode, True)

    def mul(p, q):
        return p * q

    def add(p, q):
        return p + q

    c, width = r[0].shape
    lane_a = (lax.broadcasted_iota(I32, (1, width), 1) < width // 2).astype(F32)
    time_a = (lax.broadcasted_iota(I32, (1, 2 * c), 1) < c).astype(F32)
    ti = lax.broadcasted_iota(I32, (c, 2 * c), 0)
    si = lax.broadcasted_iota(I32, (c, 2 * c), 1)
    si = jnp.where(si < c, si, si - c)
    incl, strict = si <= ti, si < ti
    ones_incl = (lax.broadcasted_iota(I32, (c, c), 1) <= lax.broadcasted_iota(I32, (c, c), 0)).astype(F32)
    same_head = _pair_block_mask(width)

    def by_head(x):
        return jnp.concatenate([x * lane_a, x * (1.0 - lane_a)], axis=0)

    def by_block(p):
        return jnp.concatenate([p * time_a, p * (1.0 - time_a)], axis=0)

    cum = each(lambda x: dot(ones_incl, x, 'nn', True), lw)
    w_incl = each(jnp.exp, cum)
    w_prev = each(lambda cu, x: jnp.exp(cu - x), cum, lw)
    w_inv = each(lambda cu: jnp.exp(-cu), cum)
    w_end = each(lambda x: jnp.exp(jnp.sum(x, axis=0, keepdims=True)), lw)
    a_t, r_t, b_t, k_t = each(mul, a, w_prev), each(mul, r, w_incl), each(mul, b, w_inv), each(mul, k, w_inv)
    b_h, k_h, v_h = each(by_head, b_t), each(by_head, k_t), each(by_head, v)
    l_ab = each(lambda p, q: jnp.where(strict, dot(p, q, 'nt', True), 0.0), a_t, b_h)
    l_ak = each(lambda p, q: jnp.where(strict, dot(p, q, 'nt', True), 0.0), a_t, k_h)
    u = each(add, each(mm('nt'), a_t, s0), each(mm('nn'), l_ak, v_h))
    u = each(add, u, each(mm('nn'), l_ab, each(by_head, u)))
    power = l_ab
    for _ in range(int(math.log2(c)) - 1):
        power = each(mm('nn'), power, each(by_block, power))
        u = each(add, u, each(mm('nn'), power, each(by_head, u)))
    m_rb = each(lambda p, q: jnp.where(incl, dot(p, q, 'nt', True), 0.0), r_t, b_h)
    m_rk = each(lambda p, q: jnp.where(incl, dot(p, q, 'nt', True), 0.0), r_t, k_h)
    y = each(add, each(add, each(mm('nt'), r_t, s0), each(mm('nn'), m_rb, each(by_head, u))), each(mm('nn'), m_rk, v_h))
    outer = each(lambda uu, vv, bb, kk: dot(jnp.concatenate([uu, vv], axis=0), jnp.concatenate([bb, kk], axis=0),
                                             'tn', True), u, v, b_t, k_t)
    s1 = each(lambda s, o, we: (s + jnp.where(same_head, o, 0.0)) * we, s0, outer, w_end)
    return y, s1


def _pair_block_mask(width):
    rows = lax.broadcasted_iota(I32, (width, width), 0) < width // 2
    cols = lax.broadcasted_iota(I32, (width, width), 1) < width // 2
    return rows == cols


def _rwkv_tiling(t, width):
    pair = 2 * RWKV_HEAD
    n_pairs = width // pair
    assert n_pairs * pair == width and t % RWKV_CHUNK == 0
    return pair, n_pairs, math.gcd(n_pairs, RWKV_PAIRS_PER_STEP), t // RWKV_CHUNK


def rwkv_fwd(r, lw, k, v, a, b, name, gather=()):
    t, width = r.shape
    pair, n_pairs, ps, nc = _rwkv_tiling(t, width)
    c = RWKV_CHUNK
    steps = n_pairs // ps * nc
    ng = len(gather)
    row = pl.BlockSpec((c, ps * pair), lambda i, j: (j, i))
    cols = [slice(q * pair, (q + 1) * pair) for q in range(ps)]

    def body(*refs):
        ins, x_refs = refs[:6], refs[6:6 + ng]
        y_ref, s0_ref = refs[6 + ng:8 + ng]
        out_refs, state, sems = refs[8 + ng:8 + 2 * ng], refs[8 + 2 * ng], refs[9 + 2 * ng:]
        step = pl.program_id(0) * nc + pl.program_id(1)
        if ng:
            start, forward, finish = _gather_schedule(x_refs, out_refs, *sems)
            pl.when(step == 0)(start)
            pl.when(step == steps - 1 - steps // 8)(forward)

        @pl.when(pl.program_id(1) == 0)
        def _():
            state[...] = jnp.zeros_like(state)

        s0 = [state[q] for q in range(ps)]
        y, s1 = _rwkv_chunk(_dot, s0, *[[ref[:, cs] for cs in cols] for ref in ins])
        for q in range(ps):
            s0_ref[q] = s0[q]
            y_ref[:, cols[q]] = y[q]
            state[q] = s1[q]
        if ng:
            pl.when(step == steps - 1)(finish)

    return pl.pallas_call(
        body, grid=(n_pairs // ps, nc), in_specs=[row] * 6 + [_ANY] * ng,
        out_specs=[row, pl.BlockSpec((ps, None, pair, pair), lambda i, j: (i, j, 0, 0))] + [_ANY] * ng,
        out_shape=[jax.ShapeDtypeStruct((t, width), F32), jax.ShapeDtypeStruct((n_pairs, nc, pair, pair), F32)]
        + [jax.ShapeDtypeStruct((N_DEV,) + g.shape, g.dtype) for g in gather],
        scratch_shapes=[pltpu.VMEM((ps, pair, pair), F32)] + (_exchange_scratch(ng) if ng else []),
        compiler_params=_params(("arbitrary", "arbitrary")), name=name,
    )(r, lw, k, v, a, b, *gather)


def rwkv_bwd(r, lw, k, v, a, b, s0, dy, name, scatter=()):
    t, width = r.shape
    pair, n_pairs, ps, nc = _rwkv_tiling(t, width)
    c = RWKV_CHUNK
    steps = n_pairs // ps * nc
    ns = len(scatter)
    row = pl.BlockSpec((c, ps * pair), lambda i, j: (nc - 1 - j, i))
    st = pl.BlockSpec((ps, None, pair, pair), lambda i, j: (i, nc - 1 - j, 0, 0))
    cols = [slice(q * pair, (q + 1) * pair) for q in range(ps)]

    def body(*refs):
        ins, s0_ref, dy_ref, send_refs = refs[:6], refs[6], refs[7], refs[8:8 + ns]
        grad_refs, recv_refs = refs[8 + ns:14 + ns], refs[14 + ns:14 + 2 * ns]
        dstate, sems = refs[14 + 2 * ns], refs[15 + 2 * ns:]
        step = pl.program_id(0) * nc + pl.program_id(1)
        if ns:
            start, finish = _scatter_schedule(send_refs, recv_refs, *sems)
            pl.when(step == 0)(start)

        @pl.when(pl.program_id(1) == 0)
        def _():
            dstate[...] = jnp.zeros_like(dstate)

        pairs = range(ps)
        _, vjp = jax.vjp(functools.partial(_rwkv_chunk, _dot_ad), [s0_ref[q] for q in pairs],
                         *[[ref[:, cs] for cs in cols] for ref in ins])
        grads = vjp(([dy_ref[:, cs] for cs in cols], [dstate[q] for q in pairs]))
        same_head = _pair_block_mask(pair)
        for q in pairs:
            dstate[q] = jnp.where(same_head, grads[0][q], 0.0)
            for ref, g in zip(grad_refs, grads[1:]):
                ref[:, cols[q]] = g[q]
        if ns:
            pl.when(step == steps - 1)(finish)

    return pl.pallas_call(
        body, grid=(n_pairs // ps, nc), in_specs=[row] * 6 + [st, row] + [_ANY] * ns,
        out_specs=[row] * 6 + [_ANY] * ns,
        out_shape=[jax.ShapeDtypeStruct((t, width), F32)] * 6 + [jax.ShapeDtypeStruct(s.shape, s.dtype) for s in scatter],
        scratch_shapes=[pltpu.VMEM((ps, pair, pair), F32)] + (_exchange_scratch(ns) if ns else []),
        compiler_params=_params(("arbitrary", "arbitrary")), name=name,
    )(r, lw, k, v, a, b, s0, dy, *scatter)


_ANY = pl.BlockSpec(memory_space=pl.ANY)


def _exchange_scratch(n_arrays):
    return [pltpu.SemaphoreType.DMA((n_arrays, N_DEV - 1)), pltpu.SemaphoreType.DMA((n_arrays, N_DEV - 1)),
            pltpu.SemaphoreType.DMA((n_arrays,))]


def _gather_schedule(x_refs, out_refs, send_sems, recv_sems, local_sems):
    x, y, c = lax.axis_index("x"), lax.axis_index("y"), lax.axis_index("c")
    me, sibling = (x, y, c), (x, y, 1 - c)
    chips = [(1 - x, y), (x, 1 - y), (1 - x, 1 - y)]
    arrays = range(len(x_refs))

    def slot(a, pos):
        return out_refs[a].at[4 * pos[0] + 2 * pos[1] + pos[2]]

    def copy(a, i, block, to, src=None):
        return pltpu.make_async_remote_copy(
            src_ref=slot(a, block) if src is None else src, dst_ref=slot(a, block), send_sem=send_sems.at[a, i],
            recv_sem=recv_sems.at[a, i], device_id=to, device_id_type=pl.DeviceIdType.MESH)

    mine = [pltpu.make_async_copy(x_refs[a], slot(a, me), local_sems.at[a]) for a in arrays]
    first = [[copy(a, 0, me, sibling, src=x_refs[a])]
             + [copy(a, 1 + j, me, (*chip, c), src=x_refs[a]) for j, chip in enumerate(chips)] for a in arrays]
    passed = [[copy(a, 4 + j, (*chip, c), sibling) for j, chip in enumerate(chips)] for a in arrays]

    def start():
        for a in arrays:
            mine[a].start()
            for cp in first[a]:
                cp.start()

    def forward():
        for j, chip in enumerate(chips):
            for a in arrays:
                copy(a, 1 + j, (*chip, c), me).wait_recv()
                passed[a][j].start()

    def finish():
        for a in arrays:
            copy(a, 0, sibling, me).wait_recv()
            for j, chip in enumerate(chips):
                copy(a, 4 + j, (*chip, 1 - c), me).wait_recv()
            for cp in first[a] + passed[a]:
                cp.wait_send()
            mine[a].wait()

    return start, forward, finish


def _scatter_schedule(in_refs, out_refs, send_sems, recv_sems, local_sems):
    x, y, c = lax.axis_index("x"), lax.axis_index("y"), lax.axis_index("c")
    my_chip = 2 * x + y
    mine, remote = [], []
    for a, (src, dst) in enumerate(zip(in_refs, out_refs)):
        mine.append(pltpu.make_async_copy(src.at[my_chip], dst.at[my_chip], local_sems.at[a]))
        for i in range(1, N_CHIPS):
            px, py = x ^ (i >> 1), y ^ (i & 1)
            remote.append(pltpu.make_async_remote_copy(
                src_ref=src.at[2 * px + py], dst_ref=dst.at[my_chip], send_sem=send_sems.at[a, i - 1],
                recv_sem=recv_sems.at[a, i - 1], device_id=(px, py, c), device_id_type=pl.DeviceIdType.MESH))

    def start():
        for cp in mine + remote:
            cp.start()

    def finish():
        for cp in remote:
            cp.wait_recv()
        for cp in remote:
            cp.wait_send()
        for cp in mine:
            cp.wait()

    return start, finish


def pair_exchange(parts, name):
    n = len(parts)

    def body(*refs):
        x, y, c = lax.axis_index("x"), lax.axis_index("y"), lax.axis_index("c")
        send_sems, recv_sems = refs[2 * n:]
        copies = [pltpu.make_async_remote_copy(
            src_ref=refs[a].at[q, 1 - c], dst_ref=refs[n + a].at[q], send_sem=send_sems.at[a, q],
            recv_sem=recv_sems.at[a, q], device_id=(x, y, 1 - c), device_id_type=pl.DeviceIdType.MESH)
            for a in range(n) for q in range(N_CHIPS)]
        for cp in copies:
            cp.start()
        for cp in copies:
            cp.wait_recv()
        for cp in copies:
            cp.wait_send()

    return pl.pallas_call(
        body, in_specs=[_ANY] * n, out_specs=[_ANY] * n,
        out_shape=[jax.ShapeDtypeStruct((N_CHIPS,) + s.shape[2:], s.dtype) for s in parts],
        scratch_shapes=[pltpu.SemaphoreType.DMA((n, N_CHIPS)), pltpu.SemaphoreType.DMA((n, N_CHIPS))], name=name,
    )(*parts)


def pair_add(mine, theirs, core, name):
    _, _, k, n = mine.shape
    tc = _pick(n, (2048, 1024, 512))
    tr = _row_tile(k, tc * 3 * jnp.dtype(mine.dtype).itemsize)

    def body(core_ref, a_ref, b_ref, o_ref):
        o_ref[...] = (a_ref[...].astype(F32) + b_ref[...].astype(F32)).astype(o_ref.dtype)

    one = pl.BlockSpec((None, tr, tc), lambda q, i, j, core_ref: (q, i, j))
    grid_spec = pltpu.PrefetchScalarGridSpec(
        num_scalar_prefetch=1, grid=(N_CHIPS, k // tr, n // tc),
        in_specs=[pl.BlockSpec((None, None, tr, tc), lambda q, i, j, core_ref: (q, core_ref[0], i, j)), one],
        out_specs=one)
    return pl.pallas_call(
        body, grid_spec=grid_spec, out_shape=jax.ShapeDtypeStruct(theirs.shape, BF16),
        compiler_params=_params(("parallel", "parallel", "parallel")), name=name,
    )(core, mine, theirs)


def all_gather_many(shards, name):
    n = len(shards)

    def body(*refs):
        start, forward, finish = _gather_schedule(refs[:n], refs[n:2 * n], *refs[2 * n:])
        start()
        forward()
        finish()

    return pl.pallas_call(
        body, in_specs=[_ANY] * n, out_specs=[_ANY] * n,
        out_shape=[jax.ShapeDtypeStruct((N_DEV,) + s.shape, s.dtype) for s in shards],
        scratch_shapes=_exchange_scratch(n), name=name,
    )(*shards)


def project_and_gather(h, w_all, shards):
    return matmul(h, w_all, 'nn', F32, "proj_in", exchange=('gather', shards))


def input_grad_and_scatter(d_parts, w_all, offsets, parts):
    return matmul(d_parts, [w_all] * len(d_parts), 'nt', F32, "d_h_in", b_col_offsets=offsets,
                  exchange=('scatter', parts))


def scan_and_gather(scan_in, shards):
    y, s0, *gathered = rwkv_fwd(*scan_in, name="rwkv_scan", gather=shards)
    return y, s0, gathered


def scan_bwd_and_scatter(scan_in, s0, dy, parts):
    res = rwkv_bwd(*scan_in, s0, dy, name="rwkv_scan_bwd", scatter=parts)
    return res[:6], res[6:]


def adamw_shard(parts, w, m, v, name):
    _, k, n = w.shape
    slots = parts.shape[0]
    tc = _pick(n, (2048, 1024, 512))
    tr = _row_tile(k, tc * (slots * jnp.dtype(parts.dtype).itemsize + 7 * 4))

    def body(p_ref, w_ref, m_ref, v_ref, g_ref, d_ref, nm_ref, nv_ref):
        _adamw_block(p_ref, w_ref, m_ref, v_ref, g_ref, d_ref, nm_ref, nv_ref)

    one = pl.BlockSpec((None, tr, tc), lambda i, j: (0, i, j))
    return pl.pallas_call(
        body, grid=(k // tr, n // tc), in_specs=[pl.BlockSpec((slots, tr, tc), lambda i, j: (0, i, j))] + [one] * 3,
        out_specs=[one] * 4, out_shape=[jax.ShapeDtypeStruct(w.shape, F32)] * 4,
        compiler_params=_params(("parallel", "parallel")), name=name,
    )(parts, w, m, v)


def _adamw_block(p_ref, w_ref, m_ref, v_ref, g_ref, d_ref, nm_ref, nv_ref):
    g = p_ref[0].astype(F32)
    for j in range(1, p_ref.shape[0]):
        g = g + p_ref[j].astype(F32)
    new_m = ADAM_B1 * m_ref[...] + (1.0 - ADAM_B1) * g
    new_v = ADAM_B2 * v_ref[...] + (1.0 - ADAM_B2) * jnp.square(g)
    m_hat = new_m / (1.0 - ADAM_B1 ** ADAM_STEP)
    v_hat = new_v / (1.0 - ADAM_B2 ** ADAM_STEP)
    g_ref[...] = g
    d_ref[...] = -ADAM_LR * (m_hat / (jnp.sqrt(v_hat) + ADAM_EPS) + ADAM_WD * w_ref[...])
    nm_ref[...] = new_m
    nv_ref[...] = new_v


def adamw(parts, w, m, v, name):
    rows = w.shape[0]
    tile = _row_tile(rows, N_DEV * LANES * jnp.dtype(parts.dtype).itemsize + 7 * LANES * 4)

    def body(p_ref, w_ref, m_ref, v_ref, g_ref, d_ref, nm_ref, nv_ref):
        _adamw_block(p_ref, w_ref, m_ref, v_ref, g_ref, d_ref, nm_ref, nv_ref)

    flat = pl.BlockSpec((tile, LANES), lambda i: (i, 0))
    return pl.pallas_call(
        body, grid=(rows // tile,), in_specs=[pl.BlockSpec((N_DEV, tile, LANES), lambda i: (0, i, 0))] + [flat] * 3,
        out_specs=[flat] * 4, out_shape=[jax.ShapeDtypeStruct(w.shape, F32)] * 4,
        compiler_params=_params(("parallel",)), name=name,
    )(parts, w, m, v)


def _part_rows(n_elems):
    return -(-n_elems // (PACK_ROWS * LANES)) * PACK_ROWS


def _pack(arrays, dtype, lead=()):
    parts, layout, off = [], [], 0
    for arr in arrays:
        n = math.prod(arr.shape[len(lead):])
        rows = _part_rows(n)
        flat = arr.reshape(lead + (n,)).astype(dtype)
        flat = jnp.pad(flat, [(0, 0)] * len(lead) + [(0, rows * LANES - n)])
        parts.append(flat.reshape(lead + (rows, LANES)))
        layout.append((off, rows))
        off += rows
    total = -(-off // 1024) * 1024
    if total > off:
        parts.append(jnp.zeros(lead + (total - off, LANES), dtype))
    return jnp.concatenate(parts, axis=len(lead)), layout


def _unpack(buf, layout, shapes, lead=()):
    out = []
    for (off, rows), shape in zip(layout, shapes):
        n = math.prod(shape)
        piece = lax.slice_in_dim(buf, off, off + rows, axis=len(lead))
        out.append(piece.reshape(lead + (rows * LANES,))[..., :n].reshape(lead + tuple(shape)))
    return out


def _split_shards(full, axis):
    if axis == 0:
        return full.reshape((N_DEV, full.shape[0] // N_DEV, full.shape[1]))
    return full.reshape((full.shape[0], N_DEV, full.shape[1] // N_DEV)).transpose(1, 0, 2)


def _join_shards(shards, axis):
    if axis == 0:
        return shards.reshape((-1, shards.shape[2]))
    return shards.transpose(1, 0, 2).reshape((shards.shape[1], -1))


def _shift_down(t):
    return jnp.pad(t, ((1, 0), (0, 0)))[:-1]


def _shift_up(t):
    return jnp.pad(t, ((0, 1), (0, 0)))[1:]


def kernel(x, p, norm_mix, w_in, q_gain, k_gain, rel_bias, w_attn_up, shift_mix, w0, w_decay_up, a0, w_aaa_up, w_gate_up, k_k, k_a, r_k, gn_w, gn_b, w_rwkv_up, w_out, norm_mlp, w_mlp_in, w_mlp_out, norm_ple, w_ple_gate, w_ple_proj, loss_target, m_norm_mix, m_w_in, m_q_gain, m_k_gain, m_rel_bias, m_w_attn_up, m_shift_mix, m_w0, m_w_decay_up, m_a0, m_w_aaa_up, m_w_gate_up, m_k_k, m_k_a, m_r_k, m_gn_w, m_gn_b, m_w_rwkv_up, m_w_out, m_norm_mlp, m_w_mlp_in, m_w_mlp_out, m_norm_ple, m_w_ple_gate, m_w_ple_proj, v_norm_mix, v_w_in, v_q_gain, v_k_gain, v_rel_bias, v_w_attn_up, v_shift_mix, v_w0, v_w_decay_up, v_a0, v_w_aaa_up, v_w_gate_up, v_k_k, v_k_a, v_r_k, v_gn_w, v_gn_b, v_w_rwkv_up, v_w_out, v_norm_mlp, v_w_mlp_in, v_w_mlp_out, v_norm_ple, v_w_ple_gate, v_w_ple_proj):
    given = dict(locals())
    xs = x[0]
    t_len, d_model = xs.shape
    target = loss_target[0]
    p_bf = p[0, 0].astype(BF16)
    rw_width = w0.shape[1]
    n_rheads = rw_width // RWKV_HEAD
    lora_d, lora_a, lora_g = w_decay_up.shape[1], w_aaa_up.shape[1], w_gate_up.shape[1]
    z_width = shift_mix.shape[1]
    z_pad = -(-z_width // LANES) * LANES
    qkv_width = 3 * ATTN_WIDTH
    assert z_width == 3 * rw_width + lora_d + lora_a + lora_g
    assert N_DEV * w_in.shape[2] == qkv_width + z_width + 2 * d_model
    for window, dilation in DILATED_GROUPS:
        assert window // dilation == ATTN_BLOCK and t_len % (dilation * ATTN_BLOCK) == 0

    shard_bf = {n: given[n][0].astype(BF16) for n in BIG}
    early = ['w_in', 'w_decay_up', 'w_aaa_up', 'w_gate_up']
    during_proj = ['w_mlp_out']
    during_scan = [n for n in BIG if n not in early + during_proj]
    late = during_proj + during_scan
    full = {n: _join_shards(g, 1) for n, g in zip(early, all_gather_many([shard_bf[n] for n in early], "gather_w_in"))}
    z_end = qkv_width + z_width
    w_all = jnp.concatenate([full['w_in'][:, :z_end], jnp.zeros((d_model, z_pad - z_width), BF16),
                             full['w_in'][:, z_end:]], axis=1)
    gates_at = qkv_width + z_pad

    (h_in,) = rowmap_fwd(lambda a, g: st_norm(a, g)[1:], [xs], [norm_mix], [BF16], "norm_in")
    proj, proj_gathered = project_and_gather(h_in, w_all, [shard_bf[n] for n in during_proj])
    qkv, z = proj[:, :qkv_width], proj[:, qkv_width:gates_at]
    gate_a, gate_r = proj[:, gates_at:gates_at + d_model], proj[:, gates_at + d_model:]

    q_raw = qkv[:, :ATTN_WIDTH].reshape(t_len * N_HEADS, HEAD_DIM)
    k_raw = qkv[:, ATTN_WIDTH:2 * ATTN_WIDTH].reshape(t_len * N_HEADS, HEAD_DIM)
    q_n, k_n = rowmap_fwd(st_qk_norm, [q_raw, k_raw], [q_gain, k_gain], [F32, F32], "qk_norm")
    q_n, k_n = q_n.reshape(t_len, ATTN_WIDTH), k_n.reshape(t_len, ATTN_WIDTH)
    buckets = _bucket_tables()
    bias = bias_fwd(rel_bias, buckets, "attn_bias")

    att_in, att_o, att_l = [], [], []
    for g, (_, dilation) in enumerate(DILATED_GROUPS):
        hs = slice(g * HEADS_PER_GROUP, (g + 1) * HEADS_PER_GROUP)
        cols = slice(g * ATTN_OUT, (g + 1) * ATTN_OUT)
        v_g = qkv[:, 2 * ATTN_WIDTH + g * ATTN_OUT:2 * ATTN_WIDTH + (g + 1) * ATTN_OUT]
        ops = ((q_n[:, cols], 0), (k_n[:, cols], 0), (v_g, 0), bias[hs, 0], bias[hs, 1])
        o_g, l_g = attn_fwd(*ops, dilation, name=f"attn_fwd_{g}")
        att_in.append(ops)
        att_o.append(o_g)
        att_l.append(l_g)
    (attn,) = rowmap_fwd(st_merge, att_o + att_l, [], [BF16], "attn_merge")

    c0 = rw_width
    cuts = [0, c0, 2 * c0, 3 * c0, 3 * c0 + lora_d, 3 * c0 + lora_d + lora_a, z_width]
    z_parts = [z[:, lo:hi] for lo, hi in zip(cuts[:-1], cuts[1:])]
    z_prev = [_shift_down(t) for t in z_parts]
    mixes = [shift_mix[:, lo:hi] for lo, hi in zip(cuts[:-1], cuts[1:])]
    pre_params = mixes + [w0, full['w_decay_up'], a0, full['w_aaa_up'], full['w_gate_up'], k_k, k_a]
    pre_out = rowmap_fwd(functools.partial(_st_rwkv_pre, _dot), z_parts + z_prev, pre_params, [F32] * 7, "rwkv_pre")
    r_s, lw_s, k_s, v_s, kk0_s, a_s, g_s = pre_out

    aa_s, bb_s = rowmap_fwd(functools.partial(_st_rwkv_kk, _head_sum), [kk0_s, a_s], [], [F32, F32], "rwkv_kk")
    scan_in = [r_s, lw_s, k_s, v_s, aa_s, bb_s]
    y_t, s0_h, scan_gathered = scan_and_gather(scan_in, [shard_bf[n] for n in during_scan])
    late_gathered = list(proj_gathered) + list(scan_gathered)
    wt = {n: g if SHARD_AXIS[n] == 1 else g.reshape(-1, g.shape[2]) for n, g in zip(late, late_gathered)}
    post_params = [gn_w, gn_b, r_k.reshape(1, rw_width)]
    post_in = [y_t, r_s, k_s, v_s, g_s]
    (rw,) = rowmap_fwd(functools.partial(_st_rwkv_post, _head_sum), post_in, post_params, [BF16], "rwkv_post")
    attn_d = matmul(attn, wt['w_attn_up'], 'nn', F32, "attn_up", b_shards=True)
    rwkv_d = matmul(rw, wt['w_rwkv_up'], 'nn', F32, "rwkv_up", b_shards=True)

    (merged,) = rowmap_fwd(st_gate, [gate_a, gate_r, attn_d, rwkv_d], [], [BF16], "gate_merge")
    mix_out = matmul(merged, wt['w_out'], 'nn', F32, "out_proj")
    x1, h_mlp = rowmap_fwd(st_res_norm, [xs, mix_out], [norm_mlp], [F32, BF16], "res_norm_mlp")
    u, act = matmul(h_mlp, wt['w_mlp_in'], 'nn', [F32, BF16], "mlp_in", b_shards=True,
                    epilogue=lambda acc: (acc,) + st_relu2(acc))
    mlp_out = matmul(act, wt['w_mlp_out'], 'nn', F32, "mlp_out")
    x2, h_ple = rowmap_fwd(st_res_norm, [x1, mlp_out], [norm_ple], [F32, BF16], "res_norm_ple")
    pg = matmul(h_ple, wt['w_ple_gate'], 'nn', F32, "ple_gate")
    pp = matmul(p_bf, wt['w_ple_proj'], 'nn', F32, "ple_proj", b_shards=True)
    dy, d_pg, d_pp, loss_local = loss_head(x2, pg, pp, target, "loss_head")

    def row_cut(full_grad):
        return full_grad.reshape(N_DEV, full_grad.shape[0] // N_DEV, full_grad.shape[1])

    grads, sends = {}, {}
    sends['w_ple_gate'] = row_cut(matmul(h_ple, d_pg, 'tn', BF16, "d_w_ple_gate"))
    sends['w_ple_proj'] = matmul(p_bf, d_pp, 'tn', BF16, "d_w_ple_proj", out_shards=True)
    d_h_ple = matmul(d_pg, wt['w_ple_gate'], 'nt', F32, "d_h_ple")
    (d_x2, d_x2_bf), (grads['norm_ple'],) = rowmap_bwd(
        st_res_norm, [x1, mlp_out], [norm_ple], [dy, d_h_ple], [F32, BF16], "res_norm_ple_bwd")
    sends['w_mlp_out'] = row_cut(matmul(act, d_x2_bf, 'tn', BF16, "d_w_mlp_out"))
    (d_u,) = matmul(d_x2_bf, wt['w_mlp_out'], 'nt', [BF16], "d_mlp_act", extras=[u],
                    epilogue=lambda d_act, u_blk: (d_act * (2.0 * jnp.maximum(u_blk, 0.0)),))
    sends['w_mlp_in'] = matmul(h_mlp, d_u, 'tn', BF16, "d_w_mlp_in", out_shards=True)
    d_h_mlp = matmul(d_u, wt['w_mlp_in'], 'nt', F32, "d_h_mlp", b_shards=True)
    (d_x1, d_x1_bf), (grads['norm_mlp'],) = rowmap_bwd(
        st_res_norm, [xs, mix_out], [norm_mlp], [d_x2, d_h_mlp], [F32, BF16], "res_norm_mlp_bwd")

    sends['w_out'] = row_cut(matmul(merged, d_x1_bf, 'tn', BF16, "d_w_out"))
    d_merged = matmul(d_x1_bf, wt['w_out'], 'nt', F32, "d_merged")
    (d_gate_a, d_gate_r, d_attn_d, d_rwkv_d), _ = rowmap_bwd(
        st_gate, [gate_a, gate_r, attn_d, rwkv_d], [], [d_merged], [BF16] * 4, "gate_merge_bwd")
    sends['w_attn_up'] = matmul(attn, d_attn_d, 'tn', BF16, "d_w_attn_up", out_shards=True)
    sends['w_rwkv_up'] = matmul(rw, d_rwkv_d, 'tn', BF16, "d_w_rwkv_up", out_shards=True)
    d_attn = matmul(d_attn_d, wt['w_attn_up'], 'nt', F32, "d_attn", b_shards=True)
    d_rw = matmul(d_rwkv_d, wt['w_rwkv_up'], 'nt', F32, "d_rw", b_shards=True)

    (d_y, d_r1, d_k1, d_v1, d_g), (grads['gn_w'], grads['gn_b'], grads['r_k']) = rowmap_bwd(
        functools.partial(_st_rwkv_post, _head_sum_ad), post_in, post_params, [d_rw], [F32] * 5, "rwkv_post_bwd")
    core = lax.axis_index("c").astype(I32).reshape(1)

    def chip_partials(names, tag):
        mine = [sends[n].reshape((N_CHIPS, 2) + sends[n].shape[1:]) for n in names]
        theirs = pair_exchange(mine, "pair_grads_" + tag)
        return [pair_add(m, t, core, "pair_add_" + n) for n, m, t in zip(names, mine, theirs)]

    scan_grads, late_received = scan_bwd_and_scatter(scan_in, s0_h, d_y, chip_partials(late, "late"))
    received = dict(zip(late, late_received))
    d_r2, d_lw, d_k2, d_v2, d_aa, d_bb = scan_grads
    (d_kk0, d_a), _ = rowmap_bwd(functools.partial(_st_rwkv_kk, _head_sum_ad), [kk0_s, a_s], [], [d_aa, d_bb],
                                 [F32, F32], "rwkv_kk_bwd")
    pre_cts = [[d_r1, d_r2], d_lw, [d_k1, d_k2], [d_v1, d_v2], d_kk0, d_a, d_g]
    d_zp, d_pre = rowmap_bwd(functools.partial(_st_rwkv_pre, _dot_ad), z_parts + z_prev, pre_params, pre_cts,
                             [F32] * 12, "rwkv_pre_bwd")
    grads['shift_mix'] = jnp.concatenate(d_pre[:6], axis=1)
    grads['w0'], d_w_decay, grads['a0'], d_w_aaa, d_w_gate, grads['k_k'], grads['k_a'] = d_pre[6:]
    for name, full_grad in (('w_decay_up', d_w_decay), ('w_aaa_up', d_w_aaa), ('w_gate_up', d_w_gate)):
        sends[name] = _split_shards(full_grad, 1).astype(BF16)
    z_fill = [jnp.zeros((t_len, z_pad - z_width), F32)] if z_pad > z_width else []
    d_z_cur = jnp.concatenate(d_zp[:6] + z_fill, axis=1)
    d_z_prev = _shift_up(jnp.concatenate(d_zp[6:] + z_fill, axis=1))
    (d_z,) = rowmap_fwd(st_add, [d_z_cur, d_z_prev], [], [BF16], "d_z_sum")

    d_merge, _ = rowmap_bwd(st_merge, att_o + att_l, [], [d_attn], [F32] * 6, "attn_merge_bwd")
    d_qn, d_kn, d_vs, d_bias = [], [], [], []
    for g, (_, dilation) in enumerate(DILATED_GROUPS):
        dq, dk, dv, dbp, dbc = attn_bwd(*att_in[g], d_merge[g], d_merge[3 + g], dilation, name=f"attn_bwd_{g}")
        d_qn.append(dq)
        d_kn.append(dk)
        d_vs.append(dv)
        d_bias.append(jnp.stack([dbp, dbc], axis=1))
    d_table = bias_bwd(jnp.concatenate(d_bias, axis=0), buckets, "attn_bias_bwd")
    grads['rel_bias'] = d_table[:, :N_HEADS]
    d_qn = jnp.concatenate(d_qn, axis=1).reshape(t_len * N_HEADS, HEAD_DIM)
    d_kn = jnp.concatenate(d_kn, axis=1).reshape(t_len * N_HEADS, HEAD_DIM)
    (d_q, d_k), (grads['q_gain'], grads['k_gain']) = rowmap_bwd(
        st_qk_norm, [q_raw, k_raw], [q_gain, k_gain], [d_qn, d_kn], [BF16, BF16], "qk_norm_bwd")

    d_pieces = ([d_q.reshape(t_len, ATTN_WIDTH), d_k.reshape(t_len, ATTN_WIDTH)] + [t.astype(BF16) for t in d_vs]
                + [d_z, d_gate_a, d_gate_r])
    piece_cols = [0, ATTN_WIDTH] + [2 * ATTN_WIDTH + g * ATTN_OUT for g in range(len(DILATED_GROUPS))]
    piece_cols += [qkv_width, gates_at, gates_at + d_model]
    d_w_pieces = [matmul(h_in, piece, 'tn', BF16, f"d_w_in_{i}") for i, piece in enumerate(d_pieces)]
    d_w_pieces[-3] = d_w_pieces[-3][:, :z_width]
    sends['w_in'] = _split_shards(jnp.concatenate(d_w_pieces, axis=1), 1)
    d_h_in, early_received = input_grad_and_scatter(d_pieces, w_all, piece_cols, chip_partials(early, "early"))
    received.update(zip(early, early_received))
    (grad_x,), (grads['norm_mix'],) = rowmap_bwd(st_norm, [xs], [norm_mix], [d_x1, d_h_in], [F32], "norm_in_bwd")

    by_name = {n: adamw_shard(received[n], given[n], given['m_' + n], given['v_' + n], "adamw_" + n) for n in BIG}
    pk = lambda prefix: _pack([given[prefix + n] for n in SMALL], F32)[0]
    small_buf, small_layout = _pack([grads[n].reshape(given[n].shape) for n in SMALL], F32)
    (small_all,) = all_gather_many([small_buf], "gather_small_grads")
    small_out = adamw(small_all, pk(''), pk('m_'), pk('v_'), "adamw_replicated")
    small_shapes = [given[n].shape for n in SMALL]
    by_name.update(zip(SMALL, zip(*[_unpack(buf, small_layout, small_shapes) for buf in small_out])))
    loss = lax.psum(loss_local, MESH_AXES)
    return (loss, grad_x[None], *[by_name[n][0] for n in WEIGHTS], *[by_name[n][1] for n in WEIGHTS],
            *[by_name[n][2] for n in WEIGHTS], *[by_name[n][3] for n in WEIGHTS])
```

```python
import functools
import math

import jax
import jax.numpy as jnp
from jax import lax
from jax.experimental import pallas as pl
from jax.experimental.pallas import tpu as pltpu

F32 = jnp.float32
BF16 = jnp.bfloat16
I32 = jnp.int32

N_DEV = 8
N_CHIPS = 4
MESH_AXES = ("x", "y", "c")
LANES = 128
PACK_ROWS = 16
VMEM_LIMIT_BYTES = 48 * 2**20
ROW_BLOCK_BYTES = 10 * 2**20

HEAD_DIM = 128
ATTN_BLOCK = 128
HEADS_PER_GROUP = 4
DILATED_GROUPS = ((128, 1), (512, 4), (2048, 16))
N_HEADS = HEADS_PER_GROUP * len(DILATED_GROUPS)
ATTN_WIDTH = N_HEADS * HEAD_DIM
ATTN_OUT = HEADS_PER_GROUP * HEAD_DIM
N_BUCKETS = 32
MAX_DISTANCE = 2048
RWKV_HEAD = 64
RWKV_CHUNK = 64
RWKV_PAIRS_PER_STEP = 8
RMS_EPS = 1e-6
GN_EPS = 64e-5
NEG_INF = -1e30

ADAM_LR = 0.001
ADAM_B1 = 0.9
ADAM_B2 = 0.999
ADAM_EPS = 1e-08
ADAM_WD = 0.01
ADAM_STEP = 10

WEIGHTS = ['norm_mix', 'w_in', 'q_gain', 'k_gain', 'rel_bias', 'w_attn_up', 'shift_mix', 'w0', 'w_decay_up', 'a0',
           'w_aaa_up', 'w_gate_up', 'k_k', 'k_a', 'r_k', 'gn_w', 'gn_b', 'w_rwkv_up', 'w_out', 'norm_mlp', 'w_mlp_in',
           'w_mlp_out', 'norm_ple', 'w_ple_gate', 'w_ple_proj']
SHARD_AXIS = {'w_in': 1, 'w_attn_up': 1, 'w_decay_up': 1, 'w_aaa_up': 1, 'w_gate_up': 1, 'w_rwkv_up': 1, 'w_out': 0,
              'w_mlp_in': 1, 'w_mlp_out': 0, 'w_ple_gate': 0, 'w_ple_proj': 1}
BIG = [n for n in WEIGHTS if n in SHARD_AXIS]
SMALL = [n for n in WEIGHTS if n not in SHARD_AXIS]


def _params(sem):
    return pltpu.CompilerParams(dimension_semantics=sem, vmem_limit_bytes=VMEM_LIMIT_BYTES)


_DN = {'nn': (((1,), (0,)), ((), ())), 'nt': (((1,), (1,)), ((), ())), 'tn': (((0,), (0,)), ((), ()))}


def _dot(a, b, mode, exact):
    if exact:
        return lax.dot_general(a, b, _DN[mode], precision=lax.Precision.HIGH, preferred_element_type=F32)
    return lax.dot_general(a.astype(BF16), b.astype(BF16), _DN[mode], preferred_element_type=F32)


@functools.partial(jax.custom_vjp, nondiff_argnums=(2, 3))
def _dot_ad(a, b, mode, exact):
    return _dot(a, b, mode, exact)


def _dot_ad_fwd(a, b, mode, exact):
    return _dot(a, b, mode, exact), (a, b)


def _dot_ad_bwd(mode, exact, res, g):
    a, b = res
    if mode == 'nn':
        return _dot(g, b, 'nt', exact), _dot(a, g, 'tn', exact)
    if mode == 'nt':
        return _dot(g, b, 'nn', exact), _dot(g, a, 'tn', exact)
    return _dot(b, g, 'nt', exact), _dot(a, g, 'nn', exact)


_dot_ad.defvjp(_dot_ad_fwd, _dot_ad_bwd)


def _pick(n, cands):
    for c in cands:
        if n % c == 0:
            return c
    return n


def matmul(a, b, mode, out_dtype, name, b_shards=False, out_shards=False, b_col_offsets=None, epilogue=None,
           extras=(), exchange=None):
    a_list = list(a) if isinstance(a, (list, tuple)) else [a]
    b_list = list(b) if isinstance(b, (list, tuple)) else [b]
    seg = len(a_list)
    assert all(t.dtype == BF16 for t in a_list + b_list), name
    assert seg == 1 or (mode == 'nt' and not b_shards), name
    m = a_list[0].shape[1] if mode == 'tn' else a_list[0].shape[0]
    ks = [t.shape[0] if mode == 'tn' else t.shape[1] for t in a_list]
    b0 = b_list[0]
    b_rows, b_cols = (b0.shape[1], N_DEV * b0.shape[2]) if b_shards else b0.shape
    n = b_rows if mode == 'nt' else b_cols
    if seg == 1 and b_col_offsets is None:
        assert (b_cols if mode == 'nt' else b_rows) == ks[0], (name, a_list[0].shape, b0.shape)
    offsets = list(b_col_offsets) if b_col_offsets is not None else [0] * seg
    tm = _pick(m, (1024, 512, 256, 128))
    tn = _pick(n // N_DEV if (out_shards or (b_shards and mode != 'nt')) else n, (1024, 512, 256, 128))
    k_units = [kk // N_DEV if (b_shards and mode == 'nt') else kk for kk in ks] + [o for o in offsets if o]
    tk = next((c for c in (2048, 1024, 512, 256, 128) if all(u % c == 0 for u in k_units)), k_units[0])
    nks = [kk // tk for kk in ks]
    starts = [sum(nks[:s]) for s in range(seg)]
    nk = sum(nks)
    grid = (m // tm, n // tn, nk)
    total_steps = grid[0] * grid[1] * nk
    kind, moved = exchange if exchange is not None else (None, [])
    nx, ne = len(moved), len(extras)
    out_dtypes = list(out_dtype) if isinstance(out_dtype, (list, tuple)) else [out_dtype]
    no = len(out_dtypes)

    def body(*refs):
        a_refs, b_refs, x_refs = refs[:seg], refs[seg:2 * seg], refs[2 * seg:2 * seg + ne]
        pos = 2 * seg + ne
        moved_in, o_refs = refs[pos:pos + nx], refs[pos + nx:pos + nx + no]
        pos += nx + no
        moved_out, acc_ref, sems = refs[pos:pos + nx], refs[pos + nx], refs[pos + nx + 1:]
        kk = pl.program_id(2)
        step = (pl.program_id(0) * grid[1] + pl.program_id(1)) * nk + kk
        if kind == 'gather':
            start, forward, finish = _gather_schedule(moved_in, moved_out, *sems)
            pl.when(step == 0)(start)
            pl.when(step == total_steps - 1 - total_steps // 8)(forward)
        elif kind == 'scatter':
            start, finish = _scatter_schedule(moved_in, moved_out, *sems)
            pl.when(step == 0)(start)

        @pl.when(kk == 0)
        def _():
            acc_ref[...] = jnp.zeros_like(acc_ref)

        for s in range(seg):
            def accumulate(s=s):
                acc_ref[...] += lax.dot_general(a_refs[s][...], b_refs[s][...], _DN[mode], preferred_element_type=F32)

            if seg == 1:
                accumulate()
            else:
                pl.when(jnp.logical_and(kk >= starts[s], kk < starts[s] + nks[s]))(accumulate)

        @pl.when(kk == nk - 1)
        def _():
            acc = acc_ref[...]
            outs = (acc,) if epilogue is None else epilogue(acc, *[x[...] for x in x_refs])
            for r, v in zip(o_refs, outs):
                r[...] = v.astype(r.dtype)

        if kind is not None:
            pl.when(step == total_steps - 1)(finish)

    def k_of(kk, s):
        return kk if seg == 1 else jnp.clip(kk - starts[s], 0, nks[s] - 1)

    a_specs, b_specs = [], []
    for s in range(seg):
        off = offsets[s] // tk
        if mode == 'tn':
            a_specs.append(pl.BlockSpec((tk, tm), lambda i, j, kk, s=s: (k_of(kk, s), i)))
        else:
            a_specs.append(pl.BlockSpec((tm, tk), lambda i, j, kk, s=s: (i, k_of(kk, s))))
        if mode == 'nt':
            if b_shards:
                per = b0.shape[2] // tk
                b_specs.append(pl.BlockSpec((None, tn, tk), lambda i, j, kk: (kk // per, j, kk % per)))
            else:
                b_specs.append(pl.BlockSpec((tn, tk), lambda i, j, kk, s=s, off=off: (j, off + k_of(kk, s))))
        else:
            if b_shards:
                per = b0.shape[2] // tn
                b_specs.append(pl.BlockSpec((None, tk, tn), lambda i, j, kk: (j // per, kk, j % per)))
            else:
                b_specs.append(pl.BlockSpec((tk, tn), lambda i, j, kk: (kk, j)))
    tile = pl.BlockSpec((tm, tn), lambda i, j, kk: (i, j))
    if out_shards:
        assert epilogue is None
        per_o = n // N_DEV // tn
        o_specs = [pl.BlockSpec((None, tm, tn), lambda i, j, kk: (j // per_o, i, j % per_o))]
        o_shapes = [jax.ShapeDtypeStruct((N_DEV, m, n // N_DEV), out_dtypes[0])]
    else:
        o_specs = [tile] * no
        o_shapes = [jax.ShapeDtypeStruct((m, n), d) for d in out_dtypes]
    moved_shapes = [jax.ShapeDtypeStruct(((N_DEV,) + t.shape) if kind == 'gather' else t.shape, t.dtype) for t in moved]
    res = pl.pallas_call(
        body, grid=grid, in_specs=a_specs + b_specs + [tile] * ne + [_ANY] * nx,
        out_specs=o_specs + [_ANY] * nx, out_shape=o_shapes + moved_shapes,
        scratch_shapes=[pltpu.VMEM((tm, tn), F32)] + (_exchange_scratch(nx) if nx else []),
        compiler_params=_params(("arbitrary",) * 3 if nx else ("parallel", "parallel", "arbitrary")), name=name,
    )(*a_list, *b_list, *extras, *moved)
    result = res[0] if (epilogue is None) else list(res[:no])
    return (result, list(res[no:])) if nx else result


def _row_bytes(shape, dtype):
    dims = list(shape[1:])
    dims[-1] = -(-dims[-1] // LANES) * LANES
    return math.prod(dims) * jnp.dtype(dtype).itemsize


def _row_tile(n, row_bytes):
    t = 1024
    while t > 16 and (n % t or t * row_bytes > ROW_BLOCK_BYTES):
        t //= 2
    assert n % t == 0, (n, t)
    return t


def rowmap(fn, tiled, bcast, out_tiled, out_acc, name):
    n = tiled[0].shape[0]
    tile = _row_tile(n, sum(_row_bytes(t.shape, t.dtype) for t in list(tiled) + list(out_tiled)))
    n_in, n_out = len(tiled) + len(bcast), len(out_tiled)

    def body(*refs):
        outs, accs = fn(*[r[...] for r in refs[:n_in]])
        assert len(outs) == n_out and len(accs) == len(out_acc), name
        for r, v in zip(refs[n_in:n_in + n_out], outs):
            r[...] = v.astype(r.dtype)
        acc_refs = refs[n_in + n_out:]
        if acc_refs:
            @pl.when(pl.program_id(0) == 0)
            def _():
                for r, v in zip(acc_refs, accs):
                    r[...] = v.astype(r.dtype)

            @pl.when(pl.program_id(0) != 0)
            def _():
                for r, v in zip(acc_refs, accs):
                    r[...] += v.astype(r.dtype)

    def tspec(s):
        nd = len(s.shape)
        return pl.BlockSpec((tile,) + tuple(s.shape[1:]), lambda i, nd=nd: (i,) + (0,) * (nd - 1))

    def bspec(s):
        nd = len(s.shape)
        return pl.BlockSpec(tuple(s.shape), lambda i, nd=nd: (0,) * nd)

    res = pl.pallas_call(
        body, grid=(n // tile,),
        in_specs=[tspec(t) for t in tiled] + [bspec(t) for t in bcast],
        out_specs=[tspec(t) for t in out_tiled] + [bspec(t) for t in out_acc],
        out_shape=list(out_tiled) + list(out_acc),
        compiler_params=_params(("arbitrary",)), name=name,
    )(*tiled, *bcast)
    return list(res[:n_out]), list(res[n_out:])


def rowmap_fwd(fwd, tiled, bcast, out_dtypes, name):
    shapes = jax.eval_shape(fwd, *tiled, *bcast)
    out_tiled = [jax.ShapeDtypeStruct(s.shape, d) for s, d in zip(shapes, out_dtypes)]
    outs, _ = rowmap(lambda *blk: (fwd(*[b.astype(F32) for b in blk]), ()), tiled, bcast, out_tiled, [], name)
    return outs


def rowmap_bwd(fwd, tiled, bcast, cts, want, name):
    cts = [[] if c is None else (list(c) if isinstance(c, (list, tuple)) else [c]) for c in cts]
    flat_cts = [c for group in cts for c in group]
    nt_, nc_ = len(tiled), len(flat_cts)

    def fn(*blk):
        ins = [b.astype(F32) for b in blk[:nt_]] + [b.astype(F32) for b in blk[nt_ + nc_:]]
        ctb = list(blk[nt_:nt_ + nc_])
        outs, vjp = jax.vjp(fwd, *ins)
        full = []
        for o, group in zip(outs, cts):
            acc = jnp.zeros_like(o)
            for _ in group:
                acc = acc + ctb.pop(0).astype(F32)
            full.append(acc)
        g = vjp(tuple(full))
        return [g[i] for i in range(nt_) if want[i] is not None], list(g[nt_:])

    out_tiled = [jax.ShapeDtypeStruct(t.shape, w) for t, w in zip(tiled, want) if w is not None]
    out_acc = [jax.ShapeDtypeStruct(b.shape, F32) for b in bcast]
    return rowmap(fn, list(tiled) + flat_cts, bcast, out_tiled, out_acc, name)


def _rms(x, gain):
    return x * lax.rsqrt(jnp.mean(jnp.square(x), axis=-1, keepdims=True) + RMS_EPS) * gain


def _sigmoid(x):
    return 1.0 / (1.0 + jnp.exp(-x))


def _softplus(x):
    return jnp.maximum(x, 0.0) + jnp.log(1.0 + jnp.exp(-jnp.abs(x)))


def st_norm(x, gain):
    return x, _rms(x, gain)


def st_res_norm(x, delta, gain):
    y = x + delta
    return y, _rms(y, gain)


def st_qk_norm(q, k, q_gain, k_gain):
    return _rms(q, q_gain), _rms(k, k_gain)


def st_merge(o0, o1, o2, l0, l1, l2):
    m = jnp.maximum(jnp.maximum(l0, l1), l2)
    e0, e1, e2 = jnp.exp(l0 - m), jnp.exp(l1 - m), jnp.exp(l2 - m)
    return ((e0 * o0 + e1 * o1 + e2 * o2) / (e0 + e1 + e2),)


def _st_rwkv_pre(dot, zr, zk, zv, xw, xa, xg, pr, pk, pv, pw, pa, pg, mr, mk, mv, mw, ma, mg,
                 w0, w_decay, a0, w_aaa, w_gate, k_k, k_a):
    def shift(cur, prev, mix):
        return cur + mix * (prev - cur)

    r, k, v = shift(zr, pr, mr), shift(zk, pk, mk), shift(zv, pv, mv)
    xw, xa, xg = shift(xw, pw, mw), shift(xa, pa, ma), shift(xg, pg, mg)
    w = -_softplus(-(w0 + dot(jnp.tanh(xw), w_decay, 'nn', False))) - 0.5
    a = _sigmoid(a0 + dot(xa, w_aaa, 'nn', False))
    g = dot(_sigmoid(xg), w_gate, 'nn', False)
    log_decay = -jnp.exp(w)
    return r, log_decay, k * (1.0 + (a - 1.0) * k_a), v, k * k_k, a, g


def _head_sum(x):
    joins = _pair_block_mask(2 * RWKV_HEAD).astype(F32)
    tile = joins.shape[0]
    return jnp.concatenate([_dot(x[:, lo:lo + tile], joins, 'nn', True) for lo in range(0, x.shape[1], tile)], axis=1)


@jax.custom_vjp
def _head_sum_ad(x):
    return _head_sum(x)


_head_sum_ad.defvjp(lambda x: (_head_sum(x), None), lambda _, g: (_head_sum(g),))


def _st_rwkv_kk(head_sum, kk0, a):
    kk = kk0 / jnp.maximum(jnp.sqrt(head_sum(jnp.square(kk0))), 1e-12)
    return -kk, kk * a


def _st_rwkv_post(head_sum, y, r, k, v, g, gn_w, gn_b, r_k):
    mu = head_sum(y) * (1.0 / RWKV_HEAD)
    var = head_sum(jnp.square(y - mu)) * (1.0 / RWKV_HEAD)
    out = (y - mu) * lax.rsqrt(var + GN_EPS) * gn_w + gn_b
    out = out + head_sum(r * k * r_k) * v
    return (out * g,)


def st_gate(g0, g1, attn_d, rwkv_d):
    return (_sigmoid(g0) * attn_d + _sigmoid(g1) * rwkv_d,)


def st_relu2(u):
    return (jnp.square(jnp.maximum(u, 0.0)),)


def st_add(a, b):
    return (a + b,)


def loss_head(x2, pg, pp, target, name):
    d_model = x2.shape[1]

    def fn(x2, pg, pp, tgt):
        s = _sigmoid(pg)
        err = x2 + s * pp - tgt
        dy = err * (1.0 / d_model)
        part = 0.5 * jnp.sum(jnp.square(err)) * (1.0 / d_model)
        return [dy, dy * pp * s * (1.0 - s), dy * s], [jnp.full((8, LANES), part, F32)]

    sds = jax.ShapeDtypeStruct
    outs, accs = rowmap(fn, [x2, pg, pp, target], [],
                        [sds(x2.shape, F32), sds(x2.shape, BF16), sds(x2.shape, BF16)], [sds((8, LANES), F32)], name)
    return outs[0], outs[1], outs[2], accs[0][0, 0]


def _attn_block(dot, q, kp, kc, vp, vc, bp, bc, prev_offset):
    blk = q.shape[0]
    qi = lax.broadcasted_iota(I32, (blk, blk), 0)
    ki = lax.broadcasted_iota(I32, (blk, blk), 1)
    mask_c = ki <= qi
    mask_p = ki >= qi + prev_offset
    scale = HEAD_DIM ** -0.5
    s_c = jnp.where(mask_c, dot(q, kc, 'nt', False) * scale + bc, NEG_INF)
    s_p = jnp.where(mask_p, dot(q, kp, 'nt', False) * scale + bp, NEG_INF)
    m = lax.stop_gradient(jnp.maximum(jnp.max(s_c, axis=1, keepdims=True), jnp.max(s_p, axis=1, keepdims=True)))
    e_c = jnp.where(mask_c, jnp.exp(s_c - m), 0.0)
    e_p = jnp.where(mask_p, jnp.exp(s_p - m), 0.0)
    l = jnp.sum(e_c, axis=1, keepdims=True) + jnp.sum(e_p, axis=1, keepdims=True)
    o = (dot(e_c, vc, 'nn', False) + dot(e_p, vp, 'nn', False)) / l
    return o, jnp.broadcast_to(m + jnp.log(l), o.shape)


class _ClassView:
    def __init__(self, tokens, dilation, first_col):
        self.view = tokens.reshape(tokens.shape[0] // dilation, dilation * tokens.shape[1])
        self.tiles, self.first = tokens.shape[1] // HEAD_DIM, first_col // HEAD_DIM

    def spec(self, shift, nb):
        tiles, first = self.tiles, self.first
        return pl.BlockSpec((ATTN_BLOCK, HEAD_DIM),
                            lambda h, r, n: (jnp.clip(n + shift, 0, nb - 1), r * tiles + first + h))


_BIAS_SPEC = pl.BlockSpec((None, ATTN_BLOCK, ATTN_BLOCK), lambda h, r, n: (h, 0, 0))


def attn_fwd(q, k, v, bp, bc, dilation, name):
    t_len = q[0].shape[0]
    nb = t_len // dilation // ATTN_BLOCK
    qv, kv, vv = (_ClassView(arr, dilation, col) for arr, col in (q, k, v))
    o_shape = jax.ShapeDtypeStruct((t_len // dilation, dilation * ATTN_OUT), F32)
    o_spec = pl.BlockSpec((ATTN_BLOCK, HEAD_DIM), lambda h, r, n: (n, r * HEADS_PER_GROUP + h))

    def body(q_ref, kp_ref, kc_ref, vp_ref, vc_ref, bp_ref, bc_ref, o_ref, l_ref):
        off = jnp.where(pl.program_id(2) > 0, 0, ATTN_BLOCK)
        o, l = _attn_block(_dot, q_ref[...], kp_ref[...], kc_ref[...], vp_ref[...], vc_ref[...], bp_ref[...],
                           bc_ref[...], off)
        o_ref[...] = o
        l_ref[...] = l

    o, l = pl.pallas_call(
        body, grid=(HEADS_PER_GROUP, dilation, nb),
        in_specs=[qv.spec(0, nb), kv.spec(-1, nb), kv.spec(0, nb), vv.spec(-1, nb), vv.spec(0, nb),
                  _BIAS_SPEC, _BIAS_SPEC],
        out_specs=[o_spec, o_spec], out_shape=[o_shape] * 2,
        compiler_params=_params(("parallel", "parallel", "parallel")), name=name,
    )(qv.view, kv.view, kv.view, vv.view, vv.view, bp, bc)
    return o.reshape(t_len, ATTN_OUT), l.reshape(t_len, ATTN_OUT)


def attn_bwd(q, k, v, bp, bc, do, dl, dilation, name):
    t_len = q[0].shape[0]
    blk = ATTN_BLOCK
    nb = t_len // dilation // blk
    qv, kv, vv = (_ClassView(arr, dilation, col) for arr, col in (q, k, v))
    dov, dlv = _ClassView(do, dilation, 0), _ClassView(dl, dilation, 0)
    cur, bias = dov.spec(0, nb), _BIAS_SPEC

    def body(q_ref, kp_ref, kc_ref, vp_ref, vc_ref, bp_ref, bc_ref, do_ref, dl_ref,
             dq_ref, dkp_ref, dkc_ref, dvp_ref, dvc_ref, dbp_ref, dbc_ref):
        off = jnp.where(pl.program_id(2) > 0, 0, blk)
        f = functools.partial(_attn_block, _dot_ad, prev_offset=off)
        _, vjp = jax.vjp(f, q_ref[...], kp_ref[...], kc_ref[...], vp_ref[...], vc_ref[...], bp_ref[...], bc_ref[...])
        dq, dkp, dkc, dvp, dvc, dbp, dbc = vjp((do_ref[...], dl_ref[...]))
        dq_ref[...] = dq
        dkp_ref[...] = dkp
        dkc_ref[...] = dkc
        dvp_ref[...] = dvp
        dvc_ref[...] = dvc
        first = jnp.logical_and(pl.program_id(1) == 0, pl.program_id(2) == 0)

        @pl.when(first)
        def _():
            dbp_ref[...] = dbp
            dbc_ref[...] = dbc

        @pl.when(jnp.logical_not(first))
        def _():
            dbp_ref[...] += dbp
            dbc_ref[...] += dbc

    blocks = jax.ShapeDtypeStruct(dov.view.shape, F32)
    grid = (HEADS_PER_GROUP, dilation, nb)
    dq, dkp, dkc, dvp, dvc, dbp, dbc = pl.pallas_call(
        body, grid=grid,
        in_specs=[qv.spec(0, nb), kv.spec(-1, nb), kv.spec(0, nb), vv.spec(-1, nb), vv.spec(0, nb), bias, bias,
                  cur, dlv.spec(0, nb)],
        out_specs=[cur] * 5 + [bias] * 2, out_shape=[blocks] * 5 + [jax.ShapeDtypeStruct(bp.shape, F32)] * 2,
        compiler_params=_params(("arbitrary", "arbitrary", "arbitrary")), name=name,
    )(qv.view, kv.view, kv.view, vv.view, vv.view, bp, bc, dov.view, dlv.view)

    nxt = dov.spec(1, nb)

    def add_body(kc_ref, kp_ref, vc_ref, vp_ref, dk_ref, dv_ref):
        has_next = (pl.program_id(2) + 1 < nb).astype(F32)
        dk_ref[...] = kc_ref[...] + kp_ref[...] * has_next
        dv_ref[...] = vc_ref[...] + vp_ref[...] * has_next

    dk, dv = pl.pallas_call(
        add_body, grid=grid, in_specs=[cur, nxt, cur, nxt], out_specs=[cur, cur], out_shape=[blocks] * 2,
        compiler_params=_params(("parallel", "parallel", "parallel")), name=name + "_kv",
    )(dkc, dkp, dvc, dvp)
    return [t.reshape(t_len, ATTN_OUT) for t in (dq, dk, dv)] + [dbp, dbc]


def _t5_bucket(dist):
    max_exact = N_BUCKETS // 2
    d_f = jnp.maximum(dist, 1).astype(F32)
    large = max_exact + (jnp.log(d_f / max_exact) / math.log(MAX_DISTANCE / max_exact)
                         * (N_BUCKETS - max_exact)).astype(I32)
    large = jnp.minimum(large, N_BUCKETS - 1)
    return jnp.where(dist < max_exact, dist, large)


def _bucket_tables():
    blk = ATTN_BLOCK
    qi = jnp.arange(blk)[:, None]
    ki = jnp.arange(blk)[None, :]
    out = []
    for _, dilation in DILATED_GROUPS:
        rel_p = jnp.maximum(blk + qi - ki, 0) * dilation
        rel_c = jnp.maximum(qi - ki, 0) * dilation
        out.append(jnp.stack([_t5_bucket(rel_p), _t5_bucket(rel_c)]))
    return jnp.stack(out).astype(I32)


def bias_fwd(table, buckets, name):
    blk = ATTN_BLOCK

    def body(tab_ref, bkt_ref, out_ref):
        for g in range(len(DILATED_GROUPS)):
            for half in range(2):
                bk = bkt_ref[g, half]
                for hh in range(HEADS_PER_GROUP):
                    h = g * HEADS_PER_GROUP + hh
                    acc = jnp.zeros((blk, blk), F32)
                    for b in range(N_BUCKETS):
                        acc = jnp.where(bk == b, tab_ref[b, h], acc)
                    out_ref[h, half] = acc

    return pl.pallas_call(
        body, in_specs=[pl.BlockSpec(memory_space=pltpu.SMEM), pl.BlockSpec(memory_space=pltpu.VMEM)],
        out_specs=pl.BlockSpec(memory_space=pltpu.VMEM),
        out_shape=jax.ShapeDtypeStruct((N_HEADS, 2, blk, blk), F32), name=name,
    )(table, buckets)


def bias_bwd(dbias, buckets, name):
    def body(db_ref, bkt_ref, out_ref):
        rows = lax.broadcasted_iota(I32, (N_BUCKETS, LANES), 0)
        cols = lax.broadcasted_iota(I32, (N_BUCKETS, LANES), 1)
        acc = jnp.zeros((N_BUCKETS, LANES), F32)
        for g in range(len(DILATED_GROUPS)):
            bk_p, bk_c = bkt_ref[g, 0], bkt_ref[g, 1]
            for hh in range(HEADS_PER_GROUP):
                h = g * HEADS_PER_GROUP + hh
                d_p, d_c = db_ref[h, 0], db_ref[h, 1]
                for b in range(N_BUCKETS):
                    s = jnp.sum(jnp.where(bk_p == b, d_p, 0.0)) + jnp.sum(jnp.where(bk_c == b, d_c, 0.0))
                    acc = jnp.where(jnp.logical_and(rows == b, cols == h), s, acc)
        out_ref[...] = acc

    return pl.pallas_call(
        body, in_specs=[pl.BlockSpec(memory_space=pltpu.VMEM)] * 2, out_specs=pl.BlockSpec(memory_space=pltpu.VMEM),
        out_shape=jax.ShapeDtypeStruct((N_BUCKETS, LANES), F32), name=name,
    )(dbias, buckets)


def _rwkv_chunk(dot, s0, r, lw, k, v, a, b):
    def each(f, *lists):
        return [f(*xs) for xs in zip(*lists)]

    def mm(mode):
        return lambda p, q: dot(p, q, mode, True)

    def mul(p, q):
        return p * q

    def add(p, q):
        return p + q

    c, width = r[0].shape
    lane_a = (lax.broadcasted_iota(I32, (1, width), 1) < width // 2).astype(F32)
    time_a = (lax.broadcasted_iota(I32, (1, 2 * c), 1) < c).astype(F32)
    ti = lax.broadcasted_iota(I32, (c, 2 * c), 0)
    si = lax.broadcasted_iota(I32, (c, 2 * c), 1)
    si = jnp.where(si < c, si, si - c)
    incl, strict = si <= ti, si < ti
    ones_incl = (lax.broadcasted_iota(I32, (c, c), 1) <= lax.broadcasted_iota(I32, (c, c), 0)).astype(F32)
    same_head = _pair_block_mask(width)

    def by_head(x):
        return jnp.concatenate([x * lane_a, x * (1.0 - lane_a)], axis=0)

    def by_block(p):
        return jnp.concatenate([p * time_a, p * (1.0 - time_a)], axis=0)

    cum = each(lambda x: dot(ones_incl, x, 'nn', True), lw)
    w_incl = each(jnp.exp, cum)
    w_prev = each(lambda cu, x: jnp.exp(cu - x), cum, lw)
    w_inv = each(lambda cu: jnp.exp(-cu), cum)
    w_end = each(lambda x: jnp.exp(jnp.sum(x, axis=0, keepdims=True)), lw)
    a_t, r_t, b_t, k_t = each(mul, a, w_prev), each(mul, r, w_incl), each(mul, b, w_inv), each(mul, k, w_inv)
    b_h, k_h, v_h = each(by_head, b_t), each(by_head, k_t), each(by_head, v)
    l_ab = each(lambda p, q: jnp.where(strict, dot(p, q, 'nt', True), 0.0), a_t, b_h)
    l_ak = each(lambda p, q: jnp.where(strict, dot(p, q, 'nt', True), 0.0), a_t, k_h)
    u = each(add, each(mm('nt'), a_t, s0), each(mm('nn'), l_ak, v_h))
    u = each(add, u, each(mm('nn'), l_ab, each(by_head, u)))
    power = l_ab
    for _ in range(int(math.log2(c)) - 1):
        power = each(mm('nn'), power, each(by_block, power))
        u = each(add, u, each(mm('nn'), power, each(by_head, u)))
    m_rb = each(lambda p, q: jnp.where(incl, dot(p, q, 'nt', True), 0.0), r_t, b_h)
    m_rk = each(lambda p, q: jnp.where(incl, dot(p, q, 'nt', True), 0.0), r_t, k_h)
    y = each(add, each(add, each(mm('nt'), r_t, s0), each(mm('nn'), m_rb, each(by_head, u))), each(mm('nn'), m_rk, v_h))
    outer = each(lambda uu, vv, bb, kk: dot(jnp.concatenate([uu, vv], axis=0), jnp.concatenate([bb, kk], axis=0),
                                             'tn', True), u, v, b_t, k_t)
    s1 = each(lambda s, o, we: (s + jnp.where(same_head, o, 0.0)) * we, s0, outer, w_end)
    return y, s1


def _pair_block_mask(width):
    rows = lax.broadcasted_iota(I32, (width, width), 0) < width // 2
    cols = lax.broadcasted_iota(I32, (width, width), 1) < width // 2
    return rows == cols


def _rwkv_tiling(t, width):
    pair = 2 * RWKV_HEAD
    n_pairs = width // pair
    assert n_pairs * pair == width and t % RWKV_CHUNK == 0
    return pair, n_pairs, math.gcd(n_pairs, RWKV_PAIRS_PER_STEP), t // RWKV_CHUNK


def rwkv_fwd(r, lw, k, v, a, b, name, gather=()):
    t, width = r.shape
    pair, n_pairs, ps, nc = _rwkv_tiling(t, width)
    c = RWKV_CHUNK
    steps = n_pairs // ps * nc
    ng = len(gather)
    row = pl.BlockSpec((c, ps * pair), lambda i, j: (j, i))
    cols = [slice(q * pair, (q + 1) * pair) for q in range(ps)]

    def body(*refs):
        ins, x_refs = refs[:6], refs[6:6 + ng]
        y_ref, s0_ref = refs[6 + ng:8 + ng]
        out_refs, state, sems = refs[8 + ng:8 + 2 * ng], refs[8 + 2 * ng], refs[9 + 2 * ng:]
        step = pl.program_id(0) * nc + pl.program_id(1)
        if ng:
            start, forward, finish = _gather_schedule(x_refs, out_refs, *sems)
            pl.when(step == 0)(start)
            pl.when(step == steps - 1 - steps // 8)(forward)

        @pl.when(pl.program_id(1) == 0)
        def _():
            state[...] = jnp.zeros_like(state)

        s0 = [state[q] for q in range(ps)]
        y, s1 = _rwkv_chunk(_dot, s0, *[[ref[:, cs] for cs in cols] for ref in ins])
        for q in range(ps):
            s0_ref[q] = s0[q]
            y_ref[:, cols[q]] = y[q]
            state[q] = s1[q]
        if ng:
            pl.when(step == steps - 1)(finish)

    return pl.pallas_call(
        body, grid=(n_pairs // ps, nc), in_specs=[row] * 6 + [_ANY] * ng,
        out_specs=[row, pl.BlockSpec((ps, None, pair, pair), lambda i, j: (i, j, 0, 0))] + [_ANY] * ng,
        out_shape=[jax.ShapeDtypeStruct((t, width), F32), jax.ShapeDtypeStruct((n_pairs, nc, pair, pair), F32)]
        + [jax.ShapeDtypeStruct((N_DEV,) + g.shape, g.dtype) for g in gather],
        scratch_shapes=[pltpu.VMEM((ps, pair, pair), F32)] + (_exchange_scratch(ng) if ng else []),
        compiler_params=_params(("arbitrary", "arbitrary")), name=name,
    )(r, lw, k, v, a, b, *gather)


def rwkv_bwd(r, lw, k, v, a, b, s0, dy, name, scatter=()):
    t, width = r.shape
    pair, n_pairs, ps, nc = _rwkv_tiling(t, width)
    c = RWKV_CHUNK
    steps = n_pairs // ps * nc
    ns = len(scatter)
    row = pl.BlockSpec((c, ps * pair), lambda i, j: (nc - 1 - j, i))
    st = pl.BlockSpec((ps, None, pair, pair), lambda i, j: (i, nc - 1 - j, 0, 0))
    cols = [slice(q * pair, (q + 1) * pair) for q in range(ps)]

    def body(*refs):
        ins, s0_ref, dy_ref, send_refs = refs[:6], refs[6], refs[7], refs[8:8 + ns]
        grad_refs, recv_refs = refs[8 + ns:14 + ns], refs[14 + ns:14 + 2 * ns]
        dstate, sems = refs[14 + 2 * ns], refs[15 + 2 * ns:]
        step = pl.program_id(0) * nc + pl.program_id(1)
        if ns:
            start, finish = _scatter_schedule(send_refs, recv_refs, *sems)
            pl.when(step == 0)(start)

        @pl.when(pl.program_id(1) == 0)
        def _():
            dstate[...] = jnp.zeros_like(dstate)

        pairs = range(ps)
        _, vjp = jax.vjp(functools.partial(_rwkv_chunk, _dot_ad), [s0_ref[q] for q in pairs],
                         *[[ref[:, cs] for cs in cols] for ref in ins])
        grads = vjp(([dy_ref[:, cs] for cs in cols], [dstate[q] for q in pairs]))
        same_head = _pair_block_mask(pair)
        for q in pairs:
            dstate[q] = jnp.where(same_head, grads[0][q], 0.0)
            for ref, g in zip(grad_refs, grads[1:]):
                ref[:, cols[q]] = g[q]
        if ns:
            pl.when(step == steps - 1)(finish)

    return pl.pallas_call(
        body, grid=(n_pairs // ps, nc), in_specs=[row] * 6 + [st, row] + [_ANY] * ns,
        out_specs=[row] * 6 + [_ANY] * ns,
        out_shape=[jax.ShapeDtypeStruct((t, width), F32)] * 6 + [jax.ShapeDtypeStruct(s.shape, s.dtype) for s in scatter],
        scratch_shapes=[pltpu.VMEM((ps, pair, pair), F32)] + (_exchange_scratch(ns) if ns else []),
        compiler_params=_params(("arbitrary", "arbitrary")), name=name,
    )(r, lw, k, v, a, b, s0, dy, *scatter)


_ANY = pl.BlockSpec(memory_space=pl.ANY)


def _exchange_scratch(n_arrays):
    return [pltpu.SemaphoreType.DMA((n_arrays, N_DEV - 1)), pltpu.SemaphoreType.DMA((n_arrays, N_DEV - 1)),
            pltpu.SemaphoreType.DMA((n_arrays,))]


def _gather_schedule(x_refs, out_refs, send_sems, recv_sems, local_sems):
    x, y, c = lax.axis_index("x"), lax.axis_index("y"), lax.axis_index("c")
    me, sibling = (x, y, c), (x, y, 1 - c)
    chips = [(1 - x, y), (x, 1 - y), (1 - x, 1 - y)]
    arrays = range(len(x_refs))

    def slot(a, pos):
        return out_refs[a].at[4 * pos[0] + 2 * pos[1] + pos[2]]

    def copy(a, i, block, to, src=None):
        return pltpu.make_async_remote_copy(
            src_ref=slot(a, block) if src is None else src, dst_ref=slot(a, block), send_sem=send_sems.at[a, i],
            recv_sem=recv_sems.at[a, i], device_id=to, device_id_type=pl.DeviceIdType.MESH)

    mine = [pltpu.make_async_copy(x_refs[a], slot(a, me), local_sems.at[a]) for a in arrays]
    first = [[copy(a, 0, me, sibling, src=x_refs[a])]
             + [copy(a, 1 + j, me, (*chip, c), src=x_refs[a]) for j, chip in enumerate(chips)] for a in arrays]
    passed = [[copy(a, 4 + j, (*chip, c), sibling) for j, chip in enumerate(chips)] for a in arrays]

    def start():
        for a in arrays:
            mine[a].start()
            for cp in first[a]:
                cp.start()

    def forward():
        for j, chip in enumerate(chips):
            for a in arrays:
                copy(a, 1 + j, (*chip, c), me).wait_recv()
                passed[a][j].start()

    def finish():
        for a in arrays:
            copy(a, 0, sibling, me).wait_recv()
            for j, chip in enumerate(chips):
                copy(a, 4 + j, (*chip, 1 - c), me).wait_recv()
            for cp in first[a] + passed[a]:
                cp.wait_send()
            mine[a].wait()

    return start, forward, finish


def _scatter_schedule(in_refs, out_refs, send_sems, recv_sems, local_sems):
    x, y, c = lax.axis_index("x"), lax.axis_index("y"), lax.axis_index("c")
    my_chip = 2 * x + y
    mine, remote = [], []
    for a, (src, dst) in enumerate(zip(in_refs, out_refs)):
        mine.append(pltpu.make_async_copy(src.at[my_chip], dst.at[my_chip], local_sems.at[a]))
        for i in range(1, N_CHIPS):
            px, py = x ^ (i >> 1), y ^ (i & 1)
            remote.append(pltpu.make_async_remote_copy(
                src_ref=src.at[2 * px + py], dst_ref=dst.at[my_chip], send_sem=send_sems.at[a, i - 1],
                recv_sem=recv_sems.at[a, i - 1], device_id=(px, py, c), device_id_type=pl.DeviceIdType.MESH))

    def start():
        for cp in mine + remote:
            cp.start()

    def finish():
        for cp in remote:
            cp.wait_recv()
        for cp in remote:
            cp.wait_send()
        for cp in mine:
            cp.wait()

    return start, finish


def pair_exchange(parts, name):
    n = len(parts)

    def body(*refs):
        x, y, c = lax.axis_index("x"), lax.axis_index("y"), lax.axis_index("c")
        send_sems, recv_sems = refs[2 * n:]
        copies = [pltpu.make_async_remote_copy(
            src_ref=refs[a].at[q, 1 - c], dst_ref=refs[n + a].at[q], send_sem=send_sems.at[a, q],
            recv_sem=recv_sems.at[a, q], device_id=(x, y, 1 - c), device_id_type=pl.DeviceIdType.MESH)
            for a in range(n) for q in range(N_CHIPS)]
        for cp in copies:
            cp.start()
        for cp in copies:
            cp.wait_recv()
        for cp in copies:
            cp.wait_send()

    return pl.pallas_call(
        body, in_specs=[_ANY] * n, out_specs=[_ANY] * n,
        out_shape=[jax.ShapeDtypeStruct((N_CHIPS,) + s.shape[2:], s.dtype) for s in parts],
        scratch_shapes=[pltpu.SemaphoreType.DMA((n, N_CHIPS)), pltpu.SemaphoreType.DMA((n, N_CHIPS))], name=name,
    )(*parts)


def pair_add(mine, theirs, core, name):
    _, _, k, n = mine.shape
    tc = _pick(n, (2048, 1024, 512))
    tr = _row_tile(k, tc * 3 * jnp.dtype(mine.dtype).itemsize)

    def body(core_ref, a_ref, b_ref, o_ref):
        o_ref[...] = (a_ref[...].astype(F32) + b_ref[...].astype(F32)).astype(o_ref.dtype)

    one = pl.BlockSpec((None, tr, tc), lambda q, i, j, core_ref: (q, i, j))
    grid_spec = pltpu.PrefetchScalarGridSpec(
        num_scalar_prefetch=1, grid=(N_CHIPS, k // tr, n // tc),
        in_specs=[pl.BlockSpec((None, None, tr, tc), lambda q, i, j, core_ref: (q, core_ref[0], i, j)), one],
        out_specs=one)
    return pl.pallas_call(
        body, grid_spec=grid_spec, out_shape=jax.ShapeDtypeStruct(theirs.shape, BF16),
        compiler_params=_params(("parallel", "parallel", "parallel")), name=name,
    )(core, mine, theirs)


def all_gather_many(shards, name):
    n = len(shards)

    def body(*refs):
        start, forward, finish = _gather_schedule(refs[:n], refs[n:2 * n], *refs[2 * n:])
        start()
        forward()
        finish()

    return pl.pallas_call(
        body, in_specs=[_ANY] * n, out_specs=[_ANY] * n,
        out_shape=[jax.ShapeDtypeStruct((N_DEV,) + s.shape, s.dtype) for s in shards],
        scratch_shapes=_exchange_scratch(n), name=name,
    )(*shards)


def project_and_gather(h, w_all, shards):
    return matmul(h, w_all, 'nn', F32, "proj_in", exchange=('gather', shards))


def input_grad_and_scatter(d_parts, w_all, offsets, parts):
    return matmul(d_parts, [w_all] * len(d_parts), 'nt', F32, "d_h_in", b_col_offsets=offsets,
                  exchange=('scatter', parts))


def scan_and_gather(scan_in, shards):
    y, s0, *gathered = rwkv_fwd(*scan_in, name="rwkv_scan", gather=shards)
    return y, s0, gathered


def scan_bwd_and_scatter(scan_in, s0, dy, parts):
    res = rwkv_bwd(*scan_in, s0, dy, name="rwkv_scan_bwd", scatter=parts)
    return res[:6], res[6:]


def adamw_shard(parts, w, m, v, name):
    _, k, n = w.shape
    slots = parts.shape[0]
    tc = _pick(n, (2048, 1024, 512))
    tr = _row_tile(k, tc * (slots * jnp.dtype(parts.dtype).itemsize + 7 * 4))

    def body(p_ref, w_ref, m_ref, v_ref, g_ref, d_ref, nm_ref, nv_ref):
        _adamw_block(p_ref, w_ref, m_ref, v_ref, g_ref, d_ref, nm_ref, nv_ref)

    one = pl.BlockSpec((None, tr, tc), lambda i, j: (0, i, j))
    return pl.pallas_call(
        body, grid=(k // tr, n // tc), in_specs=[pl.BlockSpec((slots, tr, tc), lambda i, j: (0, i, j))] + [one] * 3,
        out_specs=[one] * 4, out_shape=[jax.ShapeDtypeStruct(w.shape, F32)] * 4,
        compiler_params=_params(("parallel", "parallel")), name=name,
    )(parts, w, m, v)


def _adamw_block(p_ref, w_ref, m_ref, v_ref, g_ref, d_ref, nm_ref, nv_ref):
    g = p_ref[0].astype(F32)
    for j in range(1, p_ref.shape[0]):
        g = g + p_ref[j].astype(F32)
    new_m = ADAM_B1 * m_ref[...] + (1.0 - ADAM_B1) * g
    new_v = ADAM_B2 * v_ref[...] + (1.0 - ADAM_B2) * jnp.square(g)
    m_hat = new_m / (1.0 - ADAM_B1 ** ADAM_STEP)
    v_hat = new_v / (1.0 - ADAM_B2 ** ADAM_STEP)
    g_ref[...] = g
    d_ref[...] = -ADAM_LR * (m_hat / (jnp.sqrt(v_hat) + ADAM_EPS) + ADAM_WD * w_ref[...])
    nm_ref[...] = new_m
    nv_ref[...] = new_v


def adamw(parts, w, m, v, name):
    rows = w.shape[0]
    tile = _row_tile(rows, N_DEV * LANES * jnp.dtype(parts.dtype).itemsize + 7 * LANES * 4)

    def body(p_ref, w_ref, m_ref, v_ref, g_ref, d_ref, nm_ref, nv_ref):
        _adamw_block(p_ref, w_ref, m_ref, v_ref, g_ref, d_ref, nm_ref, nv_ref)

    flat = pl.BlockSpec((tile, LANES), lambda i: (i, 0))
    return pl.pallas_call(
        body, grid=(rows // tile,), in_specs=[pl.BlockSpec((N_DEV, tile, LANES), lambda i: (0, i, 0))] + [flat] * 3,
        out_specs=[flat] * 4, out_shape=[jax.ShapeDtypeStruct(w.shape, F32)] * 4,
        compiler_params=_params(("parallel",)), name=name,
    )(parts, w, m, v)


def _part_rows(n_elems):
    return -(-n_elems // (PACK_ROWS * LANES)) * PACK_ROWS


def _pack(arrays, dtype, lead=()):
    parts, layout, off = [], [], 0
    for arr in arrays:
        n = math.prod(arr.shape[len(lead):])
        rows = _part_rows(n)
        flat = arr.reshape(lead + (n,)).astype(dtype)
        flat = jnp.pad(flat, [(0, 0)] * len(lead) + [(0, rows * LANES - n)])
        parts.append(flat.reshape(lead + (rows, LANES)))
        layout.append((off, rows))
        off += rows
    total = -(-off // 1024) * 1024
    if total > off:
        parts.append(jnp.zeros(lead + (total - off, LANES), dtype))
    return jnp.concatenate(parts, axis=len(lead)), layout


def _unpack(buf, layout, shapes, lead=()):
    out = []
    for (off, rows), shape in zip(layout, shapes):
        n = math.prod(shape)
        piece = lax.slice_in_dim(buf, off, off + rows, axis=len(lead))
        out.append(piece.reshape(lead + (rows * LANES,))[..., :n].reshape(lead + tuple(shape)))
    return out


def _split_shards(full, axis):
    if axis == 0:
        return full.reshape((N_DEV, full.shape[0] // N_DEV, full.shape[1]))
    return full.reshape((full.shape[0], N_DEV, full.shape[1] // N_DEV)).transpose(1, 0, 2)


def _join_shards(shards, axis):
    if axis == 0:
        return shards.reshape((-1, shards.shape[2]))
    return shards.transpose(1, 0, 2).reshape((shards.shape[1], -1))


def _shift_down(t):
    return jnp.pad(t, ((1, 0), (0, 0)))[:-1]


def _shift_up(t):
    return jnp.pad(t, ((0, 1), (0, 0)))[1:]


def kernel(x, p, norm_mix, w_in, q_gain, k_gain, rel_bias, w_attn_up, shift_mix, w0, w_decay_up, a0, w_aaa_up, w_gate_up, k_k, k_a, r_k, gn_w, gn_b, w_rwkv_up, w_out, norm_mlp, w_mlp_in, w_mlp_out, norm_ple, w_ple_gate, w_ple_proj, loss_target, m_norm_mix, m_w_in, m_q_gain, m_k_gain, m_rel_bias, m_w_attn_up, m_shift_mix, m_w0, m_w_decay_up, m_a0, m_w_aaa_up, m_w_gate_up, m_k_k, m_k_a, m_r_k, m_gn_w, m_gn_b, m_w_rwkv_up, m_w_out, m_norm_mlp, m_w_mlp_in, m_w_mlp_out, m_norm_ple, m_w_ple_gate, m_w_ple_proj, v_norm_mix, v_w_in, v_q_gain, v_k_gain, v_rel_bias, v_w_attn_up, v_shift_mix, v_w0, v_w_decay_up, v_a0, v_w_aaa_up, v_w_gate_up, v_k_k, v_k_a, v_r_k, v_gn_w, v_gn_b, v_w_rwkv_up, v_w_out, v_norm_mlp, v_w_mlp_in, v_w_mlp_out, v_norm_ple, v_w_ple_gate, v_w_ple_proj):
    given = dict(locals())
    xs = x[0]
    t_len, d_model = xs.shape
    target = loss_target[0]
    p_bf = p[0, 0].astype(BF16)
    rw_width = w0.shape[1]
    n_rheads = rw_width // RWKV_HEAD
    lora_d, lora_a, lora_g = w_decay_up.shape[1], w_aaa_up.shape[1], w_gate_up.shape[1]
    z_width = shift_mix.shape[1]
    z_pad = -(-z_width // LANES) * LANES
    qkv_width = 3 * ATTN_WIDTH
    assert z_width == 3 * rw_width + lora_d + lora_a + lora_g
    assert N_DEV * w_in.shape[2] == qkv_width + z_width + 2 * d_model
    for window, dilation in DILATED_GROUPS:
        assert window // dilation == ATTN_BLOCK and t_len % (dilation * ATTN_BLOCK) == 0

    shard_bf = {n: given[n][0].astype(BF16) for n in BIG}
    early = ['w_in', 'w_decay_up', 'w_aaa_up', 'w_gate_up']
    during_proj = ['w_mlp_out']
    during_scan = [n for n in BIG if n not in early + during_proj]
    late = during_proj + during_scan
    full = {n: _join_shards(g, 1) for n, g in zip(early, all_gather_many([shard_bf[n] for n in early], "gather_w_in"))}
    z_end = qkv_width + z_width
    w_all = jnp.concatenate([full['w_in'][:, :z_end], jnp.zeros((d_model, z_pad - z_width), BF16),
                             full['w_in'][:, z_end:]], axis=1)
    gates_at = qkv_width + z_pad

    (h_in,) = rowmap_fwd(lambda a, g: st_norm(a, g)[1:], [xs], [norm_mix], [BF16], "norm_in")
    proj, proj_gathered = project_and_gather(h_in, w_all, [shard_bf[n] for n in during_proj])
    qkv, z = proj[:, :qkv_width], proj[:, qkv_width:gates_at]
    gate_a, gate_r = proj[:, gates_at:gates_at + d_model], proj[:, gates_at + d_model:]

    q_raw = qkv[:, :ATTN_WIDTH].reshape(t_len * N_HEADS, HEAD_DIM)
    k_raw = qkv[:, ATTN_WIDTH:2 * ATTN_WIDTH].reshape(t_len * N_HEADS, HEAD_DIM)
    q_n, k_n = rowmap_fwd(st_qk_norm, [q_raw, k_raw], [q_gain, k_gain], [F32, F32], "qk_norm")
    q_n, k_n = q_n.reshape(t_len, ATTN_WIDTH), k_n.reshape(t_len, ATTN_WIDTH)
    buckets = _bucket_tables()
    bias = bias_fwd(rel_bias, buckets, "attn_bias")

    att_in, att_o, att_l = [], [], []
    for g, (_, dilation) in enumerate(DILATED_GROUPS):
        hs = slice(g * HEADS_PER_GROUP, (g + 1) * HEADS_PER_GROUP)
        cols = slice(g * ATTN_OUT, (g + 1) * ATTN_OUT)
        v_g = qkv[:, 2 * ATTN_WIDTH + g * ATTN_OUT:2 * ATTN_WIDTH + (g + 1) * ATTN_OUT]
        ops = ((q_n[:, cols], 0), (k_n[:, cols], 0), (v_g, 0), bias[hs, 0], bias[hs, 1])
        o_g, l_g = attn_fwd(*ops, dilation, name=f"attn_fwd_{g}")
        att_in.append(ops)
        att_o.append(o_g)
        att_l.append(l_g)
    (attn,) = rowmap_fwd(st_merge, att_o + att_l, [], [BF16], "attn_merge")

    c0 = rw_width
    cuts = [0, c0, 2 * c0, 3 * c0, 3 * c0 + lora_d, 3 * c0 + lora_d + lora_a, z_width]
    z_parts = [z[:, lo:hi] for lo, hi in zip(cuts[:-1], cuts[1:])]
    z_prev = [_shift_down(t) for t in z_parts]
    mixes = [shift_mix[:, lo:hi] for lo, hi in zip(cuts[:-1], cuts[1:])]
    pre_params = mixes + [w0, full['w_decay_up'], a0, full['w_aaa_up'], full['w_gate_up'], k_k, k_a]
    pre_out = rowmap_fwd(functools.partial(_st_rwkv_pre, _dot), z_parts + z_prev, pre_params, [F32] * 7, "rwkv_pre")
    r_s, lw_s, k_s, v_s, kk0_s, a_s, g_s = pre_out

    aa_s, bb_s = rowmap_fwd(functools.partial(_st_rwkv_kk, _head_sum), [kk0_s, a_s], [], [F32, F32], "rwkv_kk")
    scan_in = [r_s, lw_s, k_s, v_s, aa_s, bb_s]
    y_t, s0_h, scan_gathered = scan_and_gather(scan_in, [shard_bf[n] for n in during_scan])
    late_gathered = list(proj_gathered) + list(scan_gathered)
    wt = {n: g if SHARD_AXIS[n] == 1 else g.reshape(-1, g.shape[2]) for n, g in zip(late, late_gathered)}
    post_params = [gn_w, gn_b, r_k.reshape(1, rw_width)]
    post_in = [y_t, r_s, k_s, v_s, g_s]
    (rw,) = rowmap_fwd(functools.partial(_st_rwkv_post, _head_sum), post_in, post_params, [BF16], "rwkv_post")
    attn_d = matmul(attn, wt['w_attn_up'], 'nn', F32, "attn_up", b_shards=True)
    rwkv_d = matmul(rw, wt['w_rwkv_up'], 'nn', F32, "rwkv_up", b_shards=True)

    (merged,) = rowmap_fwd(st_gate, [gate_a, gate_r, attn_d, rwkv_d], [], [BF16], "gate_merge")
    mix_out = matmul(merged, wt['w_out'], 'nn', F32, "out_proj")
    x1, h_mlp = rowmap_fwd(st_res_norm, [xs, mix_out], [norm_mlp], [F32, BF16], "res_norm_mlp")
    u, act = matmul(h_mlp, wt['w_mlp_in'], 'nn', [F32, BF16], "mlp_in", b_shards=True,
                    epilogue=lambda acc: (acc,) + st_relu2(acc))
    mlp_out = matmul(act, wt['w_mlp_out'], 'nn', F32, "mlp_out")
    x2, h_ple = rowmap_fwd(st_res_norm, [x1, mlp_out], [norm_ple], [F32, BF16], "res_norm_ple")
    pg = matmul(h_ple, wt['w_ple_gate'], 'nn', F32, "ple_gate")
    pp = matmul(p_bf, wt['w_ple_proj'], 'nn', F32, "ple_proj", b_shards=True)
    dy, d_pg, d_pp, loss_local = loss_head(x2, pg, pp, target, "loss_head")

    def row_cut(full_grad):
        return full_grad.reshape(N_DEV, full_grad.shape[0] // N_DEV, full_grad.shape[1])

    grads, sends = {}, {}
    sends['w_ple_gate'] = row_cut(matmul(h_ple, d_pg, 'tn', BF16, "d_w_ple_gate"))
    sends['w_ple_proj'] = matmul(p_bf, d_pp, 'tn', BF16, "d_w_ple_proj", out_shards=True)
    d_h_ple = matmul(d_pg, wt['w_ple_gate'], 'nt', F32, "d_h_ple")
    (d_x2, d_x2_bf), (grads['norm_ple'],) = rowmap_bwd(
        st_res_norm, [x1, mlp_out], [norm_ple], [dy, d_h_ple], [F32, BF16], "res_norm_ple_bwd")
    sends['w_mlp_out'] = row_cut(matmul(act, d_x2_bf, 'tn', BF16, "d_w_mlp_out"))
    (d_u,) = matmul(d_x2_bf, wt['w_mlp_out'], 'nt', [BF16], "d_mlp_act", extras=[u],
                    epilogue=lambda d_act, u_blk: (d_act * (2.0 * jnp.maximum(u_blk, 0.0)),))
    sends['w_mlp_in'] = matmul(h_mlp, d_u, 'tn', BF16, "d_w_mlp_in", out_shards=True)
    d_h_mlp = matmul(d_u, wt['w_mlp_in'], 'nt', F32, "d_h_mlp", b_shards=True)
    (d_x1, d_x1_bf), (grads['norm_mlp'],) = rowmap_bwd(
        st_res_norm, [xs, mix_out], [norm_mlp], [d_x2, d_h_mlp], [F32, BF16], "res_norm_mlp_bwd")

    sends['w_out'] = row_cut(matmul(merged, d_x1_bf, 'tn', BF16, "d_w_out"))
    d_merged = matmul(d_x1_bf, wt['w_out'], 'nt', F32, "d_merged")
    (d_gate_a, d_gate_r, d_attn_d, d_rwkv_d), _ = rowmap_bwd(
        st_gate, [gate_a, gate_r, attn_d, rwkv_d], [], [d_merged], [BF16] * 4, "gate_merge_bwd")
    sends['w_attn_up'] = matmul(attn, d_attn_d, 'tn', BF16, "d_w_attn_up", out_shards=True)
    sends['w_rwkv_up'] = matmul(rw, d_rwkv_d, 'tn', BF16, "d_w_rwkv_up", out_shards=True)
    d_attn = matmul(d_attn_d, wt['w_attn_up'], 'nt', F32, "d_attn", b_shards=True)
    d_rw = matmul(d_rwkv_d, wt['w_rwkv_up'], 'nt', F32, "d_rw", b_shards=True)

    (d_y, d_r1, d_k1, d_v1, d_g), (grads['gn_w'], grads['gn_b'], grads['r_k']) = rowmap_bwd(
        functools.partial(_st_rwkv_post, _head_sum_ad), post_in, post_params, [d_rw], [F32] * 5, "rwkv_post_bwd")
    core = lax.axis_index("c").astype(I32).reshape(1)

    def chip_partials(names, tag):
        mine = [sends[n].reshape((N_CHIPS, 2) + sends[n].shape[1:]) for n in names]
        theirs = pair_exchange(mine, "pair_grads_" + tag)
        return [pair_add(m, t, core, "pair_add_" + n) for n, m, t in zip(names, mine, theirs)]

    scan_grads, late_received = scan_bwd_and_scatter(scan_in, s0_h, d_y, chip_partials(late, "late"))
    received = dict(zip(late, late_received))
    d_r2, d_lw, d_k2, d_v2, d_aa, d_bb = scan_grads
    (d_kk0, d_a), _ = rowmap_bwd(functools.partial(_st_rwkv_kk, _head_sum_ad), [kk0_s, a_s], [], [d_aa, d_bb],
                                 [F32, F32], "rwkv_kk_bwd")
    pre_cts = [[d_r1, d_r2], d_lw, [d_k1, d_k2], [d_v1, d_v2], d_kk0, d_a, d_g]
    d_zp, d_pre = rowmap_bwd(functools.partial(_st_rwkv_pre, _dot_ad), z_parts + z_prev, pre_params, pre_cts,
                             [F32] * 12, "rwkv_pre_bwd")
    grads['shift_mix'] = jnp.concatenate(d_pre[:6], axis=1)
    grads['w0'], d_w_decay, grads['a0'], d_w_aaa, d_w_gate, grads['k_k'], grads['k_a'] = d_pre[6:]
    for name, full_grad in (('w_decay_up', d_w_decay), ('w_aaa_up', d_w_aaa), ('w_gate_up', d_w_gate)):
        sends[name] = _split_shards(full_grad, 1).astype(BF16)
    z_fill = [jnp.zeros((t_len, z_pad - z_width), F32)] if z_pad > z_width else []
    d_z_cur = jnp.concatenate(d_zp[:6] + z_fill, axis=1)
    d_z_prev = _shift_up(jnp.concatenate(d_zp[6:] + z_fill, axis=1))
    (d_z,) = rowmap_fwd(st_add, [d_z_cur, d_z_prev], [], [BF16], "d_z_sum")

    d_merge, _ = rowmap_bwd(st_merge, att_o + att_l, [], [d_attn], [F32] * 6, "attn_merge_bwd")
    d_qn, d_kn, d_vs, d_bias = [], [], [], []
    for g, (_, dilation) in enumerate(DILATED_GROUPS):
        dq, dk, dv, dbp, dbc = attn_bwd(*att_in[g], d_merge[g], d_merge[3 + g], dilation, name=f"attn_bwd_{g}")
        d_qn.append(dq)
        d_kn.append(dk)
        d_vs.append(dv)
        d_bias.append(jnp.stack([dbp, dbc], axis=1))
    d_table = bias_bwd(jnp.concatenate(d_bias, axis=0), buckets, "attn_bias_bwd")
    grads['rel_bias'] = d_table[:, :N_HEADS]
    d_qn = jnp.concatenate(d_qn, axis=1).reshape(t_len * N_HEADS, HEAD_DIM)
    d_kn = jnp.concatenate(d_kn, axis=1).reshape(t_len * N_HEADS, HEAD_DIM)
    (d_q, d_k), (grads['q_gain'], grads['k_gain']) = rowmap_bwd(
        st_qk_norm, [q_raw, k_raw], [q_gain, k_gain], [d_qn, d_kn], [BF16, BF16], "qk_norm_bwd")

    d_pieces = ([d_q.reshape(t_len, ATTN_WIDTH), d_k.reshape(t_len, ATTN_WIDTH)] + [t.astype(BF16) for t in d_vs]
                + [d_z, d_gate_a, d_gate_r])
    piece_cols = [0, ATTN_WIDTH] + [2 * ATTN_WIDTH + g * ATTN_OUT for g in range(len(DILATED_GROUPS))]
    piece_cols += [qkv_width, gates_at, gates_at + d_model]
    d_w_pieces = [matmul(h_in, piece, 'tn', BF16, f"d_w_in_{i}") for i, piece in enumerate(d_pieces)]
    d_w_pieces[-3] = d_w_pieces[-3][:, :z_width]
    sends['w_in'] = _split_shards(jnp.concatenate(d_w_pieces, axis=1), 1)
    d_h_in, early_received = input_grad_and_scatter(d_pieces, w_all, piece_cols, chip_partials(early, "early"))
    received.update(zip(early, early_received))
    (grad_x,), (grads['norm_mix'],) = rowmap_bwd(st_norm, [xs], [norm_mix], [d_x1, d_h_in], [F32], "norm_in_bwd")

    by_name = {n: adamw_shard(received[n], given[n], given['m_' + n], given['v_' + n], "adamw_" + n) for n in BIG}
    pk = lambda prefix: _pack([given[prefix + n] for n in SMALL], F32)[0]
    small_buf, small_layout = _pack([grads[n].reshape(given[n].shape) for n in SMALL], F32)
    (small_all,) = all_gather_many([small_buf], "gather_small_grads")
    small_out = adamw(small_all, pk(''), pk('m_'), pk('v_'), "adamw_replicated")
    small_shapes = [given[n].shape for n in SMALL]
    by_name.update(zip(SMALL, zip(*[_unpack(buf, small_layout, small_shapes) for buf in small_out])))
    loss = lax.psum(loss_local, MESH_AXES)
    return (loss, grad_x[None], *[by_name[n][0] for n in WEIGHTS], *[by_name[n][1] for n in WEIGHTS],
            *[by_name[n][2] for n in WEIGHTS], *[by_name[n][3] for n in WEIGHTS])
```

```python
import functools
import math

import jax
import jax.numpy as jnp
from jax import lax
from jax.experimental import pallas as pl
from jax.experimental.pallas import tpu as pltpu

F32 = jnp.float32
BF16 = jnp.bfloat16
I32 = jnp.int32

N_DEV = 8
N_CHIPS = 4
MESH_AXES = ("x", "y", "c")
LANES = 128
PACK_ROWS = 16
VMEM_LIMIT_BYTES = 48 * 2**20
ROW_BLOCK_BYTES = 10 * 2**20

HEAD_DIM = 128
ATTN_BLOCK = 128
HEADS_PER_GROUP = 4
DILATED_GROUPS = ((128, 1), (512, 4), (2048, 16))
N_HEADS = HEADS_PER_GROUP * len(DILATED_GROUPS)
ATTN_WIDTH = N_HEADS * HEAD_DIM
ATTN_OUT = HEADS_PER_GROUP * HEAD_DIM
N_BUCKETS = 32
MAX_DISTANCE = 2048
RWKV_HEAD = 64
RWKV_CHUNK = 64
RWKV_PAIRS_PER_STEP = 8
RMS_EPS = 1e-6
GN_EPS = 64e-5
NEG_INF = -1e30

ADAM_LR = 0.001
ADAM_B1 = 0.9
ADAM_B2 = 0.999
ADAM_EPS = 1e-08
ADAM_WD = 0.01
ADAM_STEP = 10

WEIGHTS = ['norm_mix', 'w_in', 'q_gain', 'k_gain', 'rel_bias', 'w_attn_up', 'shift_mix', 'w0', 'w_decay_up', 'a0',
           'w_aaa_up', 'w_gate_up', 'k_k', 'k_a', 'r_k', 'gn_w', 'gn_b', 'w_rwkv_up', 'w_out', 'norm_mlp', 'w_mlp_in',
           'w_mlp_out', 'norm_ple', 'w_ple_gate', 'w_ple_proj']
SHARD_AXIS = {'w_in': 1, 'w_attn_up': 1, 'w_decay_up': 1, 'w_aaa_up': 1, 'w_gate_up': 1, 'w_rwkv_up': 1, 'w_out': 0,
              'w_mlp_in': 1, 'w_mlp_out': 0, 'w_ple_gate': 0, 'w_ple_proj': 1}
BIG = [n for n in WEIGHTS if n in SHARD_AXIS]
SMALL = [n for n in WEIGHTS if n not in SHARD_AXIS]


def _params(sem):
    return pltpu.CompilerParams(dimension_semantics=sem, vmem_limit_bytes=VMEM_LIMIT_BYTES)


_DN = {'nn': (((1,), (0,)), ((), ())), 'nt': (((1,), (1,)), ((), ())), 'tn': (((0,), (0,)), ((), ()))}


def _dot(a, b, mode, exact):
    if exact:
        return lax.dot_general(a, b, _DN[mode], precision=lax.Precision.HIGH, preferred_element_type=F32)
    return lax.dot_general(a.astype(BF16), b.astype(BF16), _DN[mode], preferred_element_type=F32)


@functools.partial(jax.custom_vjp, nondiff_argnums=(2, 3))
def _dot_ad(a, b, mode, exact):
    return _dot(a, b, mode, exact)


def _dot_ad_fwd(a, b, mode, exact):
    return _dot(a, b, mode, exact), (a, b)


def _dot_ad_bwd(mode, exact, res, g):
    a, b = res
    if mode == 'nn':
        return _dot(g, b, 'nt', exact), _dot(a, g, 'tn', exact)
    if mode == 'nt':
        return _dot(g, b, 'nn', exact), _dot(g, a, 'tn', exact)
    return _dot(b, g, 'nt', exact), _dot(a, g, 'nn', exact)


_dot_ad.defvjp(_dot_ad_fwd, _dot_ad_bwd)


def _pick(n, cands):
    for c in cands:
        if n % c == 0:
            return c
    return n


def matmul(a, b, mode, out_dtype, name, b_shards=False, out_shards=False, b_col_offsets=None, epilogue=None,
           extras=(), exchange=None):
    a_list = list(a) if isinstance(a, (list, tuple)) else [a]
    b_list = list(b) if isinstance(b, (list, tuple)) else [b]
    seg = len(a_list)
    assert all(t.dtype == BF16 for t in a_list + b_list), name
    assert seg == 1 or (mode == 'nt' and not b_shards), name
    m = a_list[0].shape[1] if mode == 'tn' else a_list[0].shape[0]
    ks = [t.shape[0] if mode == 'tn' else t.shape[1] for t in a_list]
    b0 = b_list[0]
    b_rows, b_cols = (b0.shape[1], N_DEV * b0.shape[2]) if b_shards else b0.shape
    n = b_rows if mode == 'nt' else b_cols
    if seg == 1 and b_col_offsets is None:
        assert (b_cols if mode == 'nt' else b_rows) == ks[0], (name, a_list[0].shape, b0.shape)
    offsets = list(b_col_offsets) if b_col_offsets is not None else [0] * seg
    tm = _pick(m, (1024, 512, 256, 128))
    tn = _pick(n // N_DEV if (out_shards or (b_shards and mode != 'nt')) else n, (1024, 512, 256, 128))
    k_units = [kk // N_DEV if (b_shards and mode == 'nt') else kk for kk in ks] + [o for o in offsets if o]
    tk = next((c for c in (2048, 1024, 512, 256, 128) if all(u % c == 0 for u in k_units)), k_units[0])
    nks = [kk // tk for kk in ks]
    starts = [sum(nks[:s]) for s in range(seg)]
    nk = sum(nks)
    grid = (m // tm, n // tn, nk)
    total_steps = grid[0] * grid[1] * nk
    kind, moved = exchange if exchange is not None else (None, [])
    nx, ne = len(moved), len(extras)
    out_dtypes = list(out_dtype) if isinstance(out_dtype, (list, tuple)) else [out_dtype]
    no = len(out_dtypes)

    def body(*refs):
        a_refs, b_refs, x_refs = refs[:seg], refs[seg:2 * seg], refs[2 * seg:2 * seg + ne]
        pos = 2 * seg + ne
        moved_in, o_refs = refs[pos:pos + nx], refs[pos + nx:pos + nx + no]
        pos += nx + no
        moved_out, acc_ref, sems = refs[pos:pos + nx], refs[pos + nx], refs[pos + nx + 1:]
        kk = pl.program_id(2)
        step = (pl.program_id(0) * grid[1] + pl.program_id(1)) * nk + kk
        if kind == 'gather':
            start, forward, finish = _gather_schedule(moved_in, moved_out, *sems)
            pl.when(step == 0)(start)
            pl.when(step == total_steps - 1 - total_steps // 8)(forward)
        elif kind == 'scatter':
            start, finish = _scatter_schedule(moved_in, moved_out, *sems)
            pl.when(step == 0)(start)
        elif kind == 'pair':
            start, finish = _pair_schedule(moved_in, moved_out, *sems)
            pl.when(step == 0)(start)

        @pl.when(kk == 0)
        def _():
            acc_ref[...] = jnp.zeros_like(acc_ref)

        for s in range(seg):
            def accumulate(s=s):
                acc_ref[...] += lax.dot_general(a_refs[s][...], b_refs[s][...], _DN[mode], preferred_element_type=F32)

            if seg == 1:
                accumulate()
            else:
                pl.when(jnp.logical_and(kk >= starts[s], kk < starts[s] + nks[s]))(accumulate)

        @pl.when(kk == nk - 1)
        def _():
            acc = acc_ref[...]
            outs = (acc,) if epilogue is None else epilogue(acc, *[x[...] for x in x_refs])
            for r, v in zip(o_refs, outs):
                r[...] = v.astype(r.dtype)

        if kind is not None:
            pl.when(step == total_steps - 1)(finish)

    def k_of(kk, s):
        return kk if seg == 1 else jnp.clip(kk - starts[s], 0, nks[s] - 1)

    a_specs, b_specs = [], []
    for s in range(seg):
        off = offsets[s] // tk
        if mode == 'tn':
            a_specs.append(pl.BlockSpec((tk, tm), lambda i, j, kk, s=s: (k_of(kk, s), i)))
        else:
            a_specs.append(pl.BlockSpec((tm, tk), lambda i, j, kk, s=s: (i, k_of(kk, s))))
        if mode == 'nt':
            if b_shards:
                per = b0.shape[2] // tk
                b_specs.append(pl.BlockSpec((None, tn, tk), lambda i, j, kk: (kk // per, j, kk % per)))
            else:
                b_specs.append(pl.BlockSpec((tn, tk), lambda i, j, kk, s=s, off=off: (j, off + k_of(kk, s))))
        else:
            if b_shards:
                per = b0.shape[2] // tn
                b_specs.append(pl.BlockSpec((None, tk, tn), lambda i, j, kk: (j // per, kk, j % per)))
            else:
                b_specs.append(pl.BlockSpec((tk, tn), lambda i, j, kk: (kk, j)))
    tile = pl.BlockSpec((tm, tn), lambda i, j, kk: (i, j))
    if out_shards:
        assert epilogue is None
        per_o = n // N_DEV // tn
        o_specs = [pl.BlockSpec((None, tm, tn), lambda i, j, kk: (j // per_o, i, j % per_o))]
        o_shapes = [jax.ShapeDtypeStruct((N_DEV, m, n // N_DEV), out_dtypes[0])]
    else:
        o_specs = [tile] * no
        o_shapes = [jax.ShapeDtypeStruct((m, n), d) for d in out_dtypes]
    landing = {'gather': lambda t: (N_DEV,) + t.shape, 'scatter': lambda t: t.shape,
               'pair': lambda t: (N_CHIPS,) + t.shape[2:]}
    moved_shapes = [jax.ShapeDtypeStruct(landing[kind](t), t.dtype) for t in moved]
    sem_scratch = [] if not nx else (_pair_scratch(nx) if kind == 'pair' else _exchange_scratch(nx))
    res = pl.pallas_call(
        body, grid=grid, in_specs=a_specs + b_specs + [tile] * ne + [_ANY] * nx,
        out_specs=o_specs + [_ANY] * nx, out_shape=o_shapes + moved_shapes,
        scratch_shapes=[pltpu.VMEM((tm, tn), F32)] + sem_scratch,
        compiler_params=_params(("arbitrary",) * 3 if nx else ("parallel", "parallel", "arbitrary")), name=name,
    )(*a_list, *b_list, *extras, *moved)
    result = res[0] if (epilogue is None) else list(res[:no])
    return (result, list(res[no:])) if nx else result


def _row_bytes(shape, dtype):
    dims = list(shape[1:])
    dims[-1] = -(-dims[-1] // LANES) * LANES
    return math.prod(dims) * jnp.dtype(dtype).itemsize


def _row_tile(n, row_bytes):
    t = 1024
    while t > 16 and (n % t or t * row_bytes > ROW_BLOCK_BYTES):
        t //= 2
    assert n % t == 0, (n, t)
    return t


def rowmap(fn, tiled, bcast, out_tiled, out_acc, name):
    n = tiled[0].shape[0]
    tile = _row_tile(n, sum(_row_bytes(t.shape, t.dtype) for t in list(tiled) + list(out_tiled)))
    n_in, n_out = len(tiled) + len(bcast), len(out_tiled)

    def body(*refs):
        outs, accs = fn(*[r[...] for r in refs[:n_in]])
        assert len(outs) == n_out and len(accs) == len(out_acc), name
        for r, v in zip(refs[n_in:n_in + n_out], outs):
            r[...] = v.astype(r.dtype)
        acc_refs = refs[n_in + n_out:]
        if acc_refs:
            @pl.when(pl.program_id(0) == 0)
            def _():
                for r, v in zip(acc_refs, accs):
                    r[...] = v.astype(r.dtype)

            @pl.when(pl.program_id(0) != 0)
            def _():
                for r, v in zip(acc_refs, accs):
                    r[...] += v.astype(r.dtype)

    def tspec(s):
        nd = len(s.shape)
        return pl.BlockSpec((tile,) + tuple(s.shape[1:]), lambda i, nd=nd: (i,) + (0,) * (nd - 1))

    def bspec(s):
        nd = len(s.shape)
        return pl.BlockSpec(tuple(s.shape), lambda i, nd=nd: (0,) * nd)

    res = pl.pallas_call(
        body, grid=(n // tile,),
        in_specs=[tspec(t) for t in tiled] + [bspec(t) for t in bcast],
        out_specs=[tspec(t) for t in out_tiled] + [bspec(t) for t in out_acc],
        out_shape=list(out_tiled) + list(out_acc),
        compiler_params=_params(("arbitrary",)), name=name,
    )(*tiled, *bcast)
    return list(res[:n_out]), list(res[n_out:])


def rowmap_fwd(fwd, tiled, bcast, out_dtypes, name):
    shapes = jax.eval_shape(fwd, *tiled, *bcast)
    out_tiled = [jax.ShapeDtypeStruct(s.shape, d) for s, d in zip(shapes, out_dtypes)]
    outs, _ = rowmap(lambda *blk: (fwd(*[b.astype(F32) for b in blk]), ()), tiled, bcast, out_tiled, [], name)
    return outs


def rowmap_bwd(fwd, tiled, bcast, cts, want, name):
    cts = [[] if c is None else (list(c) if isinstance(c, (list, tuple)) else [c]) for c in cts]
    flat_cts = [c for group in cts for c in group]
    nt_, nc_ = len(tiled), len(flat_cts)

    def fn(*blk):
        ins = [b.astype(F32) for b in blk[:nt_]] + [b.astype(F32) for b in blk[nt_ + nc_:]]
        ctb = list(blk[nt_:nt_ + nc_])
        outs, vjp = jax.vjp(fwd, *ins)
        full = []
        for o, group in zip(outs, cts):
            acc = jnp.zeros_like(o)
            for _ in group:
                acc = acc + ctb.pop(0).astype(F32)
            full.append(acc)
        g = vjp(tuple(full))
        return [g[i] for i in range(nt_) if want[i] is not None], list(g[nt_:])

    out_tiled = [jax.ShapeDtypeStruct(t.shape, w) for t, w in zip(tiled, want) if w is not None]
    out_acc = [jax.ShapeDtypeStruct(b.shape, F32) for b in bcast]
    return rowmap(fn, list(tiled) + flat_cts, bcast, out_tiled, out_acc, name)


def _rms(x, gain):
    return x * lax.rsqrt(jnp.mean(jnp.square(x), axis=-1, keepdims=True) + RMS_EPS) * gain


def _sigmoid(x):
    return 1.0 / (1.0 + jnp.exp(-x))


def _softplus(x):
    return jnp.maximum(x, 0.0) + jnp.log(1.0 + jnp.exp(-jnp.abs(x)))


def st_norm(x, gain):
    return x, _rms(x, gain)


def st_res_norm(x, delta, gain):
    y = x + delta
    return y, _rms(y, gain)


def st_qk_norm(q, k, q_gain, k_gain):
    return _rms(q, q_gain), _rms(k, k_gain)


def st_merge(o0, o1, o2, l0, l1, l2):
    m = jnp.maximum(jnp.maximum(l0, l1), l2)
    e0, e1, e2 = jnp.exp(l0 - m), jnp.exp(l1 - m), jnp.exp(l2 - m)
    return ((e0 * o0 + e1 * o1 + e2 * o2) / (e0 + e1 + e2),)


def _st_rwkv_pre(dot, zr, zk, zv, xw, xa, xg, pr, pk, pv, pw, pa, pg, mr, mk, mv, mw, ma, mg,
                 w0, w_decay, a0, w_aaa, w_gate, k_k, k_a):
    def shift(cur, prev, mix):
        return cur + mix * (prev - cur)

    r, k, v = shift(zr, pr, mr), shift(zk, pk, mk), shift(zv, pv, mv)
    xw, xa, xg = shift(xw, pw, mw), shift(xa, pa, ma), shift(xg, pg, mg)
    w = -_softplus(-(w0 + dot(jnp.tanh(xw), w_decay, 'nn', False))) - 0.5
    a = _sigmoid(a0 + dot(xa, w_aaa, 'nn', False))
    g = dot(_sigmoid(xg), w_gate, 'nn', False)
    log_decay = -jnp.exp(w)
    return r, log_decay, k * (1.0 + (a - 1.0) * k_a), v, k * k_k, a, g


def _head_sum(x):
    joins = _pair_block_mask(2 * RWKV_HEAD).astype(F32)
    tile = joins.shape[0]
    return jnp.concatenate([_dot(x[:, lo:lo + tile], joins, 'nn', True) for lo in range(0, x.shape[1], tile)], axis=1)


@jax.custom_vjp
def _head_sum_ad(x):
    return _head_sum(x)


_head_sum_ad.defvjp(lambda x: (_head_sum(x), None), lambda _, g: (_head_sum(g),))


def _st_rwkv_kk(head_sum, kk0, a):
    kk = kk0 / jnp.maximum(jnp.sqrt(head_sum(jnp.square(kk0))), 1e-12)
    return -kk, kk * a


def _st_rwkv_post(head_sum, y, r, k, v, g, gn_w, gn_b, r_k):
    mu = head_sum(y) * (1.0 / RWKV_HEAD)
    var = head_sum(jnp.square(y - mu)) * (1.0 / RWKV_HEAD)
    out = (y - mu) * lax.rsqrt(var + GN_EPS) * gn_w + gn_b
    out = out + head_sum(r * k * r_k) * v
    return (out * g,)


def st_gate(g0, g1, attn_d, rwkv_d):
    return (_sigmoid(g0) * attn_d + _sigmoid(g1) * rwkv_d,)


def st_relu2(u):
    return (jnp.square(jnp.maximum(u, 0.0)),)


def st_add(a, b):
    return (a + b,)


def loss_head(x2, pg, pp, target, name):
    d_model = x2.shape[1]

    def fn(x2, pg, pp, tgt):
        s = _sigmoid(pg)
        err = x2 + s * pp - tgt
        dy = err * (1.0 / d_model)
        part = 0.5 * jnp.sum(jnp.square(err)) * (1.0 / d_model)
        return [dy, dy * pp * s * (1.0 - s), dy * s], [jnp.full((8, LANES), part, F32)]

    sds = jax.ShapeDtypeStruct
    outs, accs = rowmap(fn, [x2, pg, pp, target], [],
                        [sds(x2.shape, F32), sds(x2.shape, BF16), sds(x2.shape, BF16)], [sds((8, LANES), F32)], name)
    return outs[0], outs[1], outs[2], accs[0][0, 0]


def _attn_block(dot, q, kp, kc, vp, vc, bp, bc, prev_offset):
    blk = q.shape[0]
    qi = lax.broadcasted_iota(I32, (blk, blk), 0)
    ki = lax.broadcasted_iota(I32, (blk, blk), 1)
    mask_c = ki <= qi
    mask_p = ki >= qi + prev_offset
    scale = HEAD_DIM ** -0.5
    s_c = jnp.where(mask_c, dot(q, kc, 'nt', False) * scale + bc, NEG_INF)
    s_p = jnp.where(mask_p, dot(q, kp, 'nt', False) * scale + bp, NEG_INF)
    m = lax.stop_gradient(jnp.maximum(jnp.max(s_c, axis=1, keepdims=True), jnp.max(s_p, axis=1, keepdims=True)))
    e_c = jnp.where(mask_c, jnp.exp(s_c - m), 0.0)
    e_p = jnp.where(mask_p, jnp.exp(s_p - m), 0.0)
    l = jnp.sum(e_c, axis=1, keepdims=True) + jnp.sum(e_p, axis=1, keepdims=True)
    o = (dot(e_c, vc, 'nn', False) + dot(e_p, vp, 'nn', False)) / l
    return o, jnp.broadcast_to(m + jnp.log(l), o.shape)


class _ClassView:
    def __init__(self, tokens, dilation, first_col):
        self.view = tokens.reshape(tokens.shape[0] // dilation, dilation * tokens.shape[1])
        self.tiles, self.first = tokens.shape[1] // HEAD_DIM, first_col // HEAD_DIM

    def spec(self, shift, nb):
        tiles, first = self.tiles, self.first
        return pl.BlockSpec((ATTN_BLOCK, HEAD_DIM),
                            lambda h, r, n: (jnp.clip(n + shift, 0, nb - 1), r * tiles + first + h))


_BIAS_SPEC = pl.BlockSpec((None, ATTN_BLOCK, ATTN_BLOCK), lambda h, r, n: (h, 0, 0))


def attn_fwd(q, k, v, bp, bc, dilation, name):
    t_len = q[0].shape[0]
    nb = t_len // dilation // ATTN_BLOCK
    qv, kv, vv = (_ClassView(arr, dilation, col) for arr, col in (q, k, v))
    o_shape = jax.ShapeDtypeStruct((t_len // dilation, dilation * ATTN_OUT), F32)
    o_spec = pl.BlockSpec((ATTN_BLOCK, HEAD_DIM), lambda h, r, n: (n, r * HEADS_PER_GROUP + h))

    def body(q_ref, kp_ref, kc_ref, vp_ref, vc_ref, bp_ref, bc_ref, o_ref, l_ref):
        off = jnp.where(pl.program_id(2) > 0, 0, ATTN_BLOCK)
        o, l = _attn_block(_dot, q_ref[...], kp_ref[...], kc_ref[...], vp_ref[...], vc_ref[...], bp_ref[...],
                           bc_ref[...], off)
        o_ref[...] = o
        l_ref[...] = l

    o, l = pl.pallas_call(
        body, grid=(HEADS_PER_GROUP, dilation, nb),
        in_specs=[qv.spec(0, nb), kv.spec(-1, nb), kv.spec(0, nb), vv.spec(-1, nb), vv.spec(0, nb),
                  _BIAS_SPEC, _BIAS_SPEC],
        out_specs=[o_spec, o_spec], out_shape=[o_shape] * 2,
        compiler_params=_params(("parallel", "parallel", "parallel")), name=name,
    )(qv.view, kv.view, kv.view, vv.view, vv.view, bp, bc)
    return o.reshape(t_len, ATTN_OUT), l.reshape(t_len, ATTN_OUT)


def attn_bwd(q, k, v, bp, bc, do, dl, dilation, name):
    t_len = q[0].shape[0]
    blk = ATTN_BLOCK
    nb = t_len // dilation // blk
    qv, kv, vv = (_ClassView(arr, dilation, col) for arr, col in (q, k, v))
    dov, dlv = _ClassView(do, dilation, 0), _ClassView(dl, dilation, 0)
    cur, bias = dov.spec(0, nb), _BIAS_SPEC

    def body(q_ref, kp_ref, kc_ref, vp_ref, vc_ref, bp_ref, bc_ref, do_ref, dl_ref,
             dq_ref, dkp_ref, dkc_ref, dvp_ref, dvc_ref, dbp_ref, dbc_ref):
        off = jnp.where(pl.program_id(2) > 0, 0, blk)
        f = functools.partial(_attn_block, _dot_ad, prev_offset=off)
        _, vjp = jax.vjp(f, q_ref[...], kp_ref[...], kc_ref[...], vp_ref[...], vc_ref[...], bp_ref[...], bc_ref[...])
        dq, dkp, dkc, dvp, dvc, dbp, dbc = vjp((do_ref[...], dl_ref[...]))
        dq_ref[...] = dq
        dkp_ref[...] = dkp
        dkc_ref[...] = dkc
        dvp_ref[...] = dvp
        dvc_ref[...] = dvc
        first = jnp.logical_and(pl.program_id(1) == 0, pl.program_id(2) == 0)

        @pl.when(first)
        def _():
            dbp_ref[...] = dbp
            dbc_ref[...] = dbc

        @pl.when(jnp.logical_not(first))
        def _():
            dbp_ref[...] += dbp
            dbc_ref[...] += dbc

    blocks = jax.ShapeDtypeStruct(dov.view.shape, F32)
    grid = (HEADS_PER_GROUP, dilation, nb)
    dq, dkp, dkc, dvp, dvc, dbp, dbc = pl.pallas_call(
        body, grid=grid,
        in_specs=[qv.spec(0, nb), kv.spec(-1, nb), kv.spec(0, nb), vv.spec(-1, nb), vv.spec(0, nb), bias, bias,
                  cur, dlv.spec(0, nb)],
        out_specs=[cur] * 5 + [bias] * 2, out_shape=[blocks] * 5 + [jax.ShapeDtypeStruct(bp.shape, F32)] * 2,
        compiler_params=_params(("arbitrary", "arbitrary", "arbitrary")), name=name,
    )(qv.view, kv.view, kv.view, vv.view, vv.view, bp, bc, dov.view, dlv.view)

    nxt = dov.spec(1, nb)

    def add_body(kc_ref, kp_ref, vc_ref, vp_ref, dk_ref, dv_ref):
        has_next = (pl.program_id(2) + 1 < nb).astype(F32)
        dk_ref[...] = kc_ref[...] + kp_ref[...] * has_next
        dv_ref[...] = vc_ref[...] + vp_ref[...] * has_next

    dk, dv = pl.pallas_call(
        add_body, grid=grid, in_specs=[cur, nxt, cur, nxt], out_specs=[cur, cur], out_shape=[blocks] * 2,
        compiler_params=_params(("parallel", "parallel", "parallel")), name=name + "_kv",
    )(dkc, dkp, dvc, dvp)
    return [t.reshape(t_len, ATTN_OUT) for t in (dq, dk, dv)] + [dbp, dbc]


def _t5_bucket(dist):
    max_exact = N_BUCKETS // 2
    d_f = jnp.maximum(dist, 1).astype(F32)
    large = max_exact + (jnp.log(d_f / max_exact) / math.log(MAX_DISTANCE / max_exact)
                         * (N_BUCKETS - max_exact)).astype(I32)
    large = jnp.minimum(large, N_BUCKETS - 1)
    return jnp.where(dist < max_exact, dist, large)


def _bucket_tables():
    blk = ATTN_BLOCK
    qi = jnp.arange(blk)[:, None]
    ki = jnp.arange(blk)[None, :]
    out = []
    for _, dilation in DILATED_GROUPS:
        rel_p = jnp.maximum(blk + qi - ki, 0) * dilation
        rel_c = jnp.maximum(qi - ki, 0) * dilation
        out.append(jnp.stack([_t5_bucket(rel_p), _t5_bucket(rel_c)]))
    return jnp.stack(out).astype(I32)


def bias_fwd(table, buckets, name):
    blk = ATTN_BLOCK

    def body(tab_ref, bkt_ref, out_ref):
        for g in range(len(DILATED_GROUPS)):
            for half in range(2):
                bk = bkt_ref[g, half]
                for hh in range(HEADS_PER_GROUP):
                    h = g * HEADS_PER_GROUP + hh
                    acc = jnp.zeros((blk, blk), F32)
                    for b in range(N_BUCKETS):
                        acc = jnp.where(bk == b, tab_ref[b, h], acc)
                    out_ref[h, half] = acc

    return pl.pallas_call(
        body, in_specs=[pl.BlockSpec(memory_space=pltpu.SMEM), pl.BlockSpec(memory_space=pltpu.VMEM)],
        out_specs=pl.BlockSpec(memory_space=pltpu.VMEM),
        out_shape=jax.ShapeDtypeStruct((N_HEADS, 2, blk, blk), F32), name=name,
    )(table, buckets)


def bias_bwd(dbias, buckets, name):
    def body(db_ref, bkt_ref, out_ref):
        rows = lax.broadcasted_iota(I32, (N_BUCKETS, LANES), 0)
        cols = lax.broadcasted_iota(I32, (N_BUCKETS, LANES), 1)
        acc = jnp.zeros((N_BUCKETS, LANES), F32)
        for g in range(len(DILATED_GROUPS)):
            bk_p, bk_c = bkt_ref[g, 0], bkt_ref[g, 1]
            for hh in range(HEADS_PER_GROUP):
                h = g * HEADS_PER_GROUP + hh
                d_p, d_c = db_ref[h, 0], db_ref[h, 1]
                for b in range(N_BUCKETS):
                    s = jnp.sum(jnp.where(bk_p == b, d_p, 0.0)) + jnp.sum(jnp.where(bk_c == b, d_c, 0.0))
                    acc = jnp.where(jnp.logical_and(rows == b, cols == h), s, acc)
        out_ref[...] = acc

    return pl.pallas_call(
        body, in_specs=[pl.BlockSpec(memory_space=pltpu.VMEM)] * 2, out_specs=pl.BlockSpec(memory_space=pltpu.VMEM),
        out_shape=jax.ShapeDtypeStruct((N_BUCKETS, LANES), F32), name=name,
    )(dbias, buckets)


def _rwkv_chunk(dot, s0, r, lw, k, v, a, b):
    def each(f, *lists):
        return [f(*xs) for xs in zip(*lists)]

    def mm(mode):
        return lambda p, q: dot(p, q, mode, True)

    def mul(p, q):
        return p * q

    def add(p, q):
        return p + q

    c, width = r[0].shape
    lane_a = (lax.broadcasted_iota(I32, (1, width), 1) < width // 2).astype(F32)
    time_a = (lax.broadcasted_iota(I32, (1, 2 * c), 1) < c).astype(F32)
    ti = lax.broadcasted_iota(I32, (c, 2 * c), 0)
    si = lax.broadcasted_iota(I32, (c, 2 * c), 1)
    si = jnp.where(si < c, si, si - c)
    incl, strict = si <= ti, si < ti
    ones_incl = (lax.broadcasted_iota(I32, (c, c), 1) <= lax.broadcasted_iota(I32, (c, c), 0)).astype(F32)
    same_head = _pair_block_mask(width)

    def by_head(x):
        return jnp.concatenate([x * lane_a, x * (1.0 - lane_a)], axis=0)

    def by_block(p):
        return jnp.concatenate([p * time_a, p * (1.0 - time_a)], axis=0)

    cum = each(lambda x: dot(ones_incl, x, 'nn', True), lw)
    w_incl = each(jnp.exp, cum)
    w_prev = each(lambda cu, x: jnp.exp(cu - x), cum, lw)
    w_inv = each(lambda cu: jnp.exp(-cu), cum)
    w_end = each(lambda x: jnp.exp(jnp.sum(x, axis=0, keepdims=True)), lw)
    a_t, r_t, b_t, k_t = each(mul, a, w_prev), each(mul, r, w_incl), each(mul, b, w_inv), each(mul, k, w_inv)
    b_h, k_h, v_h = each(by_head, b_t), each(by_head, k_t), each(by_head, v)
    l_ab = each(lambda p, q: jnp.where(strict, dot(p, q, 'nt', True), 0.0), a_t, b_h)
    l_ak = each(lambda p, q: jnp.where(strict, dot(p, q, 'nt', True), 0.0), a_t, k_h)
    u = each(add, each(mm('nt'), a_t, s0), each(mm('nn'), l_ak, v_h))
    u = each(add, u, each(mm('nn'), l_ab, each(by_head, u)))
    power = l_ab
    for _ in range(int(math.log2(c)) - 1):
        power = each(mm('nn'), power, each(by_block, power))
        u = each(add, u, each(mm('nn'), power, each(by_head, u)))
    m_rb = each(lambda p, q: jnp.where(incl, dot(p, q, 'nt', True), 0.0), r_t, b_h)
    m_rk = each(lambda p, q: jnp.where(incl, dot(p, q, 'nt', True), 0.0), r_t, k_h)
    y = each(add, each(add, each(mm('nt'), r_t, s0), each(mm('nn'), m_rb, each(by_head, u))), each(mm('nn'), m_rk, v_h))
    outer = each(lambda uu, vv, bb, kk: dot(jnp.concatenate([uu, vv], axis=0), jnp.concatenate([bb, kk], axis=0),
                                             'tn', True), u, v, b_t, k_t)
    s1 = each(lambda s, o, we: (s + jnp.where(same_head, o, 0.0)) * we, s0, outer, w_end)
    return y, s1


def _pair_block_mask(width):
    rows = lax.broadcasted_iota(I32, (width, width), 0) < width // 2
    cols = lax.broadcasted_iota(I32, (width, width), 1) < width // 2
    return rows == cols


def _rwkv_tiling(t, width):
    pair = 2 * RWKV_HEAD
    n_pairs = width // pair
    assert n_pairs * pair == width and t % RWKV_CHUNK == 0
    return pair, n_pairs, math.gcd(n_pairs, RWKV_PAIRS_PER_STEP), t // RWKV_CHUNK


def rwkv_fwd(r, lw, k, v, a, b, name, gather=()):
    t, width = r.shape
    pair, n_pairs, ps, nc = _rwkv_tiling(t, width)
    c = RWKV_CHUNK
    steps = n_pairs // ps * nc
    ng = len(gather)
    row = pl.BlockSpec((c, ps * pair), lambda i, j: (j, i))
    cols = [slice(q * pair, (q + 1) * pair) for q in range(ps)]

    def body(*refs):
        ins, x_refs = refs[:6], refs[6:6 + ng]
        y_ref, s0_ref = refs[6 + ng:8 + ng]
        out_refs, state, sems = refs[8 + ng:8 + 2 * ng], refs[8 + 2 * ng], refs[9 + 2 * ng:]
        step = pl.program_id(0) * nc + pl.program_id(1)
        if ng:
            start, forward, finish = _gather_schedule(x_refs, out_refs, *sems)
            pl.when(step == 0)(start)
            pl.when(step == steps - 1 - steps // 8)(forward)

        @pl.when(pl.program_id(1) == 0)
        def _():
            state[...] = jnp.zeros_like(state)

        s0 = [state[q] for q in range(ps)]
        y, s1 = _rwkv_chunk(_dot, s0, *[[ref[:, cs] for cs in cols] for ref in ins])
        for q in range(ps):
            s0_ref[q] = s0[q]
            y_ref[:, cols[q]] = y[q]
            state[q] = s1[q]
        if ng:
            pl.when(step == steps - 1)(finish)

    return pl.pallas_call(
        body, grid=(n_pairs // ps, nc), in_specs=[row] * 6 + [_ANY] * ng,
        out_specs=[row, pl.BlockSpec((ps, None, pair, pair), lambda i, j: (i, j, 0, 0))] + [_ANY] * ng,
        out_shape=[jax.ShapeDtypeStruct((t, width), F32), jax.ShapeDtypeStruct((n_pairs, nc, pair, pair), F32)]
        + [jax.ShapeDtypeStruct((N_DEV,) + g.shape, g.dtype) for g in gather],
        scratch_shapes=[pltpu.VMEM((ps, pair, pair), F32)] + (_exchange_scratch(ng) if ng else []),
        compiler_params=_params(("arbitrary", "arbitrary")), name=name,
    )(r, lw, k, v, a, b, *gather)


def rwkv_bwd(r, lw, k, v, a, b, s0, dy, name, scatter=()):
    t, width = r.shape
    pair, n_pairs, ps, nc = _rwkv_tiling(t, width)
    c = RWKV_CHUNK
    steps = n_pairs // ps * nc
    ns = len(scatter)
    row = pl.BlockSpec((c, ps * pair), lambda i, j: (nc - 1 - j, i))
    st = pl.BlockSpec((ps, None, pair, pair), lambda i, j: (i, nc - 1 - j, 0, 0))
    cols = [slice(q * pair, (q + 1) * pair) for q in range(ps)]

    def body(*refs):
        ins, s0_ref, dy_ref, send_refs = refs[:6], refs[6], refs[7], refs[8:8 + ns]
        grad_refs, recv_refs = refs[8 + ns:14 + ns], refs[14 + ns:14 + 2 * ns]
        dstate, sems = refs[14 + 2 * ns], refs[15 + 2 * ns:]
        step = pl.program_id(0) * nc + pl.program_id(1)
        if ns:
            start, finish = _scatter_schedule(send_refs, recv_refs, *sems)
            pl.when(step == 0)(start)

        @pl.when(pl.program_id(1) == 0)
        def _():
            dstate[...] = jnp.zeros_like(dstate)

        pairs = range(ps)
        _, vjp = jax.vjp(functools.partial(_rwkv_chunk, _dot_ad), [s0_ref[q] for q in pairs],
                         *[[ref[:, cs] for cs in cols] for ref in ins])
        grads = vjp(([dy_ref[:, cs] for cs in cols], [dstate[q] for q in pairs]))
        same_head = _pair_block_mask(pair)
        for q in pairs:
            dstate[q] = jnp.where(same_head, grads[0][q], 0.0)
            for ref, g in zip(grad_refs, grads[1:]):
                ref[:, cols[q]] = g[q]
        if ns:
            pl.when(step == steps - 1)(finish)

    return pl.pallas_call(
        body, grid=(n_pairs // ps, nc), in_specs=[row] * 6 + [st, row] + [_ANY] * ns,
        out_specs=[row] * 6 + [_ANY] * ns,
        out_shape=[jax.ShapeDtypeStruct((t, width), F32)] * 6 + [jax.ShapeDtypeStruct(s.shape, s.dtype) for s in scatter],
        scratch_shapes=[pltpu.VMEM((ps, pair, pair), F32)] + (_exchange_scratch(ns) if ns else []),
        compiler_params=_params(("arbitrary", "arbitrary")), name=name,
    )(r, lw, k, v, a, b, s0, dy, *scatter)


_ANY = pl.BlockSpec(memory_space=pl.ANY)


def _exchange_scratch(n_arrays):
    return [pltpu.SemaphoreType.DMA((n_arrays, N_DEV - 1)), pltpu.SemaphoreType.DMA((n_arrays, N_DEV - 1)),
            pltpu.SemaphoreType.DMA((n_arrays,))]


def _gather_schedule(x_refs, out_refs, send_sems, recv_sems, local_sems):
    x, y, c = lax.axis_index("x"), lax.axis_index("y"), lax.axis_index("c")
    me, sibling = (x, y, c), (x, y, 1 - c)
    chips = [(1 - x, y), (x, 1 - y), (1 - x, 1 - y)]
    arrays = range(len(x_refs))

    def slot(a, pos):
        return out_refs[a].at[4 * pos[0] + 2 * pos[1] + pos[2]]

    def copy(a, i, block, to, src=None):
        return pltpu.make_async_remote_copy(
            src_ref=slot(a, block) if src is None else src, dst_ref=slot(a, block), send_sem=send_sems.at[a, i],
            recv_sem=recv_sems.at[a, i], device_id=to, device_id_type=pl.DeviceIdType.MESH)

    mine = [pltpu.make_async_copy(x_refs[a], slot(a, me), local_sems.at[a]) for a in arrays]
    first = [[copy(a, 0, me, sibling, src=x_refs[a])]
             + [copy(a, 1 + j, me, (*chip, c), src=x_refs[a]) for j, chip in enumerate(chips)] for a in arrays]
    passed = [[copy(a, 4 + j, (*chip, c), sibling) for j, chip in enumerate(chips)] for a in arrays]

    def start():
        for a in arrays:
            mine[a].start()
            for cp in first[a]:
                cp.start()

    def forward():
        for j, chip in enumerate(chips):
            for a in arrays:
                copy(a, 1 + j, (*chip, c), me).wait_recv()
                passed[a][j].start()

    def finish():
        for a in arrays:
            copy(a, 0, sibling, me).wait_recv()
            for j, chip in enumerate(chips):
                copy(a, 4 + j, (*chip, 1 - c), me).wait_recv()
            for cp in first[a] + passed[a]:
                cp.wait_send()
            mine[a].wait()

    return start, forward, finish


def _scatter_schedule(in_refs, out_refs, send_sems, recv_sems, local_sems):
    x, y, c = lax.axis_index("x"), lax.axis_index("y"), lax.axis_index("c")
    my_chip = 2 * x + y
    mine, remote = [], []
    for a, (src, dst) in enumerate(zip(in_refs, out_refs)):
        mine.append(pltpu.make_async_copy(src.at[my_chip], dst.at[my_chip], local_sems.at[a]))
        for i in range(1, N_CHIPS):
            px, py = x ^ (i >> 1), y ^ (i & 1)
            remote.append(pltpu.make_async_remote_copy(
                src_ref=src.at[2 * px + py], dst_ref=dst.at[my_chip], send_sem=send_sems.at[a, i - 1],
                recv_sem=recv_sems.at[a, i - 1], device_id=(px, py, c), device_id_type=pl.DeviceIdType.MESH))

    def start():
        for cp in mine + remote:
            cp.start()

    def finish():
        for cp in remote:
            cp.wait_recv()
        for cp in remote:
            cp.wait_send()
        for cp in mine:
            cp.wait()

    return start, finish


def _pair_schedule(in_refs, out_refs, send_sems, recv_sems):
    x, y, c = lax.axis_index("x"), lax.axis_index("y"), lax.axis_index("c")
    copies = [pltpu.make_async_remote_copy(
        src_ref=src.at[q, 1 - c], dst_ref=dst.at[q], send_sem=send_sems.at[a, q], recv_sem=recv_sems.at[a, q],
        device_id=(x, y, 1 - c), device_id_type=pl.DeviceIdType.MESH)
        for a, (src, dst) in enumerate(zip(in_refs, out_refs)) for q in range(N_CHIPS)]

    def start():
        for cp in copies:
            cp.start()

    def finish():
        for cp in copies:
            cp.wait_recv()
        for cp in copies:
            cp.wait_send()

    return start, finish


def _pair_scratch(n_arrays):
    return [pltpu.SemaphoreType.DMA((n_arrays, N_CHIPS)), pltpu.SemaphoreType.DMA((n_arrays, N_CHIPS))]


def pair_exchange(parts, name):
    n = len(parts)

    def body(*refs):
        start, finish = _pair_schedule(refs[:n], refs[n:2 * n], *refs[2 * n:])
        start()
        finish()

    return pl.pallas_call(
        body, in_specs=[_ANY] * n, out_specs=[_ANY] * n,
        out_shape=[jax.ShapeDtypeStruct((N_CHIPS,) + s.shape[2:], s.dtype) for s in parts],
        scratch_shapes=_pair_scratch(n), name=name,
    )(*parts)


def pair_add(mine, theirs, core, name):
    _, _, k, n = mine.shape
    tc = _pick(n, (2048, 1024, 512))
    tr = _row_tile(k, tc * 3 * jnp.dtype(mine.dtype).itemsize)

    def body(core_ref, a_ref, b_ref, o_ref):
        o_ref[...] = (a_ref[...].astype(F32) + b_ref[...].astype(F32)).astype(o_ref.dtype)

    one = pl.BlockSpec((None, tr, tc), lambda q, i, j, core_ref: (q, i, j))
    grid_spec = pltpu.PrefetchScalarGridSpec(
        num_scalar_prefetch=1, grid=(N_CHIPS, k // tr, n // tc),
        in_specs=[pl.BlockSpec((None, None, tr, tc), lambda q, i, j, core_ref: (q, core_ref[0], i, j)), one],
        out_specs=one)
    return pl.pallas_call(
        body, grid_spec=grid_spec, out_shape=jax.ShapeDtypeStruct(theirs.shape, BF16),
        compiler_params=_params(("parallel", "parallel", "parallel")), name=name,
    )(core, mine, theirs)


def all_gather_many(shards, name):
    n = len(shards)

    def body(*refs):
        start, forward, finish = _gather_schedule(refs[:n], refs[n:2 * n], *refs[2 * n:])
        start()
        forward()
        finish()

    return pl.pallas_call(
        body, in_specs=[_ANY] * n, out_specs=[_ANY] * n,
        out_shape=[jax.ShapeDtypeStruct((N_DEV,) + s.shape, s.dtype) for s in shards],
        scratch_shapes=_exchange_scratch(n), name=name,
    )(*shards)


def project_and_gather(h, w_all, shards):
    return matmul(h, w_all, 'nn', F32, "proj_in", exchange=('gather', shards))


def input_grad_and_scatter(d_parts, w_all, offsets, parts):
    return matmul(d_parts, [w_all] * len(d_parts), 'nt', F32, "d_h_in", b_col_offsets=offsets,
                  exchange=('scatter', parts))


def mlp_input_grad_and_pair(d_u, w_mlp_in, parts):
    return matmul(d_u, w_mlp_in, 'nt', F32, "d_h_mlp", b_shards=True, exchange=('pair', parts))


def scan_and_gather(scan_in, shards):
    y, s0, *gathered = rwkv_fwd(*scan_in, name="rwkv_scan", gather=shards)
    return y, s0, gathered


def scan_bwd_and_scatter(scan_in, s0, dy, parts):
    res = rwkv_bwd(*scan_in, s0, dy, name="rwkv_scan_bwd", scatter=parts)
    return res[:6], res[6:]


def adamw_shard(parts, w, m, v, name):
    _, k, n = w.shape
    slots = parts.shape[0]
    tc = _pick(n, (2048, 1024, 512))
    tr = _row_tile(k, tc * (slots * jnp.dtype(parts.dtype).itemsize + 7 * 4))

    def body(p_ref, w_ref, m_ref, v_ref, g_ref, d_ref, nm_ref, nv_ref):
        _adamw_block(p_ref, w_ref, m_ref, v_ref, g_ref, d_ref, nm_ref, nv_ref)

    one = pl.BlockSpec((None, tr, tc), lambda i, j: (0, i, j))
    return pl.pallas_call(
        body, grid=(k // tr, n // tc), in_specs=[pl.BlockSpec((slots, tr, tc), lambda i, j: (0, i, j))] + [one] * 3,
        out_specs=[one] * 4, out_shape=[jax.ShapeDtypeStruct(w.shape, F32)] * 4,
        compiler_params=_params(("parallel", "parallel")), name=name,
    )(parts, w, m, v)


def _adamw_block(p_ref, w_ref, m_ref, v_ref, g_ref, d_ref, nm_ref, nv_ref):
    g = p_ref[0].astype(F32)
    for j in range(1, p_ref.shape[0]):
        g = g + p_ref[j].astype(F32)
    new_m = ADAM_B1 * m_ref[...] + (1.0 - ADAM_B1) * g
    new_v = ADAM_B2 * v_ref[...] + (1.0 - ADAM_B2) * jnp.square(g)
    m_hat = new_m / (1.0 - ADAM_B1 ** ADAM_STEP)
    v_hat = new_v / (1.0 - ADAM_B2 ** ADAM_STEP)
    g_ref[...] = g
    d_ref[...] = -ADAM_LR * (m_hat / (jnp.sqrt(v_hat) + ADAM_EPS) + ADAM_WD * w_ref[...])
    nm_ref[...] = new_m
    nv_ref[...] = new_v


def adamw(parts, w, m, v, name):
    rows = w.shape[0]
    tile = _row_tile(rows, N_DEV * LANES * jnp.dtype(parts.dtype).itemsize + 7 * LANES * 4)

    def body(p_ref, w_ref, m_ref, v_ref, g_ref, d_ref, nm_ref, nv_ref):
        _adamw_block(p_ref, w_ref, m_ref, v_ref, g_ref, d_ref, nm_ref, nv_ref)

    flat = pl.BlockSpec((tile, LANES), lambda i: (i, 0))
    return pl.pallas_call(
        body, grid=(rows // tile,), in_specs=[pl.BlockSpec((N_DEV, tile, LANES), lambda i: (0, i, 0))] + [flat] * 3,
        out_specs=[flat] * 4, out_shape=[jax.ShapeDtypeStruct(w.shape, F32)] * 4,
        compiler_params=_params(("parallel",)), name=name,
    )(parts, w, m, v)


def _part_rows(n_elems):
    return -(-n_elems // (PACK_ROWS * LANES)) * PACK_ROWS


def _pack(arrays, dtype, lead=()):
    parts, layout, off = [], [], 0
    for arr in arrays:
        n = math.prod(arr.shape[len(lead):])
        rows = _part_rows(n)
        flat = arr.reshape(lead + (n,)).astype(dtype)
        flat = jnp.pad(flat, [(0, 0)] * len(lead) + [(0, rows * LANES - n)])
        parts.append(flat.reshape(lead + (rows, LANES)))
        layout.append((off, rows))
        off += rows
    total = -(-off // 1024) * 1024
    if total > off:
        parts.append(jnp.zeros(lead + (total - off, LANES), dtype))
    return jnp.concatenate(parts, axis=len(lead)), layout


def _unpack(buf, layout, shapes, lead=()):
    out = []
    for (off, rows), shape in zip(layout, shapes):
        n = math.prod(shape)
        piece = lax.slice_in_dim(buf, off, off + rows, axis=len(lead))
        out.append(piece.reshape(lead + (rows * LANES,))[..., :n].reshape(lead + tuple(shape)))
    return out


def _split_shards(full, axis):
    if axis == 0:
        return full.reshape((N_DEV, full.shape[0] // N_DEV, full.shape[1]))
    return full.reshape((full.shape[0], N_DEV, full.shape[1] // N_DEV)).transpose(1, 0, 2)


def _join_shards(shards, axis):
    if axis == 0:
        return shards.reshape((-1, shards.shape[2]))
    return shards.transpose(1, 0, 2).reshape((shards.shape[1], -1))


def _shift_down(t):
    return jnp.pad(t, ((1, 0), (0, 0)))[:-1]


def _shift_up(t):
    return jnp.pad(t, ((0, 1), (0, 0)))[1:]


def kernel(x, p, norm_mix, w_in, q_gain, k_gain, rel_bias, w_attn_up, shift_mix, w0, w_decay_up, a0, w_aaa_up, w_gate_up, k_k, k_a, r_k, gn_w, gn_b, w_rwkv_up, w_out, norm_mlp, w_mlp_in, w_mlp_out, norm_ple, w_ple_gate, w_ple_proj, loss_target, m_norm_mix, m_w_in, m_q_gain, m_k_gain, m_rel_bias, m_w_attn_up, m_shift_mix, m_w0, m_w_decay_up, m_a0, m_w_aaa_up, m_w_gate_up, m_k_k, m_k_a, m_r_k, m_gn_w, m_gn_b, m_w_rwkv_up, m_w_out, m_norm_mlp, m_w_mlp_in, m_w_mlp_out, m_norm_ple, m_w_ple_gate, m_w_ple_proj, v_norm_mix, v_w_in, v_q_gain, v_k_gain, v_rel_bias, v_w_attn_up, v_shift_mix, v_w0, v_w_decay_up, v_a0, v_w_aaa_up, v_w_gate_up, v_k_k, v_k_a, v_r_k, v_gn_w, v_gn_b, v_w_rwkv_up, v_w_out, v_norm_mlp, v_w_mlp_in, v_w_mlp_out, v_norm_ple, v_w_ple_gate, v_w_ple_proj):
    given = dict(locals())
    xs = x[0]
    t_len, d_model = xs.shape
    target = loss_target[0]
    p_bf = p[0, 0].astype(BF16)
    rw_width = w0.shape[1]
    n_rheads = rw_width // RWKV_HEAD
    lora_d, lora_a, lora_g = w_decay_up.shape[1], w_aaa_up.shape[1], w_gate_up.shape[1]
    z_width = shift_mix.shape[1]
    z_pad = -(-z_width // LANES) * LANES
    qkv_width = 3 * ATTN_WIDTH
    assert z_width == 3 * rw_width + lora_d + lora_a + lora_g
    assert N_DEV * w_in.shape[2] == qkv_width + z_width + 2 * d_model
    for window, dilation in DILATED_GROUPS:
        assert window // dilation == ATTN_BLOCK and t_len % (dilation * ATTN_BLOCK) == 0

    shard_bf = {n: given[n][0].astype(BF16) for n in BIG}
    early = ['w_in', 'w_decay_up', 'w_aaa_up', 'w_gate_up']
    during_proj = ['w_mlp_out']
    during_scan = [n for n in BIG if n not in early + during_proj]
    late = during_proj + during_scan
    full = {n: _join_shards(g, 1) for n, g in zip(early, all_gather_many([shard_bf[n] for n in early], "gather_w_in"))}
    z_end = qkv_width + z_width
    w_all = jnp.concatenate([full['w_in'][:, :z_end], jnp.zeros((d_model, z_pad - z_width), BF16),
                             full['w_in'][:, z_end:]], axis=1)
    gates_at = qkv_width + z_pad

    (h_in,) = rowmap_fwd(lambda a, g: st_norm(a, g)[1:], [xs], [norm_mix], [BF16], "norm_in")
    proj, proj_gathered = project_and_gather(h_in, w_all, [shard_bf[n] for n in during_proj])
    qkv, z = proj[:, :qkv_width], proj[:, qkv_width:gates_at]
    gate_a, gate_r = proj[:, gates_at:gates_at + d_model], proj[:, gates_at + d_model:]

    q_raw = qkv[:, :ATTN_WIDTH].reshape(t_len * N_HEADS, HEAD_DIM)
    k_raw = qkv[:, ATTN_WIDTH:2 * ATTN_WIDTH].reshape(t_len * N_HEADS, HEAD_DIM)
    q_n, k_n = rowmap_fwd(st_qk_norm, [q_raw, k_raw], [q_gain, k_gain], [F32, F32], "qk_norm")
    q_n, k_n = q_n.reshape(t_len, ATTN_WIDTH), k_n.reshape(t_len, ATTN_WIDTH)
    buckets = _bucket_tables()
    bias = bias_fwd(rel_bias, buckets, "attn_bias")

    att_in, att_o, att_l = [], [], []
    for g, (_, dilation) in enumerate(DILATED_GROUPS):
        hs = slice(g * HEADS_PER_GROUP, (g + 1) * HEADS_PER_GROUP)
        cols = slice(g * ATTN_OUT, (g + 1) * ATTN_OUT)
        v_g = qkv[:, 2 * ATTN_WIDTH + g * ATTN_OUT:2 * ATTN_WIDTH + (g + 1) * ATTN_OUT]
        ops = ((q_n[:, cols], 0), (k_n[:, cols], 0), (v_g, 0), bias[hs, 0], bias[hs, 1])
        o_g, l_g = attn_fwd(*ops, dilation, name=f"attn_fwd_{g}")
        att_in.append(ops)
        att_o.append(o_g)
        att_l.append(l_g)
    (attn,) = rowmap_fwd(st_merge, att_o + att_l, [], [BF16], "attn_merge")

    c0 = rw_width
    cuts = [0, c0, 2 * c0, 3 * c0, 3 * c0 + lora_d, 3 * c0 + lora_d + lora_a, z_width]
    z_parts = [z[:, lo:hi] for lo, hi in zip(cuts[:-1], cuts[1:])]
    z_prev = [_shift_down(t) for t in z_parts]
    mixes = [shift_mix[:, lo:hi] for lo, hi in zip(cuts[:-1], cuts[1:])]
    pre_params = mixes + [w0, full['w_decay_up'], a0, full['w_aaa_up'], full['w_gate_up'], k_k, k_a]
    pre_out = rowmap_fwd(functools.partial(_st_rwkv_pre, _dot), z_parts + z_prev, pre_params, [F32] * 7, "rwkv_pre")
    r_s, lw_s, k_s, v_s, kk0_s, a_s, g_s = pre_out

    aa_s, bb_s = rowmap_fwd(functools.partial(_st_rwkv_kk, _head_sum), [kk0_s, a_s], [], [F32, F32], "rwkv_kk")
    scan_in = [r_s, lw_s, k_s, v_s, aa_s, bb_s]
    y_t, s0_h, scan_gathered = scan_and_gather(scan_in, [shard_bf[n] for n in during_scan])
    late_gathered = list(proj_gathered) + list(scan_gathered)
    wt = {n: g if SHARD_AXIS[n] == 1 else g.reshape(-1, g.shape[2]) for n, g in zip(late, late_gathered)}
    post_params = [gn_w, gn_b, r_k.reshape(1, rw_width)]
    post_in = [y_t, r_s, k_s, v_s, g_s]
    (rw,) = rowmap_fwd(functools.partial(_st_rwkv_post, _head_sum), post_in, post_params, [BF16], "rwkv_post")
    attn_d = matmul(attn, wt['w_attn_up'], 'nn', F32, "attn_up", b_shards=True)
    rwkv_d = matmul(rw, wt['w_rwkv_up'], 'nn', F32, "rwkv_up", b_shards=True)

    (merged,) = rowmap_fwd(st_gate, [gate_a, gate_r, attn_d, rwkv_d], [], [BF16], "gate_merge")
    mix_out = matmul(merged, wt['w_out'], 'nn', F32, "out_proj")
    x1, h_mlp = rowmap_fwd(st_res_norm, [xs, mix_out], [norm_mlp], [F32, BF16], "res_norm_mlp")
    u, act = matmul(h_mlp, wt['w_mlp_in'], 'nn', [F32, BF16], "mlp_in", b_shards=True,
                    epilogue=lambda acc: (acc,) + st_relu2(acc))
    mlp_out = matmul(act, wt['w_mlp_out'], 'nn', F32, "mlp_out")
    x2, h_ple = rowmap_fwd(st_res_norm, [x1, mlp_out], [norm_ple], [F32, BF16], "res_norm_ple")
    pg = matmul(h_ple, wt['w_ple_gate'], 'nn', F32, "ple_gate")
    pp = matmul(p_bf, wt['w_ple_proj'], 'nn', F32, "ple_proj", b_shards=True)
    dy, d_pg, d_pp, loss_local = loss_head(x2, pg, pp, target, "loss_head")

    def row_cut(full_grad):
        return full_grad.reshape(N_DEV, full_grad.shape[0] // N_DEV, full_grad.shape[1])

    grads, sends = {}, {}
    sends['w_ple_gate'] = row_cut(matmul(h_ple, d_pg, 'tn', BF16, "d_w_ple_gate"))
    sends['w_ple_proj'] = matmul(p_bf, d_pp, 'tn', BF16, "d_w_ple_proj", out_shards=True)
    d_h_ple = matmul(d_pg, wt['w_ple_gate'], 'nt', F32, "d_h_ple")
    (d_x2, d_x2_bf), (grads['norm_ple'],) = rowmap_bwd(
        st_res_norm, [x1, mlp_out], [norm_ple], [dy, d_h_ple], [F32, BF16], "res_norm_ple_bwd")
    sends['w_mlp_out'] = row_cut(matmul(act, d_x2_bf, 'tn', BF16, "d_w_mlp_out"))
    (d_u,) = matmul(d_x2_bf, wt['w_mlp_out'], 'nt', [BF16], "d_mlp_act", extras=[u],
                    epilogue=lambda d_act, u_blk: (d_act * (2.0 * jnp.maximum(u_blk, 0.0)),))
    sends['w_mlp_in'] = matmul(h_mlp, d_u, 'tn', BF16, "d_w_mlp_in", out_shards=True)

    def by_chip_and_core(name):
        return sends[name].reshape((N_CHIPS, 2) + sends[name].shape[1:])

    paired_early = ['w_ple_gate', 'w_ple_proj', 'w_mlp_out', 'w_mlp_in']
    d_h_mlp, from_sibling = mlp_input_grad_and_pair(d_u, wt['w_mlp_in'], [by_chip_and_core(n) for n in paired_early])
    from_sibling = dict(zip(paired_early, from_sibling))
    (d_x1, d_x1_bf), (grads['norm_mlp'],) = rowmap_bwd(
        st_res_norm, [xs, mix_out], [norm_mlp], [d_x2, d_h_mlp], [F32, BF16], "res_norm_mlp_bwd")

    sends['w_out'] = row_cut(matmul(merged, d_x1_bf, 'tn', BF16, "d_w_out"))
    d_merged = matmul(d_x1_bf, wt['w_out'], 'nt', F32, "d_merged")
    (d_gate_a, d_gate_r, d_attn_d, d_rwkv_d), _ = rowmap_bwd(
        st_gate, [gate_a, gate_r, attn_d, rwkv_d], [], [d_merged], [BF16] * 4, "gate_merge_bwd")
    sends['w_attn_up'] = matmul(attn, d_attn_d, 'tn', BF16, "d_w_attn_up", out_shards=True)
    sends['w_rwkv_up'] = matmul(rw, d_rwkv_d, 'tn', BF16, "d_w_rwkv_up", out_shards=True)
    d_attn = matmul(d_attn_d, wt['w_attn_up'], 'nt', F32, "d_attn", b_shards=True)
    d_rw = matmul(d_rwkv_d, wt['w_rwkv_up'], 'nt', F32, "d_rw", b_shards=True)

    (d_y, d_r1, d_k1, d_v1, d_g), (grads['gn_w'], grads['gn_b'], grads['r_k']) = rowmap_bwd(
        functools.partial(_st_rwkv_post, _head_sum_ad), post_in, post_params, [d_rw], [F32] * 5, "rwkv_post_bwd")
    core = lax.axis_index("c").astype(I32).reshape(1)

    def chip_partials(names, tag):
        todo = [n for n in names if n not in from_sibling]
        from_sibling.update(zip(todo, pair_exchange([by_chip_and_core(n) for n in todo], "pair_grads_" + tag)))
        return [pair_add(by_chip_and_core(n), from_sibling[n], core, "pair_add_" + n) for n in names]

    scan_grads, late_received = scan_bwd_and_scatter(scan_in, s0_h, d_y, chip_partials(late, "late"))
    received = dict(zip(late, late_received))
    d_r2, d_lw, d_k2, d_v2, d_aa, d_bb = scan_grads
    (d_kk0, d_a), _ = rowmap_bwd(functools.partial(_st_rwkv_kk, _head_sum_ad), [kk0_s, a_s], [], [d_aa, d_bb],
                                 [F32, F32], "rwkv_kk_bwd")
    pre_cts = [[d_r1, d_r2], d_lw, [d_k1, d_k2], [d_v1, d_v2], d_kk0, d_a, d_g]
    d_zp, d_pre = rowmap_bwd(functools.partial(_st_rwkv_pre, _dot_ad), z_parts + z_prev, pre_params, pre_cts,
                             [F32] * 12, "rwkv_pre_bwd")
    grads['shift_mix'] = jnp.concatenate(d_pre[:6], axis=1)
    grads['w0'], d_w_decay, grads['a0'], d_w_aaa, d_w_gate, grads['k_k'], grads['k_a'] = d_pre[6:]
    for name, full_grad in (('w_decay_up', d_w_decay), ('w_aaa_up', d_w_aaa), ('w_gate_up', d_w_gate)):
        sends[name] = _split_shards(full_grad, 1).astype(BF16)
    z_fill = [jnp.zeros((t_len, z_pad - z_width), F32)] if z_pad > z_width else []
    d_z_cur = jnp.concatenate(d_zp[:6] + z_fill, axis=1)
    d_z_prev = _shift_up(jnp.concatenate(d_zp[6:] + z_fill, axis=1))
    (d_z,) = rowmap_fwd(st_add, [d_z_cur, d_z_prev], [], [BF16], "d_z_sum")

    d_merge, _ = rowmap_bwd(st_merge, att_o + att_l, [], [d_attn], [F32] * 6, "attn_merge_bwd")
    d_qn, d_kn, d_vs, d_bias = [], [], [], []
    for g, (_, dilation) in enumerate(DILATED_GROUPS):
        dq, dk, dv, dbp, dbc = attn_bwd(*att_in[g], d_merge[g], d_merge[3 + g], dilation, name=f"attn_bwd_{g}")
        d_qn.append(dq)
        d_kn.append(dk)
        d_vs.append(dv)
        d_bias.append(jnp.stack([dbp, dbc], axis=1))
    d_table = bias_bwd(jnp.concatenate(d_bias, axis=0), buckets, "attn_bias_bwd")
    grads['rel_bias'] = d_table[:, :N_HEADS]
    d_qn = jnp.concatenate(d_qn, axis=1).reshape(t_len * N_HEADS, HEAD_DIM)
    d_kn = jnp.concatenate(d_kn, axis=1).reshape(t_len * N_HEADS, HEAD_DIM)
    (d_q, d_k), (grads['q_gain'], grads['k_gain']) = rowmap_bwd(
        st_qk_norm, [q_raw, k_raw], [q_gain, k_gain], [d_qn, d_kn], [BF16, BF16], "qk_norm_bwd")

    d_pieces = ([d_q.reshape(t_len, ATTN_WIDTH), d_k.reshape(t_len, ATTN_WIDTH)] + [t.astype(BF16) for t in d_vs]
                + [d_z, d_gate_a, d_gate_r])
    piece_cols = [0, ATTN_WIDTH] + [2 * ATTN_WIDTH + g * ATTN_OUT for g in range(len(DILATED_GROUPS))]
    piece_cols += [qkv_width, gates_at, gates_at + d_model]
    d_w_pieces = [matmul(h_in, piece, 'tn', BF16, f"d_w_in_{i}") for i, piece in enumerate(d_pieces)]
    d_w_pieces[-3] = d_w_pieces[-3][:, :z_width]
    sends['w_in'] = _split_shards(jnp.concatenate(d_w_pieces, axis=1), 1)
    d_h_in, early_received = input_grad_and_scatter(d_pieces, w_all, piece_cols, chip_partials(early, "early"))
    received.update(zip(early, early_received))
    (grad_x,), (grads['norm_mix'],) = rowmap_bwd(st_norm, [xs], [norm_mix], [d_x1, d_h_in], [F32], "norm_in_bwd")

    by_name = {n: adamw_shard(received[n], given[n], given['m_' + n], given['v_' + n], "adamw_" + n) for n in BIG}
    pk = lambda prefix: _pack([given[prefix + n] for n in SMALL], F32)[0]
    small_buf, small_layout = _pack([grads[n].reshape(given[n].shape) for n in SMALL], F32)
    (small_all,) = all_gather_many([small_buf], "gather_small_grads")
    small_out = adamw(small_all, pk(''), pk('m_'), pk('v_'), "adamw_replicated")
    small_shapes = [given[n].shape for n in SMALL]
    by_name.update(zip(SMALL, zip(*[_unpack(buf, small_layout, small_shapes) for buf in small_out])))
    loss = lax.psum(loss_local, MESH_AXES)
    return (loss, grad_x[None], *[by_name[n][0] for n in WEIGHTS], *[by_name[n][1] for n in WEIGHTS],
            *[by_name[n][2] for n in WEIGHTS], *[by_name[n][3] for n in WEIGHTS])
```

```python
import functools
import math

import jax
import jax.numpy as jnp
from jax import lax
from jax.experimental import pallas as pl
from jax.experimental.pallas import tpu as pltpu

F32 = jnp.float32
BF16 = jnp.bfloat16
I32 = jnp.int32

N_DEV = 8
N_CHIPS = 4
MESH_AXES = ("x", "y", "c")
LANES = 128
PACK_ROWS = 16
VMEM_LIMIT_BYTES = 48 * 2**20
ROW_BLOCK_BYTES = 10 * 2**20

HEAD_DIM = 128
ATTN_BLOCK = 128
HEADS_PER_GROUP = 4
DILATED_GROUPS = ((128, 1), (512, 4), (2048, 16))
N_HEADS = HEADS_PER_GROUP * len(DILATED_GROUPS)
ATTN_WIDTH = N_HEADS * HEAD_DIM
ATTN_OUT = HEADS_PER_GROUP * HEAD_DIM
N_BUCKETS = 32
MAX_DISTANCE = 2048
RWKV_HEAD = 64
RWKV_CHUNK = 64
RWKV_PAIRS_PER_STEP = 8
RMS_EPS = 1e-6
GN_EPS = 64e-5
NEG_INF = -1e30

ADAM_LR = 0.001
ADAM_B1 = 0.9
ADAM_B2 = 0.999
ADAM_EPS = 1e-08
ADAM_WD = 0.01
ADAM_STEP = 10

WEIGHTS = ['norm_mix', 'w_in', 'q_gain', 'k_gain', 'rel_bias', 'w_attn_up', 'shift_mix', 'w0', 'w_decay_up', 'a0',
           'w_aaa_up', 'w_gate_up', 'k_k', 'k_a', 'r_k', 'gn_w', 'gn_b', 'w_rwkv_up', 'w_out', 'norm_mlp', 'w_mlp_in',
           'w_mlp_out', 'norm_ple', 'w_ple_gate', 'w_ple_proj']
SHARD_AXIS = {'w_in': 1, 'w_attn_up': 1, 'w_decay_up': 1, 'w_aaa_up': 1, 'w_gate_up': 1, 'w_rwkv_up': 1, 'w_out': 0,
              'w_mlp_in': 1, 'w_mlp_out': 0, 'w_ple_gate': 0, 'w_ple_proj': 1}
BIG = [n for n in WEIGHTS if n in SHARD_AXIS]
SMALL = [n for n in WEIGHTS if n not in SHARD_AXIS]


def _params(sem):
    return pltpu.CompilerParams(dimension_semantics=sem, vmem_limit_bytes=VMEM_LIMIT_BYTES)


_DN = {'nn': (((1,), (0,)), ((), ())), 'nt': (((1,), (1,)), ((), ())), 'tn': (((0,), (0,)), ((), ()))}


def _dot(a, b, mode, exact):
    if exact:
        return lax.dot_general(a, b, _DN[mode], precision=lax.Precision.HIGH, preferred_element_type=F32)
    return lax.dot_general(a.astype(BF16), b.astype(BF16), _DN[mode], preferred_element_type=F32)


@functools.partial(jax.custom_vjp, nondiff_argnums=(2, 3))
def _dot_ad(a, b, mode, exact):
    return _dot(a, b, mode, exact)


def _dot_ad_fwd(a, b, mode, exact):
    return _dot(a, b, mode, exact), (a, b)


def _dot_ad_bwd(mode, exact, res, g):
    a, b = res
    if mode == 'nn':
        return _dot(g, b, 'nt', exact), _dot(a, g, 'tn', exact)
    if mode == 'nt':
        return _dot(g, b, 'nn', exact), _dot(g, a, 'tn', exact)
    return _dot(b, g, 'nt', exact), _dot(a, g, 'nn', exact)


_dot_ad.defvjp(_dot_ad_fwd, _dot_ad_bwd)


def _pick(n, cands):
    for c in cands:
        if n % c == 0:
            return c
    return n


def matmul(a, b, mode, out_dtype, name, b_shards=False, out_shards=False, b_col_offsets=None, epilogue=None,
           extras=(), exchange=None):
    a_list = list(a) if isinstance(a, (list, tuple)) else [a]
    b_list = list(b) if isinstance(b, (list, tuple)) else [b]
    seg = len(a_list)
    assert all(t.dtype == BF16 for t in a_list + b_list), name
    assert seg == 1 or (mode == 'nt' and not b_shards), name
    m = a_list[0].shape[1] if mode == 'tn' else a_list[0].shape[0]
    ks = [t.shape[0] if mode == 'tn' else t.shape[1] for t in a_list]
    b0 = b_list[0]
    b_rows, b_cols = (b0.shape[1], N_DEV * b0.shape[2]) if b_shards else b0.shape
    n = b_rows if mode == 'nt' else b_cols
    if seg == 1 and b_col_offsets is None:
        assert (b_cols if mode == 'nt' else b_rows) == ks[0], (name, a_list[0].shape, b0.shape)
    offsets = list(b_col_offsets) if b_col_offsets is not None else [0] * seg
    tm = _pick(m, (1024, 512, 256, 128))
    tn = _pick(n // N_DEV if (out_shards or (b_shards and mode != 'nt')) else n, (1024, 512, 256, 128))
    k_units = [kk // N_DEV if (b_shards and mode == 'nt') else kk for kk in ks] + [o for o in offsets if o]
    tk = next((c for c in (2048, 1024, 512, 256, 128) if all(u % c == 0 for u in k_units)), k_units[0])
    nks = [kk // tk for kk in ks]
    starts = [sum(nks[:s]) for s in range(seg)]
    nk = sum(nks)
    grid = (m // tm, n // tn, nk)
    total_steps = grid[0] * grid[1] * nk
    kind, moved = exchange if exchange is not None else (None, [])
    nx, ne = len(moved), len(extras)
    out_dtypes = list(out_dtype) if isinstance(out_dtype, (list, tuple)) else [out_dtype]
    no = len(out_dtypes)

    def body(*refs):
        a_refs, b_refs, x_refs = refs[:seg], refs[seg:2 * seg], refs[2 * seg:2 * seg + ne]
        pos = 2 * seg + ne
        moved_in, o_refs = refs[pos:pos + nx], refs[pos + nx:pos + nx + no]
        pos += nx + no
        moved_out, acc_ref, sems = refs[pos:pos + nx], refs[pos + nx], refs[pos + nx + 1:]
        kk = pl.program_id(2)
        step = (pl.program_id(0) * grid[1] + pl.program_id(1)) * nk + kk
        if kind == 'gather':
            start, forward, finish = _gather_schedule(moved_in, moved_out, *sems)
            pl.when(step == 0)(start)
            pl.when(step == total_steps - 1 - total_steps // 8)(forward)
        elif kind == 'scatter':
            start, finish = _scatter_schedule(moved_in, moved_out, *sems)
            pl.when(step == 0)(start)
        elif kind == 'pair':
            start, finish = _pair_schedule(moved_in, moved_out, *sems)
            pl.when(step == 0)(start)

        @pl.when(kk == 0)
        def _():
            acc_ref[...] = jnp.zeros_like(acc_ref)

        for s in range(seg):
            def accumulate(s=s):
                acc_ref[...] += lax.dot_general(a_refs[s][...], b_refs[s][...], _DN[mode], preferred_element_type=F32)

            if seg == 1:
                accumulate()
            else:
                pl.when(jnp.logical_and(kk >= starts[s], kk < starts[s] + nks[s]))(accumulate)

        @pl.when(kk == nk - 1)
        def _():
            acc = acc_ref[...]
            outs = (acc,) if epilogue is None else epilogue(acc, *[x[...] for x in x_refs])
            for r, v in zip(o_refs, outs):
                r[...] = v.astype(r.dtype)

        if kind is not None:
            pl.when(step == total_steps - 1)(finish)

    def k_of(kk, s):
        return kk if seg == 1 else jnp.clip(kk - starts[s], 0, nks[s] - 1)

    a_specs, b_specs = [], []
    for s in range(seg):
        off = offsets[s] // tk
        if mode == 'tn':
            a_specs.append(pl.BlockSpec((tk, tm), lambda i, j, kk, s=s: (k_of(kk, s), i)))
        else:
            a_specs.append(pl.BlockSpec((tm, tk), lambda i, j, kk, s=s: (i, k_of(kk, s))))
        if mode == 'nt':
            if b_shards:
                per = b0.shape[2] // tk
                b_specs.append(pl.BlockSpec((None, tn, tk), lambda i, j, kk: (kk // per, j, kk % per)))
            else:
                b_specs.append(pl.BlockSpec((tn, tk), lambda i, j, kk, s=s, off=off: (j, off + k_of(kk, s))))
        else:
            if b_shards:
                per = b0.shape[2] // tn
                b_specs.append(pl.BlockSpec((None, tk, tn), lambda i, j, kk: (j // per, kk, j % per)))
            else:
                b_specs.append(pl.BlockSpec((tk, tn), lambda i, j, kk: (kk, j)))
    tile = pl.BlockSpec((tm, tn), lambda i, j, kk: (i, j))
    if out_shards:
        assert epilogue is None
        per_o = n // N_DEV // tn
        o_specs = [pl.BlockSpec((None, tm, tn), lambda i, j, kk: (j // per_o, i, j % per_o))]
        o_shapes = [jax.ShapeDtypeStruct((N_DEV, m, n // N_DEV), out_dtypes[0])]
    else:
        o_specs = [tile] * no
        o_shapes = [jax.ShapeDtypeStruct((m, n), d) for d in out_dtypes]
    landing = {'gather': lambda t: (N_DEV,) + t.shape, 'scatter': lambda t: t.shape,
               'pair': lambda t: (N_CHIPS,) + t.shape[2:]}
    moved_shapes = [jax.ShapeDtypeStruct(landing[kind](t), t.dtype) for t in moved]
    sem_scratch = [] if not nx else (_pair_scratch(nx) if kind == 'pair' else _exchange_scratch(nx))
    res = pl.pallas_call(
        body, grid=grid, in_specs=a_specs + b_specs + [tile] * ne + [_ANY] * nx,
        out_specs=o_specs + [_ANY] * nx, out_shape=o_shapes + moved_shapes,
        scratch_shapes=[pltpu.VMEM((tm, tn), F32)] + sem_scratch,
        compiler_params=_params(("arbitrary",) * 3 if nx else ("parallel", "parallel", "arbitrary")), name=name,
    )(*a_list, *b_list, *extras, *moved)
    result = res[0] if (epilogue is None) else list(res[:no])
    return (result, list(res[no:])) if nx else result


def _row_bytes(shape, dtype):
    dims = list(shape[1:])
    dims[-1] = -(-dims[-1] // LANES) * LANES
    return math.prod(dims) * jnp.dtype(dtype).itemsize


def _row_tile(n, row_bytes):
    t = 1024
    while t > 16 and (n % t or t * row_bytes > ROW_BLOCK_BYTES):
        t //= 2
    assert n % t == 0, (n, t)
    return t


def rowmap(fn, tiled, bcast, out_tiled, out_acc, name):
    n = tiled[0].shape[0]
    tile = _row_tile(n, sum(_row_bytes(t.shape, t.dtype) for t in list(tiled) + list(out_tiled)))
    n_in, n_out = len(tiled) + len(bcast), len(out_tiled)

    def body(*refs):
        outs, accs = fn(*[r[...] for r in refs[:n_in]])
        assert len(outs) == n_out and len(accs) == len(out_acc), name
        for r, v in zip(refs[n_in:n_in + n_out], outs):
            r[...] = v.astype(r.dtype)
        acc_refs = refs[n_in + n_out:]
        if acc_refs:
            @pl.when(pl.program_id(0) == 0)
            def _():
                for r, v in zip(acc_refs, accs):
                    r[...] = v.astype(r.dtype)

            @pl.when(pl.program_id(0) != 0)
            def _():
                for r, v in zip(acc_refs, accs):
                    r[...] += v.astype(r.dtype)

    def tspec(s):
        nd = len(s.shape)
        return pl.BlockSpec((tile,) + tuple(s.shape[1:]), lambda i, nd=nd: (i,) + (0,) * (nd - 1))

    def bspec(s):
        nd = len(s.shape)
        return pl.BlockSpec(tuple(s.shape), lambda i, nd=nd: (0,) * nd)

    res = pl.pallas_call(
        body, grid=(n // tile,),
        in_specs=[tspec(t) for t in tiled] + [bspec(t) for t in bcast],
        out_specs=[tspec(t) for t in out_tiled] + [bspec(t) for t in out_acc],
        out_shape=list(out_tiled) + list(out_acc),
        compiler_params=_params(("arbitrary",)), name=name,
    )(*tiled, *bcast)
    return list(res[:n_out]), list(res[n_out:])


def rowmap_fwd(fwd, tiled, bcast, out_dtypes, name):
    shapes = jax.eval_shape(fwd, *tiled, *bcast)
    out_tiled = [jax.ShapeDtypeStruct(s.shape, d) for s, d in zip(shapes, out_dtypes)]
    outs, _ = rowmap(lambda *blk: (fwd(*[b.astype(F32) for b in blk]), ()), tiled, bcast, out_tiled, [], name)
    return outs


def rowmap_bwd(fwd, tiled, bcast, cts, want, name):
    cts = [[] if c is None else (list(c) if isinstance(c, (list, tuple)) else [c]) for c in cts]
    flat_cts = [c for group in cts for c in group]
    nt_, nc_ = len(tiled), len(flat_cts)

    def fn(*blk):
        ins = [b.astype(F32) for b in blk[:nt_]] + [b.astype(F32) for b in blk[nt_ + nc_:]]
        ctb = list(blk[nt_:nt_ + nc_])
        outs, vjp = jax.vjp(fwd, *ins)
        full = []
        for o, group in zip(outs, cts):
            acc = jnp.zeros_like(o)
            for _ in group:
                acc = acc + ctb.pop(0).astype(F32)
            full.append(acc)
        g = vjp(tuple(full))
        return [g[i] for i in range(nt_) if want[i] is not None], list(g[nt_:])

    out_tiled = [jax.ShapeDtypeStruct(t.shape, w) for t, w in zip(tiled, want) if w is not None]
    out_acc = [jax.ShapeDtypeStruct(b.shape, F32) for b in bcast]
    return rowmap(fn, list(tiled) + flat_cts, bcast, out_tiled, out_acc, name)


def _rms(x, gain):
    return x * lax.rsqrt(jnp.mean(jnp.square(x), axis=-1, keepdims=True) + RMS_EPS) * gain


def _sigmoid(x):
    return 1.0 / (1.0 + jnp.exp(-x))


def _softplus(x):
    return jnp.maximum(x, 0.0) + jnp.log(1.0 + jnp.exp(-jnp.abs(x)))


def st_norm(x, gain):
    return x, _rms(x, gain)


def st_res_norm(x, delta, gain):
    y = x + delta
    return y, _rms(y, gain)


def st_qk_norm(q, k, q_gain, k_gain):
    return _rms(q, q_gain), _rms(k, k_gain)


def st_merge(o0, o1, o2, l0, l1, l2):
    m = jnp.maximum(jnp.maximum(l0, l1), l2)
    e0, e1, e2 = jnp.exp(l0 - m), jnp.exp(l1 - m), jnp.exp(l2 - m)
    return ((e0 * o0 + e1 * o1 + e2 * o2) / (e0 + e1 + e2),)


def _st_rwkv_pre(dot, zr, zk, zv, xw, xa, xg, pr, pk, pv, pw, pa, pg, mr, mk, mv, mw, ma, mg,
                 w0, w_decay, a0, w_aaa, w_gate, k_k, k_a):
    def shift(cur, prev, mix):
        return cur + mix * (prev - cur)

    r, k, v = shift(zr, pr, mr), shift(zk, pk, mk), shift(zv, pv, mv)
    xw, xa, xg = shift(xw, pw, mw), shift(xa, pa, ma), shift(xg, pg, mg)
    w = -_softplus(-(w0 + dot(jnp.tanh(xw), w_decay, 'nn', False))) - 0.5
    a = _sigmoid(a0 + dot(xa, w_aaa, 'nn', False))
    g = dot(_sigmoid(xg), w_gate, 'nn', False)
    log_decay = -jnp.exp(w)
    return r, log_decay, k * (1.0 + (a - 1.0) * k_a), v, k * k_k, a, g


def _head_sum(x):
    joins = _pair_block_mask(2 * RWKV_HEAD).astype(F32)
    tile = joins.shape[0]
    return jnp.concatenate([_dot(x[:, lo:lo + tile], joins, 'nn', True) for lo in range(0, x.shape[1], tile)], axis=1)


@jax.custom_vjp
def _head_sum_ad(x):
    return _head_sum(x)


_head_sum_ad.defvjp(lambda x: (_head_sum(x), None), lambda _, g: (_head_sum(g),))


def _st_rwkv_kk(head_sum, kk0, a):
    kk = kk0 / jnp.maximum(jnp.sqrt(head_sum(jnp.square(kk0))), 1e-12)
    return -kk, kk * a


def _st_rwkv_post(head_sum, y, r, k, v, g, gn_w, gn_b, r_k):
    mu = head_sum(y) * (1.0 / RWKV_HEAD)
    var = head_sum(jnp.square(y - mu)) * (1.0 / RWKV_HEAD)
    out = (y - mu) * lax.rsqrt(var + GN_EPS) * gn_w + gn_b
    out = out + head_sum(r * k * r_k) * v
    return (out * g,)


def st_gate(g0, g1, attn_d, rwkv_d):
    return (_sigmoid(g0) * attn_d + _sigmoid(g1) * rwkv_d,)


def st_relu2(u):
    return (jnp.square(jnp.maximum(u, 0.0)),)


def st_add(a, b):
    return (a + b,)


def loss_head(x2, pg, pp, target, name):
    d_model = x2.shape[1]

    def fn(x2, pg, pp, tgt):
        s = _sigmoid(pg)
        err = x2 + s * pp - tgt
        dy = err * (1.0 / d_model)
        part = 0.5 * jnp.sum(jnp.square(err)) * (1.0 / d_model)
        return [dy, dy * pp * s * (1.0 - s), dy * s], [jnp.full((8, LANES), part, F32)]

    sds = jax.ShapeDtypeStruct
    outs, accs = rowmap(fn, [x2, pg, pp, target], [],
                        [sds(x2.shape, F32), sds(x2.shape, BF16), sds(x2.shape, BF16)], [sds((8, LANES), F32)], name)
    return outs[0], outs[1], outs[2], accs[0][0, 0]


def _attn_block(dot, q, kp, kc, vp, vc, bp, bc, prev_offset):
    blk = q.shape[0]
    qi = lax.broadcasted_iota(I32, (blk, blk), 0)
    ki = lax.broadcasted_iota(I32, (blk, blk), 1)
    mask_c = ki <= qi
    mask_p = ki >= qi + prev_offset
    scale = HEAD_DIM ** -0.5
    s_c = jnp.where(mask_c, dot(q, kc, 'nt', False) * scale + bc, NEG_INF)
    s_p = jnp.where(mask_p, dot(q, kp, 'nt', False) * scale + bp, NEG_INF)
    m = lax.stop_gradient(jnp.maximum(jnp.max(s_c, axis=1, keepdims=True), jnp.max(s_p, axis=1, keepdims=True)))
    e_c = jnp.where(mask_c, jnp.exp(s_c - m), 0.0)
    e_p = jnp.where(mask_p, jnp.exp(s_p - m), 0.0)
    l = jnp.sum(e_c, axis=1, keepdims=True) + jnp.sum(e_p, axis=1, keepdims=True)
    o = (dot(e_c, vc, 'nn', False) + dot(e_p, vp, 'nn', False)) / l
    return o, jnp.broadcast_to(m + jnp.log(l), o.shape)


class _ClassView:
    def __init__(self, tokens, dilation, first_col):
        self.view = tokens.reshape(tokens.shape[0] // dilation, dilation * tokens.shape[1])
        self.tiles, self.first = tokens.shape[1] // HEAD_DIM, first_col // HEAD_DIM

    def spec(self, shift, nb):
        tiles, first = self.tiles, self.first
        return pl.BlockSpec((ATTN_BLOCK, HEAD_DIM),
                            lambda h, r, n: (jnp.clip(n + shift, 0, nb - 1), r * tiles + first + h))


_BIAS_SPEC = pl.BlockSpec((None, ATTN_BLOCK, ATTN_BLOCK), lambda h, r, n: (h, 0, 0))


def attn_fwd(q, k, v, bp, bc, dilation, name):
    t_len = q[0].shape[0]
    nb = t_len // dilation // ATTN_BLOCK
    qv, kv, vv = (_ClassView(arr, dilation, col) for arr, col in (q, k, v))
    o_shape = jax.ShapeDtypeStruct((t_len // dilation, dilation * ATTN_OUT), F32)
    o_spec = pl.BlockSpec((ATTN_BLOCK, HEAD_DIM), lambda h, r, n: (n, r * HEADS_PER_GROUP + h))

    def body(q_ref, kp_ref, kc_ref, vp_ref, vc_ref, bp_ref, bc_ref, o_ref, l_ref):
        off = jnp.where(pl.program_id(2) > 0, 0, ATTN_BLOCK)
        o, l = _attn_block(_dot, q_ref[...], kp_ref[...], kc_ref[...], vp_ref[...], vc_ref[...], bp_ref[...],
                           bc_ref[...], off)
        o_ref[...] = o
        l_ref[...] = l

    o, l = pl.pallas_call(
        body, grid=(HEADS_PER_GROUP, dilation, nb),
        in_specs=[qv.spec(0, nb), kv.spec(-1, nb), kv.spec(0, nb), vv.spec(-1, nb), vv.spec(0, nb),
                  _BIAS_SPEC, _BIAS_SPEC],
        out_specs=[o_spec, o_spec], out_shape=[o_shape] * 2,
        compiler_params=_params(("parallel", "parallel", "parallel")), name=name,
    )(qv.view, kv.view, kv.view, vv.view, vv.view, bp, bc)
    return o.reshape(t_len, ATTN_OUT), l.reshape(t_len, ATTN_OUT)


def attn_bwd(q, k, v, bp, bc, do, dl, dilation, name):
    t_len = q[0].shape[0]
    blk = ATTN_BLOCK
    nb = t_len // dilation // blk
    qv, kv, vv = (_ClassView(arr, dilation, col) for arr, col in (q, k, v))
    dov, dlv = _ClassView(do, dilation, 0), _ClassView(dl, dilation, 0)
    cur, bias = dov.spec(0, nb), _BIAS_SPEC

    def body(q_ref, kp_ref, kc_ref, vp_ref, vc_ref, bp_ref, bc_ref, do_ref, dl_ref,
             dq_ref, dkp_ref, dkc_ref, dvp_ref, dvc_ref, dbp_ref, dbc_ref):
        off = jnp.where(pl.program_id(2) > 0, 0, blk)
        f = functools.partial(_attn_block, _dot_ad, prev_offset=off)
        _, vjp = jax.vjp(f, q_ref[...], kp_ref[...], kc_ref[...], vp_ref[...], vc_ref[...], bp_ref[...], bc_ref[...])
        dq, dkp, dkc, dvp, dvc, dbp, dbc = vjp((do_ref[...], dl_ref[...]))
        dq_ref[...] = dq
        dkp_ref[...] = dkp
        dkc_ref[...] = dkc
        dvp_ref[...] = dvp
        dvc_ref[...] = dvc
        first = jnp.logical_and(pl.program_id(1) == 0, pl.program_id(2) == 0)

        @pl.when(first)
        def _():
            dbp_ref[...] = dbp
            dbc_ref[...] = dbc

        @pl.when(jnp.logical_not(first))
        def _():
            dbp_ref[...] += dbp
            dbc_ref[...] += dbc

    blocks = jax.ShapeDtypeStruct(dov.view.shape, F32)
    grid = (HEADS_PER_GROUP, dilation, nb)
    dq, dkp, dkc, dvp, dvc, dbp, dbc = pl.pallas_call(
        body, grid=grid,
        in_specs=[qv.spec(0, nb), kv.spec(-1, nb), kv.spec(0, nb), vv.spec(-1, nb), vv.spec(0, nb), bias, bias,
                  cur, dlv.spec(0, nb)],
        out_specs=[cur] * 5 + [bias] * 2, out_shape=[blocks] * 5 + [jax.ShapeDtypeStruct(bp.shape, F32)] * 2,
        compiler_params=_params(("arbitrary", "arbitrary", "arbitrary")), name=name,
    )(qv.view, kv.view, kv.view, vv.view, vv.view, bp, bc, dov.view, dlv.view)

    nxt = dov.spec(1, nb)

    def add_body(kc_ref, kp_ref, vc_ref, vp_ref, dk_ref, dv_ref):
        has_next = (pl.program_id(2) + 1 < nb).astype(F32)
        dk_ref[...] = kc_ref[...] + kp_ref[...] * has_next
        dv_ref[...] = vc_ref[...] + vp_ref[...] * has_next

    dk, dv = pl.pallas_call(
        add_body, grid=grid, in_specs=[cur, nxt, cur, nxt], out_specs=[cur, cur], out_shape=[blocks] * 2,
        compiler_params=_params(("parallel", "parallel", "parallel")), name=name + "_kv",
    )(dkc, dkp, dvc, dvp)
    return [t.reshape(t_len, ATTN_OUT) for t in (dq, dk, dv)] + [dbp, dbc]


def _t5_bucket(dist):
    max_exact = N_BUCKETS // 2
    d_f = jnp.maximum(dist, 1).astype(F32)
    large = max_exact + (jnp.log(d_f / max_exact) / math.log(MAX_DISTANCE / max_exact)
                         * (N_BUCKETS - max_exact)).astype(I32)
    large = jnp.minimum(large, N_BUCKETS - 1)
    return jnp.where(dist < max_exact, dist, large)


def _bucket_tables():
    blk = ATTN_BLOCK
    qi = jnp.arange(blk)[:, None]
    ki = jnp.arange(blk)[None, :]
    out = []
    for _, dilation in DILATED_GROUPS:
        rel_p = jnp.maximum(blk + qi - ki, 0) * dilation
        rel_c = jnp.maximum(qi - ki, 0) * dilation
        out.append(jnp.stack([_t5_bucket(rel_p), _t5_bucket(rel_c)]))
    return jnp.stack(out).astype(I32)


def bias_fwd(table, buckets, name):
    blk = ATTN_BLOCK

    def body(tab_ref, bkt_ref, out_ref):
        for g in range(len(DILATED_GROUPS)):
            for half in range(2):
                bk = bkt_ref[g, half]
                for hh in range(HEADS_PER_GROUP):
                    h = g * HEADS_PER_GROUP + hh
                    acc = jnp.zeros((blk, blk), F32)
                    for b in range(N_BUCKETS):
                        acc = jnp.where(bk == b, tab_ref[b, h], acc)
                    out_ref[h, half] = acc

    return pl.pallas_call(
        body, in_specs=[pl.BlockSpec(memory_space=pltpu.SMEM), pl.BlockSpec(memory_space=pltpu.VMEM)],
        out_specs=pl.BlockSpec(memory_space=pltpu.VMEM),
        out_shape=jax.ShapeDtypeStruct((N_HEADS, 2, blk, blk), F32), name=name,
    )(table, buckets)


def bias_bwd(dbias, buckets, name):
    def body(db_ref, bkt_ref, out_ref):
        rows = lax.broadcasted_iota(I32, (N_BUCKETS, LANES), 0)
        cols = lax.broadcasted_iota(I32, (N_BUCKETS, LANES), 1)
        acc = jnp.zeros((N_BUCKETS, LANES), F32)
        for g in range(len(DILATED_GROUPS)):
            bk_p, bk_c = bkt_ref[g, 0], bkt_ref[g, 1]
            for hh in range(HEADS_PER_GROUP):
                h = g * HEADS_PER_GROUP + hh
                d_p, d_c = db_ref[h, 0], db_ref[h, 1]
                for b in range(N_BUCKETS):
                    s = jnp.sum(jnp.where(bk_p == b, d_p, 0.0)) + jnp.sum(jnp.where(bk_c == b, d_c, 0.0))
                    acc = jnp.where(jnp.logical_and(rows == b, cols == h), s, acc)
        out_ref[...] = acc

    return pl.pallas_call(
        body, in_specs=[pl.BlockSpec(memory_space=pltpu.VMEM)] * 2, out_specs=pl.BlockSpec(memory_space=pltpu.VMEM),
        out_shape=jax.ShapeDtypeStruct((N_BUCKETS, LANES), F32), name=name,
    )(dbias, buckets)


def _rwkv_chunk(dot, s0, r, lw, k, v, a, b):
    def each(f, *lists):
        return [f(*xs) for xs in zip(*lists)]

    def mm(mode):
        return lambda p, q: dot(p, q, mode, True)

    def mul(p, q):
        return p * q

    def add(p, q):
        return p + q

    c, width = r[0].shape
    lane_a = (lax.broadcasted_iota(I32, (1, width), 1) < width // 2).astype(F32)
    time_a = (lax.broadcasted_iota(I32, (1, 2 * c), 1) < c).astype(F32)
    ti = lax.broadcasted_iota(I32, (c, 2 * c), 0)
    si = lax.broadcasted_iota(I32, (c, 2 * c), 1)
    si = jnp.where(si < c, si, si - c)
    incl, strict = si <= ti, si < ti
    ones_incl = (lax.broadcasted_iota(I32, (c, c), 1) <= lax.broadcasted_iota(I32, (c, c), 0)).astype(F32)
    same_head = _pair_block_mask(width)

    def by_head(x):
        return jnp.concatenate([x * lane_a, x * (1.0 - lane_a)], axis=0)

    def by_block(p):
        return jnp.concatenate([p * time_a, p * (1.0 - time_a)], axis=0)

    cum = each(lambda x: dot(ones_incl, x, 'nn', True), lw)
    w_incl = each(jnp.exp, cum)
    w_prev = each(lambda cu, x: jnp.exp(cu - x), cum, lw)
    w_inv = each(lambda cu: jnp.exp(-cu), cum)
    w_end = each(lambda x: jnp.exp(jnp.sum(x, axis=0, keepdims=True)), lw)
    a_t, r_t, b_t, k_t = each(mul, a, w_prev), each(mul, r, w_incl), each(mul, b, w_inv), each(mul, k, w_inv)
    b_h, k_h, v_h = each(by_head, b_t), each(by_head, k_t), each(by_head, v)
    l_ab = each(lambda p, q: jnp.where(strict, dot(p, q, 'nt', True), 0.0), a_t, b_h)
    l_ak = each(lambda p, q: jnp.where(strict, dot(p, q, 'nt', True), 0.0), a_t, k_h)
    u = each(add, each(mm('nt'), a_t, s0), each(mm('nn'), l_ak, v_h))
    u = each(add, u, each(mm('nn'), l_ab, each(by_head, u)))
    power = l_ab
    for _ in range(int(math.log2(c)) - 1):
        power = each(mm('nn'), power, each(by_block, power))
        u = each(add, u, each(mm('nn'), power, each(by_head, u)))
    m_rb = each(lambda p, q: jnp.where(incl, dot(p, q, 'nt', True), 0.0), r_t, b_h)
    m_rk = each(lambda p, q: jnp.where(incl, dot(p, q, 'nt', True), 0.0), r_t, k_h)
    y = each(add, each(add, each(mm('nt'), r_t, s0), each(mm('nn'), m_rb, each(by_head, u))), each(mm('nn'), m_rk, v_h))
    outer = each(lambda uu, vv, bb, kk: dot(jnp.concatenate([uu, vv], axis=0), jnp.concatenate([bb, kk], axis=0),
                                             'tn', True), u, v, b_t, k_t)
    s1 = each(lambda s, o, we: (s + jnp.where(same_head, o, 0.0)) * we, s0, outer, w_end)
    return y, s1


def _pair_block_mask(width):
    rows = lax.broadcasted_iota(I32, (width, width), 0) < width // 2
    cols = lax.broadcasted_iota(I32, (width, width), 1) < width // 2
    return rows == cols


def _rwkv_tiling(t, width):
    pair = 2 * RWKV_HEAD
    n_pairs = width // pair
    assert n_pairs * pair == width and t % RWKV_CHUNK == 0
    return pair, n_pairs, math.gcd(n_pairs, RWKV_PAIRS_PER_STEP), t // RWKV_CHUNK


def rwkv_fwd(r, lw, k, v, a, b, name, gather=()):
    t, width = r.shape
    pair, n_pairs, ps, nc = _rwkv_tiling(t, width)
    c = RWKV_CHUNK
    steps = n_pairs // ps * nc
    ng = len(gather)
    row = pl.BlockSpec((c, ps * pair), lambda i, j: (j, i))
    cols = [slice(q * pair, (q + 1) * pair) for q in range(ps)]

    def body(*refs):
        ins, x_refs = refs[:6], refs[6:6 + ng]
        y_ref, s0_ref = refs[6 + ng:8 + ng]
        out_refs, state, sems = refs[8 + ng:8 + 2 * ng], refs[8 + 2 * ng], refs[9 + 2 * ng:]
        step = pl.program_id(0) * nc + pl.program_id(1)
        if ng:
            start, forward, finish = _gather_schedule(x_refs, out_refs, *sems)
            pl.when(step == 0)(start)
            pl.when(step == steps - 1 - steps // 8)(forward)

        @pl.when(pl.program_id(1) == 0)
        def _():
            state[...] = jnp.zeros_like(state)

        s0 = [state[q] for q in range(ps)]
        y, s1 = _rwkv_chunk(_dot, s0, *[[ref[:, cs] for cs in cols] for ref in ins])
        for q in range(ps):
            s0_ref[q] = s0[q]
            y_ref[:, cols[q]] = y[q]
            state[q] = s1[q]
        if ng:
            pl.when(step == steps - 1)(finish)

    return pl.pallas_call(
        body, grid=(n_pairs // ps, nc), in_specs=[row] * 6 + [_ANY] * ng,
        out_specs=[row, pl.BlockSpec((ps, None, pair, pair), lambda i, j: (i, j, 0, 0))] + [_ANY] * ng,
        out_shape=[jax.ShapeDtypeStruct((t, width), F32), jax.ShapeDtypeStruct((n_pairs, nc, pair, pair), F32)]
        + [jax.ShapeDtypeStruct((N_DEV,) + g.shape, g.dtype) for g in gather],
        scratch_shapes=[pltpu.VMEM((ps, pair, pair), F32)] + (_exchange_scratch(ng) if ng else []),
        compiler_params=_params(("arbitrary", "arbitrary")), name=name,
    )(r, lw, k, v, a, b, *gather)


def rwkv_bwd(r, lw, k, v, a, b, s0, dy, name, scatter=()):
    t, width = r.shape
    pair, n_pairs, ps, nc = _rwkv_tiling(t, width)
    c = RWKV_CHUNK
    steps = n_pairs // ps * nc
    ns = len(scatter)
    row = pl.BlockSpec((c, ps * pair), lambda i, j: (nc - 1 - j, i))
    st = pl.BlockSpec((ps, None, pair, pair), lambda i, j: (i, nc - 1 - j, 0, 0))
    cols = [slice(q * pair, (q + 1) * pair) for q in range(ps)]

    def body(*refs):
        ins, s0_ref, dy_ref, send_refs = refs[:6], refs[6], refs[7], refs[8:8 + ns]
        grad_refs, recv_refs = refs[8 + ns:14 + ns], refs[14 + ns:14 + 2 * ns]
        dstate, sems = refs[14 + 2 * ns], refs[15 + 2 * ns:]
        step = pl.program_id(0) * nc + pl.program_id(1)
        if ns:
            start, finish = _scatter_schedule(send_refs, recv_refs, *sems)
            pl.when(step == 0)(start)

        @pl.when(pl.program_id(1) == 0)
        def _():
            dstate[...] = jnp.zeros_like(dstate)

        pairs = range(ps)
        _, vjp = jax.vjp(functools.partial(_rwkv_chunk, _dot_ad), [s0_ref[q] for q in pairs],
                         *[[ref[:, cs] for cs in cols] for ref in ins])
        grads = vjp(([dy_ref[:, cs] for cs in cols], [dstate[q] for q in pairs]))
        same_head = _pair_block_mask(pair)
        for q in pairs:
            dstate[q] = jnp.where(same_head, grads[0][q], 0.0)
            for ref, g in zip(grad_refs, grads[1:]):
                ref[:, cols[q]] = g[q]
        if ns:
            pl.when(step == steps - 1)(finish)

    return pl.pallas_call(
        body, grid=(n_pairs // ps, nc), in_specs=[row] * 6 + [st, row] + [_ANY] * ns,
        out_specs=[row] * 6 + [_ANY] * ns,
        out_shape=[jax.ShapeDtypeStruct((t, width), F32)] * 6 + [jax.ShapeDtypeStruct(s.shape, s.dtype) for s in scatter],
        scratch_shapes=[pltpu.VMEM((ps, pair, pair), F32)] + (_exchange_scratch(ns) if ns else []),
        compiler_params=_params(("arbitrary", "arbitrary")), name=name,
    )(r, lw, k, v, a, b, s0, dy, *scatter)


_ANY = pl.BlockSpec(memory_space=pl.ANY)


def _exchange_scratch(n_arrays):
    return [pltpu.SemaphoreType.DMA((n_arrays, N_DEV - 1)), pltpu.SemaphoreType.DMA((n_arrays, N_DEV - 1)),
            pltpu.SemaphoreType.DMA((n_arrays,))]


def _gather_schedule(x_refs, out_refs, send_sems, recv_sems, local_sems):
    x, y, c = lax.axis_index("x"), lax.axis_index("y"), lax.axis_index("c")
    me, sibling = (x, y, c), (x, y, 1 - c)
    chips = [(1 - x, y), (x, 1 - y), (1 - x, 1 - y)]
    arrays = range(len(x_refs))

    def slot(a, pos):
        return out_refs[a].at[4 * pos[0] + 2 * pos[1] + pos[2]]

    def copy(a, i, block, to, src=None):
        return pltpu.make_async_remote_copy(
            src_ref=slot(a, block) if src is None else src, dst_ref=slot(a, block), send_sem=send_sems.at[a, i],
            recv_sem=recv_sems.at[a, i], device_id=to, device_id_type=pl.DeviceIdType.MESH)

    mine = [pltpu.make_async_copy(x_refs[a], slot(a, me), local_sems.at[a]) for a in arrays]
    first = [[copy(a, 0, me, sibling, src=x_refs[a])]
             + [copy(a, 1 + j, me, (*chip, c), src=x_refs[a]) for j, chip in enumerate(chips)] for a in arrays]
    passed = [[copy(a, 4 + j, (*chip, c), sibling) for j, chip in enumerate(chips)] for a in arrays]

    def start():
        for a in arrays:
            mine[a].start()
            for cp in first[a]:
                cp.start()

    def forward():
        for j, chip in enumerate(chips):
            for a in arrays:
                copy(a, 1 + j, (*chip, c), me).wait_recv()
                passed[a][j].start()

    def finish():
        for a in arrays:
            copy(a, 0, sibling, me).wait_recv()
            for j, chip in enumerate(chips):
                copy(a, 4 + j, (*chip, 1 - c), me).wait_recv()
            for cp in first[a] + passed[a]:
                cp.wait_send()
            mine[a].wait()

    return start, forward, finish


def _scatter_schedule(in_refs, out_refs, send_sems, recv_sems, local_sems):
    x, y, c = lax.axis_index("x"), lax.axis_index("y"), lax.axis_index("c")
    my_chip = 2 * x + y
    mine, remote = [], []
    for a, (src, dst) in enumerate(zip(in_refs, out_refs)):
        mine.append(pltpu.make_async_copy(src.at[my_chip], dst.at[my_chip], local_sems.at[a]))
        for i in range(1, N_CHIPS):
            px, py = x ^ (i >> 1), y ^ (i & 1)
            remote.append(pltpu.make_async_remote_copy(
                src_ref=src.at[2 * px + py], dst_ref=dst.at[my_chip], send_sem=send_sems.at[a, i - 1],
                recv_sem=recv_sems.at[a, i - 1], device_id=(px, py, c), device_id_type=pl.DeviceIdType.MESH))

    def start():
        for cp in mine + remote:
            cp.start()

    def finish():
        for cp in remote:
            cp.wait_recv()
        for cp in remote:
            cp.wait_send()
        for cp in mine:
            cp.wait()

    return start, finish


def _pair_schedule(in_refs, out_refs, send_sems, recv_sems):
    x, y, c = lax.axis_index("x"), lax.axis_index("y"), lax.axis_index("c")
    copies = [pltpu.make_async_remote_copy(
        src_ref=src.at[q, 1 - c], dst_ref=dst.at[q], send_sem=send_sems.at[a, q], recv_sem=recv_sems.at[a, q],
        device_id=(x, y, 1 - c), device_id_type=pl.DeviceIdType.MESH)
        for a, (src, dst) in enumerate(zip(in_refs, out_refs)) for q in range(N_CHIPS)]

    def start():
        for cp in copies:
            cp.start()

    def finish():
        for cp in copies:
            cp.wait_recv()
        for cp in copies:
            cp.wait_send()

    return start, finish


def _pair_scratch(n_arrays):
    return [pltpu.SemaphoreType.DMA((n_arrays, N_CHIPS)), pltpu.SemaphoreType.DMA((n_arrays, N_CHIPS))]


def pair_exchange(parts, name):
    n = len(parts)

    def body(*refs):
        start, finish = _pair_schedule(refs[:n], refs[n:2 * n], *refs[2 * n:])
        start()
        finish()

    return pl.pallas_call(
        body, in_specs=[_ANY] * n, out_specs=[_ANY] * n,
        out_shape=[jax.ShapeDtypeStruct((N_CHIPS,) + s.shape[2:], s.dtype) for s in parts],
        scratch_shapes=_pair_scratch(n), name=name,
    )(*parts)


def pair_add(mine, theirs, core, name):
    _, _, k, n = mine.shape
    tc = _pick(n, (2048, 1024, 512))
    tr = _row_tile(k, tc * 3 * jnp.dtype(mine.dtype).itemsize)

    def body(core_ref, a_ref, b_ref, o_ref):
        o_ref[...] = (a_ref[...].astype(F32) + b_ref[...].astype(F32)).astype(o_ref.dtype)

    one = pl.BlockSpec((None, tr, tc), lambda q, i, j, core_ref: (q, i, j))
    grid_spec = pltpu.PrefetchScalarGridSpec(
        num_scalar_prefetch=1, grid=(N_CHIPS, k // tr, n // tc),
        in_specs=[pl.BlockSpec((None, None, tr, tc), lambda q, i, j, core_ref: (q, core_ref[0], i, j)), one],
        out_specs=one)
    return pl.pallas_call(
        body, grid_spec=grid_spec, out_shape=jax.ShapeDtypeStruct(theirs.shape, BF16),
        compiler_params=_params(("parallel", "parallel", "parallel")), name=name,
    )(core, mine, theirs)


def all_gather_many(shards, name):
    n = len(shards)

    def body(*refs):
        start, forward, finish = _gather_schedule(refs[:n], refs[n:2 * n], *refs[2 * n:])
        start()
        forward()
        finish()

    return pl.pallas_call(
        body, in_specs=[_ANY] * n, out_specs=[_ANY] * n,
        out_shape=[jax.ShapeDtypeStruct((N_DEV,) + s.shape, s.dtype) for s in shards],
        scratch_shapes=_exchange_scratch(n), name=name,
    )(*shards)


def project_and_gather(h, w_all, shards):
    return matmul(h, w_all, 'nn', F32, "proj_in", exchange=('gather', shards))


def input_grad_and_scatter(d_parts, w_all, offsets, parts):
    return matmul(d_parts, [w_all] * len(d_parts), 'nt', F32, "d_h_in", b_col_offsets=offsets,
                  exchange=('scatter', parts))


def mlp_in_and_gather(h, w_mlp_in, shards):
    return matmul(h, w_mlp_in, 'nn', [F32, BF16], "mlp_in", b_shards=True, epilogue=lambda acc: (acc,) + st_relu2(acc),
                  exchange=('gather', shards))


def mlp_input_grad_and_pair(d_u, w_mlp_in, parts):
    return matmul(d_u, w_mlp_in, 'nt', F32, "d_h_mlp", b_shards=True, exchange=('pair', parts))


def scan_and_gather(scan_in, shards):
    y, s0, *gathered = rwkv_fwd(*scan_in, name="rwkv_scan", gather=shards)
    return y, s0, gathered


def scan_bwd_and_scatter(scan_in, s0, dy, parts):
    res = rwkv_bwd(*scan_in, s0, dy, name="rwkv_scan_bwd", scatter=parts)
    return res[:6], res[6:]


def adamw_shard(parts, w, m, v, name):
    _, k, n = w.shape
    slots = parts.shape[0]
    tc = _pick(n, (2048, 1024, 512))
    tr = _row_tile(k, tc * (slots * jnp.dtype(parts.dtype).itemsize + 7 * 4))

    def body(p_ref, w_ref, m_ref, v_ref, g_ref, d_ref, nm_ref, nv_ref):
        _adamw_block(p_ref, w_ref, m_ref, v_ref, g_ref, d_ref, nm_ref, nv_ref)

    one = pl.BlockSpec((None, tr, tc), lambda i, j: (0, i, j))
    return pl.pallas_call(
        body, grid=(k // tr, n // tc), in_specs=[pl.BlockSpec((slots, tr, tc), lambda i, j: (0, i, j))] + [one] * 3,
        out_specs=[one] * 4, out_shape=[jax.ShapeDtypeStruct(w.shape, F32)] * 4,
        compiler_params=_params(("parallel", "parallel")), name=name,
    )(parts, w, m, v)


def _adamw_block(p_ref, w_ref, m_ref, v_ref, g_ref, d_ref, nm_ref, nv_ref):
    g = p_ref[0].astype(F32)
    for j in range(1, p_ref.shape[0]):
        g = g + p_ref[j].astype(F32)
    new_m = ADAM_B1 * m_ref[...] + (1.0 - ADAM_B1) * g
    new_v = ADAM_B2 * v_ref[...] + (1.0 - ADAM_B2) * jnp.square(g)
    m_hat = new_m / (1.0 - ADAM_B1 ** ADAM_STEP)
    v_hat = new_v / (1.0 - ADAM_B2 ** ADAM_STEP)
    g_ref[...] = g
    d_ref[...] = -ADAM_LR * (m_hat / (jnp.sqrt(v_hat) + ADAM_EPS) + ADAM_WD * w_ref[...])
    nm_ref[...] = new_m
    nv_ref[...] = new_v


def adamw(parts, w, m, v, name):
    rows = w.shape[0]
    tile = _row_tile(rows, N_DEV * LANES * jnp.dtype(parts.dtype).itemsize + 7 * LANES * 4)

    def body(p_ref, w_ref, m_ref, v_ref, g_ref, d_ref, nm_ref, nv_ref):
        _adamw_block(p_ref, w_ref, m_ref, v_ref, g_ref, d_ref, nm_ref, nv_ref)

    flat = pl.BlockSpec((tile, LANES), lambda i: (i, 0))
    return pl.pallas_call(
        body, grid=(rows // tile,), in_specs=[pl.BlockSpec((N_DEV, tile, LANES), lambda i: (0, i, 0))] + [flat] * 3,
        out_specs=[flat] * 4, out_shape=[jax.ShapeDtypeStruct(w.shape, F32)] * 4,
        compiler_params=_params(("parallel",)), name=name,
    )(parts, w, m, v)


def _part_rows(n_elems):
    return -(-n_elems // (PACK_ROWS * LANES)) * PACK_ROWS


def _pack(arrays, dtype, lead=()):
    parts, layout, off = [], [], 0
    for arr in arrays:
        n = math.prod(arr.shape[len(lead):])
        rows = _part_rows(n)
        flat = arr.reshape(lead + (n,)).astype(dtype)
        flat = jnp.pad(flat, [(0, 0)] * len(lead) + [(0, rows * LANES - n)])
        parts.append(flat.reshape(lead + (rows, LANES)))
        layout.append((off, rows))
        off += rows
    total = -(-off // 1024) * 1024
    if total > off:
        parts.append(jnp.zeros(lead + (total - off, LANES), dtype))
    return jnp.concatenate(parts, axis=len(lead)), layout


def _unpack(buf, layout, shapes, lead=()):
    out = []
    for (off, rows), shape in zip(layout, shapes):
        n = math.prod(shape)
        piece = lax.slice_in_dim(buf, off, off + rows, axis=len(lead))
        out.append(piece.reshape(lead + (rows * LANES,))[..., :n].reshape(lead + tuple(shape)))
    return out


def _split_shards(full, axis):
    if axis == 0:
        return full.reshape((N_DEV, full.shape[0] // N_DEV, full.shape[1]))
    return full.reshape((full.shape[0], N_DEV, full.shape[1] // N_DEV)).transpose(1, 0, 2)


def _join_shards(shards, axis):
    if axis == 0:
        return shards.reshape((-1, shards.shape[2]))
    return shards.transpose(1, 0, 2).reshape((shards.shape[1], -1))


def _shift_down(t):
    return jnp.pad(t, ((1, 0), (0, 0)))[:-1]


def _shift_up(t):
    return jnp.pad(t, ((0, 1), (0, 0)))[1:]


def kernel(x, p, norm_mix, w_in, q_gain, k_gain, rel_bias, w_attn_up, shift_mix, w0, w_decay_up, a0, w_aaa_up, w_gate_up, k_k, k_a, r_k, gn_w, gn_b, w_rwkv_up, w_out, norm_mlp, w_mlp_in, w_mlp_out, norm_ple, w_ple_gate, w_ple_proj, loss_target, m_norm_mix, m_w_in, m_q_gain, m_k_gain, m_rel_bias, m_w_attn_up, m_shift_mix, m_w0, m_w_decay_up, m_a0, m_w_aaa_up, m_w_gate_up, m_k_k, m_k_a, m_r_k, m_gn_w, m_gn_b, m_w_rwkv_up, m_w_out, m_norm_mlp, m_w_mlp_in, m_w_mlp_out, m_norm_ple, m_w_ple_gate, m_w_ple_proj, v_norm_mix, v_w_in, v_q_gain, v_k_gain, v_rel_bias, v_w_attn_up, v_shift_mix, v_w0, v_w_decay_up, v_a0, v_w_aaa_up, v_w_gate_up, v_k_k, v_k_a, v_r_k, v_gn_w, v_gn_b, v_w_rwkv_up, v_w_out, v_norm_mlp, v_w_mlp_in, v_w_mlp_out, v_norm_ple, v_w_ple_gate, v_w_ple_proj):
    given = dict(locals())
    xs = x[0]
    t_len, d_model = xs.shape
    target = loss_target[0]
    p_bf = p[0, 0].astype(BF16)
    rw_width = w0.shape[1]
    n_rheads = rw_width // RWKV_HEAD
    lora_d, lora_a, lora_g = w_decay_up.shape[1], w_aaa_up.shape[1], w_gate_up.shape[1]
    z_width = shift_mix.shape[1]
    z_pad = -(-z_width // LANES) * LANES
    qkv_width = 3 * ATTN_WIDTH
    assert z_width == 3 * rw_width + lora_d + lora_a + lora_g
    assert N_DEV * w_in.shape[2] == qkv_width + z_width + 2 * d_model
    for window, dilation in DILATED_GROUPS:
        assert window // dilation == ATTN_BLOCK and t_len % (dilation * ATTN_BLOCK) == 0

    shard_bf = {n: given[n][0].astype(BF16) for n in BIG}
    early = ['w_in', 'w_decay_up', 'w_aaa_up', 'w_gate_up']
    during_proj = ['w_mlp_out']
    during_mlp = ['w_ple_gate', 'w_ple_proj']
    during_scan = [n for n in BIG if n not in early + during_proj + during_mlp]
    late = during_proj + during_scan + during_mlp
    full = {n: _join_shards(g, 1) for n, g in zip(early, all_gather_many([shard_bf[n] for n in early], "gather_w_in"))}
    z_end = qkv_width + z_width
    w_all = jnp.concatenate([full['w_in'][:, :z_end], jnp.zeros((d_model, z_pad - z_width), BF16),
                             full['w_in'][:, z_end:]], axis=1)
    gates_at = qkv_width + z_pad

    (h_in,) = rowmap_fwd(lambda a, g: st_norm(a, g)[1:], [xs], [norm_mix], [BF16], "norm_in")
    proj, proj_gathered = project_and_gather(h_in, w_all, [shard_bf[n] for n in during_proj])
    qkv, z = proj[:, :qkv_width], proj[:, qkv_width:gates_at]
    gate_a, gate_r = proj[:, gates_at:gates_at + d_model], proj[:, gates_at + d_model:]

    q_raw = qkv[:, :ATTN_WIDTH].reshape(t_len * N_HEADS, HEAD_DIM)
    k_raw = qkv[:, ATTN_WIDTH:2 * ATTN_WIDTH].reshape(t_len * N_HEADS, HEAD_DIM)
    q_n, k_n = rowmap_fwd(st_qk_norm, [q_raw, k_raw], [q_gain, k_gain], [F32, F32], "qk_norm")
    q_n, k_n = q_n.reshape(t_len, ATTN_WIDTH), k_n.reshape(t_len, ATTN_WIDTH)
    buckets = _bucket_tables()
    bias = bias_fwd(rel_bias, buckets, "attn_bias")

    att_in, att_o, att_l = [], [], []
    for g, (_, dilation) in enumerate(DILATED_GROUPS):
        hs = slice(g * HEADS_PER_GROUP, (g + 1) * HEADS_PER_GROUP)
        cols = slice(g * ATTN_OUT, (g + 1) * ATTN_OUT)
        v_g = qkv[:, 2 * ATTN_WIDTH + g * ATTN_OUT:2 * ATTN_WIDTH + (g + 1) * ATTN_OUT]
        ops = ((q_n[:, cols], 0), (k_n[:, cols], 0), (v_g, 0), bias[hs, 0], bias[hs, 1])
        o_g, l_g = attn_fwd(*ops, dilation, name=f"attn_fwd_{g}")
        att_in.append(ops)
        att_o.append(o_g)
        att_l.append(l_g)
    (attn,) = rowmap_fwd(st_merge, att_o + att_l, [], [BF16], "attn_merge")

    c0 = rw_width
    cuts = [0, c0, 2 * c0, 3 * c0, 3 * c0 + lora_d, 3 * c0 + lora_d + lora_a, z_width]
    z_parts = [z[:, lo:hi] for lo, hi in zip(cuts[:-1], cuts[1:])]
    z_prev = [_shift_down(t) for t in z_parts]
    mixes = [shift_mix[:, lo:hi] for lo, hi in zip(cuts[:-1], cuts[1:])]
    pre_params = mixes + [w0, full['w_decay_up'], a0, full['w_aaa_up'], full['w_gate_up'], k_k, k_a]
    pre_out = rowmap_fwd(functools.partial(_st_rwkv_pre, _dot), z_parts + z_prev, pre_params, [F32] * 7, "rwkv_pre")
    r_s, lw_s, k_s, v_s, kk0_s, a_s, g_s = pre_out

    aa_s, bb_s = rowmap_fwd(functools.partial(_st_rwkv_kk, _head_sum), [kk0_s, a_s], [], [F32, F32], "rwkv_kk")
    scan_in = [r_s, lw_s, k_s, v_s, aa_s, bb_s]
    y_t, s0_h, scan_gathered = scan_and_gather(scan_in, [shard_bf[n] for n in during_scan])
    late_gathered = list(proj_gathered) + list(scan_gathered)
    def usable(names, gathered):
        return {n: g if SHARD_AXIS[n] == 1 else g.reshape(-1, g.shape[2]) for n, g in zip(names, gathered)}

    wt = usable(during_proj + during_scan, late_gathered)
    post_params = [gn_w, gn_b, r_k.reshape(1, rw_width)]
    post_in = [y_t, r_s, k_s, v_s, g_s]
    (rw,) = rowmap_fwd(functools.partial(_st_rwkv_post, _head_sum), post_in, post_params, [BF16], "rwkv_post")
    attn_d = matmul(attn, wt['w_attn_up'], 'nn', F32, "attn_up", b_shards=True)
    rwkv_d = matmul(rw, wt['w_rwkv_up'], 'nn', F32, "rwkv_up", b_shards=True)

    (merged,) = rowmap_fwd(st_gate, [gate_a, gate_r, attn_d, rwkv_d], [], [BF16], "gate_merge")
    mix_out = matmul(merged, wt['w_out'], 'nn', F32, "out_proj")
    x1, h_mlp = rowmap_fwd(st_res_norm, [xs, mix_out], [norm_mlp], [F32, BF16], "res_norm_mlp")
    (u, act), mlp_gathered = mlp_in_and_gather(h_mlp, wt['w_mlp_in'], [shard_bf[n] for n in during_mlp])
    wt.update(usable(during_mlp, mlp_gathered))
    mlp_out = matmul(act, wt['w_mlp_out'], 'nn', F32, "mlp_out")
    x2, h_ple = rowmap_fwd(st_res_norm, [x1, mlp_out], [norm_ple], [F32, BF16], "res_norm_ple")
    pg = matmul(h_ple, wt['w_ple_gate'], 'nn', F32, "ple_gate")
    pp = matmul(p_bf, wt['w_ple_proj'], 'nn', F32, "ple_proj", b_shards=True)
    dy, d_pg, d_pp, loss_local = loss_head(x2, pg, pp, target, "loss_head")

    def row_cut(full_grad):
        return full_grad.reshape(N_DEV, full_grad.shape[0] // N_DEV, full_grad.shape[1])

    grads, sends = {}, {}
    sends['w_ple_gate'] = row_cut(matmul(h_ple, d_pg, 'tn', BF16, "d_w_ple_gate"))
    sends['w_ple_proj'] = matmul(p_bf, d_pp, 'tn', BF16, "d_w_ple_proj", out_shards=True)
    d_h_ple = matmul(d_pg, wt['w_ple_gate'], 'nt', F32, "d_h_ple")
    (d_x2, d_x2_bf), (grads['norm_ple'],) = rowmap_bwd(
        st_res_norm, [x1, mlp_out], [norm_ple], [dy, d_h_ple], [F32, BF16], "res_norm_ple_bwd")
    sends['w_mlp_out'] = row_cut(matmul(act, d_x2_bf, 'tn', BF16, "d_w_mlp_out"))
    (d_u,) = matmul(d_x2_bf, wt['w_mlp_out'], 'nt', [BF16], "d_mlp_act", extras=[u],
                    epilogue=lambda d_act, u_blk: (d_act * (2.0 * jnp.maximum(u_blk, 0.0)),))
    sends['w_mlp_in'] = matmul(h_mlp, d_u, 'tn', BF16, "d_w_mlp_in", out_shards=True)

    def by_chip_and_core(name):
        return sends[name].reshape((N_CHIPS, 2) + sends[name].shape[1:])

    paired_early = ['w_ple_gate', 'w_ple_proj', 'w_mlp_out', 'w_mlp_in']
    d_h_mlp, from_sibling = mlp_input_grad_and_pair(d_u, wt['w_mlp_in'], [by_chip_and_core(n) for n in paired_early])
    from_sibling = dict(zip(paired_early, from_sibling))
    (d_x1, d_x1_bf), (grads['norm_mlp'],) = rowmap_bwd(
        st_res_norm, [xs, mix_out], [norm_mlp], [d_x2, d_h_mlp], [F32, BF16], "res_norm_mlp_bwd")

    sends['w_out'] = row_cut(matmul(merged, d_x1_bf, 'tn', BF16, "d_w_out"))
    d_merged = matmul(d_x1_bf, wt['w_out'], 'nt', F32, "d_merged")
    (d_gate_a, d_gate_r, d_attn_d, d_rwkv_d), _ = rowmap_bwd(
        st_gate, [gate_a, gate_r, attn_d, rwkv_d], [], [d_merged], [BF16] * 4, "gate_merge_bwd")
    sends['w_attn_up'] = matmul(attn, d_attn_d, 'tn', BF16, "d_w_attn_up", out_shards=True)
    sends['w_rwkv_up'] = matmul(rw, d_rwkv_d, 'tn', BF16, "d_w_rwkv_up", out_shards=True)
    d_attn = matmul(d_attn_d, wt['w_attn_up'], 'nt', F32, "d_attn", b_shards=True)
    d_rw = matmul(d_rwkv_d, wt['w_rwkv_up'], 'nt', F32, "d_rw", b_shards=True)

    (d_y, d_r1, d_k1, d_v1, d_g), (grads['gn_w'], grads['gn_b'], grads['r_k']) = rowmap_bwd(
        functools.partial(_st_rwkv_post, _head_sum_ad), post_in, post_params, [d_rw], [F32] * 5, "rwkv_post_bwd")
    core = lax.axis_index("c").astype(I32).reshape(1)

    def chip_partials(names, tag):
        todo = [n for n in names if n not in from_sibling]
        from_sibling.update(zip(todo, pair_exchange([by_chip_and_core(n) for n in todo], "pair_grads_" + tag)))
        return [pair_add(by_chip_and_core(n), from_sibling[n], core, "pair_add_" + n) for n in names]

    scan_grads, late_received = scan_bwd_and_scatter(scan_in, s0_h, d_y, chip_partials(late, "late"))
    received = dict(zip(late, late_received))
    d_r2, d_lw, d_k2, d_v2, d_aa, d_bb = scan_grads
    (d_kk0, d_a), _ = rowmap_bwd(functools.partial(_st_rwkv_kk, _head_sum_ad), [kk0_s, a_s], [], [d_aa, d_bb],
                                 [F32, F32], "rwkv_kk_bwd")
    pre_cts = [[d_r1, d_r2], d_lw, [d_k1, d_k2], [d_v1, d_v2], d_kk0, d_a, d_g]
    d_zp, d_pre = rowmap_bwd(functools.partial(_st_rwkv_pre, _dot_ad), z_parts + z_prev, pre_params, pre_cts,
                             [F32] * 12, "rwkv_pre_bwd")
    grads['shift_mix'] = jnp.concatenate(d_pre[:6], axis=1)
    grads['w0'], d_w_decay, grads['a0'], d_w_aaa, d_w_gate, grads['k_k'], grads['k_a'] = d_pre[6:]
    for name, full_grad in (('w_decay_up', d_w_decay), ('w_aaa_up', d_w_aaa), ('w_gate_up', d_w_gate)):
        sends[name] = _split_shards(full_grad, 1).astype(BF16)
    z_fill = [jnp.zeros((t_len, z_pad - z_width), F32)] if z_pad > z_width else []
    d_z_cur = jnp.concatenate(d_zp[:6] + z_fill, axis=1)
    d_z_prev = _shift_up(jnp.concatenate(d_zp[6:] + z_fill, axis=1))
    (d_z,) = rowmap_fwd(st_add, [d_z_cur, d_z_prev], [], [BF16], "d_z_sum")

    d_merge, _ = rowmap_bwd(st_merge, att_o + att_l, [], [d_attn], [F32] * 6, "attn_merge_bwd")
    d_qn, d_kn, d_vs, d_bias = [], [], [], []
    for g, (_, dilation) in enumerate(DILATED_GROUPS):
        dq, dk, dv, dbp, dbc = attn_bwd(*att_in[g], d_merge[g], d_merge[3 + g], dilation, name=f"attn_bwd_{g}")
        d_qn.append(dq)
        d_kn.append(dk)
        d_vs.append(dv)
        d_bias.append(jnp.stack([dbp, dbc], axis=1))
    d_table = bias_bwd(jnp.concatenate(d_bias, axis=0), buckets, "attn_bias_bwd")
    grads['rel_bias'] = d_table[:, :N_HEADS]
    d_qn = jnp.concatenate(d_qn, axis=1).reshape(t_len * N_HEADS, HEAD_DIM)
    d_kn = jnp.concatenate(d_kn, axis=1).reshape(t_len * N_HEADS, HEAD_DIM)
    (d_q, d_k), (grads['q_gain'], grads['k_gain']) = rowmap_bwd(
        st_qk_norm, [q_raw, k_raw], [q_gain, k_gain], [d_qn, d_kn], [BF16, BF16], "qk_norm_bwd")

    d_pieces = ([d_q.reshape(t_len, ATTN_WIDTH), d_k.reshape(t_len, ATTN_WIDTH)] + [t.astype(BF16) for t in d_vs]
                + [d_z, d_gate_a, d_gate_r])
    piece_cols = [0, ATTN_WIDTH] + [2 * ATTN_WIDTH + g * ATTN_OUT for g in range(len(DILATED_GROUPS))]
    piece_cols += [qkv_width, gates_at, gates_at + d_model]
    d_w_pieces = [matmul(h_in, piece, 'tn', BF16, f"d_w_in_{i}") for i, piece in enumerate(d_pieces)]
    d_w_pieces[-3] = d_w_pieces[-3][:, :z_width]
    sends['w_in'] = _split_shards(jnp.concatenate(d_w_pieces, axis=1), 1)
    d_h_in, early_received = input_grad_and_scatter(d_pieces, w_all, piece_cols, chip_partials(early, "early"))
    received.update(zip(early, early_received))
    (grad_x,), (grads['norm_mix'],) = rowmap_bwd(st_norm, [xs], [norm_mix], [d_x1, d_h_in], [F32], "norm_in_bwd")

    by_name = {n: adamw_shard(received[n], given[n], given['m_' + n], given['v_' + n], "adamw_" + n) for n in BIG}
    pk = lambda prefix: _pack([given[prefix + n] for n in SMALL], F32)[0]
    small_buf, small_layout = _pack([grads[n].reshape(given[n].shape) for n in SMALL], F32)
    (small_all,) = all_gather_many([small_buf], "gather_small_grads")
    small_out = adamw(small_all, pk(''), pk('m_'), pk('v_'), "adamw_replicated")
    small_shapes = [given[n].shape for n in SMALL]
    by_name.update(zip(SMALL, zip(*[_unpack(buf, small_layout, small_shapes) for buf in small_out])))
    loss = lax.psum(loss_local, MESH_AXES)
    return (loss, grad_x[None], *[by_name[n][0] for n in WEIGHTS], *[by_name[n][1] for n in WEIGHTS],
            *[by_name[n][2] for n in WEIGHTS], *[by_name[n][3] for n in WEIGHTS])
```
